```python
import math
import jax, jax.numpy as jnp
from jax import lax
import numpy as np

D_MODEL = 1024
BATCH = 4
SEQ = 4096
DEPTH = 2
DEC_BATCH = 32
DEC_SEQ = 1
PAST_LEN = 8192
PAGE_SIZE = 128

N_EVEN = (DEPTH + 1) // 2
N_ODD = DEPTH // 2
D_CONV = D_MODEL // 2
CONV_W = 3
H_B = 4
DH_B = 64
ROT_DIM = DH_B // 4
ROPE_THETA = 500000.0
Q_BLOCK = 128
H_C = 4
DK_C = (D_MODEL // 2) // H_C
DV_C = D_MODEL // H_C
CHUNK = 128
N_GROUPS = 4
EXP_PER_GROUP = 4
N_EXPERTS = N_GROUPS * EXP_PER_GROUP
TOP_K = 2
D_EXPERT = 512
D_PLE = 256
LN_EPS = 1e-5
DEEPNORM_ALPHA = (2 * DEPTH) ** 0.25
DEEPNORM_BETA = (8 * DEPTH) ** -0.25

QK_B = 2 * H_B * DH_B
V_B = H_B * 2 * DH_B
EVEN_WIDTHS = [D_CONV, D_CONV, D_CONV, QK_B, QK_B, V_B]
EVEN_SPLITS = [int(s) for s in np.cumsum(EVEN_WIDTHS)[:-1]]
D_IN_EVEN = sum(EVEN_WIDTHS)
D_MIX_EVEN = D_CONV + V_B
ODD_WIDTHS = [H_C * DK_C, H_C * DK_C, H_C * DV_C, H_C * DV_C, 2 * H_C]
ODD_SPLITS = [int(s) for s in np.cumsum(ODD_WIDTHS)[:-1]]
D_IN_ODD = sum(ODD_WIDTHS)
D_MIX_ODD = H_C * DV_C

kernel_name = 'hybrid_conv_diffattn_mlstm_hmoe_step'


def layer_norm(x, g, b):
    xf = x.astype(jnp.float32)
    mu = jnp.mean(xf, axis=-1, keepdims=True)
    var = jnp.mean(jnp.square(xf - mu), axis=-1, keepdims=True)
    return ((xf - mu) * lax.rsqrt(var + LN_EPS)).astype(x.dtype) * g + b


def rms_norm(x, w):
    xf = x.astype(jnp.float32)
    return (xf * lax.rsqrt(jnp.mean(jnp.square(xf), axis=-1, keepdims=True) + LN_EPS)).astype(x.dtype) * w


def partial_rope(x, pos):
    half = ROT_DIM // 2
    inv = ROPE_THETA ** (-jnp.arange(half, dtype=jnp.float32) / half)
    ang = pos.astype(jnp.float32)[:, None] * inv[None, :]
    cos = jnp.cos(ang)[:, None, :].astype(x.dtype)
    sin = jnp.sin(ang)[:, None, :].astype(x.dtype)
    x1, x2 = x[..., :half], x[..., half:ROT_DIM]
    return jnp.concatenate([x1 * cos - x2 * sin, x2 * cos + x1 * sin, x[..., ROT_DIM:]], axis=-1)


def diff_attn_block(q, k, v, q_pos, k_pos, lam):
    b, tq = q.shape[:2]
    tk = k.shape[1]
    s = jnp.einsum('bqhd,bkhd->bhqk', q, k).astype(jnp.float32) * (DH_B ** -0.5)
    s = jnp.where(k_pos[None, :] <= q_pos[:, None], s, -jnp.inf)
    a = jax.nn.softmax(s, axis=-1).reshape(b, H_B, 2, tq, tk)
    a = (a[:, :, 0] - lam * a[:, :, 1]).astype(v.dtype)
    return jnp.einsum('bhqk,bkhe->bqhe', a, v)


def diff_attention_prompt(q, k, v, pos, lam):
    b, t = q.shape[:2]
    nb = t // Q_BLOCK
    qb = jnp.moveaxis(q.reshape(b, nb, Q_BLOCK, 2 * H_B, DH_B), 1, 0)
    pb = pos.reshape(nb, Q_BLOCK)
    ob = lax.map(lambda qp: diff_attn_block(qp[0], k, v, qp[1], pos, lam), (qb, pb))
    return jnp.moveaxis(ob, 0, 1).reshape(b, t, H_B, 2 * DH_B)


def even_mixer(x, conv_prev, pos, attend, lam_init, w_in, conv_w, subln_w, w_out):
    b, t, _ = x.shape
    gb, gc, xin, q, k, v = jnp.split(x @ w_in, EVEN_SPLITS, axis=-1)
    u = gc * xin
    full = jnp.concatenate([conv_prev.astype(u.dtype), u], axis=1)
    conv = sum(full[:, j:j + t] * conv_w[j] for j in range(CONV_W))
    y_conv = gb * conv
    q = partial_rope(q.reshape(b, t, 2 * H_B, DH_B), pos)
    k = partial_rope(k.reshape(b, t, 2 * H_B, DH_B), pos)
    v = v.reshape(b, t, H_B, 2 * DH_B)
    o = rms_norm(attend(q, k, v), subln_w) * (1.0 - lam_init)
    y = jnp.concatenate([y_conv, o.reshape(b, t, V_B)], axis=-1) @ w_out
    return y, k, v, full[:, t:]


def mlstm_chunk(q, k, v, ig, logf, C0, n0, m0):
    L = q.shape[2]
    bcum = jnp.cumsum(logf, axis=-1)
    causal = jnp.tril(jnp.ones((L, L), dtype=bool))
    dmat = jnp.where(causal, bcum[..., :, None] - bcum[..., None, :] + ig[..., None, :], -jnp.inf)
    inter = bcum + m0[..., None]
    m = jnp.maximum(inter, jnp.max(dmat, axis=-1))
    w = jnp.exp(dmat - m[..., None])
    g = jnp.exp(inter - m)
    s = jnp.einsum('bhtd,bhsd->bhts', q, k) * w
    num = g[..., None] * jnp.einsum('bhtd,bhde->bhte', q, C0) + jnp.einsum('bhts,bhse->bhte', s, v)
    den = g * jnp.einsum('bhtd,bhd->bht', q, n0) + jnp.sum(s, axis=-1)
    h = num / jnp.maximum(jnp.abs(den), jnp.exp(-m))[..., None]
    m_last = m[..., -1]
    w_last = jnp.exp(bcum[..., -1:] - bcum + ig - m_last[..., None])
    g_last = jnp.exp(bcum[..., -1] + m0 - m_last)
    C = g_last[..., None, None] * C0 + jnp.einsum('bhs,bhsd,bhse->bhde', w_last, k, v)
    n = g_last[..., None] * n0 + jnp.einsum('bhs,bhsd->bhd', w_last, k)
    return h, (C, n, m_last)


def mlstm_sequence(q, k, v, ig, logf, C0, n0, m0):
    b, h, t, _ = q.shape
    L = CHUNK if t % CHUNK == 0 else t
    nc = t // L
    to_chunks = lambda a: jnp.moveaxis(a.reshape(b, h, nc, L, *a.shape[3:]), 2, 0)
    xs = (to_chunks(q), to_chunks(k), to_chunks(v), to_chunks(ig), to_chunks(logf))

    def step(carry, inp):
        hc, carry = mlstm_chunk(*inp, *carry)
        return carry, hc

    (C, n, m), hs = lax.scan(step, (C0, n0, m0), xs)
    return jnp.moveaxis(hs, 0, 2).reshape(b, h, t, DV_C), C, n, m


def odd_mixer(x, C0, n0, m0, w_in, b_gates, norm_w, w_out):
    b, t, _ = x.shape
    q, k, v, o, g = jnp.split(x @ w_in, ODD_SPLITS, axis=-1)
    heads = lambda a, d: jnp.moveaxis(a.reshape(b, t, H_C, d), 2, 1).astype(jnp.float32)
    q = heads(q, DK_C) * (DK_C ** -0.5)
    k = heads(k, DK_C)
    v = heads(v, DV_C)
    g = jnp.moveaxis((g + b_gates).astype(jnp.float32), -1, 1)
    ig = g[:, :H_C]
    logf = jax.nn.log_sigmoid(g[:, H_C:])
    hseq, C, n, m = mlstm_sequence(q, k, v, ig, logf, C0.astype(jnp.float32),
                                   n0.astype(jnp.float32), m0.astype(jnp.float32))
    hseq = jnp.moveaxis(hseq, 1, 2)
    mu = jnp.mean(hseq, axis=-1, keepdims=True)
    var = jnp.mean(jnp.square(hseq - mu), axis=-1, keepdims=True)
    hn = ((hseq - mu) * lax.rsqrt(var + LN_EPS)).reshape(b, t, D_MIX_ODD).astype(x.dtype) * norm_w
    return (jax.nn.sigmoid(o) * hn) @ w_out, C, n, m


def hier_moe(x, w_group, w_router, w_gate, w_up, w_down):
    b, t, d = x.shape
    xt = x.reshape(b * t, d)
    gl = (xt @ w_group).astype(jnp.float32)
    gp = jax.nn.softmax(gl, axis=-1)
    g_w, g_idx = lax.top_k(gp, 1)
    el = (xt @ w_router).astype(jnp.float32).reshape(b * t, N_GROUPS, EXP_PER_GROUP)
    el = jnp.take_along_axis(el, g_idx[:, :, None], axis=1)[:, 0]
    top_w, top_i = lax.top_k(jax.nn.softmax(el, axis=-1), TOP_K)
    top_w = top_w / jnp.sum(top_w, axis=-1, keepdims=True) * g_w
    e_idx = g_idx * EXP_PER_GROUP + top_i
    gate = jnp.sum(jax.nn.one_hot(e_idx, N_EXPERTS, dtype=jnp.float32) * top_w[..., None], axis=1).astype(x.dtype)
    y = jnp.zeros_like(xt)
    for e in range(N_EXPERTS):
        hid = jax.nn.silu(xt @ w_gate[e]) * (xt @ w_up[e])
        y = y + gate[:, e:e + 1] * (hid @ w_down[e])
    return y.reshape(b, t, d)


def layer_tail(x, mix, p, ln1_g, ln1_b, ln2_g, ln2_b, wg, wr, we_g, we_u, we_d, wpp, wpg):
    x = layer_norm(DEEPNORM_ALPHA * x + mix, ln1_g, ln1_b)
    x = layer_norm(DEEPNORM_ALPHA * x + hier_moe(x, wg, wr, we_g, we_u, we_d), ln2_g, ln2_b)
    return x + jax.nn.sigmoid(x @ wpg) * (p @ wpp)


def setup_inputs(seed: int = 0) -> dict:
    key = jax.random.key(seed)
    keys = jax.random.split(key, 48)
    ctr = iter(range(48))
    nrm = lambda shape, scale: jax.random.normal(keys[next(ctr)], shape, jnp.float32) * scale
    n_pages = PAST_LEN // PAGE_SIZE
    n_used = DEC_BATCH * n_pages
    n_pool = n_used + (n_used + 3) // 4
    page_table = jax.random.permutation(keys[next(ctr)], n_pool)[:n_used].reshape(DEC_BATCH, n_pages).astype(jnp.int32)
    beta = DEEPNORM_BETA
    f_bias = jnp.broadcast_to(jnp.linspace(3.0, 6.0, H_C, dtype=jnp.float32), (N_ODD, H_C)) + nrm((N_ODD, H_C), 0.1)
    b_gates_odd = jnp.concatenate([nrm((N_ODD, H_C), 0.1), f_bias], axis=-1)
    return {
        'x_prompt': nrm((BATCH, SEQ, D_MODEL), 1.0),
        'x_sample': nrm((DEC_BATCH, DEC_SEQ, D_MODEL), 1.0),
        'cache_k': nrm((N_EVEN, n_pool, PAGE_SIZE, 2 * H_B, DH_B), 1.0),
        'cache_v': nrm((N_EVEN, n_pool, PAGE_SIZE, H_B, 2 * DH_B), 1.0),
        'page_table': page_table,
        'state_conv': nrm((N_EVEN, DEC_BATCH, CONV_W - 1, D_CONV), 1.0),
        'state_mlstm_C': nrm((N_ODD, DEC_BATCH, H_C, DK_C, DV_C), 0.1),
        'state_mlstm_n': nrm((N_ODD, DEC_BATCH, H_C, DK_C), 0.1),
        'state_mlstm_m': nrm((N_ODD, DEC_BATCH, H_C), 1.0),
        'p_prompt': nrm((DEPTH, BATCH, SEQ, D_PLE), 1.0),
        'p_sample': nrm((DEPTH, DEC_BATCH, DEC_SEQ, D_PLE), 1.0),
        'w_in_even': nrm((N_EVEN, D_MODEL, D_IN_EVEN), D_MODEL ** -0.5),
        'conv_w': nrm((N_EVEN, CONV_W, D_CONV), CONV_W ** -0.5),
        'lambda_q1': nrm((N_EVEN, DH_B), 0.1),
        'lambda_k1': nrm((N_EVEN, DH_B), 0.1),
        'lambda_q2': nrm((N_EVEN, DH_B), 0.1),
        'lambda_k2': nrm((N_EVEN, DH_B), 0.1),
        'subln_w': 1.0 + nrm((N_EVEN, 2 * DH_B), 0.02),
        'w_out_even': nrm((N_EVEN, D_MIX_EVEN, D_MODEL), beta * D_MIX_EVEN ** -0.5),
        'w_in_odd': nrm((N_ODD, D_MODEL, D_IN_ODD), D_MODEL ** -0.5),
        'b_gates_odd': b_gates_odd,
        'mh_norm_w': 1.0 + nrm((N_ODD, D_MIX_ODD), 0.02),
        'w_out_odd': nrm((N_ODD, D_MIX_ODD, D_MODEL), beta * D_MIX_ODD ** -0.5),
        'ln_mix_g': 1.0 + nrm((DEPTH, D_MODEL), 0.02),
        'ln_mix_b': nrm((DEPTH, D_MODEL), 0.02),
        'ln_ffn_g': 1.0 + nrm((DEPTH, D_MODEL), 0.02),
        'ln_ffn_b': nrm((DEPTH, D_MODEL), 0.02),
        'w_group': nrm((DEPTH, D_MODEL, N_GROUPS), D_MODEL ** -0.5),
        'w_router': nrm((DEPTH, D_MODEL, N_EXPERTS), D_MODEL ** -0.5),
        'w_exp_gate': nrm((DEPTH, N_EXPERTS, D_MODEL, D_EXPERT), D_MODEL ** -0.5),
        'w_exp_up': nrm((DEPTH, N_EXPERTS, D_MODEL, D_EXPERT), D_MODEL ** -0.5),
        'w_exp_down': nrm((DEPTH, N_EXPERTS, D_EXPERT, D_MODEL), beta * D_EXPERT ** -0.5),
        'w_ple_proj': nrm((DEPTH, D_PLE, D_MODEL), D_PLE ** -0.5),
        'w_ple_gate': nrm((DEPTH, D_MODEL, D_MODEL), D_MODEL ** -0.5),
    }


def reference(x_prompt, x_sample, cache_k, cache_v, page_table, state_conv, state_mlstm_C, state_mlstm_n,
              state_mlstm_m, p_prompt, p_sample, w_in_even, conv_w, lambda_q1, lambda_k1, lambda_q2, lambda_k2,
              subln_w, w_out_even, w_in_odd, b_gates_odd, mh_norm_w, w_out_odd, ln_mix_g, ln_mix_b, ln_ffn_g,
              ln_ffn_b, w_group, w_router, w_exp_gate, w_exp_up, w_exp_down, w_ple_proj, w_ple_gate):
    bp, tp, _ = x_prompt.shape
    bs, ts, _ = x_sample.shape
    past_len = page_table.shape[1] * cache_k.shape[2]
    pos_p = jnp.arange(tp)
    pos_s = past_len + jnp.arange(ts)
    kpos_s = jnp.arange(past_len + ts)
    xp, xs = x_prompt, x_sample
    kp_l, vp_l, cp_l, Cp_l, np_l, mp_l = [], [], [], [], [], []
    ks_l, vs_l, cs_l, Cs_l, ns_l, ms_l = [], [], [], [], [], []
    for i in range(DEPTH):
        j = i // 2
        if i % 2 == 0:
            lam_init = 0.8 - 0.6 * math.exp(-0.3 * i)
            lam = (jnp.exp(jnp.sum(lambda_q1[j] * lambda_k1[j]).astype(jnp.float32))
                   - jnp.exp(jnp.sum(lambda_q2[j] * lambda_k2[j]).astype(jnp.float32)) + lam_init)
            attend_p = lambda q, k, v: diff_attention_prompt(q, k, v, pos_p, lam)
            mix_p, k_new, v_new, c_new = even_mixer(xp, jnp.zeros((bp, CONV_W - 1, D_CONV), xp.dtype), pos_p,
                                                    attend_p, lam_init, w_in_even[j], conv_w[j], subln_w[j], w_out_even[j])
            kp_l.append(k_new); vp_l.append(v_new); cp_l.append(c_new)
            k_past = cache_k[j][page_table].reshape(bs, past_len, 2 * H_B, DH_B)
            v_past = cache_v[j][page_table].reshape(bs, past_len, H_B, 2 * DH_B)
            attend_s = lambda q, k, v: diff_attn_block(
                q, jnp.concatenate([k_past.astype(k.dtype), k], axis=1),
                jnp.concatenate([v_past.astype(v.dtype), v], axis=1), pos_s, kpos_s, lam)
            mix_s, k_new, v_new, c_new = even_mixer(xs, state_conv[j], pos_s, attend_s, lam_init,
                                                    w_in_even[j], conv_w[j], subln_w[j], w_out_even[j])
            ks_l.append(k_new); vs_l.append(v_new); cs_l.append(c_new)
        else:
            mix_p, C_new, n_new, m_new = odd_mixer(
                xp, jnp.zeros((bp, H_C, DK_C, DV_C), jnp.float32), jnp.zeros((bp, H_C, DK_C), jnp.float32),
                jnp.zeros((bp, H_C), jnp.float32), w_in_odd[j], b_gates_odd[j], mh_norm_w[j], w_out_odd[j])
            Cp_l.append(C_new); np_l.append(n_new); mp_l.append(m_new)
            mix_s, C_new, n_new, m_new = odd_mixer(xs, state_mlstm_C[j], state_mlstm_n[j], state_mlstm_m[j],
                                                   w_in_odd[j], b_gates_odd[j], mh_norm_w[j], w_out_odd[j])
            Cs_l.append(C_new); ns_l.append(n_new); ms_l.append(m_new)
        tail = (ln_mix_g[i], ln_mix_b[i], ln_ffn_g[i], ln_ffn_b[i], w_group[i], w_router[i],
                w_exp_gate[i], w_exp_up[i], w_exp_down[i], w_ple_proj[i], w_ple_gate[i])
        xp = layer_tail(xp, mix_p, p_prompt[i], *tail)
        xs = layer_tail(xs, mix_s, p_sample[i], *tail)
    return (xp, xs,
            jnp.stack(kp_l), jnp.stack(vp_l), jnp.stack(cp_l), jnp.stack(Cp_l), jnp.stack(np_l), jnp.stack(mp_l),
            jnp.stack(ks_l), jnp.stack(vs_l), jnp.stack(cs_l), jnp.stack(Cs_l), jnp.stack(ns_l), jnp.stack(ms_l))
```

```python
import functools
import math

import numpy as np
import jax
import jax.numpy as jnp
from jax import lax
from jax.experimental import pallas as pl
from jax.experimental.pallas import tpu as pltpu

F32 = jnp.float32
BF16 = jnp.bfloat16
I32 = jnp.int32

D_MODEL = 1024
DEPTH = 2
PAGE_SIZE = 128
D_CONV = D_MODEL // 2
CONV_W = 3
H_B = 4
DH_B = 64
ROT_DIM = DH_B // 4
ROPE_THETA = 500000.0
H_C = 4
DK_C = (D_MODEL // 2) // H_C
DV_C = D_MODEL // H_C
N_GROUPS = 4
EXP_PER_GROUP = 4
N_EXPERTS = N_GROUPS * EXP_PER_GROUP
D_EXPERT = 512
D_PLE = 256
LN_EPS = 1e-5
DEEPNORM_ALPHA = (2 * DEPTH) ** 0.25
QK_B = 2 * H_B * DH_B
V_B = H_B * 2 * DH_B
N_PAIRS = EXP_PER_GROUP * (EXP_PER_GROUP - 1) // 2
N_CLASSES = N_GROUPS * N_PAIRS

LANES = 128
VMEM_LIMIT = 56 * 1024 * 1024
ROW_TILE = 512
ATTN_TILE = 256
MLSTM_CHUNK = 128
MOE_TILE = 256
PAGES_PER_STEP = 8
ROUTE_GROUP_LANE = 0
ROUTE_EXPERT_LANE = 16


def _cparams(*sem):
    return pltpu.CompilerParams(dimension_semantics=sem, vmem_limit_bytes=VMEM_LIMIT)


def _dot(a, b):
    return jnp.dot(a, b, preferred_element_type=F32)


def _dot_nt(a, b):
    return lax.dot_general(a, b, (((1,), (1,)), ((), ())), preferred_element_type=F32)


def _dot_tn(a, b):
    return lax.dot_general(a, b, (((0,), (0,)), ((), ())), preferred_element_type=F32)


def _layer_norm_rows(z, g, b):
    mu = jnp.mean(z, axis=-1, keepdims=True)
    zc = z - mu
    var = jnp.mean(zc * zc, axis=-1, keepdims=True)
    return zc * lax.rsqrt(var + LN_EPS) * g + b


def _log_sigmoid(x):
    return jnp.minimum(x, 0.0) - jnp.log1p(jnp.exp(-jnp.abs(x)))


def _even_in_kernel(x_ref, w_ref, cw_ref, prev_ref, rc_ref, rs1_ref, rs2_ref,
                    yc_ref, q_ref, kf_ref, vf_ref, kb_ref, vb_ref, u_ref, carry_ref, *, decode):
    xb = x_ref[0].astype(BF16) if not decode else x_ref[...].astype(BF16)
    tm = xb.shape[0]

    def proj(c0, n):
        return _dot(xb, w_ref[:, c0:c0 + n])

    gate_b = proj(0, D_CONV)
    u = proj(D_CONV, D_CONV) * proj(2 * D_CONV, D_CONV)
    if decode:
        um2 = prev_ref[0]
        um1 = prev_ref[1]
        u_ref[...] = u
    else:
        j = pl.program_id(1)

        @pl.when(j == 0)
        def _():
            carry_ref[...] = prev_ref[0]

        row = lax.broadcasted_iota(I32, u.shape, 0)
        c2 = carry_ref[0:1, :]
        c1 = carry_ref[1:2, :]
        um1 = jnp.where(row == 0, c1, pltpu.roll(u, 1, 0))
        um2 = jnp.where(row == 0, c2, jnp.where(row == 1, c1, pltpu.roll(u, 2, 0)))
        carry_ref[...] = u[tm - 2:tm, :]
        u_ref[0] = u[tm - 2:tm, :]
    cw = cw_ref[...]
    conv = um2 * cw[0:1, :] + um1 * cw[1:2, :] + u * cw[2:3, :]
    y_conv = (gate_b * conv).astype(BF16)

    reps = QK_B // LANES
    cos = jnp.concatenate([rc_ref[...]] * reps, axis=1)
    sin_lo = jnp.concatenate([rs1_ref[...]] * reps, axis=1)
    sin_hi = jnp.concatenate([rs2_ref[...]] * reps, axis=1)
    half = ROT_DIM // 2

    def rope(z):
        return z * cos + pltpu.roll(z, QK_B - half, 1) * sin_lo + pltpu.roll(z, half, 1) * sin_hi

    q = rope(proj(3 * D_CONV, QK_B)) * (DH_B ** -0.5)
    k = rope(proj(3 * D_CONV + QK_B, QK_B))
    v = proj(3 * D_CONV + 2 * QK_B, V_B)
    if decode:
        yc_ref[...] = y_conv
        q_ref[...] = q.astype(BF16)
        kf_ref[...] = k
        vf_ref[...] = v
        kb_ref[...] = k.astype(BF16)
        vb_ref[...] = v.astype(BF16)
    else:
        yc_ref[0] = y_conv
        q_ref[0] = q.astype(BF16)
        kf_ref[0] = k
        vf_ref[0] = v
        kb_ref[0] = k.astype(BF16)
        vb_ref[0] = v.astype(BF16)


def _rope_tables(pos):
    half = ROT_DIM // 2
    inv = ROPE_THETA ** (-jnp.arange(half, dtype=F32) / half)
    ang = pos.astype(F32)[:, None] * inv[None, :]
    cos, sin = jnp.cos(ang), jnp.sin(ang)
    t = pos.shape[0]
    ones = jnp.ones((t, DH_B - ROT_DIM), F32)
    zeros = jnp.zeros((t, DH_B - ROT_DIM), F32)
    zh = jnp.zeros((t, half), F32)
    c = jnp.concatenate([cos, cos, ones], axis=1)
    s_lo = jnp.concatenate([-sin, zh, zeros], axis=1)
    s_hi = jnp.concatenate([zh, sin, zeros], axis=1)
    tile2 = lambda a: jnp.concatenate([a, a], axis=1)
    return tile2(c), tile2(s_lo), tile2(s_hi)


def _even_in_prompt(x, w_bf, conv_w, conv_prev, tables):
    b, t, _ = x.shape
    tm = ROW_TILE
    d_in = w_bf.shape[1]
    row3 = lambda n: pl.BlockSpec((1, tm, n), lambda i, j: (i, j, 0))
    full2 = lambda a: pl.BlockSpec(a.shape, lambda i, j: (0, 0))
    tab = pl.BlockSpec((tm, LANES), lambda i, j: (j, 0))
    st = pl.BlockSpec((1, CONV_W - 1, D_CONV), lambda i, j: (i, 0, 0))
    outs = [jax.ShapeDtypeStruct((b, t, D_CONV), BF16), jax.ShapeDtypeStruct((b, t, QK_B), BF16),
            jax.ShapeDtypeStruct((b, t, QK_B), F32), jax.ShapeDtypeStruct((b, t, V_B), F32),
            jax.ShapeDtypeStruct((b, t, QK_B), BF16), jax.ShapeDtypeStruct((b, t, V_B), BF16),
            jax.ShapeDtypeStruct((b, CONV_W - 1, D_CONV), F32)]
    return pl.pallas_call(
        functools.partial(_even_in_kernel, decode=False),
        grid=(b, t // tm),
        in_specs=[row3(D_MODEL), full2(w_bf), full2(conv_w), st, tab, tab, tab],
        out_specs=[row3(D_CONV), row3(QK_B), row3(QK_B), row3(V_B), row3(QK_B), row3(V_B), st],
        out_shape=outs,
        scratch_shapes=[pltpu.VMEM((CONV_W - 1, D_CONV), F32)],
        compiler_params=_cparams("arbitrary", "arbitrary"),
        name="even_in_prompt",
    )(x, w_bf, conv_w, conv_prev, *tables)


def _even_in_decode(x, w_bf, conv_w, conv_prev_t, tables):
    n = x.shape[0]
    full = lambda a: pl.BlockSpec(a.shape, lambda i: (0,) * a.ndim)
    o2 = lambda c, dt: jax.ShapeDtypeStruct((n, c), dt)
    outs = [o2(D_CONV, BF16), o2(QK_B, BF16), o2(QK_B, F32), o2(V_B, F32), o2(QK_B, BF16), o2(V_B, BF16),
            o2(D_CONV, F32)]
    ins = [x, w_bf, conv_w, conv_prev_t, *tables]
    return pl.pallas_call(
        functools.partial(_even_in_kernel, decode=True),
        grid=(1,),
        in_specs=[full(a) for a in ins],
        out_specs=[pl.BlockSpec(o.shape, lambda i: (0, 0)) for o in outs],
        out_shape=outs,
        scratch_shapes=[pltpu.VMEM((CONV_W - 1, D_CONV), F32)],
        compiler_params=_cparams("arbitrary"),
        name="even_in_decode",
    )(*ins)


def _lambda_value(lam_ref, lam_init):
    lv = lam_ref[...]
    a = jnp.sum(lv[0:1, :] * lv[1:2, :], axis=1, keepdims=True)
    b = jnp.sum(lv[2:3, :] * lv[3:4, :], axis=1, keepdims=True)
    return jnp.exp(a) - jnp.exp(b) + lam_init


def _sub_norm(o, sub_ref, lam_init):
    ms = jnp.mean(o * o, axis=-1, keepdims=True)
    return o * lax.rsqrt(ms + LN_EPS) * sub_ref[...] * (1.0 - lam_init)


def _attn_prompt_kernel(q_ref, k_ref, v_ref, lam_ref, sub_ref, o_ref, m_scr, l_scr, acc_scr, *, lam_init):
    i = pl.program_id(2)
    tq = q_ref.shape[1]
    tk = tq
    q = q_ref[0]
    lane = lax.broadcasted_iota(I32, q.shape, 1)
    zero = jnp.zeros_like(q)
    qq = jnp.concatenate([jnp.where(lane < DH_B, q, zero), jnp.where(lane >= DH_B, q, zero)], axis=0)
    m_scr[...] = jnp.full(m_scr.shape, -jnp.inf, F32)
    l_scr[...] = jnp.zeros(l_scr.shape, F32)
    acc_scr[...] = jnp.zeros(acc_scr.shape, F32)

    def step(j, masked):
        start = pl.multiple_of(j * tk, tk)
        kj = k_ref[0, pl.ds(start, tk), :]
        vj = v_ref[0, pl.ds(start, tk), :]
        s = _dot_nt(qq, kj)
        if masked:
            r = lax.broadcasted_iota(I32, s.shape, 0)
            c = lax.broadcasted_iota(I32, s.shape, 1)
            r = jnp.where(r >= tq, r - tq, r)
            s = jnp.where(c <= r, s, -jnp.inf)
        m_prev = m_scr[...]
        m_new = jnp.maximum(m_prev, jnp.max(s, axis=1, keepdims=True))
        alpha = jnp.exp(m_prev - m_new)
        p = jnp.exp(s - m_new)
        l_scr[...] = alpha * l_scr[...] + jnp.sum(p, axis=1, keepdims=True)
        acc_scr[...] = alpha * acc_scr[...] + _dot(p.astype(BF16), vj)
        m_scr[...] = m_new

    def body(j, carry):
        step(j, False)
        return carry

    lax.fori_loop(0, i, body, 0)
    step(i, True)
    on = acc_scr[...] / l_scr[...]
    lam = _lambda_value(lam_ref, lam_init)
    o = on[0:tq, :] - lam * on[tq:2 * tq, :]
    o_ref[0] = _sub_norm(o, sub_ref, lam_init).astype(BF16)


def _attn_prompt(q, k, v, lam_vecs, subln, lam_init):
    b, t, _ = q.shape
    tq = ATTN_TILE
    qs = pl.BlockSpec((1, tq, LANES), lambda bi, h, i: (bi, i, h))
    kv = pl.BlockSpec((1, t, LANES), lambda bi, h, i: (bi, 0, h))
    full = lambda a: pl.BlockSpec(a.shape, lambda bi, h, i: (0, 0))
    return pl.pallas_call(
        functools.partial(_attn_prompt_kernel, lam_init=lam_init),
        grid=(b, H_B, t // tq),
        in_specs=[qs, kv, kv, full(lam_vecs), full(subln)],
        out_specs=qs,
        out_shape=jax.ShapeDtypeStruct((b, t, V_B), BF16),
        scratch_shapes=[pltpu.VMEM((2 * tq, 1), F32), pltpu.VMEM((2 * tq, 1), F32),
                        pltpu.VMEM((2 * tq, LANES), F32)],
        compiler_params=_cparams("arbitrary", "arbitrary", "arbitrary"),
        name="attn_prompt",
    )(q, k, v, lam_vecs, subln)


def _attn_decode_kernel(pt_ref, qbd_ref, kn_ref, vn_ref, lam_ref, sub_ref, *rest, lam_init, n_pages):
    del pt_ref
    k_refs = rest[:n_pages]
    v_refs = rest[n_pages:2 * n_pages]
    o_ref, m_scr, l_scr, acc_scr = rest[2 * n_pages:]
    j = pl.program_id(1)
    qbd = qbd_ref[0]

    @pl.when(j == 0)
    def _():
        s_new = jnp.sum(qbd.astype(F32) * kn_ref[0], axis=1, keepdims=True)
        m_scr[...] = s_new
        l_scr[...] = jnp.ones(l_scr.shape, F32)
        acc_scr[...] = jnp.broadcast_to(vn_ref[0], acc_scr.shape)

    s = jnp.concatenate([_dot_nt(qbd, k_refs[r][0].astype(BF16)) for r in range(n_pages)], axis=1)
    m_prev = m_scr[...]
    m_new = jnp.maximum(m_prev, jnp.max(s, axis=1, keepdims=True))
    alpha = jnp.exp(m_prev - m_new)
    p = jnp.exp(s - m_new)
    l_scr[...] = alpha * l_scr[...] + jnp.sum(p, axis=1, keepdims=True)
    pv = _dot(p[:, 0:PAGE_SIZE].astype(BF16), v_refs[0][0].astype(BF16))
    for r in range(1, n_pages):
        pv = pv + _dot(p[:, r * PAGE_SIZE:(r + 1) * PAGE_SIZE].astype(BF16), v_refs[r][0].astype(BF16))
    acc_scr[...] = alpha * acc_scr[...] + pv
    m_scr[...] = m_new

    @pl.when(j == pl.num_programs(1) - 1)
    def _():
        on = acc_scr[...] / l_scr[...]
        row = lax.broadcasted_iota(I32, (2 * H_B, 2 * DH_B), 0)
        head = jnp.where(row >= H_B, row - H_B, row)
        o8 = jnp.zeros((2 * H_B, 2 * DH_B), F32)
        for c in range(H_B):
            o8 = o8 + jnp.where(head == c, on[:, c * 2 * DH_B:(c + 1) * 2 * DH_B], 0.0)
        lam = _lambda_value(lam_ref, lam_init)
        o = o8 - lam * pltpu.roll(o8, H_B, 0)
        o_ref[0] = _sub_norm(o, sub_ref, lam_init)


def _attn_decode(qbd, k_new, v_new, cache_k, cache_v, page_table, lam_vecs, subln, lam_init):
    n = qbd.shape[0]
    n_pages = page_table.shape[1]
    pp = PAGES_PER_STEP
    width = cache_k.shape[-1]
    c2 = lambda a: pl.BlockSpec(a.shape, lambda b, j, pt: (0, 0))
    per_b = lambda a: pl.BlockSpec((1,) + a.shape[1:], lambda b, j, pt: (b, 0, 0))

    def page(r):
        return pl.BlockSpec((1, PAGE_SIZE, width), lambda b, j, pt: (pt[b, j * pp + r], 0, 0))

    grid_spec = pltpu.PrefetchScalarGridSpec(
        num_scalar_prefetch=1,
        grid=(n, n_pages // pp),
        in_specs=[per_b(qbd), per_b(k_new), per_b(v_new), c2(lam_vecs), c2(subln)]
        + [page(r) for r in range(pp)] + [page(r) for r in range(pp)],
        out_specs=pl.BlockSpec((1, 2 * H_B, 2 * DH_B), lambda b, j, pt: (b, 0, 0)),
        scratch_shapes=[pltpu.VMEM((2 * H_B, 1), F32), pltpu.VMEM((2 * H_B, 1), F32),
                        pltpu.VMEM((2 * H_B, width), F32)],
    )
    return pl.pallas_call(
        functools.partial(_attn_decode_kernel, lam_init=lam_init, n_pages=pp),
        grid_spec=grid_spec,
        out_shape=jax.ShapeDtypeStruct((n, 2 * H_B, 2 * DH_B), F32),
        compiler_params=_cparams("arbitrary", "arbitrary"),
        name="attn_decode",
    )(page_table, qbd, k_new, v_new, lam_vecs, subln, *([cache_k] * pp), *([cache_v] * pp))


def _odd_in_kernel(x_ref, w_ref, wg_ref, wgt_ref, bg_ref, bgt_ref,
                   q_ref, k_ref, v_ref, o_ref, gc_ref, gr_ref, *, decode):
    xb = x_ref[0].astype(BF16) if not decode else x_ref[...].astype(BF16)
    qw = H_C * DK_C
    vw = H_C * DV_C
    q = _dot(xb, w_ref[:, 0:qw]) * (DK_C ** -0.5)
    k = _dot(xb, w_ref[:, qw:2 * qw])
    v = _dot(xb, w_ref[:, 2 * qw:2 * qw + vw])
    o = _dot(xb, w_ref[:, 2 * qw + vw:2 * qw + 2 * vw])
    g_col = _dot(xb, wg_ref[...]) + bg_ref[...]
    lane = lax.broadcasted_iota(I32, g_col.shape, 1)
    g_col = jnp.where(lane < H_C, g_col, _log_sigmoid(g_col))
    g_row = _dot_nt(wgt_ref[...], xb) + bgt_ref[:, 0:1]
    sub = lax.broadcasted_iota(I32, g_row.shape, 0)
    g_row = jnp.where(sub < H_C, g_row, _log_sigmoid(g_row))
    if decode:
        q_ref[...] = q.astype(BF16)
        k_ref[...] = k.astype(BF16)
        v_ref[...] = v.astype(BF16)
        o_ref[...] = o
        gc_ref[...] = g_col
        gr_ref[...] = g_row
    else:
        q_ref[0] = q.astype(BF16)
        k_ref[0] = k.astype(BF16)
        v_ref[0] = v.astype(BF16)
        o_ref[0] = o
        gc_ref[0] = g_col
        gr_ref[0] = g_row


def _odd_in(x, w_bf, wg, wgt, bg, bgt, decode):
    qw, vw = H_C * DK_C, H_C * DV_C
    if decode:
        n = x.shape[0]
        ins = [x, w_bf, wg, wgt, bg, bgt]
        outs = [jax.ShapeDtypeStruct((n, qw), BF16), jax.ShapeDtypeStruct((n, qw), BF16),
                jax.ShapeDtypeStruct((n, vw), BF16), jax.ShapeDtypeStruct((n, vw), F32),
                jax.ShapeDtypeStruct((n, LANES), F32), jax.ShapeDtypeStruct((2 * H_C, n), F32)]
        return pl.pallas_call(
            functools.partial(_odd_in_kernel, decode=True),
            grid=(1,),
            in_specs=[pl.BlockSpec(a.shape, lambda i: (0, 0)) for a in ins],
            out_specs=[pl.BlockSpec(o.shape, lambda i: (0, 0)) for o in outs],
            out_shape=outs,
            compiler_params=_cparams("arbitrary"),
            name="odd_in_decode",
        )(*ins)
    b, t, _ = x.shape
    tm = ROW_TILE
    row3 = lambda n: pl.BlockSpec((1, tm, n), lambda i, j: (i, j, 0))
    full2 = lambda a: pl.BlockSpec(a.shape, lambda i, j: (0, 0))
    outs = [jax.ShapeDtypeStruct((b, t, qw), BF16), jax.ShapeDtypeStruct((b, t, qw), BF16),
            jax.ShapeDtypeStruct((b, t, vw), BF16), jax.ShapeDtypeStruct((b, t, vw), F32),
            jax.ShapeDtypeStruct((b, t, LANES), F32), jax.ShapeDtypeStruct((b, 2 * H_C, t), F32)]
    return pl.pallas_call(
        functools.partial(_odd_in_kernel, decode=False),
        grid=(b, t // tm),
        in_specs=[row3(D_MODEL), full2(w_bf), full2(wg), full2(wgt), full2(bg), full2(bgt)],
        out_specs=[row3(qw), row3(qw), row3(vw), row3(vw), row3(LANES),
                   pl.BlockSpec((1, 2 * H_C, tm), lambda i, j: (i, 0, j))],
        out_shape=outs,
        compiler_params=_cparams("arbitrary", "arbitrary"),
        name="odd_in_prompt",
    )(x, w_bf, wg, wgt, bg, bgt)


def _mlstm_kernel(q_ref, k_ref, v_ref, o_ref, gc_ref, gr_ref, nw_ref, c0_ref, n0_ref, m0_ref,
                  h_ref, c_out, n_out, m_out, c_scr, n_scr, m_scr):
    ci = pl.program_id(1)
    chunk = q_ref.shape[1]

    @pl.when(ci == 0)
    def _():
        c_scr[...] = c0_ref[0]
        n_scr[...] = n0_ref[0]
        m_scr[...] = m0_ref[0]

    t_idx = lax.broadcasted_iota(I32, (chunk, chunk), 0)
    s_idx = lax.broadcasted_iota(I32, (chunk, chunk), 1)
    causal = s_idx <= t_idx
    for h in range(H_C):
        q = q_ref[0, :, h * DK_C:(h + 1) * DK_C]
        k = k_ref[0, :, h * DK_C:(h + 1) * DK_C]
        v = v_ref[0, :, h * DV_C:(h + 1) * DV_C]
        ig_r = gr_ref[0, h:h + 1, :]
        lf_r = gr_ref[0, H_C + h:H_C + h + 1, :]
        ig_c = gc_ref[0, :, h:h + 1]
        lf_c = gc_ref[0, :, H_C + h:H_C + h + 1]
        bcum_c = jnp.sum(jnp.where(causal, lf_r, 0.0), axis=1, keepdims=True)
        bcum_r = jnp.sum(jnp.where(t_idx <= s_idx, lf_c, 0.0), axis=0, keepdims=True)
        m0 = m_scr[h:h + 1, 0:1]
        dmat = jnp.where(causal, bcum_c - bcum_r + ig_r, -jnp.inf)
        inter = bcum_c + m0
        m = jnp.maximum(inter, jnp.max(dmat, axis=1, keepdims=True))
        w = jnp.exp(dmat - m)
        g = jnp.exp(inter - m)
        s = _dot_nt(q, k) * w
        c0 = c_scr[h]
        n0 = n_scr[h:h + 1, :]
        num = g * _dot(q, c0.astype(BF16)) + _dot(s.astype(BF16), v)
        den = g * jnp.sum(q.astype(F32) * n0, axis=1, keepdims=True) + jnp.sum(s, axis=1, keepdims=True)
        hid = num / jnp.maximum(jnp.abs(den), jnp.exp(-m))
        m_last = m[chunk - 1:chunk, :]
        b_last = bcum_c[chunk - 1:chunk, :]
        w_last = jnp.exp(b_last - bcum_c + ig_c - m_last)
        g_last = jnp.exp(b_last + m0 - m_last)
        kw = k.astype(F32) * w_last
        c_scr[h] = g_last * c0 + _dot_tn(kw.astype(BF16), v)
        n_scr[h:h + 1, :] = g_last * n0 + jnp.sum(kw, axis=0, keepdims=True)
        m_scr[h:h + 1, :] = jnp.broadcast_to(m_last, (1, LANES))
        mu = jnp.mean(hid, axis=1, keepdims=True)
        hc = hid - mu
        var = jnp.mean(hc * hc, axis=1, keepdims=True)
        hn = hc * lax.rsqrt(var + LN_EPS) * nw_ref[:, h * DV_C:(h + 1) * DV_C]
        gate = jax.nn.sigmoid(o_ref[0, :, h * DV_C:(h + 1) * DV_C])
        h_ref[0, :, h * DV_C:(h + 1) * DV_C] = (gate * hn).astype(BF16)

    c_out[0] = c_scr[...]
    n_out[0] = n_scr[...]
    m_out[0] = m_scr[...]


def _mlstm(q, k, v, o, g_col, g_row, norm_w, c0, n0, m0):
    b, t, _ = q.shape
    ch = MLSTM_CHUNK
    qw, vw = H_C * DK_C, H_C * DV_C
    row = lambda n: pl.BlockSpec((1, ch, n), lambda i, j: (i, j, 0))
    st4 = pl.BlockSpec((1, H_C, DK_C, DV_C), lambda i, j: (i, 0, 0, 0))
    st3 = pl.BlockSpec((1, H_C, LANES), lambda i, j: (i, 0, 0))
    outs = [jax.ShapeDtypeStruct((b, t, vw), BF16), jax.ShapeDtypeStruct((b, H_C, DK_C, DV_C), F32),
            jax.ShapeDtypeStruct((b, H_C, DK_C), F32), jax.ShapeDtypeStruct((b, H_C, LANES), F32)]
    return pl.pallas_call(
        _mlstm_kernel,
        grid=(b, t // ch),
        in_specs=[row(qw), row(qw), row(vw), row(vw), row(LANES),
                  pl.BlockSpec((1, 2 * H_C, ch), lambda i, j: (i, 0, j)),
                  pl.BlockSpec(norm_w.shape, lambda i, j: (0, 0)), st4, st3, st3],
        out_specs=[row(vw), st4, st3, st3],
        out_shape=outs,
        scratch_shapes=[pltpu.VMEM((H_C, DK_C, DV_C), F32), pltpu.VMEM((H_C, DK_C), F32),
                        pltpu.VMEM((H_C, LANES), F32)],
        compiler_params=_cparams("arbitrary", "arbitrary"),
        name="mlstm",
    )(q, k, v, o, g_col, g_row, norm_w, c0, n0, m0)


def _mix_out_kernel(*refs, n_in):
    a_refs = refs[:n_in]
    w_refs = refs[n_in:2 * n_in]
    x_ref, g_ref, b_ref, rhi_ref, rlo_ref, x1_in, rt_in, x1_ref, rt_ref = refs[2 * n_in:]
    del x1_in, rt_in
    y = _dot(a_refs[0][...], w_refs[0][...])
    for i in range(1, n_in):
        y = y + _dot(a_refs[i][...], w_refs[i][...])
    x1 = _layer_norm_rows(DEEPNORM_ALPHA * x_ref[...] + y, g_ref[...], b_ref[...])
    x1_ref[...] = x1

    xh = x1.astype(BF16)
    xl = (x1 - xh.astype(F32)).astype(BF16)
    lg = _dot(xh, rhi_ref[...]) + _dot(xl, rhi_ref[...]) + _dot(xh, rlo_ref[...])
    lane = lax.broadcasted_iota(I32, lg.shape, 1)
    big = jnp.int32(4 * LANES)
    neg = -jnp.inf
    gl = jnp.where(lane < ROUTE_GROUP_LANE + N_GROUPS, lg, neg)
    g_max = jnp.max(gl, axis=1, keepdims=True)
    g_w = 1.0 / jnp.sum(jnp.exp(gl - g_max), axis=1, keepdims=True)
    g_idx = jnp.min(jnp.where(gl == g_max, lane, big), axis=1, keepdims=True)
    lane_group = (lane - ROUTE_EXPERT_LANE) >> 2
    el = jnp.where(lane_group == g_idx, lg, neg)
    e1 = jnp.max(el, axis=1, keepdims=True)
    i1 = jnp.min(jnp.where(el == e1, lane, big), axis=1, keepdims=True)
    z = jnp.sum(jnp.exp(el - e1), axis=1, keepdims=True)
    el2 = jnp.where(lane == i1, neg, el)
    e2 = jnp.max(el2, axis=1, keepdims=True)
    i2 = jnp.min(jnp.where(el2 == e2, lane, big), axis=1, keepdims=True)
    p1 = 1.0 / z
    p2 = jnp.exp(e2 - e1) / z
    w1 = p1 / (p1 + p2) * g_w
    w2 = p2 / (p1 + p2) * g_w
    id1 = (i1 - ROUTE_EXPERT_LANE).astype(F32)
    id2 = (i2 - ROUTE_EXPERT_LANE).astype(F32)
    rt_ref[...] = jnp.where(lane == 0, w1, jnp.where(lane == 1, w2,
                            jnp.where(lane == 2, id1, jnp.where(lane == 3, id2, 0.0))))


def _mix_out(acts, weights, x, ln_g, ln_b, r_hi, r_lo, x1_all, rt_all, row0, tm):
    rows = x.shape[0]
    blk0 = row0 // tm
    n_in = len(acts)
    row = lambda n: pl.BlockSpec((tm, n), lambda i: (i, 0))
    full = lambda a: pl.BlockSpec(a.shape, lambda i: (0, 0))
    anyspec = pl.BlockSpec(memory_space=pl.ANY)
    ins = [*acts, *weights, x, ln_g, ln_b, r_hi, r_lo, x1_all, rt_all]
    n_fixed = len(ins) - 2
    return pl.pallas_call(
        functools.partial(_mix_out_kernel, n_in=n_in),
        grid=(rows // tm,),
        in_specs=[row(a.shape[1]) for a in acts] + [full(w) for w in weights]
        + [row(D_MODEL), full(ln_g), full(ln_b), full(r_hi), full(r_lo), anyspec, anyspec],
        out_specs=[pl.BlockSpec((tm, D_MODEL), lambda i: (i + blk0, 0)),
                   pl.BlockSpec((tm, LANES), lambda i: (i + blk0, 0))],
        out_shape=[jax.ShapeDtypeStruct(x1_all.shape, F32), jax.ShapeDtypeStruct(rt_all.shape, F32)],
        input_output_aliases={n_fixed: 0, n_fixed + 1: 1},
        compiler_params=_cparams("arbitrary"),
        name="mix_out",
    )(*ins)


def _moe_kernel(ta_ref, tb_ref, nu_ref, src_ref, dst_ref, x_hbm, wa_ref, wb_ref,
                ga_ref, ua_ref, da_ref, gb_ref, ub_ref, db_ref, lg_ref, lb_ref,
                out_hbm, xbuf, obuf, sem_in, sem_out):
    del ta_ref, tb_ref
    g = pl.program_id(0)
    tm = xbuf.shape[0]

    def gather_copy(r):
        return pltpu.make_async_copy(x_hbm.at[pl.ds(src_ref[0, 0, r], 1)], xbuf.at[pl.ds(r, 1)], sem_in)

    def scatter_copy(r):
        return pltpu.make_async_copy(obuf.at[pl.ds(r, 1)], out_hbm.at[pl.ds(dst_ref[0, 0, r], 1)], sem_out)

    @pl.when(g < nu_ref[0])
    def _():
        def start_in(r, c):
            gather_copy(r).start()
            return c

        def wait_in(r, c):
            gather_copy(r).wait()
            return c

        lax.fori_loop(0, tm, start_in, 0, unroll=8)
        lax.fori_loop(0, tm, wait_in, 0, unroll=8)
        x = xbuf[...]
        xb = x.astype(BF16)

        def expert(gw, uw, dw):
            hid = jax.nn.silu(_dot(xb, gw[0])) * _dot(xb, uw[0])
            return _dot(hid.astype(BF16), dw[0])

        y = wa_ref[...] * expert(ga_ref, ua_ref, da_ref)
        y = y + wb_ref[...] * expert(gb_ref, ub_ref, db_ref)
        obuf[...] = _layer_norm_rows(DEEPNORM_ALPHA * x + y, lg_ref[...], lb_ref[...])

    @pl.when(g >= nu_ref[0])
    def _():
        obuf[...] = jnp.zeros(obuf.shape, F32)

    def start_out(r, c):
        scatter_copy(r).start()
        return c

    def wait_out(r, c):
        scatter_copy(r).wait()
        return c

    lax.fori_loop(0, tm, start_out, 0, unroll=8)
    lax.fori_loop(0, tm, wait_out, 0, unroll=8)


def _route_plan(rt, tm):
    n = rt.shape[0]
    w1, w2 = rt[:, 0], rt[:, 1]
    e1, e2 = rt[:, 2].astype(I32), rt[:, 3].astype(I32)
    first = e1 < e2
    ea, eb = jnp.where(first, e1, e2), jnp.where(first, e2, e1)
    wa, wb = jnp.where(first, w1, w2), jnp.where(first, w2, w1)
    la, lb = ea % EXP_PER_GROUP, eb % EXP_PER_GROUP
    cls = (ea // EXP_PER_GROUP) * N_PAIRS + (la * (2 * EXP_PER_GROUP - 1 - la)) // 2 + (lb - la - 1)
    onehot = (cls[:, None] == jnp.arange(N_CLASSES, dtype=I32)[None, :]).astype(I32)
    csum = jnp.cumsum(onehot, axis=0)
    rank = jnp.sum(onehot * csum, axis=1) - 1
    cnt = csum[-1]
    ntile = (cnt + tm - 1) // tm
    tile_end = jnp.cumsum(ntile)
    tile_start = tile_end - ntile
    n_used = tile_end[-1]
    n_tiles = -(-(n + N_CLASSES * (tm - 1)) // tm)
    total = n_tiles * tm
    pos = tile_start[cls] * tm + rank
    tok = jnp.arange(n, dtype=I32)
    src = jnp.zeros((total,), I32).at[pos].set(tok)
    valid = jnp.zeros((total,), I32).at[pos].set(1)
    spare = n + jnp.cumsum(1 - valid) - 1
    dst = jnp.where(valid == 1, src, spare).astype(I32)
    wa_s = jnp.zeros((total,), F32).at[pos].set(wa)
    wb_s = jnp.zeros((total,), F32).at[pos].set(wb)
    pair_lo = np.array([a for a in range(EXP_PER_GROUP) for b in range(a + 1, EXP_PER_GROUP)], np.int32)
    pair_hi = np.array([b for a in range(EXP_PER_GROUP) for b in range(a + 1, EXP_PER_GROUP)], np.int32)
    cls_ids = np.arange(N_CLASSES)
    cls_a = jnp.asarray((cls_ids // N_PAIRS) * EXP_PER_GROUP + pair_lo[cls_ids % N_PAIRS], I32)
    cls_b = jnp.asarray((cls_ids // N_PAIRS) * EXP_PER_GROUP + pair_hi[cls_ids % N_PAIRS], I32)
    tile_ids = jnp.arange(n_tiles, dtype=I32)
    tile_cls = jnp.sum((tile_end[None, :] <= jnp.minimum(tile_ids, n_used - 1)[:, None]).astype(I32), axis=1)
    tile_cls = jnp.minimum(tile_cls, N_CLASSES - 1)
    return dict(src=src.reshape(n_tiles, 1, tm), dst=dst.reshape(n_tiles, 1, tm),
                wa=wa_s.reshape(total, 1), wb=wb_s.reshape(total, 1),
                ta=cls_a[tile_cls], tb=cls_b[tile_cls], nu=n_used.reshape(1).astype(I32),
                n_tiles=n_tiles, total=total)


def _moe(x1_all, plan, wg_bf, wu_bf, wd_bf, ln_g, ln_b, tm):
    n_tiles, total = plan["n_tiles"], plan["total"]
    idx = pl.BlockSpec((1, 1, tm), lambda g, ta, tb, nu: (g, 0, 0), memory_space=pltpu.SMEM)
    wrow = pl.BlockSpec((tm, 1), lambda g, ta, tb, nu: (g, 0))
    anyspec = pl.BlockSpec(memory_space=pl.ANY)
    up_a = pl.BlockSpec((1, D_MODEL, D_EXPERT), lambda g, ta, tb, nu: (ta[g], 0, 0))
    dn_a = pl.BlockSpec((1, D_EXPERT, D_MODEL), lambda g, ta, tb, nu: (ta[g], 0, 0))
    up_b = pl.BlockSpec((1, D_MODEL, D_EXPERT), lambda g, ta, tb, nu: (tb[g], 0, 0))
    dn_b = pl.BlockSpec((1, D_EXPERT, D_MODEL), lambda g, ta, tb, nu: (tb[g], 0, 0))
    vec = pl.BlockSpec((1, D_MODEL), lambda g, ta, tb, nu: (0, 0))
    grid_spec = pltpu.PrefetchScalarGridSpec(
        num_scalar_prefetch=3,
        grid=(n_tiles,),
        in_specs=[idx, idx, anyspec, wrow, wrow, up_a, up_a, dn_a, up_b, up_b, dn_b, vec, vec],
        out_specs=anyspec,
        scratch_shapes=[pltpu.VMEM((tm, D_MODEL), F32), pltpu.VMEM((tm, D_MODEL), F32),
                        pltpu.SemaphoreType.DMA(()), pltpu.SemaphoreType.DMA(())],
    )
    return pl.pallas_call(
        _moe_kernel,
        grid_spec=grid_spec,
        out_shape=jax.ShapeDtypeStruct((total, D_MODEL), F32),
        compiler_params=_cparams("arbitrary"),
        name="moe",
    )(plan["ta"], plan["tb"], plan["nu"], plan["src"], plan["dst"], x1_all, plan["wa"], plan["wb"],
      wg_bf, wu_bf, wd_bf, wg_bf, wu_bf, wd_bf, ln_g, ln_b)


def _ple_kernel(x_ref, p_ref, wg_ref, wp_ref, o_ref):
    x = x_ref[...]
    gate = jax.nn.sigmoid(_dot(x.astype(BF16), wg_ref[...]))
    o_ref[...] = x + gate * _dot(p_ref[...].astype(BF16), wp_ref[...])


def _ple(x2_all, p, wg_bf, wp_bf, row0, tm):
    rows = p.shape[0]
    blk0 = row0 // tm
    full = lambda a: pl.BlockSpec(a.shape, lambda i: (0, 0))
    return pl.pallas_call(
        _ple_kernel,
        grid=(rows // tm,),
        in_specs=[pl.BlockSpec((tm, D_MODEL), lambda i: (i + blk0, 0)),
                  pl.BlockSpec((tm, D_PLE), lambda i: (i, 0)), full(wg_bf), full(wp_bf)],
        out_specs=pl.BlockSpec((tm, D_MODEL), lambda i: (i, 0)),
        out_shape=jax.ShapeDtypeStruct((rows, D_MODEL), F32),
        compiler_params=_cparams("arbitrary"),
        name="ple",
    )(x2_all, p, wg_bf, wp_bf)


def _router_weights(w_group, w_router):
    wr = jnp.zeros((D_MODEL, LANES), F32)
    wr = wr.at[:, ROUTE_GROUP_LANE:ROUTE_GROUP_LANE + N_GROUPS].set(w_group)
    wr = wr.at[:, ROUTE_EXPERT_LANE:ROUTE_EXPERT_LANE + N_EXPERTS].set(w_router)
    hi = wr.astype(BF16)
    lo = (wr - hi.astype(F32)).astype(BF16)
    return hi, lo


def _layer_tail(i, acts_p, acts_s, w_list, xp, xs, p_p, p_s, ln_mix_g, ln_mix_b, ln_ffn_g, ln_ffn_b,
                w_group, w_router, w_exp_gate, w_exp_up, w_exp_down, w_ple_proj, w_ple_gate):
    n_p, n_s = xp.shape[0], xs.shape[0]
    n_all = n_p + n_s
    r_hi, r_lo = _router_weights(w_group[i], w_router[i])
    g1, b1 = ln_mix_g[i][None, :], ln_mix_b[i][None, :]
    x1_all = jnp.zeros((n_all, D_MODEL), F32)
    rt_all = jnp.zeros((n_all, LANES), F32)
    x1_all, rt_all = _mix_out(acts_p, w_list, xp, g1, b1, r_hi, r_lo, x1_all, rt_all, 0, ROW_TILE)
    x1_all, rt_all = _mix_out(acts_s, w_list, xs, g1, b1, r_hi, r_lo, x1_all, rt_all, n_p, n_s)
    plan = _route_plan(rt_all, MOE_TILE)
    x2_all = _moe(x1_all, plan, w_exp_gate[i].astype(BF16), w_exp_up[i].astype(BF16),
                  w_exp_down[i].astype(BF16), ln_ffn_g[i][None, :], ln_ffn_b[i][None, :], MOE_TILE)
    wg_bf, wp_bf = w_ple_gate[i].astype(BF16), w_ple_proj[i].astype(BF16)
    xp_new = _ple(x2_all, p_p, wg_bf, wp_bf, 0, ROW_TILE)
    xs_new = _ple(x2_all, p_s, wg_bf, wp_bf, n_p, n_s)
    return xp_new, xs_new


def kernel(x_prompt, x_sample, cache_k, cache_v, page_table, state_conv, state_mlstm_C, state_mlstm_n,
           state_mlstm_m, p_prompt, p_sample, w_in_even, conv_w, lambda_q1, lambda_k1, lambda_q2, lambda_k2,
           subln_w, w_out_even, w_in_odd, b_gates_odd, mh_norm_w, w_out_odd, ln_mix_g, ln_mix_b, ln_ffn_g,
           ln_ffn_b, w_group, w_router, w_exp_gate, w_exp_up, w_exp_down, w_ple_proj, w_ple_gate):
    bp, tp, _ = x_prompt.shape
    bs, ts, _ = x_sample.shape
    assert ts == 1 and tp % ROW_TILE == 0 and tp % ATTN_TILE == 0 and tp % MLSTM_CHUNK == 0
    n_p = bp * tp
    past_len = page_table.shape[1] * cache_k.shape[2]
    xp = x_prompt.reshape(n_p, D_MODEL)
    xs = x_sample.reshape(bs, D_MODEL)
    tail_w = (ln_mix_g, ln_mix_b, ln_ffn_g, ln_ffn_b, w_group, w_router, w_exp_gate, w_exp_up, w_exp_down,
              w_ple_proj, w_ple_gate)
    outs_p, outs_s = {}, {}
    for i in range(DEPTH):
        j = i // 2
        p_p = p_prompt[i].reshape(n_p, D_PLE)
        p_s = p_sample[i].reshape(bs, D_PLE)
        if i % 2 == 0:
            lam_init = 0.8 - 0.6 * math.exp(-0.3 * i)
            lam_vecs = jnp.stack([lambda_q1[j], lambda_k1[j], lambda_q2[j], lambda_k2[j]])
            sub = subln_w[j][None, :]
            w_bf = w_in_even[j].astype(BF16)
            tabs_p = _rope_tables(jnp.arange(tp))
            yc, q, kf, vf, kb, vb, cst = _even_in_prompt(
                x_prompt if i == 0 else xp.reshape(bp, tp, D_MODEL), w_bf, conv_w[j],
                jnp.zeros((bp, CONV_W - 1, D_CONV), F32), tabs_p)
            o_p = _attn_prompt(q, kb, vb, lam_vecs, sub, lam_init)
            outs_p.setdefault("k", []).append(kf.reshape(bp, tp, 2 * H_B, DH_B))
            outs_p.setdefault("v", []).append(vf.reshape(bp, tp, H_B, 2 * DH_B))
            outs_p.setdefault("c", []).append(cst)
            tabs_s = _rope_tables(jnp.full((1,), past_len, I32))
            prev_t = jnp.swapaxes(state_conv[j], 0, 1)
            yc_s, q_s, kf_s, vf_s, _, _, u_s = _even_in_decode(xs, w_bf, conv_w[j], prev_t, tabs_s)
            sub_head = jnp.arange(2 * H_B)
            sub_head = jnp.where(sub_head < H_B, 2 * sub_head, 2 * (sub_head - H_B) + 1)
            lane_head = jnp.arange(QK_B) // DH_B
            qbd = jnp.where(lane_head[None, None, :] == sub_head[None, :, None], q_s[:, None, :],
                            jnp.zeros((), BF16))
            n_pool = cache_k.shape[1]
            o8 = _attn_decode(qbd, kf_s[:, None, :], vf_s[:, None, :],
                              cache_k[j].reshape(n_pool, PAGE_SIZE, QK_B),
                              cache_v[j].reshape(n_pool, PAGE_SIZE, V_B), page_table, lam_vecs, sub, lam_init)
            o_s = o8[:, :H_B, :].reshape(bs, V_B).astype(BF16)
            outs_s.setdefault("k", []).append(kf_s.reshape(bs, ts, 2 * H_B, DH_B))
            outs_s.setdefault("v", []).append(vf_s.reshape(bs, ts, H_B, 2 * DH_B))
            outs_s.setdefault("c", []).append(jnp.stack([state_conv[j][:, 1, :], u_s], axis=1))
            w_out = w_out_even[j].astype(BF16)
            w_list = [w_out[:D_CONV], w_out[D_CONV:]]
            acts_p = [yc.reshape(n_p, D_CONV), o_p.reshape(n_p, V_B)]
            acts_s = [yc_s, o_s]
        else:
            w_in = w_in_odd[j]
            qw, vw = H_C * DK_C, H_C * DV_C
            w_bf = w_in[:, :2 * qw + 2 * vw].astype(BF16)
            wg = jnp.zeros((D_MODEL, LANES), F32).at[:, :2 * H_C].set(w_in[:, 2 * qw + 2 * vw:]).astype(BF16)
            wgt = jnp.transpose(wg[:, :2 * H_C])
            bg = jnp.zeros((1, LANES), F32).at[0, :2 * H_C].set(b_gates_odd[j])
            bgt = jnp.broadcast_to(b_gates_odd[j][:, None], (2 * H_C, LANES))
            nw = mh_norm_w[j][None, :]
            q, k, v, o, gc, gr = _odd_in(xp.reshape(bp, tp, D_MODEL), w_bf, wg, wgt, bg, bgt, decode=False)
            h_p, c_p, n_pp, m_p = _mlstm(q, k, v, o, gc, gr, nw,
                                         jnp.zeros((bp, H_C, DK_C, DV_C), F32), jnp.zeros((bp, H_C, DK_C), F32),
                                         jnp.zeros((bp, H_C, LANES), F32))
            outs_p.setdefault("C", []).append(c_p)
            outs_p.setdefault("n", []).append(n_pp)
            outs_p.setdefault("m", []).append(m_p[:, :, 0])
            q_s, k_s, v_s, o_s2, gc_s, gr_s = _odd_in(xs, w_bf, wg, wgt, bg, bgt, decode=True)
            ch = MLSTM_CHUNK
            pad_rows = lambda a: jnp.zeros((bs, ch, a.shape[1]), a.dtype).at[:, 0, :].set(a)
            lane = jnp.arange(LANES)
            inert_c = jnp.where(lane < H_C, -jnp.inf, 0.0).astype(F32)
            gc_pad = jnp.broadcast_to(inert_c[None, None, :], (bs, ch, LANES)).at[:, 0, :].set(gc_s)
            inert_r = jnp.where(jnp.arange(2 * H_C) < H_C, -jnp.inf, 0.0).astype(F32)
            gr_pad = jnp.broadcast_to(inert_r[None, :, None], (bs, 2 * H_C, ch)).at[:, :, 0].set(gr_s.T)
            m0 = jnp.broadcast_to(state_mlstm_m[j][:, :, None], (bs, H_C, LANES))
            h_s, c_s, n_s, m_s = _mlstm(pad_rows(q_s), pad_rows(k_s), pad_rows(v_s), pad_rows(o_s2), gc_pad,
                                        gr_pad, nw, state_mlstm_C[j], state_mlstm_n[j], m0)
            outs_s.setdefault("C", []).append(c_s)
            outs_s.setdefault("n", []).append(n_s)
            outs_s.setdefault("m", []).append(m_s[:, :, 0])
            w_list = [w_out_odd[j].astype(BF16)]
            acts_p = [h_p.reshape(n_p, vw)]
            acts_s = [h_s[:, 0, :]]
        xp, xs = _layer_tail(i, acts_p, acts_s, w_list, xp, xs, p_p, p_s, *tail_w)
    st = lambda lst: jnp.stack(lst)
    return (xp.reshape(bp, tp, D_MODEL), xs.reshape(bs, ts, D_MODEL),
            st(outs_p["k"]), st(outs_p["v"]), st(outs_p["c"]), st(outs_p["C"]), st(outs_p["n"]), st(outs_p["m"]),
            st(outs_s["k"]), st(outs_s["v"]), st(outs_s["c"]), st(outs_s["C"]), st(outs_s["n"]), st(outs_s["m"]))
```

```python
import functools
import math

import numpy as np
import jax
import jax.numpy as jnp
from jax import lax
from jax.experimental import pallas as pl
from jax.experimental.pallas import tpu as pltpu

F32 = jnp.float32
BF16 = jnp.bfloat16
I32 = jnp.int32

D_MODEL = 1024
DEPTH = 2
PAGE_SIZE = 128
D_CONV = D_MODEL // 2
CONV_W = 3
H_B = 4
DH_B = 64
ROT_DIM = DH_B // 4
ROPE_THETA = 500000.0
H_C = 4
DK_C = (D_MODEL // 2) // H_C
DV_C = D_MODEL // H_C
N_GROUPS = 4
EXP_PER_GROUP = 4
N_EXPERTS = N_GROUPS * EXP_PER_GROUP
D_EXPERT = 512
D_PLE = 256
LN_EPS = 1e-5
DEEPNORM_ALPHA = (2 * DEPTH) ** 0.25
QK_B = 2 * H_B * DH_B
V_B = H_B * 2 * DH_B
N_PAIRS = EXP_PER_GROUP * (EXP_PER_GROUP - 1) // 2
N_CLASSES = N_GROUPS * N_PAIRS

LANES = 128
VMEM_LIMIT = 56 * 1024 * 1024
ROW_TILE = 512
ATTN_TILE = 512
MLSTM_CHUNK = 128
MOE_TILE = 256
PAGES_PER_STEP = 8
ROUTE_GROUP_LANE = 0
ROUTE_EXPERT_LANE = 16


def _cparams(*sem):
    return pltpu.CompilerParams(dimension_semantics=sem, vmem_limit_bytes=VMEM_LIMIT)


def _dot(a, b):
    return jnp.dot(a, b, preferred_element_type=F32)


def _dot_nt(a, b):
    return lax.dot_general(a, b, (((1,), (1,)), ((), ())), preferred_element_type=F32)


def _dot_tn(a, b):
    return lax.dot_general(a, b, (((0,), (0,)), ((), ())), preferred_element_type=F32)


def _layer_norm_rows(z, g, b):
    mu = jnp.mean(z, axis=-1, keepdims=True)
    zc = z - mu
    var = jnp.mean(zc * zc, axis=-1, keepdims=True)
    return zc * lax.rsqrt(var + LN_EPS) * g + b


def _log_sigmoid(x):
    return jnp.minimum(x, 0.0) - jnp.log1p(jnp.exp(-jnp.abs(x)))


def _gated_conv(gate_b, u, um1, um2, cw_ref):
    cw = cw_ref[...]
    conv = um2 * cw[0:1, :] + um1 * cw[1:2, :] + u * cw[2:3, :]
    return (gate_b * conv).astype(BF16)


def _rope(z, cos, sin_lo, sin_hi, axis):
    half = ROT_DIM // 2
    return z * cos + pltpu.roll(z, QK_B - half, axis) * sin_lo + pltpu.roll(z, half, axis) * sin_hi


def _even_in_prompt_kernel(x_ref, w_ref, wqt_ref, wvt_ref, cw_ref, prev_ref, rc_ref, rs1_ref, rs2_ref,
                           rct_ref, rs1t_ref, rs2t_ref,
                           yc_ref, qt_ref, kf_ref, vf_ref, kb_ref, vt_ref, u_ref, carry_ref):
    xb = x_ref[0].astype(BF16)
    tm = xb.shape[0]

    def proj(c0, n):
        return _dot(xb, w_ref[:, c0:c0 + n])

    gate_b = proj(0, D_CONV)
    u = proj(D_CONV, D_CONV) * proj(2 * D_CONV, D_CONV)
    j = pl.program_id(1)

    @pl.when(j == 0)
    def _():
        carry_ref[...] = prev_ref[0]

    row = lax.broadcasted_iota(I32, u.shape, 0)
    c2 = carry_ref[0:1, :]
    c1 = carry_ref[1:2, :]
    um1 = jnp.where(row == 0, c1, pltpu.roll(u, 1, 0))
    um2 = jnp.where(row == 0, c2, jnp.where(row == 1, c1, pltpu.roll(u, 2, 0)))
    carry_ref[...] = u[tm - 2:tm, :]
    u_ref[0] = u[tm - 2:tm, :]
    yc_ref[0] = _gated_conv(gate_b, u, um1, um2, cw_ref)

    reps = QK_B // LANES
    tile = lambda r, ax: jnp.concatenate([r[...]] * reps, axis=ax)
    k = _rope(proj(3 * D_CONV + QK_B, QK_B), tile(rc_ref, 1), tile(rs1_ref, 1), tile(rs2_ref, 1), 1)
    kf_ref[0] = k
    kb_ref[0] = k.astype(BF16)
    v = proj(3 * D_CONV + 2 * QK_B, V_B)
    vf_ref[0] = v
    qt = _rope(_dot_nt(wqt_ref[...], xb), tile(rct_ref, 0), tile(rs1t_ref, 0), tile(rs2t_ref, 0), 0)
    qt_ref[0] = (qt * (DH_B ** -0.5)).astype(BF16)
    vt_ref[0] = _dot_nt(wvt_ref[...], xb).astype(BF16)


def _even_in_decode_kernel(x_ref, w_ref, cw_ref, prev_ref, rc_ref, rs1_ref, rs2_ref,
                           yc_ref, q_ref, kf_ref, vf_ref, u_ref):
    xb = x_ref[...].astype(BF16)

    def proj(c0, n):
        return _dot(xb, w_ref[:, c0:c0 + n])

    gate_b = proj(0, D_CONV)
    u = proj(D_CONV, D_CONV) * proj(2 * D_CONV, D_CONV)
    u_ref[...] = u
    yc_ref[...] = _gated_conv(gate_b, u, prev_ref[1], prev_ref[0], cw_ref)
    reps = QK_B // LANES
    tile = lambda r: jnp.concatenate([r[...]] * reps, axis=1)
    cos, sin_lo, sin_hi = tile(rc_ref), tile(rs1_ref), tile(rs2_ref)
    q_ref[...] = (_rope(proj(3 * D_CONV, QK_B), cos, sin_lo, sin_hi, 1) * (DH_B ** -0.5)).astype(BF16)
    kf_ref[...] = _rope(proj(3 * D_CONV + QK_B, QK_B), cos, sin_lo, sin_hi, 1)
    vf_ref[...] = proj(3 * D_CONV + 2 * QK_B, V_B)


def _rope_tables(pos):
    half = ROT_DIM // 2
    inv = ROPE_THETA ** (-jnp.arange(half, dtype=F32) / half)
    ang = pos.astype(F32)[:, None] * inv[None, :]
    cos, sin = jnp.cos(ang), jnp.sin(ang)
    t = pos.shape[0]
    ones = jnp.ones((t, DH_B - ROT_DIM), F32)
    zeros = jnp.zeros((t, DH_B - ROT_DIM), F32)
    zh = jnp.zeros((t, half), F32)
    c = jnp.concatenate([cos, cos, ones], axis=1)
    s_lo = jnp.concatenate([-sin, zh, zeros], axis=1)
    s_hi = jnp.concatenate([zh, sin, zeros], axis=1)
    tile2 = lambda a: jnp.concatenate([a, a], axis=1)
    return tile2(c), tile2(s_lo), tile2(s_hi)


def _even_in_prompt(x, w_bf, wqt_bf, wvt_bf, conv_w, conv_prev, tables):
    b, t, _ = x.shape
    tm = ROW_TILE
    row3 = lambda n: pl.BlockSpec((1, tm, n), lambda i, j: (i, j, 0))
    col3 = lambda n: pl.BlockSpec((1, n, tm), lambda i, j: (i, 0, j))
    full2 = lambda a: pl.BlockSpec(a.shape, lambda i, j: (0, 0))
    tab = pl.BlockSpec((tm, LANES), lambda i, j: (j, 0))
    tab_t = pl.BlockSpec((LANES, tm), lambda i, j: (0, j))
    st = pl.BlockSpec((1, CONV_W - 1, D_CONV), lambda i, j: (i, 0, 0))
    tables_t = [jnp.transpose(a) for a in tables]
    outs = [jax.ShapeDtypeStruct((b, t, D_CONV), BF16), jax.ShapeDtypeStruct((b, QK_B, t), BF16),
            jax.ShapeDtypeStruct((b, t, QK_B), F32), jax.ShapeDtypeStruct((b, t, V_B), F32),
            jax.ShapeDtypeStruct((b, t, QK_B), BF16), jax.ShapeDtypeStruct((b, V_B, t), BF16),
            jax.ShapeDtypeStruct((b, CONV_W - 1, D_CONV), F32)]
    return pl.pallas_call(
        _even_in_prompt_kernel,
        grid=(b, t // tm),
        in_specs=[row3(D_MODEL), full2(w_bf), full2(wqt_bf), full2(wvt_bf), full2(conv_w), st,
                  tab, tab, tab, tab_t, tab_t, tab_t],
        out_specs=[row3(D_CONV), col3(QK_B), row3(QK_B), row3(V_B), row3(QK_B), col3(V_B), st],
        out_shape=outs,
        scratch_shapes=[pltpu.VMEM((CONV_W - 1, D_CONV), F32)],
        compiler_params=_cparams("arbitrary", "arbitrary"),
        name="even_in_prompt",
    )(x, w_bf, wqt_bf, wvt_bf, conv_w, conv_prev, *tables, *tables_t)


def _even_in_decode(x, w_bf, conv_w, conv_prev_t, tables):
    n = x.shape[0]
    full = lambda a: pl.BlockSpec(a.shape, lambda i: (0,) * a.ndim)
    o2 = lambda c, dt: jax.ShapeDtypeStruct((n, c), dt)
    outs = [o2(D_CONV, BF16), o2(QK_B, BF16), o2(QK_B, F32), o2(V_B, F32), o2(D_CONV, F32)]
    ins = [x, w_bf, conv_w, conv_prev_t, *tables]
    return pl.pallas_call(
        _even_in_decode_kernel,
        grid=(1,),
        in_specs=[full(a) for a in ins],
        out_specs=[pl.BlockSpec(o.shape, lambda i: (0, 0)) for o in outs],
        out_shape=outs,
        compiler_params=_cparams("arbitrary"),
        name="even_in_decode",
    )(*ins)


def _lambda_value(lam_ref, lam_init):
    lv = lam_ref[...]
    a = jnp.sum(lv[0:1, :] * lv[1:2, :], axis=1, keepdims=True)
    b = jnp.sum(lv[2:3, :] * lv[3:4, :], axis=1, keepdims=True)
    return jnp.exp(a) - jnp.exp(b) + lam_init


def _sub_norm(o, sub_ref, lam_init):
    ms = jnp.mean(o * o, axis=-1, keepdims=True)
    return o * lax.rsqrt(ms + LN_EPS) * sub_ref[...] * (1.0 - lam_init)


def _attn_prompt_kernel(qt_ref, k_ref, vt_ref, lam_ref, sub_ref, o_ref, m_scr, l_scr, acc_scr, *, lam_init):
    i = pl.program_id(2)
    tq = qt_ref.shape[2]
    tk = tq
    qt = qt_ref[0].astype(F32)
    feat = lax.broadcasted_iota(I32, qt.shape, 0)
    qq = jnp.concatenate([jnp.where(feat < DH_B, qt, 0.0), jnp.where(feat >= DH_B, qt, 0.0)],
                         axis=1).astype(BF16)
    m_scr[...] = jnp.full(m_scr.shape, -jnp.inf, F32)
    l_scr[...] = jnp.zeros(l_scr.shape, F32)
    acc_scr[...] = jnp.zeros(acc_scr.shape, F32)

    def step(j, masked):
        start = pl.multiple_of(j * tk, tk)
        kj = k_ref[0, pl.ds(start, tk), :]
        vtj = vt_ref[0, :, pl.ds(start, tk)]
        s = _dot(kj, qq)
        if masked:
            key = lax.broadcasted_iota(I32, s.shape, 0)
            qry = lax.broadcasted_iota(I32, s.shape, 1)
            qry = jnp.where(qry >= tq, qry - tq, qry)
            s = jnp.where(key <= qry, s, -jnp.inf)
        m_prev = m_scr[...]
        m_new = jnp.maximum(m_prev, jnp.max(s, axis=0, keepdims=True))
        alpha = jnp.exp(m_prev - m_new)
        p = jnp.exp(s - m_new)
        l_scr[...] = alpha * l_scr[...] + jnp.sum(p, axis=0, keepdims=True)
        acc_scr[...] = alpha * acc_scr[...] + _dot(vtj, p.astype(BF16))
        m_scr[...] = m_new

    def body(j, carry):
        step(j, False)
        return carry

    lax.fori_loop(0, i, body, 0)
    step(i, True)
    on = acc_scr[...] / l_scr[...]
    lam = _lambda_value(lam_ref, lam_init)
    o = jnp.transpose(on[:, 0:tq] - lam * on[:, tq:2 * tq])
    o_ref[0] = _sub_norm(o, sub_ref, lam_init).astype(BF16)


def _attn_prompt(qt, k, vt, lam_vecs, subln, lam_init):
    b, t, _ = k.shape
    tq = ATTN_TILE
    full = lambda a: pl.BlockSpec(a.shape, lambda bi, h, i: (0, 0))
    return pl.pallas_call(
        functools.partial(_attn_prompt_kernel, lam_init=lam_init),
        grid=(b, H_B, t // tq),
        in_specs=[pl.BlockSpec((1, LANES, tq), lambda bi, h, i: (bi, h, i)),
                  pl.BlockSpec((1, t, LANES), lambda bi, h, i: (bi, 0, h)),
                  pl.BlockSpec((1, LANES, t), lambda bi, h, i: (bi, h, 0)),
                  full(lam_vecs), full(subln)],
        out_specs=pl.BlockSpec((1, tq, LANES), lambda bi, h, i: (bi, i, h)),
        out_shape=jax.ShapeDtypeStruct((b, t, V_B), BF16),
        scratch_shapes=[pltpu.VMEM((1, 2 * tq), F32), pltpu.VMEM((1, 2 * tq), F32),
                        pltpu.VMEM((LANES, 2 * tq), F32)],
        compiler_params=_cparams("arbitrary", "arbitrary", "arbitrary"),
        name="attn_prompt",
    )(qt, k, vt, lam_vecs, subln)


def _attn_decode_kernel(pt_ref, qbd_ref, kn_ref, vn_ref, lam_ref, sub_ref, *rest, lam_init, n_pages):
    del pt_ref
    k_refs = rest[:n_pages]
    v_refs = rest[n_pages:2 * n_pages]
    o_ref, m_scr, l_scr, acc_scr = rest[2 * n_pages:]
    j = pl.program_id(1)
    qbd = qbd_ref[0]

    @pl.when(j == 0)
    def _():
        s_new = jnp.sum(qbd.astype(F32) * kn_ref[0], axis=1, keepdims=True)
        m_scr[...] = s_new
        l_scr[...] = jnp.ones(l_scr.shape, F32)
        acc_scr[...] = jnp.broadcast_to(vn_ref[0], acc_scr.shape)

    s = jnp.concatenate([_dot_nt(qbd, k_refs[r][0].astype(BF16)) for r in range(n_pages)], axis=1)
    m_prev = m_scr[...]
    m_new = jnp.maximum(m_prev, jnp.max(s, axis=1, keepdims=True))
    alpha = jnp.exp(m_prev - m_new)
    p = jnp.exp(s - m_new)
    l_scr[...] = alpha * l_scr[...] + jnp.sum(p, axis=1, keepdims=True)
    pv = _dot(p[:, 0:PAGE_SIZE].astype(BF16), v_refs[0][0].astype(BF16))
    for r in range(1, n_pages):
        pv = pv + _dot(p[:, r * PAGE_SIZE:(r + 1) * PAGE_SIZE].astype(BF16), v_refs[r][0].astype(BF16))
    acc_scr[...] = alpha * acc_scr[...] + pv
    m_scr[...] = m_new

    @pl.when(j == pl.num_programs(1) - 1)
    def _():
        on = acc_scr[...] / l_scr[...]
        row = lax.broadcasted_iota(I32, (2 * H_B, 2 * DH_B), 0)
        head = jnp.where(row >= H_B, row - H_B, row)
        o8 = jnp.zeros((2 * H_B, 2 * DH_B), F32)
        for c in range(H_B):
            o8 = o8 + jnp.where(head == c, on[:, c * 2 * DH_B:(c + 1) * 2 * DH_B], 0.0)
        lam = _lambda_value(lam_ref, lam_init)
        o = o8 - lam * pltpu.roll(o8, H_B, 0)
        o_ref[0] = _sub_norm(o, sub_ref, lam_init)


def _attn_decode(qbd, k_new, v_new, cache_k, cache_v, page_table, lam_vecs, subln, lam_init):
    n = qbd.shape[0]
    n_pages = page_table.shape[1]
    pp = PAGES_PER_STEP
    width = cache_k.shape[-1]
    c2 = lambda a: pl.BlockSpec(a.shape, lambda b, j, pt: (0, 0))
    per_b = lambda a: pl.BlockSpec((1,) + a.shape[1:], lambda b, j, pt: (b, 0, 0))

    def page(r):
        return pl.BlockSpec((1, PAGE_SIZE, width), lambda b, j, pt: (pt[b, j * pp + r], 0, 0))

    grid_spec = pltpu.PrefetchScalarGridSpec(
        num_scalar_prefetch=1,
        grid=(n, n_pages // pp),
        in_specs=[per_b(qbd), per_b(k_new), per_b(v_new), c2(lam_vecs), c2(subln)]
        + [page(r) for r in range(pp)] + [page(r) for r in range(pp)],
        out_specs=pl.BlockSpec((1, 2 * H_B, 2 * DH_B), lambda b, j, pt: (b, 0, 0)),
        scratch_shapes=[pltpu.VMEM((2 * H_B, 1), F32), pltpu.VMEM((2 * H_B, 1), F32),
                        pltpu.VMEM((2 * H_B, width), F32)],
    )
    return pl.pallas_call(
        functools.partial(_attn_decode_kernel, lam_init=lam_init, n_pages=pp),
        grid_spec=grid_spec,
        out_shape=jax.ShapeDtypeStruct((n, 2 * H_B, 2 * DH_B), F32),
        compiler_params=_cparams("arbitrary", "arbitrary"),
        name="attn_decode",
    )(page_table, qbd, k_new, v_new, lam_vecs, subln, *([cache_k] * pp), *([cache_v] * pp))


def _odd_in_kernel(x_ref, w_ref, wg_ref, wgt_ref, bg_ref, bgt_ref,
                   q_ref, k_ref, v_ref, o_ref, gc_ref, gr_ref, *, decode):
    xb = x_ref[0].astype(BF16) if not decode else x_ref[...].astype(BF16)
    qw = H_C * DK_C
    vw = H_C * DV_C
    q = _dot(xb, w_ref[:, 0:qw]) * (DK_C ** -0.5)
    k = _dot(xb, w_ref[:, qw:2 * qw])
    v = _dot(xb, w_ref[:, 2 * qw:2 * qw + vw])
    o = _dot(xb, w_ref[:, 2 * qw + vw:2 * qw + 2 * vw])
    g_col = _dot(xb, wg_ref[...]) + bg_ref[...]
    lane = lax.broadcasted_iota(I32, g_col.shape, 1)
    g_col = jnp.where(lane < H_C, g_col, _log_sigmoid(g_col))
    g_row = _dot_nt(wgt_ref[...], xb) + bgt_ref[:, 0:1]
    sub = lax.broadcasted_iota(I32, g_row.shape, 0)
    g_row = jnp.where(sub < H_C, g_row, _log_sigmoid(g_row))
    if decode:
        q_ref[...] = q.astype(BF16)
        k_ref[...] = k.astype(BF16)
        v_ref[...] = v.astype(BF16)
        o_ref[...] = o
        gc_ref[...] = g_col
        gr_ref[...] = g_row
    else:
        q_ref[0] = q.astype(BF16)
        k_ref[0] = k.astype(BF16)
        v_ref[0] = v.astype(BF16)
        o_ref[0] = o
        gc_ref[0] = g_col
        gr_ref[0] = g_row


def _odd_in(x, w_bf, wg, wgt, bg, bgt, decode):
    qw, vw = H_C * DK_C, H_C * DV_C
    if decode:
        n = x.shape[0]
        ins = [x, w_bf, wg, wgt, bg, bgt]
        outs = [jax.ShapeDtypeStruct((n, qw), BF16), jax.ShapeDtypeStruct((n, qw), BF16),
                jax.ShapeDtypeStruct((n, vw), BF16), jax.ShapeDtypeStruct((n, vw), F32),
                jax.ShapeDtypeStruct((n, LANES), F32), jax.ShapeDtypeStruct((2 * H_C, n), F32)]
        return pl.pallas_call(
            functools.partial(_odd_in_kernel, decode=True),
            grid=(1,),
            in_specs=[pl.BlockSpec(a.shape, lambda i: (0, 0)) for a in ins],
            out_specs=[pl.BlockSpec(o.shape, lambda i: (0, 0)) for o in outs],
            out_shape=outs,
            compiler_params=_cparams("arbitrary"),
            name="odd_in_decode",
        )(*ins)
    b, t, _ = x.shape
    tm = ROW_TILE
    row3 = lambda n: pl.BlockSpec((1, tm, n), lambda i, j: (i, j, 0))
    full2 = lambda a: pl.BlockSpec(a.shape, lambda i, j: (0, 0))
    outs = [jax.ShapeDtypeStruct((b, t, qw), BF16), jax.ShapeDtypeStruct((b, t, qw), BF16),
            jax.ShapeDtypeStruct((b, t, vw), BF16), jax.ShapeDtypeStruct((b, t, vw), F32),
            jax.ShapeDtypeStruct((b, t, LANES), F32), jax.ShapeDtypeStruct((b, 2 * H_C, t), F32)]
    return pl.pallas_call(
        functools.partial(_odd_in_kernel, decode=False),
        grid=(b, t // tm),
        in_specs=[row3(D_MODEL), full2(w_bf), full2(wg), full2(wgt), full2(bg), full2(bgt)],
        out_specs=[row3(qw), row3(qw), row3(vw), row3(vw), row3(LANES),
                   pl.BlockSpec((1, 2 * H_C, tm), lambda i, j: (i, 0, j))],
        out_shape=outs,
        compiler_params=_cparams("arbitrary", "arbitrary"),
        name="odd_in_prompt",
    )(x, w_bf, wg, wgt, bg, bgt)


def _mlstm_kernel(q_ref, k_ref, v_ref, o_ref, gc_ref, gr_ref, nw_ref, c0_ref, n0_ref, m0_ref,
                  h_ref, c_out, n_out, m_out, c_scr, n_scr, m_scr):
    ci = pl.program_id(1)
    chunk = q_ref.shape[1]

    @pl.when(ci == 0)
    def _():
        c_scr[...] = c0_ref[0]
        n_scr[...] = n0_ref[0]
        m_scr[...] = m0_ref[0]

    t_idx = lax.broadcasted_iota(I32, (chunk, chunk), 0)
    s_idx = lax.broadcasted_iota(I32, (chunk, chunk), 1)
    causal = s_idx <= t_idx
    for h in range(H_C):
        q = q_ref[0, :, h * DK_C:(h + 1) * DK_C]
        k = k_ref[0, :, h * DK_C:(h + 1) * DK_C]
        v = v_ref[0, :, h * DV_C:(h + 1) * DV_C]
        ig_r = gr_ref[0, h:h + 1, :]
        lf_r = gr_ref[0, H_C + h:H_C + h + 1, :]
        ig_c = gc_ref[0, :, h:h + 1]
        lf_c = gc_ref[0, :, H_C + h:H_C + h + 1]
        bcum_c = jnp.sum(jnp.where(causal, lf_r, 0.0), axis=1, keepdims=True)
        bcum_r = jnp.sum(jnp.where(t_idx <= s_idx, lf_c, 0.0), axis=0, keepdims=True)
        m0 = m_scr[h:h + 1, 0:1]
        dmat = jnp.where(causal, bcum_c - bcum_r + ig_r, -jnp.inf)
        inter = bcum_c + m0
        m = jnp.maximum(inter, jnp.max(dmat, axis=1, keepdims=True))
        w = jnp.exp(dmat - m)
        g = jnp.exp(inter - m)
        s = _dot_nt(q, k) * w
        c0 = c_scr[h]
        n0 = n_scr[h:h + 1, :]
        num = g * _dot(q, c0.astype(BF16)) + _dot(s.astype(BF16), v)
        den = g * jnp.sum(q.astype(F32) * n0, axis=1, keepdims=True) + jnp.sum(s, axis=1, keepdims=True)
        hid = num / jnp.maximum(jnp.abs(den), jnp.exp(-m))
        m_last = m[chunk - 1:chunk, :]
        b_last = bcum_c[chunk - 1:chunk, :]
        w_last = jnp.exp(b_last - bcum_c + ig_c - m_last)
        g_last = jnp.exp(b_last + m0 - m_last)
        kw = k.astype(F32) * w_last
        c_scr[h] = g_last * c0 + _dot_tn(kw.astype(BF16), v)
        n_scr[h:h + 1, :] = g_last * n0 + jnp.sum(kw, axis=0, keepdims=True)
        m_scr[h:h + 1, :] = jnp.broadcast_to(m_last, (1, LANES))
        mu = jnp.mean(hid, axis=1, keepdims=True)
        hc = hid - mu
        var = jnp.mean(hc * hc, axis=1, keepdims=True)
        hn = hc * lax.rsqrt(var + LN_EPS) * nw_ref[:, h * DV_C:(h + 1) * DV_C]
        gate = jax.nn.sigmoid(o_ref[0, :, h * DV_C:(h + 1) * DV_C])
        h_ref[0, :, h * DV_C:(h + 1) * DV_C] = (gate * hn).astype(BF16)

    c_out[0] = c_scr[...]
    n_out[0] = n_scr[...]
    m_out[0] = m_scr[...]


def _mlstm(q, k, v, o, g_col, g_row, norm_w, c0, n0, m0):
    b, t, _ = q.shape
    ch = MLSTM_CHUNK
    qw, vw = H_C * DK_C, H_C * DV_C
    row = lambda n: pl.BlockSpec((1, ch, n), lambda i, j: (i, j, 0))
    st4 = pl.BlockSpec((1, H_C, DK_C, DV_C), lambda i, j: (i, 0, 0, 0))
    st3 = pl.BlockSpec((1, H_C, LANES), lambda i, j: (i, 0, 0))
    outs = [jax.ShapeDtypeStruct((b, t, vw), BF16), jax.ShapeDtypeStruct((b, H_C, DK_C, DV_C), F32),
            jax.ShapeDtypeStruct((b, H_C, DK_C), F32), jax.ShapeDtypeStruct((b, H_C, LANES), F32)]
    return pl.pallas_call(
        _mlstm_kernel,
        grid=(b, t // ch),
        in_specs=[row(qw), row(qw), row(vw), row(vw), row(LANES),
                  pl.BlockSpec((1, 2 * H_C, ch), lambda i, j: (i, 0, j)),
                  pl.BlockSpec(norm_w.shape, lambda i, j: (0, 0)), st4, st3, st3],
        out_specs=[row(vw), st4, st3, st3],
        out_shape=outs,
        scratch_shapes=[pltpu.VMEM((H_C, DK_C, DV_C), F32), pltpu.VMEM((H_C, DK_C), F32),
                        pltpu.VMEM((H_C, LANES), F32)],
        compiler_params=_cparams("arbitrary", "arbitrary"),
        name="mlstm",
    )(q, k, v, o, g_col, g_row, norm_w, c0, n0, m0)


def _mix_out_kernel(*refs, n_in):
    a_refs = refs[:n_in]
    w_refs = refs[n_in:2 * n_in]
    x_ref, g_ref, b_ref, rhi_ref, rlo_ref, x1_in, rt_in, x1_ref, rt_ref = refs[2 * n_in:]
    del x1_in, rt_in
    y = _dot(a_refs[0][...], w_refs[0][...])
    for i in range(1, n_in):
        y = y + _dot(a_refs[i][...], w_refs[i][...])
    x1 = _layer_norm_rows(DEEPNORM_ALPHA * x_ref[...] + y, g_ref[...], b_ref[...])
    x1_ref[...] = x1

    xh = x1.astype(BF16)
    xl = (x1 - xh.astype(F32)).astype(BF16)
    lg = _dot(xh, rhi_ref[...]) + _dot(xl, rhi_ref[...]) + _dot(xh, rlo_ref[...])
    lane = lax.broadcasted_iota(I32, lg.shape, 1)
    big = jnp.int32(4 * LANES)
    neg = -jnp.inf
    gl = jnp.where(lane < ROUTE_GROUP_LANE + N_GROUPS, lg, neg)
    g_max = jnp.max(gl, axis=1, keepdims=True)
    g_w = 1.0 / jnp.sum(jnp.exp(gl - g_max), axis=1, keepdims=True)
    g_idx = jnp.min(jnp.where(gl == g_max, lane, big), axis=1, keepdims=True)
    lane_group = (lane - ROUTE_EXPERT_LANE) >> 2
    el = jnp.where(lane_group == g_idx, lg, neg)
    e1 = jnp.max(el, axis=1, keepdims=True)
    i1 = jnp.min(jnp.where(el == e1, lane, big), axis=1, keepdims=True)
    z = jnp.sum(jnp.exp(el - e1), axis=1, keepdims=True)
    el2 = jnp.where(lane == i1, neg, el)
    e2 = jnp.max(el2, axis=1, keepdims=True)
    i2 = jnp.min(jnp.where(el2 == e2, lane, big), axis=1, keepdims=True)
    p1 = 1.0 / z
    p2 = jnp.exp(e2 - e1) / z
    w1 = p1 / (p1 + p2) * g_w
    w2 = p2 / (p1 + p2) * g_w
    id1 = (i1 - ROUTE_EXPERT_LANE).astype(F32)
    id2 = (i2 - ROUTE_EXPERT_LANE).astype(F32)
    rt_ref[...] = jnp.where(lane == 0, w1, jnp.where(lane == 1, w2,
                            jnp.where(lane == 2, id1, jnp.where(lane == 3, id2, 0.0))))


def _mix_out(acts, weights, x, ln_g, ln_b, r_hi, r_lo, x1_all, rt_all, row0, tm):
    rows = x.shape[0]
    blk0 = row0 // tm
    n_in = len(acts)
    row = lambda n: pl.BlockSpec((tm, n), lambda i: (i, 0))
    full = lambda a: pl.BlockSpec(a.shape, lambda i: (0, 0))
    anyspec = pl.BlockSpec(memory_space=pl.ANY)
    ins = [*acts, *weights, x, ln_g, ln_b, r_hi, r_lo, x1_all, rt_all]
    n_fixed = len(ins) - 2
    return pl.pallas_call(
        functools.partial(_mix_out_kernel, n_in=n_in),
        grid=(rows // tm,),
        in_specs=[row(a.shape[1]) for a in acts] + [full(w) for w in weights]
        + [row(D_MODEL), full(ln_g), full(ln_b), full(r_hi), full(r_lo), anyspec, anyspec],
        out_specs=[pl.BlockSpec((tm, D_MODEL), lambda i: (i + blk0, 0)),
                   pl.BlockSpec((tm, LANES), lambda i: (i + blk0, 0))],
        out_shape=[jax.ShapeDtypeStruct(x1_all.shape, F32), jax.ShapeDtypeStruct(rt_all.shape, F32)],
        input_output_aliases={n_fixed: 0, n_fixed + 1: 1},
        compiler_params=_cparams("arbitrary"),
        name="mix_out",
    )(*ins)


def _moe_kernel(ta_ref, tb_ref, nu_ref, src_ref, dst_ref, x_hbm, wa_ref, wb_ref,
                ga_ref, ua_ref, da_ref, gb_ref, ub_ref, db_ref, lg_ref, lb_ref,
                out_hbm, xbuf, obuf, sem_in, sem_out):
    del ta_ref, tb_ref
    g = pl.program_id(0)
    tm = xbuf.shape[0]

    def gather_copy(r):
        return pltpu.make_async_copy(x_hbm.at[pl.ds(src_ref[0, 0, r], 1)], xbuf.at[pl.ds(r, 1)], sem_in)

    def scatter_copy(r):
        return pltpu.make_async_copy(obuf.at[pl.ds(r, 1)], out_hbm.at[pl.ds(dst_ref[0, 0, r], 1)], sem_out)

    @pl.when(g < nu_ref[0])
    def _():
        def start_in(r, c):
            gather_copy(r).start()
            return c

        def wait_in(r, c):
            gather_copy(r).wait()
            return c

        lax.fori_loop(0, tm, start_in, 0, unroll=8)
        lax.fori_loop(0, tm, wait_in, 0, unroll=8)
        x = xbuf[...]
        xb = x.astype(BF16)

        def expert(gw, uw, dw):
            hid = jax.nn.silu(_dot(xb, gw[0])) * _dot(xb, uw[0])
            return _dot(hid.astype(BF16), dw[0])

        y = wa_ref[...] * expert(ga_ref, ua_ref, da_ref)
        y = y + wb_ref[...] * expert(gb_ref, ub_ref, db_ref)
        obuf[...] = _layer_norm_rows(DEEPNORM_ALPHA * x + y, lg_ref[...], lb_ref[...])

    @pl.when(g >= nu_ref[0])
    def _():
        obuf[...] = jnp.zeros(obuf.shape, F32)

    def start_out(r, c):
        scatter_copy(r).start()
        return c

    def wait_out(r, c):
        scatter_copy(r).wait()
        return c

    lax.fori_loop(0, tm, start_out, 0, unroll=8)
    lax.fori_loop(0, tm, wait_out, 0, unroll=8)


def _route_plan(rt, tm):
    n = rt.shape[0]
    w1, w2 = rt[:, 0], rt[:, 1]
    e1, e2 = rt[:, 2].astype(I32), rt[:, 3].astype(I32)
    first = e1 < e2
    ea, eb = jnp.where(first, e1, e2), jnp.where(first, e2, e1)
    wa, wb = jnp.where(first, w1, w2), jnp.where(first, w2, w1)
    la, lb = ea % EXP_PER_GROUP, eb % EXP_PER_GROUP
    cls = (ea // EXP_PER_GROUP) * N_PAIRS + (la * (2 * EXP_PER_GROUP - 1 - la)) // 2 + (lb - la - 1)
    onehot = (cls[:, None] == jnp.arange(N_CLASSES, dtype=I32)[None, :]).astype(I32)
    csum = jnp.cumsum(onehot, axis=0)
    rank = jnp.sum(onehot * csum, axis=1) - 1
    cnt = csum[-1]
    ntile = (cnt + tm - 1) // tm
    tile_end = jnp.cumsum(ntile)
    tile_start = tile_end - ntile
    n_used = tile_end[-1]
    n_tiles = -(-(n + N_CLASSES * (tm - 1)) // tm)
    total = n_tiles * tm
    pos = tile_start[cls] * tm + rank
    tok = jnp.arange(n, dtype=I32)
    src = jnp.zeros((total,), I32).at[pos].set(tok)
    valid = jnp.zeros((total,), I32).at[pos].set(1)
    spare = n + jnp.cumsum(1 - valid) - 1
    dst = jnp.where(valid == 1, src, spare).astype(I32)
    wa_s = jnp.zeros((total,), F32).at[pos].set(wa)
    wb_s = jnp.zeros((total,), F32).at[pos].set(wb)
    pair_lo = np.array([a for a in range(EXP_PER_GROUP) for b in range(a + 1, EXP_PER_GROUP)], np.int32)
    pair_hi = np.array([b for a in range(EXP_PER_GROUP) for b in range(a + 1, EXP_PER_GROUP)], np.int32)
    cls_ids = np.arange(N_CLASSES)
    cls_a = jnp.asarray((cls_ids // N_PAIRS) * EXP_PER_GROUP + pair_lo[cls_ids % N_PAIRS], I32)
    cls_b = jnp.asarray((cls_ids // N_PAIRS) * EXP_PER_GROUP + pair_hi[cls_ids % N_PAIRS], I32)
    tile_ids = jnp.arange(n_tiles, dtype=I32)
    tile_cls = jnp.sum((tile_end[None, :] <= jnp.minimum(tile_ids, n_used - 1)[:, None]).astype(I32), axis=1)
    tile_cls = jnp.minimum(tile_cls, N_CLASSES - 1)
    return dict(src=src.reshape(n_tiles, 1, tm), dst=dst.reshape(n_tiles, 1, tm),
                wa=wa_s.reshape(total, 1), wb=wb_s.reshape(total, 1),
                ta=cls_a[tile_cls], tb=cls_b[tile_cls], nu=n_used.reshape(1).astype(I32),
                n_tiles=n_tiles, total=total)


def _moe(x1_all, plan, wg_bf, wu_bf, wd_bf, ln_g, ln_b, tm):
    n_tiles, total = plan["n_tiles"], plan["total"]
    idx = pl.BlockSpec((1, 1, tm), lambda g, ta, tb, nu: (g, 0, 0), memory_space=pltpu.SMEM)
    wrow = pl.BlockSpec((tm, 1), lambda g, ta, tb, nu: (g, 0))
    anyspec = pl.BlockSpec(memory_space=pl.ANY)
    up_a = pl.BlockSpec((1, D_MODEL, D_EXPERT), lambda g, ta, tb, nu: (ta[g], 0, 0))
    dn_a = pl.BlockSpec((1, D_EXPERT, D_MODEL), lambda g, ta, tb, nu: (ta[g], 0, 0))
    up_b = pl.BlockSpec((1, D_MODEL, D_EXPERT), lambda g, ta, tb, nu: (tb[g], 0, 0))
    dn_b = pl.BlockSpec((1, D_EXPERT, D_MODEL), lambda g, ta, tb, nu: (tb[g], 0, 0))
    vec = pl.BlockSpec((1, D_MODEL), lambda g, ta, tb, nu: (0, 0))
    grid_spec = pltpu.PrefetchScalarGridSpec(
        num_scalar_prefetch=3,
        grid=(n_tiles,),
        in_specs=[idx, idx, anyspec, wrow, wrow, up_a, up_a, dn_a, up_b, up_b, dn_b, vec, vec],
        out_specs=anyspec,
        scratch_shapes=[pltpu.VMEM((tm, D_MODEL), F32), pltpu.VMEM((tm, D_MODEL), F32),
                        pltpu.SemaphoreType.DMA(()), pltpu.SemaphoreType.DMA(())],
    )
    return pl.pallas_call(
        _moe_kernel,
        grid_spec=grid_spec,
        out_shape=jax.ShapeDtypeStruct((total, D_MODEL), F32),
        compiler_params=_cparams("arbitrary"),
        name="moe",
    )(plan["ta"], plan["tb"], plan["nu"], plan["src"], plan["dst"], x1_all, plan["wa"], plan["wb"],
      wg_bf, wu_bf, wd_bf, wg_bf, wu_bf, wd_bf, ln_g, ln_b)


def _ple_kernel(x_ref, p_ref, wg_ref, wp_ref, o_ref):
    x = x_ref[...]
    gate = jax.nn.sigmoid(_dot(x.astype(BF16), wg_ref[...]))
    o_ref[...] = x + gate * _dot(p_ref[...].astype(BF16), wp_ref[...])


def _ple(x2_all, p, wg_bf, wp_bf, row0, tm):
    rows = p.shape[0]
    blk0 = row0 // tm
    full = lambda a: pl.BlockSpec(a.shape, lambda i: (0, 0))
    return pl.pallas_call(
        _ple_kernel,
        grid=(rows // tm,),
        in_specs=[pl.BlockSpec((tm, D_MODEL), lambda i: (i + blk0, 0)),
                  pl.BlockSpec((tm, D_PLE), lambda i: (i, 0)), full(wg_bf), full(wp_bf)],
        out_specs=pl.BlockSpec((tm, D_MODEL), lambda i: (i, 0)),
        out_shape=jax.ShapeDtypeStruct((rows, D_MODEL), F32),
        compiler_params=_cparams("arbitrary"),
        name="ple",
    )(x2_all, p, wg_bf, wp_bf)


def _router_weights(w_group, w_router):
    wr = jnp.zeros((D_MODEL, LANES), F32)
    wr = wr.at[:, ROUTE_GROUP_LANE:ROUTE_GROUP_LANE + N_GROUPS].set(w_group)
    wr = wr.at[:, ROUTE_EXPERT_LANE:ROUTE_EXPERT_LANE + N_EXPERTS].set(w_router)
    hi = wr.astype(BF16)
    lo = (wr - hi.astype(F32)).astype(BF16)
    return hi, lo


def _layer_tail(i, acts_p, acts_s, w_list, xp, xs, p_p, p_s, ln_mix_g, ln_mix_b, ln_ffn_g, ln_ffn_b,
                w_group, w_router, w_exp_gate, w_exp_up, w_exp_down, w_ple_proj, w_ple_gate):
    n_p, n_s = xp.shape[0], xs.shape[0]
    n_all = n_p + n_s
    r_hi, r_lo = _router_weights(w_group[i], w_router[i])
    g1, b1 = ln_mix_g[i][None, :], ln_mix_b[i][None, :]
    x1_all = jnp.zeros((n_all, D_MODEL), F32)
    rt_all = jnp.zeros((n_all, LANES), F32)
    x1_all, rt_all = _mix_out(acts_p, w_list, xp, g1, b1, r_hi, r_lo, x1_all, rt_all, 0, ROW_TILE)
    x1_all, rt_all = _mix_out(acts_s, w_list, xs, g1, b1, r_hi, r_lo, x1_all, rt_all, n_p, n_s)
    plan = _route_plan(rt_all, MOE_TILE)
    x2_all = _moe(x1_all, plan, w_exp_gate[i].astype(BF16), w_exp_up[i].astype(BF16),
                  w_exp_down[i].astype(BF16), ln_ffn_g[i][None, :], ln_ffn_b[i][None, :], MOE_TILE)
    wg_bf, wp_bf = w_ple_gate[i].astype(BF16), w_ple_proj[i].astype(BF16)
    xp_new = _ple(x2_all, p_p, wg_bf, wp_bf, 0, ROW_TILE)
    xs_new = _ple(x2_all, p_s, wg_bf, wp_bf, n_p, n_s)
    return xp_new, xs_new


def kernel(x_prompt, x_sample, cache_k, cache_v, page_table, state_conv, state_mlstm_C, state_mlstm_n,
           state_mlstm_m, p_prompt, p_sample, w_in_even, conv_w, lambda_q1, lambda_k1, lambda_q2, lambda_k2,
           subln_w, w_out_even, w_in_odd, b_gates_odd, mh_norm_w, w_out_odd, ln_mix_g, ln_mix_b, ln_ffn_g,
           ln_ffn_b, w_group, w_router, w_exp_gate, w_exp_up, w_exp_down, w_ple_proj, w_ple_gate):
    bp, tp, _ = x_prompt.shape
    bs, ts, _ = x_sample.shape
    assert ts == 1 and tp % ROW_TILE == 0 and tp % ATTN_TILE == 0 and tp % MLSTM_CHUNK == 0
    n_p = bp * tp
    past_len = page_table.shape[1] * cache_k.shape[2]
    xp = x_prompt.reshape(n_p, D_MODEL)
    xs = x_sample.reshape(bs, D_MODEL)
    tail_w = (ln_mix_g, ln_mix_b, ln_ffn_g, ln_ffn_b, w_group, w_router, w_exp_gate, w_exp_up, w_exp_down,
              w_ple_proj, w_ple_gate)
    outs_p, outs_s = {}, {}
    for i in range(DEPTH):
        j = i // 2
        p_p = p_prompt[i].reshape(n_p, D_PLE)
        p_s = p_sample[i].reshape(bs, D_PLE)
        if i % 2 == 0:
            lam_init = 0.8 - 0.6 * math.exp(-0.3 * i)
            lam_vecs = jnp.stack([lambda_q1[j], lambda_k1[j], lambda_q2[j], lambda_k2[j]])
            sub = subln_w[j][None, :]
            w_bf = w_in_even[j].astype(BF16)
            tabs_p = _rope_tables(jnp.arange(tp))
            q0, v0 = 3 * D_CONV, 3 * D_CONV + 2 * QK_B
            wqt_bf = jnp.transpose(w_in_even[j][:, q0:q0 + QK_B]).astype(BF16)
            wvt_bf = jnp.transpose(w_in_even[j][:, v0:v0 + V_B]).astype(BF16)
            yc, qt, kf, vf, kb, vt, cst = _even_in_prompt(
                x_prompt if i == 0 else xp.reshape(bp, tp, D_MODEL), w_bf, wqt_bf, wvt_bf, conv_w[j],
                jnp.zeros((bp, CONV_W - 1, D_CONV), F32), tabs_p)
            o_p = _attn_prompt(qt, kb, vt, lam_vecs, sub, lam_init)
            outs_p.setdefault("k", []).append(kf.reshape(bp, tp, 2 * H_B, DH_B))
            outs_p.setdefault("v", []).append(vf.reshape(bp, tp, H_B, 2 * DH_B))
            outs_p.setdefault("c", []).append(cst)
            tabs_s = _rope_tables(jnp.full((1,), past_len, I32))
            prev_t = jnp.swapaxes(state_conv[j], 0, 1)
            yc_s, q_s, kf_s, vf_s, u_s = _even_in_decode(xs, w_bf, conv_w[j], prev_t, tabs_s)
            sub_head = jnp.arange(2 * H_B)
            sub_head = jnp.where(sub_head < H_B, 2 * sub_head, 2 * (sub_head - H_B) + 1)
            lane_head = jnp.arange(QK_B) // DH_B
            qbd = jnp.where(lane_head[None, None, :] == sub_head[None, :, None], q_s[:, None, :],
                            jnp.zeros((), BF16))
            n_pool = cache_k.shape[1]
            pages = cache_k.shape[0] * n_pool
            o8 = _attn_decode(qbd, kf_s[:, None, :], vf_s[:, None, :],
                              cache_k.reshape(pages, PAGE_SIZE, QK_B), cache_v.reshape(pages, PAGE_SIZE, V_B),
                              page_table + j * n_pool, lam_vecs, sub, lam_init)
            o_s = o8[:, :H_B, :].reshape(bs, V_B).astype(BF16)
            outs_s.setdefault("k", []).append(kf_s.reshape(bs, ts, 2 * H_B, DH_B))
            outs_s.setdefault("v", []).append(vf_s.reshape(bs, ts, H_B, 2 * DH_B))
            outs_s.setdefault("c", []).append(jnp.stack([state_conv[j][:, 1, :], u_s], axis=1))
            w_out = w_out_even[j].astype(BF16)
            w_list = [w_out[:D_CONV], w_out[D_CONV:]]
            acts_p = [yc.reshape(n_p, D_CONV), o_p.reshape(n_p, V_B)]
            acts_s = [yc_s, o_s]
        else:
            w_in = w_in_odd[j]
            qw, vw = H_C * DK_C, H_C * DV_C
            w_bf = w_in[:, :2 * qw + 2 * vw].astype(BF16)
            wg = jnp.zeros((D_MODEL, LANES), F32).at[:, :2 * H_C].set(w_in[:, 2 * qw + 2 * vw:]).astype(BF16)
            wgt = jnp.transpose(wg[:, :2 * H_C])
            bg = jnp.zeros((1, LANES), F32).at[0, :2 * H_C].set(b_gates_odd[j])
            bgt = jnp.broadcast_to(b_gates_odd[j][:, None], (2 * H_C, LANES))
            nw = mh_norm_w[j][None, :]
            q, k, v, o, gc, gr = _odd_in(xp.reshape(bp, tp, D_MODEL), w_bf, wg, wgt, bg, bgt, decode=False)
            h_p, c_p, n_pp, m_p = _mlstm(q, k, v, o, gc, gr, nw,
                                         jnp.zeros((bp, H_C, DK_C, DV_C), F32), jnp.zeros((bp, H_C, DK_C), F32),
                                         jnp.zeros((bp, H_C, LANES), F32))
            outs_p.setdefault("C", []).append(c_p)
            outs_p.setdefault("n", []).append(n_pp)
            outs_p.setdefault("m", []).append(m_p[:, :, 0])
            q_s, k_s, v_s, o_s2, gc_s, gr_s = _odd_in(xs, w_bf, wg, wgt, bg, bgt, decode=True)
            ch = MLSTM_CHUNK
            pad_rows = lambda a: jnp.zeros((bs, ch, a.shape[1]), a.dtype).at[:, 0, :].set(a)
            lane = jnp.arange(LANES)
            inert_c = jnp.where(lane < H_C, -jnp.inf, 0.0).astype(F32)
            gc_pad = jnp.broadcast_to(inert_c[None, None, :], (bs, ch, LANES)).at[:, 0, :].set(gc_s)
            inert_r = jnp.where(jnp.arange(2 * H_C) < H_C, -jnp.inf, 0.0).astype(F32)
            gr_pad = jnp.broadcast_to(inert_r[None, :, None], (bs, 2 * H_C, ch)).at[:, :, 0].set(gr_s.T)
            m0 = jnp.broadcast_to(state_mlstm_m[j][:, :, None], (bs, H_C, LANES))
            h_s, c_s, n_s, m_s = _mlstm(pad_rows(q_s), pad_rows(k_s), pad_rows(v_s), pad_rows(o_s2), gc_pad,
                                        gr_pad, nw, state_mlstm_C[j], state_mlstm_n[j], m0)
            outs_s.setdefault("C", []).append(c_s)
            outs_s.setdefault("n", []).append(n_s)
            outs_s.setdefault("m", []).append(m_s[:, :, 0])
            w_list = [w_out_odd[j].astype(BF16)]
            acts_p = [h_p.reshape(n_p, vw)]
            acts_s = [h_s[:, 0, :]]
        xp, xs = _layer_tail(i, acts_p, acts_s, w_list, xp, xs, p_p, p_s, *tail_w)
    st = lambda lst: jnp.stack(lst)
    return (xp.reshape(bp, tp, D_MODEL), xs.reshape(bs, ts, D_MODEL),
            st(outs_p["k"]), st(outs_p["v"]), st(outs_p["c"]), st(outs_p["C"]), st(outs_p["n"]), st(outs_p["m"]),
            st(outs_s["k"]), st(outs_s["v"]), st(outs_s["c"]), st(outs_s["C"]), st(outs_s["n"]), st(outs_s["m"]))
```

```python
import functools
import math

import numpy as np
import jax
import jax.numpy as jnp
from jax import lax
from jax.experimental import pallas as pl
from jax.experimental.pallas import tpu as pltpu

F32 = jnp.float32
BF16 = jnp.bfloat16
I32 = jnp.int32

D_MODEL = 1024
DEPTH = 2
PAGE_SIZE = 128
D_CONV = D_MODEL // 2
CONV_W = 3
H_B = 4
DH_B = 64
ROT_DIM = DH_B // 4
ROPE_THETA = 500000.0
H_C = 4
DK_C = (D_MODEL // 2) // H_C
DV_C = D_MODEL // H_C
N_GROUPS = 4
EXP_PER_GROUP = 4
N_EXPERTS = N_GROUPS * EXP_PER_GROUP
D_EXPERT = 512
D_PLE = 256
LN_EPS = 1e-5
DEEPNORM_ALPHA = (2 * DEPTH) ** 0.25
QK_B = 2 * H_B * DH_B
V_B = H_B * 2 * DH_B
N_PAIRS = EXP_PER_GROUP * (EXP_PER_GROUP - 1) // 2
N_CLASSES = N_GROUPS * N_PAIRS

LANES = 128
VMEM_LIMIT = 56 * 1024 * 1024
ROW_TILE = 512
ATTN_TILE = 512
MLSTM_CHUNK = 128
MOE_TILE = 256
PAGES_PER_STEP = 8
ROUTE_GROUP_LANE = 0
ROUTE_EXPERT_LANE = 16


def _cparams(*sem):
    return pltpu.CompilerParams(dimension_semantics=sem, vmem_limit_bytes=VMEM_LIMIT)


def _dot(a, b):
    return jnp.dot(a, b, preferred_element_type=F32)


def _dot_nt(a, b):
    return lax.dot_general(a, b, (((1,), (1,)), ((), ())), preferred_element_type=F32)


def _dot_tn(a, b):
    return lax.dot_general(a, b, (((0,), (0,)), ((), ())), preferred_element_type=F32)


def _layer_norm_rows(z, g, b):
    mu = jnp.mean(z, axis=-1, keepdims=True)
    zc = z - mu
    var = jnp.mean(zc * zc, axis=-1, keepdims=True)
    return zc * lax.rsqrt(var + LN_EPS) * g + b


def _log_sigmoid(x):
    return jnp.minimum(x, 0.0) - jnp.log1p(jnp.exp(-jnp.abs(x)))


def _gated_conv(gate_b, u, um1, um2, cw_ref):
    cw = cw_ref[...]
    conv = um2 * cw[0:1, :] + um1 * cw[1:2, :] + u * cw[2:3, :]
    return (gate_b * conv).astype(BF16)


def _rope(z, cos, sin_lo, sin_hi, axis):
    half = ROT_DIM // 2
    return z * cos + pltpu.roll(z, QK_B - half, axis) * sin_lo + pltpu.roll(z, half, axis) * sin_hi


def _even_in_prompt_kernel(x_ref, w_ref, wqt_ref, wvt_ref, cw_ref, prev_ref, rc_ref, rs1_ref, rs2_ref,
                           rct_ref, rs1t_ref, rs2t_ref,
                           yc_ref, qt_ref, kf_ref, vf_ref, kb_ref, vt_ref, u_ref, carry_ref):
    xb = x_ref[0].astype(BF16)
    tm = xb.shape[0]

    def proj(c0, n):
        return _dot(xb, w_ref[:, c0:c0 + n])

    gate_b = proj(0, D_CONV)
    u = proj(D_CONV, D_CONV) * proj(2 * D_CONV, D_CONV)
    j = pl.program_id(1)

    @pl.when(j == 0)
    def _():
        carry_ref[...] = prev_ref[0]

    row = lax.broadcasted_iota(I32, u.shape, 0)
    c2 = carry_ref[0:1, :]
    c1 = carry_ref[1:2, :]
    um1 = jnp.where(row == 0, c1, pltpu.roll(u, 1, 0))
    um2 = jnp.where(row == 0, c2, jnp.where(row == 1, c1, pltpu.roll(u, 2, 0)))
    carry_ref[...] = u[tm - 2:tm, :]
    u_ref[0] = u[tm - 2:tm, :]
    yc_ref[0] = _gated_conv(gate_b, u, um1, um2, cw_ref)

    reps = QK_B // LANES
    tile = lambda r, ax: jnp.concatenate([r[...]] * reps, axis=ax)
    k = _rope(proj(3 * D_CONV + QK_B, QK_B), tile(rc_ref, 1), tile(rs1_ref, 1), tile(rs2_ref, 1), 1)
    kf_ref[0] = k
    kb_ref[0] = k.astype(BF16)
    v = proj(3 * D_CONV + 2 * QK_B, V_B)
    vf_ref[0] = v
    qt = _rope(_dot_nt(wqt_ref[...], xb), tile(rct_ref, 0), tile(rs1t_ref, 0), tile(rs2t_ref, 0), 0)
    qt_ref[0] = (qt * (DH_B ** -0.5)).astype(BF16)
    vt_ref[0] = _dot_nt(wvt_ref[...], xb).astype(BF16)


def _even_in_decode_kernel(x_ref, w_ref, cw_ref, prev_ref, rc_ref, rs1_ref, rs2_ref,
                           yc_ref, q_ref, kf_ref, vf_ref, u_ref):
    xb = x_ref[...].astype(BF16)

    def proj(c0, n):
        return _dot(xb, w_ref[:, c0:c0 + n])

    gate_b = proj(0, D_CONV)
    u = proj(D_CONV, D_CONV) * proj(2 * D_CONV, D_CONV)
    u_ref[...] = u
    yc_ref[...] = _gated_conv(gate_b, u, prev_ref[1], prev_ref[0], cw_ref)
    reps = QK_B // LANES
    tile = lambda r: jnp.concatenate([r[...]] * reps, axis=1)
    cos, sin_lo, sin_hi = tile(rc_ref), tile(rs1_ref), tile(rs2_ref)
    q_ref[...] = (_rope(proj(3 * D_CONV, QK_B), cos, sin_lo, sin_hi, 1) * (DH_B ** -0.5)).astype(BF16)
    kf_ref[...] = _rope(proj(3 * D_CONV + QK_B, QK_B), cos, sin_lo, sin_hi, 1)
    vf_ref[...] = proj(3 * D_CONV + 2 * QK_B, V_B)


def _rope_tables(pos):
    half = ROT_DIM // 2
    inv = ROPE_THETA ** (-jnp.arange(half, dtype=F32) / half)
    ang = pos.astype(F32)[:, None] * inv[None, :]
    cos, sin = jnp.cos(ang), jnp.sin(ang)
    t = pos.shape[0]
    ones = jnp.ones((t, DH_B - ROT_DIM), F32)
    zeros = jnp.zeros((t, DH_B - ROT_DIM), F32)
    zh = jnp.zeros((t, half), F32)
    c = jnp.concatenate([cos, cos, ones], axis=1)
    s_lo = jnp.concatenate([-sin, zh, zeros], axis=1)
    s_hi = jnp.concatenate([zh, sin, zeros], axis=1)
    tile2 = lambda a: jnp.concatenate([a, a], axis=1)
    return tile2(c), tile2(s_lo), tile2(s_hi)


def _even_in_prompt(x, w_bf, wqt_bf, wvt_bf, conv_w, conv_prev, tables):
    b, t, _ = x.shape
    tm = ROW_TILE
    row3 = lambda n: pl.BlockSpec((1, tm, n), lambda i, j: (i, j, 0))
    col3 = lambda n: pl.BlockSpec((1, n, tm), lambda i, j: (i, 0, j))
    full2 = lambda a: pl.BlockSpec(a.shape, lambda i, j: (0, 0))
    tab = pl.BlockSpec((tm, LANES), lambda i, j: (j, 0))
    tab_t = pl.BlockSpec((LANES, tm), lambda i, j: (0, j))
    st = pl.BlockSpec((1, CONV_W - 1, D_CONV), lambda i, j: (i, 0, 0))
    tables_t = [jnp.transpose(a) for a in tables]
    outs = [jax.ShapeDtypeStruct((b, t, D_CONV), BF16), jax.ShapeDtypeStruct((b, QK_B, t), BF16),
            jax.ShapeDtypeStruct((b, t, QK_B), F32), jax.ShapeDtypeStruct((b, t, V_B), F32),
            jax.ShapeDtypeStruct((b, t, QK_B), BF16), jax.ShapeDtypeStruct((b, V_B, t), BF16),
            jax.ShapeDtypeStruct((b, CONV_W - 1, D_CONV), F32)]
    return pl.pallas_call(
        _even_in_prompt_kernel,
        grid=(b, t // tm),
        in_specs=[row3(D_MODEL), full2(w_bf), full2(wqt_bf), full2(wvt_bf), full2(conv_w), st,
                  tab, tab, tab, tab_t, tab_t, tab_t],
        out_specs=[row3(D_CONV), col3(QK_B), row3(QK_B), row3(V_B), row3(QK_B), col3(V_B), st],
        out_shape=outs,
        scratch_shapes=[pltpu.VMEM((CONV_W - 1, D_CONV), F32)],
        compiler_params=_cparams("arbitrary", "arbitrary"),
        name="even_in_prompt",
    )(x, w_bf, wqt_bf, wvt_bf, conv_w, conv_prev, *tables, *tables_t)


def _even_in_decode(x, w_bf, conv_w, conv_prev_t, tables):
    n = x.shape[0]
    full = lambda a: pl.BlockSpec(a.shape, lambda i: (0,) * a.ndim)
    o2 = lambda c, dt: jax.ShapeDtypeStruct((n, c), dt)
    outs = [o2(D_CONV, BF16), o2(QK_B, BF16), o2(QK_B, F32), o2(V_B, F32), o2(D_CONV, F32)]
    ins = [x, w_bf, conv_w, conv_prev_t, *tables]
    return pl.pallas_call(
        _even_in_decode_kernel,
        grid=(1,),
        in_specs=[full(a) for a in ins],
        out_specs=[pl.BlockSpec(o.shape, lambda i: (0, 0)) for o in outs],
        out_shape=outs,
        compiler_params=_cparams("arbitrary"),
        name="even_in_decode",
    )(*ins)


def _lambda_value(lam_ref, lam_init):
    lv = lam_ref[...]
    a = jnp.sum(lv[0:1, :] * lv[1:2, :], axis=1, keepdims=True)
    b = jnp.sum(lv[2:3, :] * lv[3:4, :], axis=1, keepdims=True)
    return jnp.exp(a) - jnp.exp(b) + lam_init


def _sub_norm(o, sub_ref, lam_init):
    ms = jnp.mean(o * o, axis=-1, keepdims=True)
    return o * lax.rsqrt(ms + LN_EPS) * sub_ref[...] * (1.0 - lam_init)


def _attn_prompt_kernel(qt_ref, k_ref, vt_ref, lam_ref, sub_ref, o_ref, m_scr, l_scr, acc_scr, *, lam_init):
    i = pl.program_id(2)
    tq = qt_ref.shape[2]
    tk = tq
    qt = qt_ref[0].astype(F32)
    feat = lax.broadcasted_iota(I32, qt.shape, 0)
    qq = jnp.concatenate([jnp.where(feat < DH_B, qt, 0.0), jnp.where(feat >= DH_B, qt, 0.0)],
                         axis=1).astype(BF16)
    m_scr[...] = jnp.full(m_scr.shape, -jnp.inf, F32)
    l_scr[...] = jnp.zeros(l_scr.shape, F32)
    acc_scr[...] = jnp.zeros(acc_scr.shape, F32)

    def step(j, masked):
        start = pl.multiple_of(j * tk, tk)
        kj = k_ref[0, pl.ds(start, tk), :]
        vtj = vt_ref[0, :, pl.ds(start, tk)]
        s = _dot(kj, qq)
        if masked:
            key = lax.broadcasted_iota(I32, s.shape, 0)
            qry = lax.broadcasted_iota(I32, s.shape, 1)
            qry = jnp.where(qry >= tq, qry - tq, qry)
            s = jnp.where(key <= qry, s, -jnp.inf)
        m_prev = m_scr[...]
        m_new = jnp.maximum(m_prev, jnp.max(s, axis=0, keepdims=True))
        alpha = jnp.exp(m_prev - m_new)
        p = jnp.exp(s - m_new)
        l_scr[...] = alpha * l_scr[...] + jnp.sum(p, axis=0, keepdims=True)
        acc_scr[...] = alpha * acc_scr[...] + _dot(vtj, p.astype(BF16))
        m_scr[...] = m_new

    def body(j, carry):
        step(j, False)
        return carry

    lax.fori_loop(0, i, body, 0)
    step(i, True)
    on = acc_scr[...] / l_scr[...]
    lam = _lambda_value(lam_ref, lam_init)
    o = jnp.transpose(on[:, 0:tq] - lam * on[:, tq:2 * tq])
    o_ref[0] = _sub_norm(o, sub_ref, lam_init).astype(BF16)


def _attn_prompt(qt, k, vt, lam_vecs, subln, lam_init):
    b, t, _ = k.shape
    tq = ATTN_TILE
    full = lambda a: pl.BlockSpec(a.shape, lambda bi, h, i: (0, 0))
    return pl.pallas_call(
        functools.partial(_attn_prompt_kernel, lam_init=lam_init),
        grid=(b, H_B, t // tq),
        in_specs=[pl.BlockSpec((1, LANES, tq), lambda bi, h, i: (bi, h, i)),
                  pl.BlockSpec((1, t, LANES), lambda bi, h, i: (bi, 0, h)),
                  pl.BlockSpec((1, LANES, t), lambda bi, h, i: (bi, h, 0)),
                  full(lam_vecs), full(subln)],
        out_specs=pl.BlockSpec((1, tq, LANES), lambda bi, h, i: (bi, i, h)),
        out_shape=jax.ShapeDtypeStruct((b, t, V_B), BF16),
        scratch_shapes=[pltpu.VMEM((1, 2 * tq), F32), pltpu.VMEM((1, 2 * tq), F32),
                        pltpu.VMEM((LANES, 2 * tq), F32)],
        compiler_params=_cparams("arbitrary", "arbitrary", "arbitrary"),
        name="attn_prompt",
    )(qt, k, vt, lam_vecs, subln)


def _attn_decode_kernel(pt_ref, qbd_ref, kn_ref, vn_ref, lam_ref, sub_ref, *rest, lam_init, n_pages):
    del pt_ref
    k_refs = rest[:n_pages]
    v_refs = rest[n_pages:2 * n_pages]
    o_ref, m_scr, l_scr, acc_scr = rest[2 * n_pages:]
    j = pl.program_id(1)
    qbd = qbd_ref[0]

    @pl.when(j == 0)
    def _():
        s_new = jnp.sum(qbd.astype(F32) * kn_ref[0], axis=1, keepdims=True)
        m_scr[...] = s_new
        l_scr[...] = jnp.ones(l_scr.shape, F32)
        acc_scr[...] = jnp.broadcast_to(vn_ref[0], acc_scr.shape)

    s = jnp.concatenate([_dot(qbd, k_refs[r][0].astype(BF16)) for r in range(n_pages)], axis=1)
    m_prev = m_scr[...]
    m_new = jnp.maximum(m_prev, jnp.max(s, axis=1, keepdims=True))
    alpha = jnp.exp(m_prev - m_new)
    p = jnp.exp(s - m_new)
    l_scr[...] = alpha * l_scr[...] + jnp.sum(p, axis=1, keepdims=True)

    def head_pv(h):
        acc = None
        for r in range(n_pages):
            vh = v_refs[r][0, pl.ds(h, PAGE_SIZE, stride=H_B), :].astype(BF16)
            term = _dot(p[:, r * PAGE_SIZE:(r + 1) * PAGE_SIZE].astype(BF16), vh)
            acc = term if acc is None else acc + term
        return acc

    pv = jnp.concatenate([head_pv(h) for h in range(H_B)], axis=1)
    acc_scr[...] = alpha * acc_scr[...] + pv
    m_scr[...] = m_new

    @pl.when(j == pl.num_programs(1) - 1)
    def _():
        on = acc_scr[...] / l_scr[...]
        row = lax.broadcasted_iota(I32, (2 * H_B, 2 * DH_B), 0)
        head = jnp.where(row >= H_B, row - H_B, row)
        o8 = jnp.zeros((2 * H_B, 2 * DH_B), F32)
        for c in range(H_B):
            o8 = o8 + jnp.where(head == c, on[:, c * 2 * DH_B:(c + 1) * 2 * DH_B], 0.0)
        lam = _lambda_value(lam_ref, lam_init)
        o = o8 - lam * pltpu.roll(o8, H_B, 0)
        o_ref[0] = _sub_norm(o, sub_ref, lam_init)


def _attn_decode(qbd, k_new, v_new, cache_k, cache_v, page_table, lam_vecs, subln, lam_init):
    n = qbd.shape[0]
    n_pages = page_table.shape[1]
    pp = PAGES_PER_STEP
    width = V_B
    c2 = lambda a: pl.BlockSpec(a.shape, lambda b, j, pt: (0, 0))
    per_b = lambda a: pl.BlockSpec((1,) + a.shape[1:], lambda b, j, pt: (b, 0, 0))

    def page(r, arr):
        return pl.BlockSpec((1,) + arr.shape[1:], lambda b, j, pt: (pt[b, j * pp + r], 0, 0))

    grid_spec = pltpu.PrefetchScalarGridSpec(
        num_scalar_prefetch=1,
        grid=(n, n_pages // pp),
        in_specs=[per_b(qbd), per_b(k_new), per_b(v_new), c2(lam_vecs), c2(subln)]
        + [page(r, cache_k) for r in range(pp)] + [page(r, cache_v) for r in range(pp)],
        out_specs=pl.BlockSpec((1, 2 * H_B, 2 * DH_B), lambda b, j, pt: (b, 0, 0)),
        scratch_shapes=[pltpu.VMEM((2 * H_B, 1), F32), pltpu.VMEM((2 * H_B, 1), F32),
                        pltpu.VMEM((2 * H_B, width), F32)],
    )
    return pl.pallas_call(
        functools.partial(_attn_decode_kernel, lam_init=lam_init, n_pages=pp),
        grid_spec=grid_spec,
        out_shape=jax.ShapeDtypeStruct((n, 2 * H_B, 2 * DH_B), F32),
        compiler_params=_cparams("arbitrary", "arbitrary"),
        name="attn_decode",
    )(page_table, qbd, k_new, v_new, lam_vecs, subln, *([cache_k] * pp), *([cache_v] * pp))


def _odd_in_kernel(x_ref, w_ref, wg_ref, wgt_ref, bg_ref, bgt_ref,
                   q_ref, k_ref, v_ref, o_ref, gc_ref, gr_ref, *, decode):
    xb = x_ref[0].astype(BF16) if not decode else x_ref[...].astype(BF16)
    qw = H_C * DK_C
    vw = H_C * DV_C
    q = _dot(xb, w_ref[:, 0:qw]) * (DK_C ** -0.5)
    k = _dot(xb, w_ref[:, qw:2 * qw])
    v = _dot(xb, w_ref[:, 2 * qw:2 * qw + vw])
    o = _dot(xb, w_ref[:, 2 * qw + vw:2 * qw + 2 * vw])
    g_col = _dot(xb, wg_ref[...]) + bg_ref[...]
    lane = lax.broadcasted_iota(I32, g_col.shape, 1)
    g_col = jnp.where(lane < H_C, g_col, _log_sigmoid(g_col))
    g_row = _dot_nt(wgt_ref[...], xb) + bgt_ref[:, 0:1]
    sub = lax.broadcasted_iota(I32, g_row.shape, 0)
    g_row = jnp.where(sub < H_C, g_row, _log_sigmoid(g_row))
    if decode:
        q_ref[...] = q.astype(BF16)
        k_ref[...] = k.astype(BF16)
        v_ref[...] = v.astype(BF16)
        o_ref[...] = o
        gc_ref[...] = g_col
        gr_ref[...] = g_row
    else:
        q_ref[0] = q.astype(BF16)
        k_ref[0] = k.astype(BF16)
        v_ref[0] = v.astype(BF16)
        o_ref[0] = o
        gc_ref[0] = g_col
        gr_ref[0] = g_row


def _odd_in(x, w_bf, wg, wgt, bg, bgt, decode):
    qw, vw = H_C * DK_C, H_C * DV_C
    if decode:
        n = x.shape[0]
        ins = [x, w_bf, wg, wgt, bg, bgt]
        outs = [jax.ShapeDtypeStruct((n, qw), BF16), jax.ShapeDtypeStruct((n, qw), BF16),
                jax.ShapeDtypeStruct((n, vw), BF16), jax.ShapeDtypeStruct((n, vw), F32),
                jax.ShapeDtypeStruct((n, LANES), F32), jax.ShapeDtypeStruct((2 * H_C, n), F32)]
        return pl.pallas_call(
            functools.partial(_odd_in_kernel, decode=True),
            grid=(1,),
            in_specs=[pl.BlockSpec(a.shape, lambda i: (0, 0)) for a in ins],
            out_specs=[pl.BlockSpec(o.shape, lambda i: (0, 0)) for o in outs],
            out_shape=outs,
            compiler_params=_cparams("arbitrary"),
            name="odd_in_decode",
        )(*ins)
    b, t, _ = x.shape
    tm = ROW_TILE
    row3 = lambda n: pl.BlockSpec((1, tm, n), lambda i, j: (i, j, 0))
    full2 = lambda a: pl.BlockSpec(a.shape, lambda i, j: (0, 0))
    outs = [jax.ShapeDtypeStruct((b, t, qw), BF16), jax.ShapeDtypeStruct((b, t, qw), BF16),
            jax.ShapeDtypeStruct((b, t, vw), BF16), jax.ShapeDtypeStruct((b, t, vw), F32),
            jax.ShapeDtypeStruct((b, t, LANES), F32), jax.ShapeDtypeStruct((b, 2 * H_C, t), F32)]
    return pl.pallas_call(
        functools.partial(_odd_in_kernel, decode=False),
        grid=(b, t // tm),
        in_specs=[row3(D_MODEL), full2(w_bf), full2(wg), full2(wgt), full2(bg), full2(bgt)],
        out_specs=[row3(qw), row3(qw), row3(vw), row3(vw), row3(LANES),
                   pl.BlockSpec((1, 2 * H_C, tm), lambda i, j: (i, 0, j))],
        out_shape=outs,
        compiler_params=_cparams("arbitrary", "arbitrary"),
        name="odd_in_prompt",
    )(x, w_bf, wg, wgt, bg, bgt)


def _mlstm_kernel(q_ref, k_ref, v_ref, o_ref, gc_ref, gr_ref, nw_ref, c0_ref, n0_ref, m0_ref,
                  h_ref, c_out, n_out, m_out, c_scr, n_scr, m_scr):
    ci = pl.program_id(1)
    chunk = q_ref.shape[1]

    @pl.when(ci == 0)
    def _():
        c_scr[...] = c0_ref[0]
        n_scr[...] = n0_ref[0]
        m_scr[...] = m0_ref[0]

    t_idx = lax.broadcasted_iota(I32, (chunk, chunk), 0)
    s_idx = lax.broadcasted_iota(I32, (chunk, chunk), 1)
    causal = s_idx <= t_idx
    for h in range(H_C):
        q = q_ref[0, :, h * DK_C:(h + 1) * DK_C]
        k = k_ref[0, :, h * DK_C:(h + 1) * DK_C]
        v = v_ref[0, :, h * DV_C:(h + 1) * DV_C]
        ig_r = gr_ref[0, h:h + 1, :]
        lf_r = gr_ref[0, H_C + h:H_C + h + 1, :]
        ig_c = gc_ref[0, :, h:h + 1]
        lf_c = gc_ref[0, :, H_C + h:H_C + h + 1]
        bcum_c = jnp.sum(jnp.where(causal, lf_r, 0.0), axis=1, keepdims=True)
        bcum_r = jnp.sum(jnp.where(t_idx <= s_idx, lf_c, 0.0), axis=0, keepdims=True)
        m0 = m_scr[h:h + 1, 0:1]
        dmat = jnp.where(causal, bcum_c - bcum_r + ig_r, -jnp.inf)
        inter = bcum_c + m0
        m = jnp.maximum(inter, jnp.max(dmat, axis=1, keepdims=True))
        w = jnp.exp(dmat - m)
        g = jnp.exp(inter - m)
        s = _dot_nt(q, k) * w
        c0 = c_scr[h]
        n0 = n_scr[h:h + 1, :]
        num = g * _dot(q, c0.astype(BF16)) + _dot(s.astype(BF16), v)
        den = g * jnp.sum(q.astype(F32) * n0, axis=1, keepdims=True) + jnp.sum(s, axis=1, keepdims=True)
        hid = num / jnp.maximum(jnp.abs(den), jnp.exp(-m))
        m_last = m[chunk - 1:chunk, :]
        b_last = bcum_c[chunk - 1:chunk, :]
        w_last = jnp.exp(b_last - bcum_c + ig_c - m_last)
        g_last = jnp.exp(b_last + m0 - m_last)
        kw = k.astype(F32) * w_last
        c_scr[h] = g_last * c0 + _dot_tn(kw.astype(BF16), v)
        n_scr[h:h + 1, :] = g_last * n0 + jnp.sum(kw, axis=0, keepdims=True)
        m_scr[h:h + 1, :] = jnp.broadcast_to(m_last, (1, LANES))
        mu = jnp.mean(hid, axis=1, keepdims=True)
        hc = hid - mu
        var = jnp.mean(hc * hc, axis=1, keepdims=True)
        hn = hc * lax.rsqrt(var + LN_EPS) * nw_ref[:, h * DV_C:(h + 1) * DV_C]
        gate = jax.nn.sigmoid(o_ref[0, :, h * DV_C:(h + 1) * DV_C])
        h_ref[0, :, h * DV_C:(h + 1) * DV_C] = (gate * hn).astype(BF16)

    c_out[0] = c_scr[...]
    n_out[0] = n_scr[...]
    m_out[0] = m_scr[...]


def _mlstm(q, k, v, o, g_col, g_row, norm_w, c0, n0, m0):
    b, t, _ = q.shape
    ch = MLSTM_CHUNK
    qw, vw = H_C * DK_C, H_C * DV_C
    row = lambda n: pl.BlockSpec((1, ch, n), lambda i, j: (i, j, 0))
    st4 = pl.BlockSpec((1, H_C, DK_C, DV_C), lambda i, j: (i, 0, 0, 0))
    st3 = pl.BlockSpec((1, H_C, LANES), lambda i, j: (i, 0, 0))
    outs = [jax.ShapeDtypeStruct((b, t, vw), BF16), jax.ShapeDtypeStruct((b, H_C, DK_C, DV_C), F32),
            jax.ShapeDtypeStruct((b, H_C, DK_C), F32), jax.ShapeDtypeStruct((b, H_C, LANES), F32)]
    return pl.pallas_call(
        _mlstm_kernel,
        grid=(b, t // ch),
        in_specs=[row(qw), row(qw), row(vw), row(vw), row(LANES),
                  pl.BlockSpec((1, 2 * H_C, ch), lambda i, j: (i, 0, j)),
                  pl.BlockSpec(norm_w.shape, lambda i, j: (0, 0)), st4, st3, st3],
        out_specs=[row(vw), st4, st3, st3],
        out_shape=outs,
        scratch_shapes=[pltpu.VMEM((H_C, DK_C, DV_C), F32), pltpu.VMEM((H_C, DK_C), F32),
                        pltpu.VMEM((H_C, LANES), F32)],
        compiler_params=_cparams("arbitrary", "arbitrary"),
        name="mlstm",
    )(q, k, v, o, g_col, g_row, norm_w, c0, n0, m0)


def _mixed_rows(acts, w_refs, x, g_ref, b_ref, rhi_ref, rlo_ref):
    y = _dot(acts[0], w_refs[0][...])
    for a, w in zip(acts[1:], w_refs[1:]):
        y = y + _dot(a, w[...])
    x1 = _layer_norm_rows(DEEPNORM_ALPHA * x + y, g_ref[...], b_ref[...])

    xh = x1.astype(BF16)
    xl = (x1 - xh.astype(F32)).astype(BF16)
    lg = _dot(xh, rhi_ref[...]) + _dot(xl, rhi_ref[...]) + _dot(xh, rlo_ref[...])
    lane = lax.broadcasted_iota(I32, lg.shape, 1)
    big = jnp.int32(4 * LANES)
    neg = -jnp.inf
    gl = jnp.where(lane < ROUTE_GROUP_LANE + N_GROUPS, lg, neg)
    g_max = jnp.max(gl, axis=1, keepdims=True)
    g_w = 1.0 / jnp.sum(jnp.exp(gl - g_max), axis=1, keepdims=True)
    g_idx = jnp.min(jnp.where(gl == g_max, lane, big), axis=1, keepdims=True)
    lane_group = (lane - ROUTE_EXPERT_LANE) >> 2
    el = jnp.where(lane_group == g_idx, lg, neg)
    e1 = jnp.max(el, axis=1, keepdims=True)
    i1 = jnp.min(jnp.where(el == e1, lane, big), axis=1, keepdims=True)
    z = jnp.sum(jnp.exp(el - e1), axis=1, keepdims=True)
    el2 = jnp.where(lane == i1, neg, el)
    e2 = jnp.max(el2, axis=1, keepdims=True)
    i2 = jnp.min(jnp.where(el2 == e2, lane, big), axis=1, keepdims=True)
    p1 = 1.0 / z
    p2 = jnp.exp(e2 - e1) / z
    w1 = p1 / (p1 + p2) * g_w
    w2 = p2 / (p1 + p2) * g_w
    id1 = (i1 - ROUTE_EXPERT_LANE).astype(F32)
    id2 = (i2 - ROUTE_EXPERT_LANE).astype(F32)
    rt = jnp.where(lane == 0, w1, jnp.where(lane == 1, w2,
                   jnp.where(lane == 2, id1, jnp.where(lane == 3, id2, 0.0))))
    return x1, rt


def _mix_out_kernel(*refs, n_in, n_s):
    ap_refs = refs[:n_in]
    as_refs = refs[n_in:2 * n_in]
    w_refs = refs[2 * n_in:3 * n_in]
    xp_ref, xs_ref, g_ref, b_ref, rhi_ref, rlo_ref, out_ref = refs[3 * n_in:]
    i = pl.program_id(0)
    last = pl.num_programs(0) - 1

    @pl.when(i < last)
    def _():
        x1, rt = _mixed_rows([a[...] for a in ap_refs], w_refs, xp_ref[...], g_ref, b_ref, rhi_ref, rlo_ref)
        out_ref[:, 0:D_MODEL] = x1
        out_ref[:, D_MODEL:D_MODEL + LANES] = rt

    @pl.when(i == last)
    def _():
        x1, rt = _mixed_rows([a[...] for a in as_refs], w_refs, xs_ref[...], g_ref, b_ref, rhi_ref, rlo_ref)
        out_ref[0:n_s, 0:D_MODEL] = x1
        out_ref[0:n_s, D_MODEL:D_MODEL + LANES] = rt


def _mix_out(acts_p, acts_s, weights, xp, xs, ln_g, ln_b, r_hi, r_lo):
    n_p, n_s = xp.shape[0], xs.shape[0]
    tm = ROW_TILE
    nb = n_p // tm
    n_in = len(acts_p)
    prow = lambda n: pl.BlockSpec((tm, n), lambda i: (jnp.minimum(i, nb - 1), 0))
    full = lambda a: pl.BlockSpec(a.shape, lambda i: (0, 0))
    width = D_MODEL + LANES
    return pl.pallas_call(
        functools.partial(_mix_out_kernel, n_in=n_in, n_s=n_s),
        grid=(nb + 1,),
        in_specs=[prow(a.shape[1]) for a in acts_p] + [full(a) for a in acts_s] + [full(w) for w in weights]
        + [prow(D_MODEL), full(xs), full(ln_g), full(ln_b), full(r_hi), full(r_lo)],
        out_specs=pl.BlockSpec((tm, width), lambda i: (i, 0)),
        out_shape=jax.ShapeDtypeStruct((n_p + n_s, width), F32),
        compiler_params=_cparams("arbitrary"),
        name="mix_out",
    )(*acts_p, *acts_s, *weights, xp, xs, ln_g, ln_b, r_hi, r_lo)


def _route_plan(rt, tm):
    n = rt.shape[0]
    e1, e2 = rt[:, 2].astype(I32), rt[:, 3].astype(I32)
    ea, eb = jnp.minimum(e1, e2), jnp.maximum(e1, e2)
    la, lb = ea % EXP_PER_GROUP, eb % EXP_PER_GROUP
    cls = (ea // EXP_PER_GROUP) * N_PAIRS + (la * (2 * EXP_PER_GROUP - 1 - la)) // 2 + (lb - la - 1)
    onehot = (cls[:, None] == jnp.arange(N_CLASSES, dtype=I32)[None, :]).astype(I32)
    csum = jnp.cumsum(onehot, axis=0)
    rank = jnp.sum(onehot * csum, axis=1) - 1
    cnt = csum[-1]
    ntile = (cnt + tm - 1) // tm
    tile_end = jnp.cumsum(ntile)
    tile_start = tile_end - ntile
    n_used = tile_end[-1]
    n_tiles = -(-(n + N_CLASSES * (tm - 1)) // tm)
    pos = (jnp.sum(onehot * tile_start[None, :], axis=1) * tm + rank).astype(I32)
    pair_lo = np.array([a for a in range(EXP_PER_GROUP) for b in range(a + 1, EXP_PER_GROUP)], np.int32)
    pair_hi = np.array([b for a in range(EXP_PER_GROUP) for b in range(a + 1, EXP_PER_GROUP)], np.int32)
    cls_ids = np.arange(N_CLASSES)
    cls_a = jnp.asarray((cls_ids // N_PAIRS) * EXP_PER_GROUP + pair_lo[cls_ids % N_PAIRS], I32)
    cls_b = jnp.asarray((cls_ids // N_PAIRS) * EXP_PER_GROUP + pair_hi[cls_ids % N_PAIRS], I32)
    tile_ids = jnp.arange(n_tiles, dtype=I32)
    tile_cls = jnp.sum((tile_end[None, :] <= jnp.minimum(tile_ids, n_used - 1)[:, None]).astype(I32), axis=1)
    tile_cls = jnp.minimum(tile_cls, N_CLASSES - 1)
    onehot_t = (tile_cls[:, None] == jnp.arange(N_CLASSES, dtype=I32)[None, :]).astype(I32)
    last_tile = jnp.where(ntile > 0, tile_end - 1, -1).astype(I32)
    return dict(pos=pos, ta=jnp.sum(onehot_t * cls_a[None, :], axis=1),
                tb=jnp.sum(onehot_t * cls_b[None, :], axis=1), nu=n_used.reshape(1).astype(I32),
                last_tile=last_tile, n_tiles=n_tiles)


def _dispatch_kernel(lt_ref, nu_ref, pos_ref, src_hbm, dst_hbm, zbuf, zsem, sems, *, moe_tile, n_tiles):
    i = pl.program_id(0)
    n = pl.num_programs(0)
    td = pos_ref.shape[2]
    slot = lax.rem(i, 2)

    def zero_copy(tile):
        start = pl.multiple_of(tile * moe_tile, moe_tile)
        return pltpu.make_async_copy(zbuf, dst_hbm.at[pl.ds(start, moe_tile)], zsem)

    @pl.when(i == 0)
    def _():
        zbuf[...] = jnp.zeros(zbuf.shape, F32)
        for c in range(N_CLASSES):
            @pl.when(lt_ref[c] >= 0)
            def _(c=c):
                zero_copy(lt_ref[c]).start()
        for c in range(N_CLASSES):
            @pl.when(lt_ref[c] >= 0)
            def _(c=c):
                zero_copy(lt_ref[c]).wait()

        def spare_start(t, carry):
            zero_copy(t).start()
            return carry

        def spare_wait(t, carry):
            zero_copy(t).wait()
            return carry

        lax.fori_loop(nu_ref[0], n_tiles, spare_start, 0)
        lax.fori_loop(nu_ref[0], n_tiles, spare_wait, 0)

    def row_copy(r):
        return pltpu.make_async_copy(src_hbm.at[pl.ds(i * td + r, 1)], dst_hbm.at[pl.ds(pos_ref[0, 0, r], 1)],
                                     sems.at[slot])

    def start_row(r, carry):
        row_copy(r).start()
        return carry

    lax.fori_loop(0, td, start_row, 0, unroll=8)

    def wait_step(s):
        pltpu.make_async_copy(src_hbm.at[pl.ds(0, td)], dst_hbm.at[pl.ds(0, td)], sems.at[s]).wait()

    @pl.when(i > 0)
    def _():
        wait_step(1 - slot)

    @pl.when(i == n - 1)
    def _():
        wait_step(slot)


def _largest_divisor_tile(n, cap):
    for t in range(cap - cap % 8, 7, -8):
        if n % t == 0:
            return t
    raise ValueError(f"no row tile for {n} rows")


def _dispatch(x1e, plan, tm):
    n, width = x1e.shape
    n_tiles = plan["n_tiles"]
    td = _largest_divisor_tile(n, 1024)
    steps = n // td
    grid_spec = pltpu.PrefetchScalarGridSpec(
        num_scalar_prefetch=2,
        grid=(steps,),
        in_specs=[pl.BlockSpec((1, 1, td), lambda i, lt, nu: (i, 0, 0), memory_space=pltpu.SMEM),
                  pl.BlockSpec(memory_space=pl.ANY)],
        out_specs=pl.BlockSpec(memory_space=pl.ANY),
        scratch_shapes=[pltpu.VMEM((tm, width), F32), pltpu.SemaphoreType.DMA(()),
                        pltpu.SemaphoreType.DMA((2,))],
    )
    return pl.pallas_call(
        functools.partial(_dispatch_kernel, moe_tile=tm, n_tiles=n_tiles),
        grid_spec=grid_spec,
        out_shape=jax.ShapeDtypeStruct((n_tiles * tm, width), F32),
        compiler_params=_cparams("arbitrary"),
        name="dispatch",
    )(plan["last_tile"], plan["nu"], plan["pos"].reshape(steps, 1, td), x1e)


def _moe_kernel(ta_ref, tb_ref, nu_ref, x_ref, ga_ref, ua_ref, da_ref, gb_ref, ub_ref, db_ref, lg_ref, lb_ref,
                o_ref):
    del ta_ref, tb_ref
    g = pl.program_id(0)

    @pl.when(g < nu_ref[0])
    def _():
        x = x_ref[:, 0:D_MODEL]
        slab = x_ref[:, D_MODEL:D_MODEL + LANES]
        w1, w2, e1, e2 = slab[:, 0:1], slab[:, 1:2], slab[:, 2:3], slab[:, 3:4]
        first = e1 < e2
        wa = jnp.where(first, w1, w2)
        wb = jnp.where(first, w2, w1)
        xb = x.astype(BF16)

        def expert(gw, uw, dw):
            hid = jax.nn.silu(_dot(xb, gw[0])) * _dot(xb, uw[0])
            return _dot(hid.astype(BF16), dw[0])

        y = wa * expert(ga_ref, ua_ref, da_ref)
        y = y + wb * expert(gb_ref, ub_ref, db_ref)
        o_ref[...] = _layer_norm_rows(DEEPNORM_ALPHA * x + y, lg_ref[...], lb_ref[...])

    @pl.when(g >= nu_ref[0])
    def _():
        o_ref[...] = jnp.zeros(o_ref.shape, F32)


def _moe(xs_sorted, plan, wg_bf, wu_bf, wd_bf, ln_g, ln_b, tm):
    n_tiles = plan["n_tiles"]
    width = xs_sorted.shape[1]
    up_a = pl.BlockSpec((1, D_MODEL, D_EXPERT), lambda g, ta, tb, nu: (ta[g], 0, 0))
    dn_a = pl.BlockSpec((1, D_EXPERT, D_MODEL), lambda g, ta, tb, nu: (ta[g], 0, 0))
    up_b = pl.BlockSpec((1, D_MODEL, D_EXPERT), lambda g, ta, tb, nu: (tb[g], 0, 0))
    dn_b = pl.BlockSpec((1, D_EXPERT, D_MODEL), lambda g, ta, tb, nu: (tb[g], 0, 0))
    vec = pl.BlockSpec((1, D_MODEL), lambda g, ta, tb, nu: (0, 0))
    grid_spec = pltpu.PrefetchScalarGridSpec(
        num_scalar_prefetch=3,
        grid=(n_tiles,),
        in_specs=[pl.BlockSpec((tm, width), lambda g, ta, tb, nu: (g, 0)),
                  up_a, up_a, dn_a, up_b, up_b, dn_b, vec, vec],
        out_specs=pl.BlockSpec((tm, D_MODEL), lambda g, ta, tb, nu: (g, 0)),
    )
    return pl.pallas_call(
        _moe_kernel,
        grid_spec=grid_spec,
        out_shape=jax.ShapeDtypeStruct((n_tiles * tm, D_MODEL), F32),
        compiler_params=_cparams("arbitrary"),
        name="moe",
    )(plan["ta"], plan["tb"], plan["nu"], xs_sorted, wg_bf, wu_bf, wd_bf, wg_bf, wu_bf, wd_bf, ln_g, ln_b)


def _ple_kernel(pos_ref, posn_ref, x2_hbm, pp_ref, ps_ref, wg_ref, wp_ref, op_ref, os_ref, buf, sems, *, n_s):
    i = pl.program_id(0)
    last = pl.num_programs(0) - 1
    tm = buf.shape[1]
    slot = lax.rem(i, 2)

    def issue(idx_ref, s):
        def body(r, carry):
            pltpu.make_async_copy(x2_hbm.at[pl.ds(idx_ref[0, 0, r], 1)], buf.at[s, pl.ds(r, 1)], sems.at[s]).start()
            return carry

        lax.fori_loop(0, tm, body, 0, unroll=8)

    @pl.when(i == 0)
    def _():
        issue(pos_ref, slot)

    @pl.when(i < last)
    def _():
        issue(posn_ref, 1 - slot)

    pltpu.make_async_copy(x2_hbm.at[pl.ds(0, tm)], buf.at[slot], sems.at[slot]).wait()

    def rows(x, p):
        gate = jax.nn.sigmoid(_dot(x.astype(BF16), wg_ref[...]))
        return x + gate * _dot(p.astype(BF16), wp_ref[...])

    @pl.when(i < last)
    def _():
        op_ref[...] = rows(buf[slot], pp_ref[...])

    @pl.when(i == last)
    def _():
        os_ref[...] = rows(buf[slot, 0:n_s, :], ps_ref[...])


def _ple(x2_sorted, pos, p_p, p_s, wg_bf, wp_bf):
    n_p, n_s = p_p.shape[0], p_s.shape[0]
    tm = ROW_TILE
    nb = n_p // tm
    steps = nb + 1
    pos_pad = jnp.zeros((steps * tm,), I32).at[:n_p + n_s].set(pos).reshape(steps, 1, tm)
    full = lambda a: pl.BlockSpec(a.shape, lambda i: (0, 0))
    prow = lambda n: pl.BlockSpec((tm, n), lambda i: (jnp.minimum(i, nb - 1), 0))
    return pl.pallas_call(
        functools.partial(_ple_kernel, n_s=n_s),
        grid=(steps,),
        in_specs=[pl.BlockSpec((1, 1, tm), lambda i: (i, 0, 0), memory_space=pltpu.SMEM),
                  pl.BlockSpec((1, 1, tm), lambda i: (jnp.minimum(i + 1, nb), 0, 0), memory_space=pltpu.SMEM),
                  pl.BlockSpec(memory_space=pl.ANY), prow(D_PLE), full(p_s), full(wg_bf), full(wp_bf)],
        out_specs=[prow(D_MODEL), pl.BlockSpec((n_s, D_MODEL), lambda i: (0, 0))],
        out_shape=[jax.ShapeDtypeStruct((n_p, D_MODEL), F32), jax.ShapeDtypeStruct((n_s, D_MODEL), F32)],
        scratch_shapes=[pltpu.VMEM((2, tm, D_MODEL), F32), pltpu.SemaphoreType.DMA((2,))],
        compiler_params=_cparams("arbitrary"),
        name="ple",
    )(pos_pad, pos_pad, x2_sorted, p_p, p_s, wg_bf, wp_bf)


def _router_weights(w_group, w_router):
    wr = jnp.zeros((D_MODEL, LANES), F32)
    wr = wr.at[:, ROUTE_GROUP_LANE:ROUTE_GROUP_LANE + N_GROUPS].set(w_group)
    wr = wr.at[:, ROUTE_EXPERT_LANE:ROUTE_EXPERT_LANE + N_EXPERTS].set(w_router)
    hi = wr.astype(BF16)
    lo = (wr - hi.astype(F32)).astype(BF16)
    return hi, lo


def _layer_tail(i, acts_p, acts_s, w_list, xp, xs, p_p, p_s, ln_mix_g, ln_mix_b, ln_ffn_g, ln_ffn_b,
                w_group, w_router, w_exp_gate, w_exp_up, w_exp_down, w_ple_proj, w_ple_gate):
    r_hi, r_lo = _router_weights(w_group[i], w_router[i])
    x1e = _mix_out(acts_p, acts_s, w_list, xp, xs, ln_mix_g[i][None, :], ln_mix_b[i][None, :], r_hi, r_lo)
    plan = _route_plan(x1e[:, D_MODEL:D_MODEL + 4], MOE_TILE)
    xs_sorted = _dispatch(x1e, plan, MOE_TILE)
    x2_sorted = _moe(xs_sorted, plan, w_exp_gate[i].astype(BF16), w_exp_up[i].astype(BF16),
                     w_exp_down[i].astype(BF16), ln_ffn_g[i][None, :], ln_ffn_b[i][None, :], MOE_TILE)
    return _ple(x2_sorted, plan["pos"], p_p, p_s, w_ple_gate[i].astype(BF16), w_ple_proj[i].astype(BF16))


def kernel(x_prompt, x_sample, cache_k, cache_v, page_table, state_conv, state_mlstm_C, state_mlstm_n,
           state_mlstm_m, p_prompt, p_sample, w_in_even, conv_w, lambda_q1, lambda_k1, lambda_q2, lambda_k2,
           subln_w, w_out_even, w_in_odd, b_gates_odd, mh_norm_w, w_out_odd, ln_mix_g, ln_mix_b, ln_ffn_g,
           ln_ffn_b, w_group, w_router, w_exp_gate, w_exp_up, w_exp_down, w_ple_proj, w_ple_gate):
    bp, tp, _ = x_prompt.shape
    bs, ts, _ = x_sample.shape
    assert ts == 1 and tp % ROW_TILE == 0 and tp % ATTN_TILE == 0 and tp % MLSTM_CHUNK == 0
    n_p = bp * tp
    past_len = page_table.shape[1] * cache_k.shape[2]
    xp = x_prompt.reshape(n_p, D_MODEL)
    xs = x_sample.reshape(bs, D_MODEL)
    tail_w = (ln_mix_g, ln_mix_b, ln_ffn_g, ln_ffn_b, w_group, w_router, w_exp_gate, w_exp_up, w_exp_down,
              w_ple_proj, w_ple_gate)
    outs_p, outs_s = {}, {}
    for i in range(DEPTH):
        j = i // 2
        p_p = p_prompt[i].reshape(n_p, D_PLE)
        p_s = p_sample[i].reshape(bs, D_PLE)
        if i % 2 == 0:
            lam_init = 0.8 - 0.6 * math.exp(-0.3 * i)
            lam_vecs = jnp.stack([lambda_q1[j], lambda_k1[j], lambda_q2[j], lambda_k2[j]])
            sub = subln_w[j][None, :]
            w_bf = w_in_even[j].astype(BF16)
            tabs_p = _rope_tables(jnp.arange(tp))
            q0, v0 = 3 * D_CONV, 3 * D_CONV + 2 * QK_B
            wqt_bf = jnp.transpose(w_in_even[j][:, q0:q0 + QK_B]).astype(BF16)
            wvt_bf = jnp.transpose(w_in_even[j][:, v0:v0 + V_B]).astype(BF16)
            yc, qt, kf, vf, kb, vt, cst = _even_in_prompt(
                x_prompt if i == 0 else xp.reshape(bp, tp, D_MODEL), w_bf, wqt_bf, wvt_bf, conv_w[j],
                jnp.zeros((bp, CONV_W - 1, D_CONV), F32), tabs_p)
            o_p = _attn_prompt(qt, kb, vt, lam_vecs, sub, lam_init)
            outs_p.setdefault("k", []).append(kf.reshape(bp, tp, 2 * H_B, DH_B))
            outs_p.setdefault("v", []).append(vf.reshape(bp, tp, H_B, 2 * DH_B))
            outs_p.setdefault("c", []).append(cst)
            tabs_s = _rope_tables(jnp.full((1,), past_len, I32))
            prev_t = jnp.swapaxes(state_conv[j], 0, 1)
            yc_s, q_s, kf_s, vf_s, u_s = _even_in_decode(xs, w_bf, conv_w[j], prev_t, tabs_s)
            sub_head = jnp.arange(2 * H_B)
            sub_head = jnp.where(sub_head < H_B, 2 * sub_head, 2 * (sub_head - H_B) + 1)
            lane_head = jnp.arange(QK_B) // DH_B
            qbd = jnp.where(lane_head[None, None, :] == sub_head[None, :, None], q_s[:, None, :],
                            jnp.zeros((), BF16))
            n_pool = cache_k.shape[1]
            pages = cache_k.shape[0] * n_pool
            k_view = jnp.transpose(cache_k, (0, 1, 3, 4, 2)).reshape(pages, QK_B, PAGE_SIZE)
            v_view = cache_v.reshape(pages, PAGE_SIZE * H_B, 2 * DH_B)
            o8 = _attn_decode(qbd, kf_s[:, None, :], vf_s[:, None, :], k_view, v_view,
                              page_table + j * n_pool, lam_vecs, sub, lam_init)
            o_s = o8[:, :H_B, :].reshape(bs, V_B).astype(BF16)
            outs_s.setdefault("k", []).append(kf_s.reshape(bs, ts, 2 * H_B, DH_B))
            outs_s.setdefault("v", []).append(vf_s.reshape(bs, ts, H_B, 2 * DH_B))
            outs_s.setdefault("c", []).append(jnp.stack([state_conv[j][:, 1, :], u_s], axis=1))
            w_out = w_out_even[j].astype(BF16)
            w_list = [w_out[:D_CONV], w_out[D_CONV:]]
            acts_p = [yc.reshape(n_p, D_CONV), o_p.reshape(n_p, V_B)]
            acts_s = [yc_s, o_s]
        else:
            w_in = w_in_odd[j]
            qw, vw = H_C * DK_C, H_C * DV_C
            w_bf = w_in[:, :2 * qw + 2 * vw].astype(BF16)
            wg = jnp.zeros((D_MODEL, LANES), F32).at[:, :2 * H_C].set(w_in[:, 2 * qw + 2 * vw:]).astype(BF16)
            wgt = jnp.transpose(wg[:, :2 * H_C])
            bg = jnp.zeros((1, LANES), F32).at[0, :2 * H_C].set(b_gates_odd[j])
            bgt = jnp.broadcast_to(b_gates_odd[j][:, None], (2 * H_C, LANES))
            nw = mh_norm_w[j][None, :]
            q, k, v, o, gc, gr = _odd_in(xp.reshape(bp, tp, D_MODEL), w_bf, wg, wgt, bg, bgt, decode=False)
            h_p, c_p, n_pp, m_p = _mlstm(q, k, v, o, gc, gr, nw,
                                         jnp.zeros((bp, H_C, DK_C, DV_C), F32), jnp.zeros((bp, H_C, DK_C), F32),
                                         jnp.zeros((bp, H_C, LANES), F32))
            outs_p.setdefault("C", []).append(c_p)
            outs_p.setdefault("n", []).append(n_pp)
            outs_p.setdefault("m", []).append(m_p[:, :, 0])
            q_s, k_s, v_s, o_s2, gc_s, gr_s = _odd_in(xs, w_bf, wg, wgt, bg, bgt, decode=True)
            ch = MLSTM_CHUNK
            pad_rows = lambda a: jnp.zeros((bs, ch, a.shape[1]), a.dtype).at[:, 0, :].set(a)
            lane = jnp.arange(LANES)
            inert_c = jnp.where(lane < H_C, -jnp.inf, 0.0).astype(F32)
            gc_pad = jnp.broadcast_to(inert_c[None, None, :], (bs, ch, LANES)).at[:, 0, :].set(gc_s)
            inert_r = jnp.where(jnp.arange(2 * H_C) < H_C, -jnp.inf, 0.0).astype(F32)
            gr_pad = jnp.broadcast_to(inert_r[None, :, None], (bs, 2 * H_C, ch)).at[:, :, 0].set(gr_s.T)
            m0 = jnp.broadcast_to(state_mlstm_m[j][:, :, None], (bs, H_C, LANES))
            h_s, c_s, n_s, m_s = _mlstm(pad_rows(q_s), pad_rows(k_s), pad_rows(v_s), pad_rows(o_s2), gc_pad,
                                        gr_pad, nw, state_mlstm_C[j], state_mlstm_n[j], m0)
            outs_s.setdefault("C", []).append(c_s)
            outs_s.setdefault("n", []).append(n_s)
            outs_s.setdefault("m", []).append(m_s[:, :, 0])
            w_list = [w_out_odd[j].astype(BF16)]
            acts_p = [h_p.reshape(n_p, vw)]
            acts_s = [h_s[:, 0, :]]
        xp, xs = _layer_tail(i, acts_p, acts_s, w_list, xp, xs, p_p, p_s, *tail_w)
    st = lambda lst: jnp.stack(lst)
    return (xp.reshape(bp, tp, D_MODEL), xs.reshape(bs, ts, D_MODEL),
            st(outs_p["k"]), st(outs_p["v"]), st(outs_p["c"]), st(outs_p["C"]), st(outs_p["n"]), st(outs_p["m"]),
            st(outs_s["k"]), st(outs_s["v"]), st(outs_s["c"]), st(outs_s["C"]), st(outs_s["n"]), st(outs_s["m"]))
```

```python
import functools
import math

import numpy as np
import jax
import jax.numpy as jnp
from jax import lax
from jax.experimental import pallas as pl
from jax.experimental.pallas import tpu as pltpu

F32 = jnp.float32
BF16 = jnp.bfloat16
I32 = jnp.int32

D_MODEL = 1024
DEPTH = 2
PAGE_SIZE = 128
D_CONV = D_MODEL // 2
CONV_W = 3
H_B = 4
DH_B = 64
ROT_DIM = DH_B // 4
ROPE_THETA = 500000.0
H_C = 4
DK_C = (D_MODEL // 2) // H_C
DV_C = D_MODEL // H_C
N_GROUPS = 4
EXP_PER_GROUP = 4
N_EXPERTS = N_GROUPS * EXP_PER_GROUP
D_EXPERT = 512
D_PLE = 256
LN_EPS = 1e-5
DEEPNORM_ALPHA = (2 * DEPTH) ** 0.25
QK_B = 2 * H_B * DH_B
V_B = H_B * 2 * DH_B
N_PAIRS = EXP_PER_GROUP * (EXP_PER_GROUP - 1) // 2
N_CLASSES = N_GROUPS * N_PAIRS

LANES = 128
VMEM_LIMIT = 56 * 1024 * 1024
ROW_TILE = 512
ATTN_TILE = 512
MLSTM_CHUNK = 128
MOE_TILE = 256
PAGES_PER_STEP = 8
ROUTE_GROUP_LANE = 0
ROUTE_EXPERT_LANE = 16


def _cparams(*sem):
    return pltpu.CompilerParams(dimension_semantics=sem, vmem_limit_bytes=VMEM_LIMIT)


def _dot(a, b):
    return jnp.dot(a, b, preferred_element_type=F32)


def _dot_nt(a, b):
    return lax.dot_general(a, b, (((1,), (1,)), ((), ())), preferred_element_type=F32)


def _dot_tn(a, b):
    return lax.dot_general(a, b, (((0,), (0,)), ((), ())), preferred_element_type=F32)


def _layer_norm_rows(z, g, b):
    mu = jnp.mean(z, axis=-1, keepdims=True)
    zc = z - mu
    var = jnp.mean(zc * zc, axis=-1, keepdims=True)
    return zc * lax.rsqrt(var + LN_EPS) * g + b


def _log_sigmoid(x):
    return jnp.minimum(x, 0.0) - jnp.log1p(jnp.exp(-jnp.abs(x)))


def _gated_conv(gate_b, u, um1, um2, cw_ref):
    cw = cw_ref[...]
    conv = um2 * cw[0:1, :] + um1 * cw[1:2, :] + u * cw[2:3, :]
    return (gate_b * conv).astype(BF16)


def _rope(z, cos, sin_lo, sin_hi, axis):
    half = ROT_DIM // 2
    return z * cos + pltpu.roll(z, QK_B - half, axis) * sin_lo + pltpu.roll(z, half, axis) * sin_hi


def _even_in_prompt_kernel(x_ref, w_ref, wqt_ref, wvt_ref, cw_ref, prev_ref, rc_ref, rs1_ref, rs2_ref,
                           rct_ref, rs1t_ref, rs2t_ref,
                           yc_ref, qt_ref, kf_ref, vf_ref, kb_ref, vt_ref, u_ref, carry_ref):
    xb = x_ref[0].astype(BF16)
    tm = xb.shape[0]

    def proj(c0, n):
        return _dot(xb, w_ref[:, c0:c0 + n])

    gate_b = proj(0, D_CONV)
    u = proj(D_CONV, D_CONV) * proj(2 * D_CONV, D_CONV)
    j = pl.program_id(1)

    @pl.when(j == 0)
    def _():
        carry_ref[...] = prev_ref[0]

    row = lax.broadcasted_iota(I32, u.shape, 0)
    c2 = carry_ref[0:1, :]
    c1 = carry_ref[1:2, :]
    um1 = jnp.where(row == 0, c1, pltpu.roll(u, 1, 0))
    um2 = jnp.where(row == 0, c2, jnp.where(row == 1, c1, pltpu.roll(u, 2, 0)))
    carry_ref[...] = u[tm - 2:tm, :]
    u_ref[0] = u[tm - 2:tm, :]
    yc_ref[0] = _gated_conv(gate_b, u, um1, um2, cw_ref)

    reps = QK_B // LANES
    tile = lambda r, ax: jnp.concatenate([r[...]] * reps, axis=ax)
    k = _rope(proj(3 * D_CONV + QK_B, QK_B), tile(rc_ref, 1), tile(rs1_ref, 1), tile(rs2_ref, 1), 1)
    kf_ref[0] = k
    kb_ref[0] = k.astype(BF16)
    v = proj(3 * D_CONV + 2 * QK_B, V_B)
    vf_ref[0] = v
    qt = _rope(_dot_nt(wqt_ref[...], xb), tile(rct_ref, 0), tile(rs1t_ref, 0), tile(rs2t_ref, 0), 0)
    qt_ref[0] = (qt * (DH_B ** -0.5)).astype(BF16)
    vt_ref[0] = _dot_nt(wvt_ref[...], xb).astype(BF16)


def _even_in_decode_kernel(x_ref, w_ref, cw_ref, prev_ref, rc_ref, rs1_ref, rs2_ref,
                           yc_ref, q_ref, kf_ref, vf_ref, u_ref):
    xb = x_ref[...].astype(BF16)

    def proj(c0, n):
        return _dot(xb, w_ref[:, c0:c0 + n])

    gate_b = proj(0, D_CONV)
    u = proj(D_CONV, D_CONV) * proj(2 * D_CONV, D_CONV)
    u_ref[...] = u
    yc_ref[...] = _gated_conv(gate_b, u, prev_ref[1], prev_ref[0], cw_ref)
    reps = QK_B // LANES
    tile = lambda r: jnp.concatenate([r[...]] * reps, axis=1)
    cos, sin_lo, sin_hi = tile(rc_ref), tile(rs1_ref), tile(rs2_ref)
    q_ref[...] = (_rope(proj(3 * D_CONV, QK_B), cos, sin_lo, sin_hi, 1) * (DH_B ** -0.5)).astype(BF16)
    kf_ref[...] = _rope(proj(3 * D_CONV + QK_B, QK_B), cos, sin_lo, sin_hi, 1)
    vf_ref[...] = proj(3 * D_CONV + 2 * QK_B, V_B)


def _rope_tables(pos):
    half = ROT_DIM // 2
    inv = ROPE_THETA ** (-jnp.arange(half, dtype=F32) / half)
    ang = pos.astype(F32)[:, None] * inv[None, :]
    cos, sin = jnp.cos(ang), jnp.sin(ang)
    t = pos.shape[0]
    ones = jnp.ones((t, DH_B - ROT_DIM), F32)
    zeros = jnp.zeros((t, DH_B - ROT_DIM), F32)
    zh = jnp.zeros((t, half), F32)
    c = jnp.concatenate([cos, cos, ones], axis=1)
    s_lo = jnp.concatenate([-sin, zh, zeros], axis=1)
    s_hi = jnp.concatenate([zh, sin, zeros], axis=1)
    tile2 = lambda a: jnp.concatenate([a, a], axis=1)
    return tile2(c), tile2(s_lo), tile2(s_hi)


def _even_in_prompt(x, w_bf, wqt_bf, wvt_bf, conv_w, conv_prev, tables):
    b, t, _ = x.shape
    tm = ROW_TILE
    row3 = lambda n: pl.BlockSpec((1, tm, n), lambda i, j: (i, j, 0))
    col3 = lambda n: pl.BlockSpec((1, n, tm), lambda i, j: (i, 0, j))
    full2 = lambda a: pl.BlockSpec(a.shape, lambda i, j: (0, 0))
    tab = pl.BlockSpec((tm, LANES), lambda i, j: (j, 0))
    tab_t = pl.BlockSpec((LANES, tm), lambda i, j: (0, j))
    st = pl.BlockSpec((1, CONV_W - 1, D_CONV), lambda i, j: (i, 0, 0))
    tables_t = [jnp.transpose(a) for a in tables]
    outs = [jax.ShapeDtypeStruct((b, t, D_CONV), BF16), jax.ShapeDtypeStruct((b, QK_B, t), BF16),
            jax.ShapeDtypeStruct((b, t, QK_B), F32), jax.ShapeDtypeStruct((b, t, V_B), F32),
            jax.ShapeDtypeStruct((b, t, QK_B), BF16), jax.ShapeDtypeStruct((b, V_B, t), BF16),
            jax.ShapeDtypeStruct((b, CONV_W - 1, D_CONV), F32)]
    return pl.pallas_call(
        _even_in_prompt_kernel,
        grid=(b, t // tm),
        in_specs=[row3(D_MODEL), full2(w_bf), full2(wqt_bf), full2(wvt_bf), full2(conv_w), st,
                  tab, tab, tab, tab_t, tab_t, tab_t],
        out_specs=[row3(D_CONV), col3(QK_B), row3(QK_B), row3(V_B), row3(QK_B), col3(V_B), st],
        out_shape=outs,
        scratch_shapes=[pltpu.VMEM((CONV_W - 1, D_CONV), F32)],
        compiler_params=_cparams("arbitrary", "arbitrary"),
        name="even_in_prompt",
    )(x, w_bf, wqt_bf, wvt_bf, conv_w, conv_prev, *tables, *tables_t)


def _even_in_decode(x, w_bf, conv_w, conv_prev_t, tables):
    n = x.shape[0]
    full = lambda a: pl.BlockSpec(a.shape, lambda i: (0,) * a.ndim)
    o2 = lambda c, dt: jax.ShapeDtypeStruct((n, c), dt)
    outs = [o2(D_CONV, BF16), o2(QK_B, BF16), o2(QK_B, F32), o2(V_B, F32), o2(D_CONV, F32)]
    ins = [x, w_bf, conv_w, conv_prev_t, *tables]
    return pl.pallas_call(
        _even_in_decode_kernel,
        grid=(1,),
        in_specs=[full(a) for a in ins],
        out_specs=[pl.BlockSpec(o.shape, lambda i: (0, 0)) for o in outs],
        out_shape=outs,
        compiler_params=_cparams("arbitrary"),
        name="even_in_decode",
    )(*ins)


def _lambda_value(lam_ref, lam_init):
    lv = lam_ref[...]
    a = jnp.sum(lv[0:1, :] * lv[1:2, :], axis=1, keepdims=True)
    b = jnp.sum(lv[2:3, :] * lv[3:4, :], axis=1, keepdims=True)
    return jnp.exp(a) - jnp.exp(b) + lam_init


def _sub_norm(o, sub_ref, lam_init):
    ms = jnp.mean(o * o, axis=-1, keepdims=True)
    return o * lax.rsqrt(ms + LN_EPS) * sub_ref[...] * (1.0 - lam_init)


def _attn_prompt_kernel(qt_ref, k_ref, vt_ref, lam_ref, sub_ref, o_ref, m_scr, l_scr, acc_scr, *, lam_init):
    i = pl.program_id(2)
    tq = qt_ref.shape[2]
    tk = tq
    qt = qt_ref[0].astype(F32)
    feat = lax.broadcasted_iota(I32, qt.shape, 0)
    qq = jnp.concatenate([jnp.where(feat < DH_B, qt, 0.0), jnp.where(feat >= DH_B, qt, 0.0)],
                         axis=1).astype(BF16)
    m_scr[...] = jnp.full(m_scr.shape, -jnp.inf, F32)
    l_scr[...] = jnp.zeros(l_scr.shape, F32)
    acc_scr[...] = jnp.zeros(acc_scr.shape, F32)

    def step(j, masked):
        start = pl.multiple_of(j * tk, tk)
        kj = k_ref[0, pl.ds(start, tk), :]
        vtj = vt_ref[0, :, pl.ds(start, tk)]
        s = _dot(kj, qq)
        if masked:
            key = lax.broadcasted_iota(I32, s.shape, 0)
            qry = lax.broadcasted_iota(I32, s.shape, 1)
            qry = jnp.where(qry >= tq, qry - tq, qry)
            s = jnp.where(key <= qry, s, -jnp.inf)
        m_prev = m_scr[...]
        m_new = jnp.maximum(m_prev, jnp.max(s, axis=0, keepdims=True))
        alpha = jnp.exp(m_prev - m_new)
        p = jnp.exp(s - m_new)
        l_scr[...] = alpha * l_scr[...] + jnp.sum(p, axis=0, keepdims=True)
        acc_scr[...] = alpha * acc_scr[...] + _dot(vtj, p.astype(BF16))
        m_scr[...] = m_new

    def body(j, carry):
        step(j, False)
        return carry

    lax.fori_loop(0, i, body, 0)
    step(i, True)
    on = acc_scr[...] / l_scr[...]
    lam = _lambda_value(lam_ref, lam_init)
    o = jnp.transpose(on[:, 0:tq] - lam * on[:, tq:2 * tq])
    o_ref[0] = _sub_norm(o, sub_ref, lam_init).astype(BF16)


def _attn_prompt(qt, k, vt, lam_vecs, subln, lam_init):
    b, t, _ = k.shape
    tq = ATTN_TILE
    full = lambda a: pl.BlockSpec(a.shape, lambda bi, h, i: (0, 0))
    return pl.pallas_call(
        functools.partial(_attn_prompt_kernel, lam_init=lam_init),
        grid=(b, H_B, t // tq),
        in_specs=[pl.BlockSpec((1, LANES, tq), lambda bi, h, i: (bi, h, i)),
                  pl.BlockSpec((1, t, LANES), lambda bi, h, i: (bi, 0, h)),
                  pl.BlockSpec((1, LANES, t), lambda bi, h, i: (bi, h, 0)),
                  full(lam_vecs), full(subln)],
        out_specs=pl.BlockSpec((1, tq, LANES), lambda bi, h, i: (bi, i, h)),
        out_shape=jax.ShapeDtypeStruct((b, t, V_B), BF16),
        scratch_shapes=[pltpu.VMEM((1, 2 * tq), F32), pltpu.VMEM((1, 2 * tq), F32),
                        pltpu.VMEM((LANES, 2 * tq), F32)],
        compiler_params=_cparams("arbitrary", "arbitrary", "arbitrary"),
        name="attn_prompt",
    )(qt, k, vt, lam_vecs, subln)


def _attn_decode_kernel(pt_ref, qbd_ref, kn_ref, vn_ref, lam_ref, sub_ref, *rest, lam_init, n_pages):
    del pt_ref
    k_refs = rest[:n_pages]
    v_refs = rest[n_pages:2 * n_pages]
    o_ref, m_scr, l_scr, acc_scr = rest[2 * n_pages:]
    j = pl.program_id(1)
    qbd = qbd_ref[0]

    @pl.when(j == 0)
    def _():
        s_new = jnp.sum(qbd.astype(F32) * kn_ref[0], axis=1, keepdims=True)
        m_scr[...] = s_new
        l_scr[...] = jnp.ones(l_scr.shape, F32)
        acc_scr[...] = jnp.broadcast_to(vn_ref[0], acc_scr.shape)

    s = jnp.concatenate([_dot(qbd, k_refs[r][0].astype(BF16)) for r in range(n_pages)], axis=1)
    m_prev = m_scr[...]
    m_new = jnp.maximum(m_prev, jnp.max(s, axis=1, keepdims=True))
    alpha = jnp.exp(m_prev - m_new)
    p = jnp.exp(s - m_new)
    l_scr[...] = alpha * l_scr[...] + jnp.sum(p, axis=1, keepdims=True)

    def head_pv(h):
        acc = None
        for r in range(n_pages):
            vh = v_refs[r][0, pl.ds(h, PAGE_SIZE, stride=H_B), :].astype(BF16)
            term = _dot(p[:, r * PAGE_SIZE:(r + 1) * PAGE_SIZE].astype(BF16), vh)
            acc = term if acc is None else acc + term
        return acc

    pv = jnp.concatenate([head_pv(h) for h in range(H_B)], axis=1)
    acc_scr[...] = alpha * acc_scr[...] + pv
    m_scr[...] = m_new

    @pl.when(j == pl.num_programs(1) - 1)
    def _():
        on = acc_scr[...] / l_scr[...]
        row = lax.broadcasted_iota(I32, (2 * H_B, 2 * DH_B), 0)
        head = jnp.where(row >= H_B, row - H_B, row)
        o8 = jnp.zeros((2 * H_B, 2 * DH_B), F32)
        for c in range(H_B):
            o8 = o8 + jnp.where(head == c, on[:, c * 2 * DH_B:(c + 1) * 2 * DH_B], 0.0)
        lam = _lambda_value(lam_ref, lam_init)
        o = o8 - lam * pltpu.roll(o8, H_B, 0)
        o_ref[0] = _sub_norm(o, sub_ref, lam_init)


def _attn_decode(qbd, k_new, v_new, cache_k, cache_v, page_table, lam_vecs, subln, lam_init):
    n = qbd.shape[0]
    n_pages = page_table.shape[1]
    pp = PAGES_PER_STEP
    width = V_B
    c2 = lambda a: pl.BlockSpec(a.shape, lambda b, j, pt: (0, 0))
    per_b = lambda a: pl.BlockSpec((1,) + a.shape[1:], lambda b, j, pt: (b, 0, 0))

    def page(r, arr):
        return pl.BlockSpec((1,) + arr.shape[1:], lambda b, j, pt: (pt[b, j * pp + r], 0, 0))

    grid_spec = pltpu.PrefetchScalarGridSpec(
        num_scalar_prefetch=1,
        grid=(n, n_pages // pp),
        in_specs=[per_b(qbd), per_b(k_new), per_b(v_new), c2(lam_vecs), c2(subln)]
        + [page(r, cache_k) for r in range(pp)] + [page(r, cache_v) for r in range(pp)],
        out_specs=pl.BlockSpec((1, 2 * H_B, 2 * DH_B), lambda b, j, pt: (b, 0, 0)),
        scratch_shapes=[pltpu.VMEM((2 * H_B, 1), F32), pltpu.VMEM((2 * H_B, 1), F32),
                        pltpu.VMEM((2 * H_B, width), F32)],
    )
    return pl.pallas_call(
        functools.partial(_attn_decode_kernel, lam_init=lam_init, n_pages=pp),
        grid_spec=grid_spec,
        out_shape=jax.ShapeDtypeStruct((n, 2 * H_B, 2 * DH_B), F32),
        compiler_params=_cparams("arbitrary", "arbitrary"),
        name="attn_decode",
    )(page_table, qbd, k_new, v_new, lam_vecs, subln, *([cache_k] * pp), *([cache_v] * pp))


def _odd_in_kernel(x_ref, w_ref, wg_ref, wgt_ref, bg_ref, bgt_ref,
                   q_ref, k_ref, v_ref, o_ref, gc_ref, gr_ref, *, decode):
    xb = x_ref[0].astype(BF16) if not decode else x_ref[...].astype(BF16)
    qw = H_C * DK_C
    vw = H_C * DV_C
    q = _dot(xb, w_ref[:, 0:qw]) * (DK_C ** -0.5)
    k = _dot(xb, w_ref[:, qw:2 * qw])
    v = _dot(xb, w_ref[:, 2 * qw:2 * qw + vw])
    o = _dot(xb, w_ref[:, 2 * qw + vw:2 * qw + 2 * vw])
    g_col = _dot(xb, wg_ref[...]) + bg_ref[...]
    lane = lax.broadcasted_iota(I32, g_col.shape, 1)
    g_col = jnp.where(lane < H_C, g_col, _log_sigmoid(g_col))
    g_row = _dot_nt(wgt_ref[...], xb) + bgt_ref[:, 0:1]
    sub = lax.broadcasted_iota(I32, g_row.shape, 0)
    g_row = jnp.where(sub < H_C, g_row, _log_sigmoid(g_row))
    if decode:
        q_ref[...] = q.astype(BF16)
        k_ref[...] = k.astype(BF16)
        v_ref[...] = v.astype(BF16)
        o_ref[...] = o
        gc_ref[...] = g_col
        gr_ref[...] = g_row
    else:
        q_ref[0] = q.astype(BF16)
        k_ref[0] = k.astype(BF16)
        v_ref[0] = v.astype(BF16)
        o_ref[0] = o
        gc_ref[0] = g_col
        gr_ref[0] = g_row


def _odd_in(x, w_bf, wg, wgt, bg, bgt, decode):
    qw, vw = H_C * DK_C, H_C * DV_C
    if decode:
        n = x.shape[0]
        ins = [x, w_bf, wg, wgt, bg, bgt]
        outs = [jax.ShapeDtypeStruct((n, qw), BF16), jax.ShapeDtypeStruct((n, qw), BF16),
                jax.ShapeDtypeStruct((n, vw), BF16), jax.ShapeDtypeStruct((n, vw), F32),
                jax.ShapeDtypeStruct((n, LANES), F32), jax.ShapeDtypeStruct((2 * H_C, n), F32)]
        return pl.pallas_call(
            functools.partial(_odd_in_kernel, decode=True),
            grid=(1,),
            in_specs=[pl.BlockSpec(a.shape, lambda i: (0, 0)) for a in ins],
            out_specs=[pl.BlockSpec(o.shape, lambda i: (0, 0)) for o in outs],
            out_shape=outs,
            compiler_params=_cparams("arbitrary"),
            name="odd_in_decode",
        )(*ins)
    b, t, _ = x.shape
    tm = ROW_TILE
    row3 = lambda n: pl.BlockSpec((1, tm, n), lambda i, j: (i, j, 0))
    full2 = lambda a: pl.BlockSpec(a.shape, lambda i, j: (0, 0))
    outs = [jax.ShapeDtypeStruct((b, t, qw), BF16), jax.ShapeDtypeStruct((b, t, qw), BF16),
            jax.ShapeDtypeStruct((b, t, vw), BF16), jax.ShapeDtypeStruct((b, t, vw), F32),
            jax.ShapeDtypeStruct((b, t, LANES), F32), jax.ShapeDtypeStruct((b, 2 * H_C, t), F32)]
    return pl.pallas_call(
        functools.partial(_odd_in_kernel, decode=False),
        grid=(b, t // tm),
        in_specs=[row3(D_MODEL), full2(w_bf), full2(wg), full2(wgt), full2(bg), full2(bgt)],
        out_specs=[row3(qw), row3(qw), row3(vw), row3(vw), row3(LANES),
                   pl.BlockSpec((1, 2 * H_C, tm), lambda i, j: (i, 0, j))],
        out_shape=outs,
        compiler_params=_cparams("arbitrary", "arbitrary"),
        name="odd_in_prompt",
    )(x, w_bf, wg, wgt, bg, bgt)


def _mlstm_kernel(q_ref, k_ref, v_ref, o_ref, gc_ref, gr_ref, nw_ref, c0_ref, n0_ref, m0_ref,
                  h_ref, c_out, n_out, m_out, c_scr, n_scr, m_scr):
    ci = pl.program_id(1)
    chunk = q_ref.shape[1]

    @pl.when(ci == 0)
    def _():
        c_scr[...] = c0_ref[0]
        n_scr[...] = n0_ref[0]
        m_scr[...] = m0_ref[0]

    t_idx = lax.broadcasted_iota(I32, (chunk, chunk), 0)
    s_idx = lax.broadcasted_iota(I32, (chunk, chunk), 1)
    causal = s_idx <= t_idx
    for h in range(H_C):
        q = q_ref[0, :, h * DK_C:(h + 1) * DK_C]
        k = k_ref[0, :, h * DK_C:(h + 1) * DK_C]
        v = v_ref[0, :, h * DV_C:(h + 1) * DV_C]
        ig_r = gr_ref[0, h:h + 1, :]
        lf_r = gr_ref[0, H_C + h:H_C + h + 1, :]
        ig_c = gc_ref[0, :, h:h + 1]
        lf_c = gc_ref[0, :, H_C + h:H_C + h + 1]
        bcum_c = jnp.sum(jnp.where(causal, lf_r, 0.0), axis=1, keepdims=True)
        bcum_r = jnp.sum(jnp.where(t_idx <= s_idx, lf_c, 0.0), axis=0, keepdims=True)
        m0 = m_scr[h:h + 1, 0:1]
        dmat = jnp.where(causal, bcum_c - bcum_r + ig_r, -jnp.inf)
        inter = bcum_c + m0
        m = jnp.maximum(inter, jnp.max(dmat, axis=1, keepdims=True))
        w = jnp.exp(dmat - m)
        g = jnp.exp(inter - m)
        s = _dot_nt(q, k) * w
        c0 = c_scr[h]
        n0 = n_scr[h:h + 1, :]
        num = g * _dot(q, c0.astype(BF16)) + _dot(s.astype(BF16), v)
        den = g * jnp.sum(q.astype(F32) * n0, axis=1, keepdims=True) + jnp.sum(s, axis=1, keepdims=True)
        hid = num / jnp.maximum(jnp.abs(den), jnp.exp(-m))
        m_last = m[chunk - 1:chunk, :]
        b_last = bcum_c[chunk - 1:chunk, :]
        w_last = jnp.exp(b_last - bcum_c + ig_c - m_last)
        g_last = jnp.exp(b_last + m0 - m_last)
        kw = k.astype(F32) * w_last
        c_scr[h] = g_last * c0 + _dot_tn(kw.astype(BF16), v)
        n_scr[h:h + 1, :] = g_last * n0 + jnp.sum(kw, axis=0, keepdims=True)
        m_scr[h:h + 1, :] = jnp.broadcast_to(m_last, (1, LANES))
        mu = jnp.mean(hid, axis=1, keepdims=True)
        hc = hid - mu
        var = jnp.mean(hc * hc, axis=1, keepdims=True)
        hn = hc * lax.rsqrt(var + LN_EPS) * nw_ref[:, h * DV_C:(h + 1) * DV_C]
        gate = jax.nn.sigmoid(o_ref[0, :, h * DV_C:(h + 1) * DV_C])
        h_ref[0, :, h * DV_C:(h + 1) * DV_C] = (gate * hn).astype(BF16)

    c_out[0] = c_scr[...]
    n_out[0] = n_scr[...]
    m_out[0] = m_scr[...]


def _mlstm(q, k, v, o, g_col, g_row, norm_w, c0, n0, m0):
    b, t, _ = q.shape
    ch = MLSTM_CHUNK
    qw, vw = H_C * DK_C, H_C * DV_C
    row = lambda n: pl.BlockSpec((1, ch, n), lambda i, j: (i, j, 0))
    st4 = pl.BlockSpec((1, H_C, DK_C, DV_C), lambda i, j: (i, 0, 0, 0))
    st3 = pl.BlockSpec((1, H_C, LANES), lambda i, j: (i, 0, 0))
    outs = [jax.ShapeDtypeStruct((b, t, vw), BF16), jax.ShapeDtypeStruct((b, H_C, DK_C, DV_C), F32),
            jax.ShapeDtypeStruct((b, H_C, DK_C), F32), jax.ShapeDtypeStruct((b, H_C, LANES), F32)]
    return pl.pallas_call(
        _mlstm_kernel,
        grid=(b, t // ch),
        in_specs=[row(qw), row(qw), row(vw), row(vw), row(LANES),
                  pl.BlockSpec((1, 2 * H_C, ch), lambda i, j: (i, 0, j)),
                  pl.BlockSpec(norm_w.shape, lambda i, j: (0, 0)), st4, st3, st3],
        out_specs=[row(vw), st4, st3, st3],
        out_shape=outs,
        scratch_shapes=[pltpu.VMEM((H_C, DK_C, DV_C), F32), pltpu.VMEM((H_C, DK_C), F32),
                        pltpu.VMEM((H_C, LANES), F32)],
        compiler_params=_cparams("arbitrary", "arbitrary"),
        name="mlstm",
    )(q, k, v, o, g_col, g_row, norm_w, c0, n0, m0)


def _mixed_rows(acts, w_refs, x, g_ref, b_ref, rhi_ref, rlo_ref):
    y = _dot(acts[0], w_refs[0][...])
    for a, w in zip(acts[1:], w_refs[1:]):
        y = y + _dot(a, w[...])
    x1 = _layer_norm_rows(DEEPNORM_ALPHA * x + y, g_ref[...], b_ref[...])

    xh = x1.astype(BF16)
    xl = (x1 - xh.astype(F32)).astype(BF16)
    lg = _dot(xh, rhi_ref[...]) + _dot(xl, rhi_ref[...]) + _dot(xh, rlo_ref[...])
    lane = lax.broadcasted_iota(I32, lg.shape, 1)
    big = jnp.int32(4 * LANES)
    neg = -jnp.inf
    gl = jnp.where(lane < ROUTE_GROUP_LANE + N_GROUPS, lg, neg)
    g_max = jnp.max(gl, axis=1, keepdims=True)
    g_w = 1.0 / jnp.sum(jnp.exp(gl - g_max), axis=1, keepdims=True)
    g_idx = jnp.min(jnp.where(gl == g_max, lane, big), axis=1, keepdims=True)
    lane_group = (lane - ROUTE_EXPERT_LANE) >> 2
    el = jnp.where(lane_group == g_idx, lg, neg)
    e1 = jnp.max(el, axis=1, keepdims=True)
    i1 = jnp.min(jnp.where(el == e1, lane, big), axis=1, keepdims=True)
    z = jnp.sum(jnp.exp(el - e1), axis=1, keepdims=True)
    el2 = jnp.where(lane == i1, neg, el)
    e2 = jnp.max(el2, axis=1, keepdims=True)
    i2 = jnp.min(jnp.where(el2 == e2, lane, big), axis=1, keepdims=True)
    p1 = 1.0 / z
    p2 = jnp.exp(e2 - e1) / z
    w1 = p1 / (p1 + p2) * g_w
    w2 = p2 / (p1 + p2) * g_w
    id1 = (i1 - ROUTE_EXPERT_LANE).astype(F32)
    id2 = (i2 - ROUTE_EXPERT_LANE).astype(F32)
    rt = jnp.where(lane == 0, w1, jnp.where(lane == 1, w2,
                   jnp.where(lane == 2, id1, jnp.where(lane == 3, id2, 0.0))))
    return x1, rt


def _mix_out_kernel(*refs, n_in, n_s):
    ap_refs = refs[:n_in]
    as_refs = refs[n_in:2 * n_in]
    w_refs = refs[2 * n_in:3 * n_in]
    xp_ref, xs_ref, g_ref, b_ref, rhi_ref, rlo_ref, out_ref = refs[3 * n_in:]
    i = pl.program_id(0)
    last = pl.num_programs(0) - 1

    @pl.when(i < last)
    def _():
        x1, rt = _mixed_rows([a[...] for a in ap_refs], w_refs, xp_ref[...], g_ref, b_ref, rhi_ref, rlo_ref)
        out_ref[:, 0:D_MODEL] = x1
        out_ref[:, D_MODEL:D_MODEL + LANES] = rt

    @pl.when(i == last)
    def _():
        x1, rt = _mixed_rows([a[...] for a in as_refs], w_refs, xs_ref[...], g_ref, b_ref, rhi_ref, rlo_ref)
        out_ref[0:n_s, 0:D_MODEL] = x1
        out_ref[0:n_s, D_MODEL:D_MODEL + LANES] = rt


def _mix_out(acts_p, acts_s, weights, xp, xs, ln_g, ln_b, r_hi, r_lo):
    n_p, n_s = xp.shape[0], xs.shape[0]
    tm = ROW_TILE
    nb = n_p // tm
    n_in = len(acts_p)
    prow = lambda n: pl.BlockSpec((tm, n), lambda i: (jnp.minimum(i, nb - 1), 0))
    full = lambda a: pl.BlockSpec(a.shape, lambda i: (0, 0))
    width = D_MODEL + LANES
    return pl.pallas_call(
        functools.partial(_mix_out_kernel, n_in=n_in, n_s=n_s),
        grid=(nb + 1,),
        in_specs=[prow(a.shape[1]) for a in acts_p] + [full(a) for a in acts_s] + [full(w) for w in weights]
        + [prow(D_MODEL), full(xs), full(ln_g), full(ln_b), full(r_hi), full(r_lo)],
        out_specs=pl.BlockSpec((tm, width), lambda i: (i, 0)),
        out_shape=jax.ShapeDtypeStruct((n_p + n_s, width), F32),
        compiler_params=_cparams("arbitrary"),
        name="mix_out",
    )(*acts_p, *acts_s, *weights, xp, xs, ln_g, ln_b, r_hi, r_lo)


def _route_plan(rt, tm):
    n = rt.shape[0]
    e1, e2 = rt[:, 2].astype(I32), rt[:, 3].astype(I32)
    ea, eb = jnp.minimum(e1, e2), jnp.maximum(e1, e2)
    la, lb = ea % EXP_PER_GROUP, eb % EXP_PER_GROUP
    cls = (ea // EXP_PER_GROUP) * N_PAIRS + (la * (2 * EXP_PER_GROUP - 1 - la)) // 2 + (lb - la - 1)
    onehot = (cls[:, None] == jnp.arange(N_CLASSES, dtype=I32)[None, :]).astype(I32)
    csum = jnp.cumsum(onehot, axis=0)
    rank = jnp.sum(onehot * csum, axis=1) - 1
    cnt = csum[-1]
    ntile = (cnt + tm - 1) // tm
    tile_end = jnp.cumsum(ntile)
    tile_start = tile_end - ntile
    n_used = tile_end[-1]
    n_tiles = -(-(n + N_CLASSES * (tm - 1)) // tm)
    pos = (jnp.sum(onehot * tile_start[None, :], axis=1) * tm + rank).astype(I32)
    pair_lo = np.array([a for a in range(EXP_PER_GROUP) for b in range(a + 1, EXP_PER_GROUP)], np.int32)
    pair_hi = np.array([b for a in range(EXP_PER_GROUP) for b in range(a + 1, EXP_PER_GROUP)], np.int32)
    cls_ids = np.arange(N_CLASSES)
    cls_a = jnp.asarray((cls_ids // N_PAIRS) * EXP_PER_GROUP + pair_lo[cls_ids % N_PAIRS], I32)
    cls_b = jnp.asarray((cls_ids // N_PAIRS) * EXP_PER_GROUP + pair_hi[cls_ids % N_PAIRS], I32)
    tile_ids = jnp.arange(n_tiles, dtype=I32)
    tile_cls = jnp.sum((tile_end[None, :] <= jnp.minimum(tile_ids, n_used - 1)[:, None]).astype(I32), axis=1)
    tile_cls = jnp.minimum(tile_cls, N_CLASSES - 1)
    onehot_t = (tile_cls[:, None] == jnp.arange(N_CLASSES, dtype=I32)[None, :]).astype(I32)
    last_tile = jnp.where(ntile > 0, tile_end - 1, -1).astype(I32)
    return dict(pos=pos, ta=jnp.sum(onehot_t * cls_a[None, :], axis=1),
                tb=jnp.sum(onehot_t * cls_b[None, :], axis=1), nu=n_used.reshape(1).astype(I32),
                last_tile=last_tile, n_tiles=n_tiles)


def _dispatch_kernel(lt_ref, nu_ref, pos_ref, src_ref, dst_hbm, zbuf, zsem, rsem, *, moe_tile, n_tiles):
    i = pl.program_id(0)
    td = pos_ref.shape[2]

    def zero_copy(tile):
        start = pl.multiple_of(tile * moe_tile, moe_tile)
        return pltpu.make_async_copy(zbuf, dst_hbm.at[pl.ds(start, moe_tile)], zsem)

    @pl.when(i == 0)
    def _():
        zbuf[...] = jnp.zeros(zbuf.shape, F32)
        for c in range(N_CLASSES):
            @pl.when(lt_ref[c] >= 0)
            def _(c=c):
                zero_copy(lt_ref[c]).start()
        for c in range(N_CLASSES):
            @pl.when(lt_ref[c] >= 0)
            def _(c=c):
                zero_copy(lt_ref[c]).wait()

        def spare_start(t, carry):
            zero_copy(t).start()
            return carry

        def spare_wait(t, carry):
            zero_copy(t).wait()
            return carry

        lax.fori_loop(nu_ref[0], n_tiles, spare_start, 0)
        lax.fori_loop(nu_ref[0], n_tiles, spare_wait, 0)

    def start_row(r, carry):
        pltpu.make_async_copy(src_ref.at[pl.ds(r, 1)], dst_hbm.at[pl.ds(pos_ref[0, 0, r], 1)], rsem).start()
        return carry

    lax.fori_loop(0, td, start_row, 0, unroll=8)
    pltpu.make_async_copy(src_ref, dst_hbm.at[pl.ds(0, td)], rsem).wait()


def _largest_divisor_tile(n, cap):
    for t in range(cap - cap % 8, 7, -8):
        if n % t == 0:
            return t
    raise ValueError(f"no row tile for {n} rows")


def _dispatch(x1e, plan, tm):
    n, width = x1e.shape
    n_tiles = plan["n_tiles"]
    td = _largest_divisor_tile(n, 1024)
    steps = n // td
    grid_spec = pltpu.PrefetchScalarGridSpec(
        num_scalar_prefetch=2,
        grid=(steps,),
        in_specs=[pl.BlockSpec((1, 1, td), lambda i, lt, nu: (i, 0, 0), memory_space=pltpu.SMEM),
                  pl.BlockSpec((td, width), lambda i, lt, nu: (i, 0))],
        out_specs=pl.BlockSpec(memory_space=pl.ANY),
        scratch_shapes=[pltpu.VMEM((tm, width), F32), pltpu.SemaphoreType.DMA(()),
                        pltpu.SemaphoreType.DMA(())],
    )
    return pl.pallas_call(
        functools.partial(_dispatch_kernel, moe_tile=tm, n_tiles=n_tiles),
        grid_spec=grid_spec,
        out_shape=jax.ShapeDtypeStruct((n_tiles * tm, width), F32),
        compiler_params=_cparams("arbitrary"),
        name="dispatch",
    )(plan["last_tile"], plan["nu"], plan["pos"].reshape(steps, 1, td), x1e)


def _moe_kernel(ta_ref, tb_ref, nu_ref, x_ref, ga_ref, ua_ref, da_ref, gb_ref, ub_ref, db_ref, lg_ref, lb_ref,
                o_ref):
    del ta_ref, tb_ref
    g = pl.program_id(0)

    @pl.when(g < nu_ref[0])
    def _():
        x = x_ref[:, 0:D_MODEL]
        slab = x_ref[:, D_MODEL:D_MODEL + LANES]
        w1, w2, e1, e2 = slab[:, 0:1], slab[:, 1:2], slab[:, 2:3], slab[:, 3:4]
        first = e1 < e2
        wa = jnp.where(first, w1, w2)
        wb = jnp.where(first, w2, w1)
        xb = x.astype(BF16)

        def expert(gw, uw, dw):
            hid = jax.nn.silu(_dot(xb, gw[0])) * _dot(xb, uw[0])
            return _dot(hid.astype(BF16), dw[0])

        y = wa * expert(ga_ref, ua_ref, da_ref)
        y = y + wb * expert(gb_ref, ub_ref, db_ref)
        o_ref[...] = _layer_norm_rows(DEEPNORM_ALPHA * x + y, lg_ref[...], lb_ref[...])

    @pl.when(g >= nu_ref[0])
    def _():
        o_ref[...] = jnp.zeros(o_ref.shape, F32)


def _moe(xs_sorted, plan, wg_bf, wu_bf, wd_bf, ln_g, ln_b, tm):
    n_tiles = plan["n_tiles"]
    width = xs_sorted.shape[1]
    up_a = pl.BlockSpec((1, D_MODEL, D_EXPERT), lambda g, ta, tb, nu: (ta[g], 0, 0))
    dn_a = pl.BlockSpec((1, D_EXPERT, D_MODEL), lambda g, ta, tb, nu: (ta[g], 0, 0))
    up_b = pl.BlockSpec((1, D_MODEL, D_EXPERT), lambda g, ta, tb, nu: (tb[g], 0, 0))
    dn_b = pl.BlockSpec((1, D_EXPERT, D_MODEL), lambda g, ta, tb, nu: (tb[g], 0, 0))
    vec = pl.BlockSpec((1, D_MODEL), lambda g, ta, tb, nu: (0, 0))
    grid_spec = pltpu.PrefetchScalarGridSpec(
        num_scalar_prefetch=3,
        grid=(n_tiles,),
        in_specs=[pl.BlockSpec((tm, width), lambda g, ta, tb, nu: (g, 0)),
                  up_a, up_a, dn_a, up_b, up_b, dn_b, vec, vec],
        out_specs=pl.BlockSpec((tm, D_MODEL), lambda g, ta, tb, nu: (g, 0)),
    )
    return pl.pallas_call(
        _moe_kernel,
        grid_spec=grid_spec,
        out_shape=jax.ShapeDtypeStruct((n_tiles * tm, D_MODEL), F32),
        compiler_params=_cparams("arbitrary"),
        name="moe",
    )(plan["ta"], plan["tb"], plan["nu"], xs_sorted, wg_bf, wu_bf, wd_bf, wg_bf, wu_bf, wd_bf, ln_g, ln_b)


def _ple_kernel(pos_ref, posn_ref, x2_hbm, pp_ref, ps_ref, wg_ref, wp_ref, op_ref, os_ref, buf, sems, *, n_s):
    i = pl.program_id(0)
    last = pl.num_programs(0) - 1
    tm = buf.shape[1]
    slot = lax.rem(i, 2)

    def issue(idx_ref, s):
        def body(r, carry):
            pltpu.make_async_copy(x2_hbm.at[pl.ds(idx_ref[0, 0, r], 1)], buf.at[s, pl.ds(r, 1)], sems.at[s]).start()
            return carry

        lax.fori_loop(0, tm, body, 0, unroll=8)

    @pl.when(i == 0)
    def _():
        issue(pos_ref, slot)

    @pl.when(i < last)
    def _():
        issue(posn_ref, 1 - slot)

    pltpu.make_async_copy(x2_hbm.at[pl.ds(0, tm)], buf.at[slot], sems.at[slot]).wait()

    def rows(x, p):
        gate = jax.nn.sigmoid(_dot(x.astype(BF16), wg_ref[...]))
        return x + gate * _dot(p.astype(BF16), wp_ref[...])

    @pl.when(i < last)
    def _():
        op_ref[...] = rows(buf[slot], pp_ref[...])

    @pl.when(i == last)
    def _():
        os_ref[...] = rows(buf[slot, 0:n_s, :], ps_ref[...])


def _ple(x2_sorted, pos, p_p, p_s, wg_bf, wp_bf):
    n_p, n_s = p_p.shape[0], p_s.shape[0]
    tm = ROW_TILE
    nb = n_p // tm
    steps = nb + 1
    pos_pad = jnp.zeros((steps * tm,), I32).at[:n_p + n_s].set(pos).reshape(steps, 1, tm)
    full = lambda a: pl.BlockSpec(a.shape, lambda i: (0, 0))
    prow = lambda n: pl.BlockSpec((tm, n), lambda i: (jnp.minimum(i, nb - 1), 0))
    return pl.pallas_call(
        functools.partial(_ple_kernel, n_s=n_s),
        grid=(steps,),
        in_specs=[pl.BlockSpec((1, 1, tm), lambda i: (i, 0, 0), memory_space=pltpu.SMEM),
                  pl.BlockSpec((1, 1, tm), lambda i: (jnp.minimum(i + 1, nb), 0, 0), memory_space=pltpu.SMEM),
                  pl.BlockSpec(memory_space=pl.ANY), prow(D_PLE), full(p_s), full(wg_bf), full(wp_bf)],
        out_specs=[prow(D_MODEL), pl.BlockSpec((n_s, D_MODEL), lambda i: (0, 0))],
        out_shape=[jax.ShapeDtypeStruct((n_p, D_MODEL), F32), jax.ShapeDtypeStruct((n_s, D_MODEL), F32)],
        scratch_shapes=[pltpu.VMEM((2, tm, D_MODEL), F32), pltpu.SemaphoreType.DMA((2,))],
        compiler_params=_cparams("arbitrary"),
        name="ple",
    )(pos_pad, pos_pad, x2_sorted, p_p, p_s, wg_bf, wp_bf)


def _router_weights(w_group, w_router):
    wr = jnp.zeros((D_MODEL, LANES), F32)
    wr = wr.at[:, ROUTE_GROUP_LANE:ROUTE_GROUP_LANE + N_GROUPS].set(w_group)
    wr = wr.at[:, ROUTE_EXPERT_LANE:ROUTE_EXPERT_LANE + N_EXPERTS].set(w_router)
    hi = wr.astype(BF16)
    lo = (wr - hi.astype(F32)).astype(BF16)
    return hi, lo


def _layer_tail(i, acts_p, acts_s, w_list, xp, xs, p_p, p_s, ln_mix_g, ln_mix_b, ln_ffn_g, ln_ffn_b,
                w_group, w_router, w_exp_gate, w_exp_up, w_exp_down, w_ple_proj, w_ple_gate):
    r_hi, r_lo = _router_weights(w_group[i], w_router[i])
    x1e = _mix_out(acts_p, acts_s, w_list, xp, xs, ln_mix_g[i][None, :], ln_mix_b[i][None, :], r_hi, r_lo)
    plan = _route_plan(x1e[:, D_MODEL:D_MODEL + 4], MOE_TILE)
    xs_sorted = _dispatch(x1e, plan, MOE_TILE)
    x2_sorted = _moe(xs_sorted, plan, w_exp_gate[i].astype(BF16), w_exp_up[i].astype(BF16),
                     w_exp_down[i].astype(BF16), ln_ffn_g[i][None, :], ln_ffn_b[i][None, :], MOE_TILE)
    return _ple(x2_sorted, plan["pos"], p_p, p_s, w_ple_gate[i].astype(BF16), w_ple_proj[i].astype(BF16))


def kernel(x_prompt, x_sample, cache_k, cache_v, page_table, state_conv, state_mlstm_C, state_mlstm_n,
           state_mlstm_m, p_prompt, p_sample, w_in_even, conv_w, lambda_q1, lambda_k1, lambda_q2, lambda_k2,
           subln_w, w_out_even, w_in_odd, b_gates_odd, mh_norm_w, w_out_odd, ln_mix_g, ln_mix_b, ln_ffn_g,
           ln_ffn_b, w_group, w_router, w_exp_gate, w_exp_up, w_exp_down, w_ple_proj, w_ple_gate):
    bp, tp, _ = x_prompt.shape
    bs, ts, _ = x_sample.shape
    assert ts == 1 and tp % ROW_TILE == 0 and tp % ATTN_TILE == 0 and tp % MLSTM_CHUNK == 0
    n_p = bp * tp
    past_len = page_table.shape[1] * cache_k.shape[2]
    xp = x_prompt.reshape(n_p, D_MODEL)
    xs = x_sample.reshape(bs, D_MODEL)
    tail_w = (ln_mix_g, ln_mix_b, ln_ffn_g, ln_ffn_b, w_group, w_router, w_exp_gate, w_exp_up, w_exp_down,
              w_ple_proj, w_ple_gate)
    outs_p, outs_s = {}, {}
    for i in range(DEPTH):
        j = i // 2
        p_p = p_prompt[i].reshape(n_p, D_PLE)
        p_s = p_sample[i].reshape(bs, D_PLE)
        if i % 2 == 0:
            lam_init = 0.8 - 0.6 * math.exp(-0.3 * i)
            lam_vecs = jnp.stack([lambda_q1[j], lambda_k1[j], lambda_q2[j], lambda_k2[j]])
            sub = subln_w[j][None, :]
            w_bf = w_in_even[j].astype(BF16)
            tabs_p = _rope_tables(jnp.arange(tp))
            q0, v0 = 3 * D_CONV, 3 * D_CONV + 2 * QK_B
            wqt_bf = jnp.transpose(w_in_even[j][:, q0:q0 + QK_B]).astype(BF16)
            wvt_bf = jnp.transpose(w_in_even[j][:, v0:v0 + V_B]).astype(BF16)
            yc, qt, kf, vf, kb, vt, cst = _even_in_prompt(
                x_prompt if i == 0 else xp.reshape(bp, tp, D_MODEL), w_bf, wqt_bf, wvt_bf, conv_w[j],
                jnp.zeros((bp, CONV_W - 1, D_CONV), F32), tabs_p)
            o_p = _attn_prompt(qt, kb, vt, lam_vecs, sub, lam_init)
            outs_p.setdefault("k", []).append(kf.reshape(bp, tp, 2 * H_B, DH_B))
            outs_p.setdefault("v", []).append(vf.reshape(bp, tp, H_B, 2 * DH_B))
            outs_p.setdefault("c", []).append(cst)
            tabs_s = _rope_tables(jnp.full((1,), past_len, I32))
            prev_t = jnp.swapaxes(state_conv[j], 0, 1)
            yc_s, q_s, kf_s, vf_s, u_s = _even_in_decode(xs, w_bf, conv_w[j], prev_t, tabs_s)
            sub_head = jnp.arange(2 * H_B)
            sub_head = jnp.where(sub_head < H_B, 2 * sub_head, 2 * (sub_head - H_B) + 1)
            lane_head = jnp.arange(QK_B) // DH_B
            qbd = jnp.where(lane_head[None, None, :] == sub_head[None, :, None], q_s[:, None, :],
                            jnp.zeros((), BF16))
            n_pool = cache_k.shape[1]
            pages = cache_k.shape[0] * n_pool
            k_view = jnp.transpose(cache_k, (0, 1, 3, 4, 2)).reshape(pages, QK_B, PAGE_SIZE)
            v_view = cache_v.reshape(pages, PAGE_SIZE * H_B, 2 * DH_B)
            o8 = _attn_decode(qbd, kf_s[:, None, :], vf_s[:, None, :], k_view, v_view,
                              page_table + j * n_pool, lam_vecs, sub, lam_init)
            o_s = o8[:, :H_B, :].reshape(bs, V_B).astype(BF16)
            outs_s.setdefault("k", []).append(kf_s.reshape(bs, ts, 2 * H_B, DH_B))
            outs_s.setdefault("v", []).append(vf_s.reshape(bs, ts, H_B, 2 * DH_B))
            outs_s.setdefault("c", []).append(jnp.stack([state_conv[j][:, 1, :], u_s], axis=1))
            w_out = w_out_even[j].astype(BF16)
            w_list = [w_out[:D_CONV], w_out[D_CONV:]]
            acts_p = [yc.reshape(n_p, D_CONV), o_p.reshape(n_p, V_B)]
            acts_s = [yc_s, o_s]
        else:
            w_in = w_in_odd[j]
            qw, vw = H_C * DK_C, H_C * DV_C
            w_bf = w_in[:, :2 * qw + 2 * vw].astype(BF16)
            wg = jnp.zeros((D_MODEL, LANES), F32).at[:, :2 * H_C].set(w_in[:, 2 * qw + 2 * vw:]).astype(BF16)
            wgt = jnp.transpose(wg[:, :2 * H_C])
            bg = jnp.zeros((1, LANES), F32).at[0, :2 * H_C].set(b_gates_odd[j])
            bgt = jnp.broadcast_to(b_gates_odd[j][:, None], (2 * H_C, LANES))
            nw = mh_norm_w[j][None, :]
            q, k, v, o, gc, gr = _odd_in(xp.reshape(bp, tp, D_MODEL), w_bf, wg, wgt, bg, bgt, decode=False)
            h_p, c_p, n_pp, m_p = _mlstm(q, k, v, o, gc, gr, nw,
                                         jnp.zeros((bp, H_C, DK_C, DV_C), F32), jnp.zeros((bp, H_C, DK_C), F32),
                                         jnp.zeros((bp, H_C, LANES), F32))
            outs_p.setdefault("C", []).append(c_p)
            outs_p.setdefault("n", []).append(n_pp)
            outs_p.setdefault("m", []).append(m_p[:, :, 0])
            q_s, k_s, v_s, o_s2, gc_s, gr_s = _odd_in(xs, w_bf, wg, wgt, bg, bgt, decode=True)
            ch = MLSTM_CHUNK
            pad_rows = lambda a: jnp.zeros((bs, ch, a.shape[1]), a.dtype).at[:, 0, :].set(a)
            lane = jnp.arange(LANES)
            inert_c = jnp.where(lane < H_C, -jnp.inf, 0.0).astype(F32)
            gc_pad = jnp.broadcast_to(inert_c[None, None, :], (bs, ch, LANES)).at[:, 0, :].set(gc_s)
            inert_r = jnp.where(jnp.arange(2 * H_C) < H_C, -jnp.inf, 0.0).astype(F32)
            gr_pad = jnp.broadcast_to(inert_r[None, :, None], (bs, 2 * H_C, ch)).at[:, :, 0].set(gr_s.T)
            m0 = jnp.broadcast_to(state_mlstm_m[j][:, :, None], (bs, H_C, LANES))
            h_s, c_s, n_s, m_s = _mlstm(pad_rows(q_s), pad_rows(k_s), pad_rows(v_s), pad_rows(o_s2), gc_pad,
                                        gr_pad, nw, state_mlstm_C[j], state_mlstm_n[j], m0)
            outs_s.setdefault("C", []).append(c_s)
            outs_s.setdefault("n", []).append(n_s)
            outs_s.setdefault("m", []).append(m_s[:, :, 0])
            w_list = [w_out_odd[j].astype(BF16)]
            acts_p = [h_p.reshape(n_p, vw)]
            acts_s = [h_s[:, 0, :]]
        xp, xs = _layer_tail(i, acts_p, acts_s, w_list, xp, xs, p_p, p_s, *tail_w)
    st = lambda lst: jnp.stack(lst)
    return (xp.reshape(bp, tp, D_MODEL), xs.reshape(bs, ts, D_MODEL),
            st(outs_p["k"]), st(outs_p["v"]), st(outs_p["c"]), st(outs_p["C"]), st(outs_p["n"]), st(outs_p["m"]),
            st(outs_s["k"]), st(outs_s["v"]), st(outs_s["c"]), st(outs_s["C"]), st(outs_s["n"]), st(outs_s["m"]))
```

```python
import functools
import math

import numpy as np
import jax
import jax.numpy as jnp
from jax import lax
from jax.experimental import pallas as pl
from jax.experimental.pallas import tpu as pltpu

F32 = jnp.float32
BF16 = jnp.bfloat16
I32 = jnp.int32

D_MODEL = 1024
DEPTH = 2
PAGE_SIZE = 128
D_CONV = D_MODEL // 2
CONV_W = 3
H_B = 4
DH_B = 64
ROT_DIM = DH_B // 4
ROPE_THETA = 500000.0
H_C = 4
DK_C = (D_MODEL // 2) // H_C
DV_C = D_MODEL // H_C
N_GROUPS = 4
EXP_PER_GROUP = 4
N_EXPERTS = N_GROUPS * EXP_PER_GROUP
D_EXPERT = 512
D_PLE = 256
LN_EPS = 1e-5
LOG2_E = 1.4426950408889634
DEEPNORM_ALPHA = (2 * DEPTH) ** 0.25
QK_B = 2 * H_B * DH_B
V_B = H_B * 2 * DH_B
N_PAIRS = EXP_PER_GROUP * (EXP_PER_GROUP - 1) // 2
N_CLASSES = N_GROUPS * N_PAIRS

LANES = 128
VMEM_LIMIT = 56 * 1024 * 1024
ROW_TILE = 512
ATTN_TILE = 512
ONES_ROWS = 16
MLSTM_CHUNK = 128
MOE_TILE = 256
PAGES_PER_STEP = 16
ROUTE_ROWS = 32
ROUTE_GROUP_ROW = 0
ROUTE_EXPERT_ROW = 8


def _cparams(*sem):
    return pltpu.CompilerParams(dimension_semantics=sem, vmem_limit_bytes=VMEM_LIMIT)


def _dot(a, b):
    return jnp.dot(a, b, preferred_element_type=F32)


def _dot_nt(a, b):
    return lax.dot_general(a, b, (((1,), (1,)), ((), ())), preferred_element_type=F32)


def _dot_tn(a, b):
    return lax.dot_general(a, b, (((0,), (0,)), ((), ())), preferred_element_type=F32)


def _layer_norm_rows(z, g, b):
    mu = jnp.mean(z, axis=-1, keepdims=True)
    zc = z - mu
    var = jnp.mean(zc * zc, axis=-1, keepdims=True)
    return zc * lax.rsqrt(var + LN_EPS) * g + b


def _log_sigmoid(x):
    return jnp.minimum(x, 0.0) - jnp.log1p(jnp.exp(-jnp.abs(x)))


def _gated_conv(gate_b, u, um1, um2, cw_ref):
    cw = cw_ref[...]
    conv = um2 * cw[0:1, :] + um1 * cw[1:2, :] + u * cw[2:3, :]
    return (gate_b * conv).astype(BF16)


def _rope(z, cos, sin_lo, sin_hi, axis):
    half = ROT_DIM // 2
    return z * cos + pltpu.roll(z, QK_B - half, axis) * sin_lo + pltpu.roll(z, half, axis) * sin_hi


def _even_in_prompt_kernel(x_ref, w_ref, wqt_ref, wvt_ref, cw_ref, prev_ref, rc_ref, rs1_ref, rs2_ref,
                           rct_ref, rs1t_ref, rs2t_ref,
                           yc_ref, qt_ref, kf_ref, vf_ref, kb_ref, vt_ref, u_ref, carry_ref):
    xb = x_ref[0].astype(BF16)
    tm = xb.shape[0]

    def proj(c0, n):
        return _dot(xb, w_ref[:, c0:c0 + n])

    gate_b = proj(0, D_CONV)
    u = proj(D_CONV, D_CONV) * proj(2 * D_CONV, D_CONV)
    j = pl.program_id(1)

    @pl.when(j == 0)
    def _():
        carry_ref[...] = prev_ref[0]

    row = lax.broadcasted_iota(I32, u.shape, 0)
    c2 = carry_ref[0:1, :]
    c1 = carry_ref[1:2, :]
    um1 = jnp.where(row == 0, c1, pltpu.roll(u, 1, 0))
    um2 = jnp.where(row == 0, c2, jnp.where(row == 1, c1, pltpu.roll(u, 2, 0)))
    carry_ref[...] = u[tm - 2:tm, :]
    u_ref[0] = u[tm - 2:tm, :]
    yc_ref[0] = _gated_conv(gate_b, u, um1, um2, cw_ref)

    reps = QK_B // LANES
    tile = lambda r, ax: jnp.concatenate([r[...]] * reps, axis=ax)
    k = _rope(proj(3 * D_CONV + QK_B, QK_B), tile(rc_ref, 1), tile(rs1_ref, 1), tile(rs2_ref, 1), 1)
    kf_ref[0] = k
    kb_ref[0] = k.astype(BF16)
    v = proj(3 * D_CONV + 2 * QK_B, V_B)
    for h in range(H_B):
        vf_ref[0, pl.ds(h, tm, stride=H_B), :] = v[:, h * 2 * DH_B:(h + 1) * 2 * DH_B]
    qt = _rope(_dot_nt(wqt_ref[...], xb), tile(rct_ref, 0), tile(rs1t_ref, 0), tile(rs2t_ref, 0), 0)
    qt_ref[0] = (qt * (DH_B ** -0.5 * LOG2_E)).astype(BF16)
    vt_ref[0] = _dot_nt(wvt_ref[...], xb).astype(BF16)


def _even_in_decode_kernel(x_ref, w_ref, cw_ref, prev_ref, rc_ref, rs1_ref, rs2_ref,
                           yc_ref, q_ref, kf_ref, vf_ref, u_ref):
    xb = x_ref[...].astype(BF16)

    def proj(c0, n):
        return _dot(xb, w_ref[:, c0:c0 + n])

    gate_b = proj(0, D_CONV)
    u = proj(D_CONV, D_CONV) * proj(2 * D_CONV, D_CONV)
    u_ref[...] = u
    yc_ref[...] = _gated_conv(gate_b, u, prev_ref[1], prev_ref[0], cw_ref)
    reps = QK_B // LANES
    tile = lambda r: jnp.concatenate([r[...]] * reps, axis=1)
    cos, sin_lo, sin_hi = tile(rc_ref), tile(rs1_ref), tile(rs2_ref)
    q_ref[...] = (_rope(proj(3 * D_CONV, QK_B), cos, sin_lo, sin_hi, 1) * (DH_B ** -0.5)).astype(BF16)
    kf_ref[...] = _rope(proj(3 * D_CONV + QK_B, QK_B), cos, sin_lo, sin_hi, 1)
    vf_ref[...] = proj(3 * D_CONV + 2 * QK_B, V_B)


def _rope_tables(pos):
    half = ROT_DIM // 2
    inv = ROPE_THETA ** (-jnp.arange(half, dtype=F32) / half)
    ang = pos.astype(F32)[:, None] * inv[None, :]
    cos, sin = jnp.cos(ang), jnp.sin(ang)
    t = pos.shape[0]
    ones = jnp.ones((t, DH_B - ROT_DIM), F32)
    zeros = jnp.zeros((t, DH_B - ROT_DIM), F32)
    zh = jnp.zeros((t, half), F32)
    c = jnp.concatenate([cos, cos, ones], axis=1)
    s_lo = jnp.concatenate([-sin, zh, zeros], axis=1)
    s_hi = jnp.concatenate([zh, sin, zeros], axis=1)
    tile2 = lambda a: jnp.concatenate([a, a], axis=1)
    return tile2(c), tile2(s_lo), tile2(s_hi)


def _even_in_prompt(x, w_bf, wqt_bf, wvt_bf, conv_w, conv_prev, tables):
    b, t, _ = x.shape
    tm = ROW_TILE
    row3 = lambda n: pl.BlockSpec((1, tm, n), lambda i, j: (i, j, 0))
    col3 = lambda n: pl.BlockSpec((1, n, tm), lambda i, j: (i, 0, j))
    full2 = lambda a: pl.BlockSpec(a.shape, lambda i, j: (0, 0))
    tab = pl.BlockSpec((tm, LANES), lambda i, j: (j, 0))
    tab_t = pl.BlockSpec((LANES, tm), lambda i, j: (0, j))
    st = pl.BlockSpec((1, CONV_W - 1, D_CONV), lambda i, j: (i, 0, 0))
    tables_t = [jnp.transpose(a) for a in tables]
    outs = [jax.ShapeDtypeStruct((b, t, D_CONV), BF16), jax.ShapeDtypeStruct((b, QK_B, t), BF16),
            jax.ShapeDtypeStruct((b, t, QK_B), F32), jax.ShapeDtypeStruct((b, t * H_B, 2 * DH_B), F32),
            jax.ShapeDtypeStruct((b, t, QK_B), BF16), jax.ShapeDtypeStruct((b, V_B, t), BF16),
            jax.ShapeDtypeStruct((b, CONV_W - 1, D_CONV), F32)]
    return pl.pallas_call(
        _even_in_prompt_kernel,
        grid=(b, t // tm),
        in_specs=[row3(D_MODEL), full2(w_bf), full2(wqt_bf), full2(wvt_bf), full2(conv_w), st,
                  tab, tab, tab, tab_t, tab_t, tab_t],
        out_specs=[row3(D_CONV), col3(QK_B), row3(QK_B),
                   pl.BlockSpec((1, tm * H_B, 2 * DH_B), lambda i, j: (i, j, 0)), row3(QK_B), col3(V_B), st],
        out_shape=outs,
        scratch_shapes=[pltpu.VMEM((CONV_W - 1, D_CONV), F32)],
        compiler_params=_cparams("arbitrary", "arbitrary"),
        name="even_in_prompt",
    )(x, w_bf, wqt_bf, wvt_bf, conv_w, conv_prev, *tables, *tables_t)


def _even_in_decode(x, w_bf, conv_w, conv_prev_t, tables):
    n = x.shape[0]
    full = lambda a: pl.BlockSpec(a.shape, lambda i: (0,) * a.ndim)
    o2 = lambda c, dt: jax.ShapeDtypeStruct((n, c), dt)
    outs = [o2(D_CONV, BF16), o2(QK_B, BF16), o2(QK_B, F32), o2(V_B, F32), o2(D_CONV, F32)]
    ins = [x, w_bf, conv_w, conv_prev_t, *tables]
    return pl.pallas_call(
        _even_in_decode_kernel,
        grid=(1,),
        in_specs=[full(a) for a in ins],
        out_specs=[pl.BlockSpec(o.shape, lambda i: (0, 0)) for o in outs],
        out_shape=outs,
        compiler_params=_cparams("arbitrary"),
        name="even_in_decode",
    )(*ins)


def _lambda_value(lam_ref, lam_init):
    lv = lam_ref[...]
    a = jnp.sum(lv[0:1, :] * lv[1:2, :], axis=1, keepdims=True)
    b = jnp.sum(lv[2:3, :] * lv[3:4, :], axis=1, keepdims=True)
    return jnp.exp(a) - jnp.exp(b) + lam_init


def _sub_norm(o, sub_ref, lam_init):
    ms = jnp.mean(o * o, axis=-1, keepdims=True)
    return o * lax.rsqrt(ms + LN_EPS) * sub_ref[...] * (1.0 - lam_init)


def _attn_prompt_kernel(qt_ref, k_ref, vt_ref, lam_ref, sub_ref, o_ref, m_scr, acc_scr, *, lam_init):
    i = pl.program_id(2)
    tq = qt_ref.shape[2]
    tk = tq
    qt = qt_ref[0].astype(F32)
    feat = lax.broadcasted_iota(I32, qt.shape, 0)
    qq = jnp.concatenate([jnp.where(feat < DH_B, qt, 0.0), jnp.where(feat >= DH_B, qt, 0.0)],
                         axis=1).astype(BF16)
    m_scr[...] = jnp.full(m_scr.shape, -jnp.inf, F32)
    acc_scr[...] = jnp.zeros(acc_scr.shape, F32)
    ones = jnp.ones((ONES_ROWS, tk), BF16)

    def step(j, masked):
        start = pl.multiple_of(j * tk, tk)
        kj = k_ref[0, pl.ds(start, tk), :]
        vtj = jnp.concatenate([vt_ref[0, :, pl.ds(start, tk)], ones], axis=0)
        s = _dot(kj, qq)
        if masked:
            key = lax.broadcasted_iota(I32, s.shape, 0)
            qry = lax.broadcasted_iota(I32, s.shape, 1)
            qry = jnp.where(qry >= tq, qry - tq, qry)
            s = jnp.where(key <= qry, s, -jnp.inf)
        m_prev = m_scr[...]
        m_new = jnp.maximum(m_prev, jnp.max(s, axis=0, keepdims=True))
        alpha = jnp.exp2(m_prev - m_new)
        p = jnp.exp2(s - m_new).astype(BF16)
        acc_scr[...] = alpha * acc_scr[...] + _dot(vtj, p)
        m_scr[...] = m_new

    def body(j, carry):
        step(j, False)
        return carry

    lax.fori_loop(0, i, body, 0)
    step(i, True)
    on = acc_scr[0:LANES, :] / acc_scr[LANES:LANES + 1, :]
    lam = _lambda_value(lam_ref, lam_init)
    o = jnp.transpose(on[:, 0:tq] - lam * on[:, tq:2 * tq])
    o_ref[0] = _sub_norm(o, sub_ref, lam_init).astype(BF16)


def _attn_prompt(qt, k, vt, lam_vecs, subln, lam_init):
    b, t, _ = k.shape
    tq = ATTN_TILE
    full = lambda a: pl.BlockSpec(a.shape, lambda bi, h, i: (0, 0))
    return pl.pallas_call(
        functools.partial(_attn_prompt_kernel, lam_init=lam_init),
        grid=(b, H_B, t // tq),
        in_specs=[pl.BlockSpec((1, LANES, tq), lambda bi, h, i: (bi, h, i)),
                  pl.BlockSpec((1, t, LANES), lambda bi, h, i: (bi, 0, h)),
                  pl.BlockSpec((1, LANES, t), lambda bi, h, i: (bi, h, 0)),
                  full(lam_vecs), full(subln)],
        out_specs=pl.BlockSpec((1, tq, LANES), lambda bi, h, i: (bi, i, h)),
        out_shape=jax.ShapeDtypeStruct((b, t, V_B), BF16),
        scratch_shapes=[pltpu.VMEM((1, 2 * tq), F32), pltpu.VMEM((LANES + ONES_ROWS, 2 * tq), F32)],
        compiler_params=_cparams("arbitrary", "arbitrary", "arbitrary"),
        name="attn_prompt",
    )(qt, k, vt, lam_vecs, subln)


def _attn_decode_kernel(pt_ref, qbd_ref, kn_ref, vn_ref, lam_ref, sub_ref, *rest, lam_init, n_pages):
    del pt_ref
    k_refs = rest[:n_pages]
    v_refs = rest[n_pages:2 * n_pages]
    o_ref, m_scr, l_scr, acc_scr = rest[2 * n_pages:]
    j = pl.program_id(1)
    qbd = qbd_ref[0]

    @pl.when(j == 0)
    def _():
        s_new = jnp.sum(qbd.astype(F32) * kn_ref[0], axis=1, keepdims=True)
        m_scr[...] = s_new
        l_scr[...] = jnp.ones(l_scr.shape, F32)
        acc_scr[...] = jnp.broadcast_to(vn_ref[0], acc_scr.shape)

    s = jnp.concatenate([_dot(qbd, k_refs[r][0].astype(BF16)) for r in range(n_pages)], axis=1)
    m_prev = m_scr[...]
    m_new = jnp.maximum(m_prev, jnp.max(s, axis=1, keepdims=True))
    alpha = jnp.exp(m_prev - m_new)
    p = jnp.exp(s - m_new)
    l_scr[...] = alpha * l_scr[...] + jnp.sum(p, axis=1, keepdims=True)

    def head_pv(h):
        acc = None
        for r in range(n_pages):
            vh = v_refs[r][0, pl.ds(h, PAGE_SIZE, stride=H_B), :].astype(BF16)
            term = _dot(p[:, r * PAGE_SIZE:(r + 1) * PAGE_SIZE].astype(BF16), vh)
            acc = term if acc is None else acc + term
        return acc

    pv = jnp.concatenate([head_pv(h) for h in range(H_B)], axis=1)
    acc_scr[...] = alpha * acc_scr[...] + pv
    m_scr[...] = m_new

    @pl.when(j == pl.num_programs(1) - 1)
    def _():
        on = acc_scr[...] / l_scr[...]
        row = lax.broadcasted_iota(I32, (2 * H_B, 2 * DH_B), 0)
        head = jnp.where(row >= H_B, row - H_B, row)
        o8 = jnp.zeros((2 * H_B, 2 * DH_B), F32)
        for c in range(H_B):
            o8 = o8 + jnp.where(head == c, on[:, c * 2 * DH_B:(c + 1) * 2 * DH_B], 0.0)
        lam = _lambda_value(lam_ref, lam_init)
        o = o8 - lam * pltpu.roll(o8, H_B, 0)
        o_ref[0] = _sub_norm(o, sub_ref, lam_init)


def _attn_decode(qbd, k_new, v_new, cache_k, cache_v, page_table, lam_vecs, subln, lam_init):
    n = qbd.shape[0]
    n_pages = page_table.shape[1]
    pp = PAGES_PER_STEP
    width = V_B
    c2 = lambda a: pl.BlockSpec(a.shape, lambda b, j, pt: (0, 0))
    per_b = lambda a: pl.BlockSpec((1,) + a.shape[1:], lambda b, j, pt: (b, 0, 0))

    def page(r, arr):
        return pl.BlockSpec((1,) + arr.shape[1:], lambda b, j, pt: (pt[b, j * pp + r], 0, 0))

    grid_spec = pltpu.PrefetchScalarGridSpec(
        num_scalar_prefetch=1,
        grid=(n, n_pages // pp),
        in_specs=[per_b(qbd), per_b(k_new), per_b(v_new), c2(lam_vecs), c2(subln)]
        + [page(r, cache_k) for r in range(pp)] + [page(r, cache_v) for r in range(pp)],
        out_specs=pl.BlockSpec((1, 2 * H_B, 2 * DH_B), lambda b, j, pt: (b, 0, 0)),
        scratch_shapes=[pltpu.VMEM((2 * H_B, 1), F32), pltpu.VMEM((2 * H_B, 1), F32),
                        pltpu.VMEM((2 * H_B, width), F32)],
    )
    return pl.pallas_call(
        functools.partial(_attn_decode_kernel, lam_init=lam_init, n_pages=pp),
        grid_spec=grid_spec,
        out_shape=jax.ShapeDtypeStruct((n, 2 * H_B, 2 * DH_B), F32),
        compiler_params=_cparams("arbitrary", "arbitrary"),
        name="attn_decode",
    )(page_table, qbd, k_new, v_new, lam_vecs, subln, *([cache_k] * pp), *([cache_v] * pp))


def _odd_in_kernel(x_ref, w_ref, wg_ref, wgt_ref, bg_ref, bgt_ref,
                   q_ref, k_ref, v_ref, o_ref, gc_ref, gr_ref, *, decode):
    xb = x_ref[0].astype(BF16) if not decode else x_ref[...].astype(BF16)
    qw = H_C * DK_C
    vw = H_C * DV_C
    q = _dot(xb, w_ref[:, 0:qw]) * (DK_C ** -0.5)
    k = _dot(xb, w_ref[:, qw:2 * qw])
    v = _dot(xb, w_ref[:, 2 * qw:2 * qw + vw])
    o = _dot(xb, w_ref[:, 2 * qw + vw:2 * qw + 2 * vw])
    g_col = _dot(xb, wg_ref[...]) + bg_ref[...]
    lane = lax.broadcasted_iota(I32, g_col.shape, 1)
    g_col = jnp.where(lane < H_C, g_col, _log_sigmoid(g_col))
    g_row = _dot_nt(wgt_ref[...], xb) + bgt_ref[:, 0:1]
    sub = lax.broadcasted_iota(I32, g_row.shape, 0)
    g_row = jnp.where(sub < H_C, g_row, _log_sigmoid(g_row))
    if decode:
        q_ref[...] = q.astype(BF16)
        k_ref[...] = k.astype(BF16)
        v_ref[...] = v.astype(BF16)
        o_ref[...] = o
        gc_ref[...] = g_col
        gr_ref[...] = g_row
    else:
        q_ref[0] = q.astype(BF16)
        k_ref[0] = k.astype(BF16)
        v_ref[0] = v.astype(BF16)
        o_ref[0] = o
        gc_ref[0] = g_col
        gr_ref[0] = g_row


def _odd_in(x, w_bf, wg, wgt, bg, bgt, decode):
    qw, vw = H_C * DK_C, H_C * DV_C
    if decode:
        n = x.shape[0]
        ins = [x, w_bf, wg, wgt, bg, bgt]
        outs = [jax.ShapeDtypeStruct((n, qw), BF16), jax.ShapeDtypeStruct((n, qw), BF16),
                jax.ShapeDtypeStruct((n, vw), BF16), jax.ShapeDtypeStruct((n, vw), F32),
                jax.ShapeDtypeStruct((n, LANES), F32), jax.ShapeDtypeStruct((2 * H_C, n), F32)]
        return pl.pallas_call(
            functools.partial(_odd_in_kernel, decode=True),
            grid=(1,),
            in_specs=[pl.BlockSpec(a.shape, lambda i: (0, 0)) for a in ins],
            out_specs=[pl.BlockSpec(o.shape, lambda i: (0, 0)) for o in outs],
            out_shape=outs,
            compiler_params=_cparams("arbitrary"),
            name="odd_in_decode",
        )(*ins)
    b, t, _ = x.shape
    tm = ROW_TILE
    row3 = lambda n: pl.BlockSpec((1, tm, n), lambda i, j: (i, j, 0))
    full2 = lambda a: pl.BlockSpec(a.shape, lambda i, j: (0, 0))
    outs = [jax.ShapeDtypeStruct((b, t, qw), BF16), jax.ShapeDtypeStruct((b, t, qw), BF16),
            jax.ShapeDtypeStruct((b, t, vw), BF16), jax.ShapeDtypeStruct((b, t, vw), F32),
            jax.ShapeDtypeStruct((b, t, LANES), F32), jax.ShapeDtypeStruct((b, 2 * H_C, t), F32)]
    return pl.pallas_call(
        functools.partial(_odd_in_kernel, decode=False),
        grid=(b, t // tm),
        in_specs=[row3(D_MODEL), full2(w_bf), full2(wg), full2(wgt), full2(bg), full2(bgt)],
        out_specs=[row3(qw), row3(qw), row3(vw), row3(vw), row3(LANES),
                   pl.BlockSpec((1, 2 * H_C, tm), lambda i, j: (i, 0, j))],
        out_shape=outs,
        compiler_params=_cparams("arbitrary", "arbitrary"),
        name="odd_in_prompt",
    )(x, w_bf, wg, wgt, bg, bgt)


def _mlstm_kernel(q_ref, k_ref, v_ref, o_ref, gc_ref, gr_ref, nw_ref, c0_ref, n0_ref, m0_ref,
                  h_ref, c_out, n_out, m_out, c_scr, n_scr, m_scr):
    ci = pl.program_id(1)
    chunk = q_ref.shape[1]

    @pl.when(ci == 0)
    def _():
        c_scr[...] = c0_ref[0]
        n_scr[...] = n0_ref[0]
        m_scr[...] = m0_ref[0]

    t_idx = lax.broadcasted_iota(I32, (chunk, chunk), 0)
    s_idx = lax.broadcasted_iota(I32, (chunk, chunk), 1)
    causal = s_idx <= t_idx
    for h in range(H_C):
        q = q_ref[0, :, h * DK_C:(h + 1) * DK_C]
        k = k_ref[0, :, h * DK_C:(h + 1) * DK_C]
        v = v_ref[0, :, h * DV_C:(h + 1) * DV_C]
        ig_r = gr_ref[0, h:h + 1, :]
        lf_r = gr_ref[0, H_C + h:H_C + h + 1, :]
        ig_c = gc_ref[0, :, h:h + 1]
        lf_c = gc_ref[0, :, H_C + h:H_C + h + 1]
        bcum_c = jnp.sum(jnp.where(causal, lf_r, 0.0), axis=1, keepdims=True)
        bcum_r = jnp.sum(jnp.where(t_idx <= s_idx, lf_c, 0.0), axis=0, keepdims=True)
        m0 = m_scr[h:h + 1, 0:1]
        dmat = jnp.where(causal, bcum_c - bcum_r + ig_r, -jnp.inf)
        inter = bcum_c + m0
        m = jnp.maximum(inter, jnp.max(dmat, axis=1, keepdims=True))
        w = jnp.exp(dmat - m)
        g = jnp.exp(inter - m)
        s = _dot_nt(q, k) * w
        c0 = c_scr[h]
        n0 = n_scr[h:h + 1, :]
        num = g * _dot(q, c0.astype(BF16)) + _dot(s.astype(BF16), v)
        den = g * jnp.sum(q.astype(F32) * n0, axis=1, keepdims=True) + jnp.sum(s, axis=1, keepdims=True)
        hid = num / jnp.maximum(jnp.abs(den), jnp.exp(-m))
        m_last = m[chunk - 1:chunk, :]
        b_last = bcum_c[chunk - 1:chunk, :]
        w_last = jnp.exp(b_last - bcum_c + ig_c - m_last)
        g_last = jnp.exp(b_last + m0 - m_last)
        kw = k.astype(F32) * w_last
        c_scr[h] = g_last * c0 + _dot_tn(kw.astype(BF16), v)
        n_scr[h:h + 1, :] = g_last * n0 + jnp.sum(kw, axis=0, keepdims=True)
        m_scr[h:h + 1, :] = jnp.broadcast_to(m_last, (1, LANES))
        mu = jnp.mean(hid, axis=1, keepdims=True)
        hc = hid - mu
        var = jnp.mean(hc * hc, axis=1, keepdims=True)
        hn = hc * lax.rsqrt(var + LN_EPS) * nw_ref[:, h * DV_C:(h + 1) * DV_C]
        gate = jax.nn.sigmoid(o_ref[0, :, h * DV_C:(h + 1) * DV_C])
        h_ref[0, :, h * DV_C:(h + 1) * DV_C] = (gate * hn).astype(BF16)

    c_out[0] = c_scr[...]
    n_out[0] = n_scr[...]
    m_out[0] = m_scr[...]


def _mlstm(q, k, v, o, g_col, g_row, norm_w, c0, n0, m0):
    b, t, _ = q.shape
    ch = MLSTM_CHUNK
    qw, vw = H_C * DK_C, H_C * DV_C
    row = lambda n: pl.BlockSpec((1, ch, n), lambda i, j: (i, j, 0))
    st4 = pl.BlockSpec((1, H_C, DK_C, DV_C), lambda i, j: (i, 0, 0, 0))
    st3 = pl.BlockSpec((1, H_C, LANES), lambda i, j: (i, 0, 0))
    outs = [jax.ShapeDtypeStruct((b, t, vw), BF16), jax.ShapeDtypeStruct((b, H_C, DK_C, DV_C), F32),
            jax.ShapeDtypeStruct((b, H_C, DK_C), F32), jax.ShapeDtypeStruct((b, H_C, LANES), F32)]
    return pl.pallas_call(
        _mlstm_kernel,
        grid=(b, t // ch),
        in_specs=[row(qw), row(qw), row(vw), row(vw), row(LANES),
                  pl.BlockSpec((1, 2 * H_C, ch), lambda i, j: (i, 0, j)),
                  pl.BlockSpec(norm_w.shape, lambda i, j: (0, 0)), st4, st3, st3],
        out_specs=[row(vw), st4, st3, st3],
        out_shape=outs,
        scratch_shapes=[pltpu.VMEM((H_C, DK_C, DV_C), F32), pltpu.VMEM((H_C, DK_C), F32),
                        pltpu.VMEM((H_C, LANES), F32)],
        compiler_params=_cparams("arbitrary", "arbitrary"),
        name="mlstm",
    )(q, k, v, o, g_col, g_row, norm_w, c0, n0, m0)


def _mixed_rows(acts, w_refs, x, g_ref, b_ref):
    y = _dot(acts[0], w_refs[0][...])
    for a, w in zip(acts[1:], w_refs[1:]):
        y = y + _dot(a, w[...])
    return _layer_norm_rows(DEEPNORM_ALPHA * x + y, g_ref[...], b_ref[...])


def _route(x1, rw_ref):
    xh = x1.astype(BF16)
    xl = (x1 - xh.astype(F32)).astype(BF16)
    both = _dot_nt(rw_ref[...], xh)
    lg = both[0:ROUTE_ROWS, :] + both[ROUTE_ROWS:2 * ROUTE_ROWS, :] + _dot_nt(rw_ref[0:ROUTE_ROWS, :], xl)
    sub = lax.broadcasted_iota(I32, lg.shape, 0)
    big = jnp.int32(4 * ROUTE_ROWS)
    neg = -jnp.inf
    gl = jnp.where(sub < ROUTE_GROUP_ROW + N_GROUPS, lg, neg)
    g_max = jnp.max(gl, axis=0, keepdims=True)
    g_w = 1.0 / jnp.sum(jnp.exp(gl - g_max), axis=0, keepdims=True)
    g_idx = jnp.min(jnp.where(gl == g_max, sub, big), axis=0, keepdims=True)
    row_group = (sub - ROUTE_EXPERT_ROW) >> 2
    el = jnp.where(row_group == g_idx, lg, neg)
    e1 = jnp.max(el, axis=0, keepdims=True)
    i1 = jnp.min(jnp.where(el == e1, sub, big), axis=0, keepdims=True)
    z = jnp.sum(jnp.exp(el - e1), axis=0, keepdims=True)
    el2 = jnp.where(sub == i1, neg, el)
    e2 = jnp.max(el2, axis=0, keepdims=True)
    i2 = jnp.min(jnp.where(el2 == e2, sub, big), axis=0, keepdims=True)
    p1 = 1.0 / z
    p2 = jnp.exp(e2 - e1) / z
    w1 = p1 / (p1 + p2) * g_w
    w2 = p2 / (p1 + p2) * g_w
    id1 = (i1 - ROUTE_EXPERT_ROW).astype(F32)
    id2 = (i2 - ROUTE_EXPERT_ROW).astype(F32)
    return jnp.where(sub == 0, w1, jnp.where(sub == 1, w2, jnp.where(sub == 2, id1, jnp.where(sub == 3, id2, 0.0))))


def _slab_columns(rt):
    pad = jnp.zeros((LANES - rt.shape[0], rt.shape[1]), F32)
    return jnp.transpose(jnp.concatenate([rt, pad], axis=0))


def _mix_out_kernel(*refs, n_in, n_s):
    ap_refs = refs[:n_in]
    as_refs = refs[n_in:2 * n_in]
    w_refs = refs[2 * n_in:3 * n_in]
    xp_ref, xs_ref, g_ref, b_ref, rw_ref, out_ref, rt_ref = refs[3 * n_in:]
    i = pl.program_id(0)
    last = pl.num_programs(0) - 1

    @pl.when(i < last)
    def _():
        x1 = _mixed_rows([a[...] for a in ap_refs], w_refs, xp_ref[...], g_ref, b_ref)
        rt = _route(x1, rw_ref)
        out_ref[:, 0:D_MODEL] = x1
        out_ref[:, D_MODEL:D_MODEL + LANES] = _slab_columns(rt)
        rt_ref[...] = rt[0:8, :]

    @pl.when(i == last)
    def _():
        x1 = _mixed_rows([a[...] for a in as_refs], w_refs, xs_ref[...], g_ref, b_ref)
        rt = _route(jnp.concatenate([x1, jnp.zeros((LANES - n_s, D_MODEL), F32)], axis=0), rw_ref)
        out_ref[0:n_s, 0:D_MODEL] = x1
        out_ref[0:n_s, D_MODEL:D_MODEL + LANES] = _slab_columns(rt)[0:n_s, :]
        rt_ref[...] = jnp.zeros(rt_ref.shape, F32)
        rt_ref[:, 0:LANES] = rt[0:8, :]


def _mix_out(acts_p, acts_s, weights, xp, xs, ln_g, ln_b, rw):
    n_p, n_s = xp.shape[0], xs.shape[0]
    assert n_s <= LANES
    tm = ROW_TILE
    nb = n_p // tm
    n_in = len(acts_p)
    prow = lambda n: pl.BlockSpec((tm, n), lambda i: (jnp.minimum(i, nb - 1), 0))
    full = lambda a: pl.BlockSpec(a.shape, lambda i: (0, 0))
    width = D_MODEL + LANES
    return pl.pallas_call(
        functools.partial(_mix_out_kernel, n_in=n_in, n_s=n_s),
        grid=(nb + 1,),
        in_specs=[prow(a.shape[1]) for a in acts_p] + [full(a) for a in acts_s] + [full(w) for w in weights]
        + [prow(D_MODEL), full(xs), full(ln_g), full(ln_b), full(rw)],
        out_specs=[pl.BlockSpec((tm, width), lambda i: (i, 0)), pl.BlockSpec((8, tm), lambda i: (0, i))],
        out_shape=[jax.ShapeDtypeStruct((n_p + n_s, width), F32),
                   jax.ShapeDtypeStruct((8, (nb + 1) * tm), F32)],
        compiler_params=_cparams("arbitrary"),
        name="mix_out",
    )(*acts_p, *acts_s, *weights, xp, xs, ln_g, ln_b, rw)


def _route_plan(rt, tm):
    n = rt.shape[1]
    e1, e2 = rt[2, :].astype(I32), rt[3, :].astype(I32)
    ea, eb = jnp.minimum(e1, e2), jnp.maximum(e1, e2)
    la, lb = ea % EXP_PER_GROUP, eb % EXP_PER_GROUP
    cls = (ea // EXP_PER_GROUP) * N_PAIRS + (la * (2 * EXP_PER_GROUP - 1 - la)) // 2 + (lb - la - 1)
    onehot = (cls[:, None] == jnp.arange(N_CLASSES, dtype=I32)[None, :]).astype(I32)
    csum = jnp.cumsum(onehot, axis=0)
    rank = jnp.sum(onehot * csum, axis=1) - 1
    cnt = csum[-1]
    ntile = (cnt + tm - 1) // tm
    tile_end = jnp.cumsum(ntile)
    tile_start = tile_end - ntile
    n_used = tile_end[-1]
    n_tiles = -(-(n + N_CLASSES * (tm - 1)) // tm)
    pos = (jnp.sum(onehot * tile_start[None, :], axis=1) * tm + rank).astype(I32)
    pair_lo = np.array([a for a in range(EXP_PER_GROUP) for b in range(a + 1, EXP_PER_GROUP)], np.int32)
    pair_hi = np.array([b for a in range(EXP_PER_GROUP) for b in range(a + 1, EXP_PER_GROUP)], np.int32)
    cls_ids = np.arange(N_CLASSES)
    cls_a = jnp.asarray((cls_ids // N_PAIRS) * EXP_PER_GROUP + pair_lo[cls_ids % N_PAIRS], I32)
    cls_b = jnp.asarray((cls_ids // N_PAIRS) * EXP_PER_GROUP + pair_hi[cls_ids % N_PAIRS], I32)
    tile_ids = jnp.arange(n_tiles, dtype=I32)
    tile_cls = jnp.sum((tile_end[None, :] <= jnp.minimum(tile_ids, n_used - 1)[:, None]).astype(I32), axis=1)
    tile_cls = jnp.minimum(tile_cls, N_CLASSES - 1)
    onehot_t = (tile_cls[:, None] == jnp.arange(N_CLASSES, dtype=I32)[None, :]).astype(I32)
    last_tile = jnp.where(ntile > 0, tile_end - 1, -1).astype(I32)
    chg = jnp.concatenate([jnp.ones((1,), I32), (tile_cls[1:] != tile_cls[:-1]).astype(I32)])
    return dict(pos=pos, ta=jnp.sum(onehot_t * cls_a[None, :], axis=1),
                tb=jnp.sum(onehot_t * cls_b[None, :], axis=1), nu=n_used.reshape(1).astype(I32),
                chg=chg, last_tile=last_tile, n_tiles=n_tiles)


def _dispatch_kernel(lt_ref, nu_ref, pos_ref, src_ref, dst_hbm, zbuf, zsem, rsem, *, moe_tile, n_tiles):
    i = pl.program_id(0)
    td = pos_ref.shape[2]

    def zero_copy(tile):
        start = pl.multiple_of(tile * moe_tile, moe_tile)
        return pltpu.make_async_copy(zbuf, dst_hbm.at[pl.ds(start, moe_tile)], zsem)

    @pl.when(i == 0)
    def _():
        zbuf[...] = jnp.zeros(zbuf.shape, F32)
        for c in range(N_CLASSES):
            @pl.when(lt_ref[c] >= 0)
            def _(c=c):
                zero_copy(lt_ref[c]).start()
        for c in range(N_CLASSES):
            @pl.when(lt_ref[c] >= 0)
            def _(c=c):
                zero_copy(lt_ref[c]).wait()

        def spare_start(t, carry):
            zero_copy(t).start()
            return carry

        def spare_wait(t, carry):
            zero_copy(t).wait()
            return carry

        lax.fori_loop(nu_ref[0], n_tiles, spare_start, 0)
        lax.fori_loop(nu_ref[0], n_tiles, spare_wait, 0)

    def start_row(r, carry):
        pltpu.make_async_copy(src_ref.at[pl.ds(r, 1)], dst_hbm.at[pl.ds(pos_ref[0, 0, r], 1)], rsem).start()
        return carry

    lax.fori_loop(0, td, start_row, 0, unroll=8)
    pltpu.make_async_copy(src_ref, dst_hbm.at[pl.ds(0, td)], rsem).wait()


def _largest_divisor_tile(n, cap):
    for t in range(cap - cap % 8, 7, -8):
        if n % t == 0:
            return t
    raise ValueError(f"no row tile for {n} rows")


def _dispatch(x1e, plan, tm):
    n, width = x1e.shape
    n_tiles = plan["n_tiles"]
    td = _largest_divisor_tile(n, 1024)
    steps = n // td
    grid_spec = pltpu.PrefetchScalarGridSpec(
        num_scalar_prefetch=2,
        grid=(steps,),
        in_specs=[pl.BlockSpec((1, 1, td), lambda i, lt, nu: (i, 0, 0), memory_space=pltpu.SMEM),
                  pl.BlockSpec((td, width), lambda i, lt, nu: (i, 0))],
        out_specs=pl.BlockSpec(memory_space=pl.ANY),
        scratch_shapes=[pltpu.VMEM((tm, width), F32), pltpu.SemaphoreType.DMA(()),
                        pltpu.SemaphoreType.DMA(())],
    )
    return pl.pallas_call(
        functools.partial(_dispatch_kernel, moe_tile=tm, n_tiles=n_tiles),
        grid_spec=grid_spec,
        out_shape=jax.ShapeDtypeStruct((n_tiles * tm, width), F32),
        compiler_params=_cparams("arbitrary"),
        name="dispatch",
    )(plan["last_tile"], plan["nu"], plan["pos"].reshape(steps, 1, td), x1e)


def _moe_kernel(ta_ref, tb_ref, nu_ref, chg_ref, x_ref, ga_f32, ua_f32, da_f32, gb_f32, ub_f32, db_f32,
                lg_ref, lb_ref, o_ref, ga_ref, ua_ref, da_ref, gb_ref, ub_ref, db_ref):
    del ta_ref, tb_ref
    g = pl.program_id(0)

    @pl.when(chg_ref[g] == 1)
    def _():
        for src, dst in ((ga_f32, ga_ref), (ua_f32, ua_ref), (da_f32, da_ref),
                         (gb_f32, gb_ref), (ub_f32, ub_ref), (db_f32, db_ref)):
            dst[0] = src[0, 0].astype(BF16)

    @pl.when(g < nu_ref[0])
    def _():
        x = x_ref[:, 0:D_MODEL]
        slab = x_ref[:, D_MODEL:D_MODEL + LANES]
        w1, w2, e1, e2 = slab[:, 0:1], slab[:, 1:2], slab[:, 2:3], slab[:, 3:4]
        first = e1 < e2
        wa = jnp.where(first, w1, w2)
        wb = jnp.where(first, w2, w1)
        xb = x.astype(BF16)

        def expert(gw, uw, dw):
            hid = jax.nn.silu(_dot(xb, gw[0])) * _dot(xb, uw[0])
            return _dot(hid.astype(BF16), dw[0])

        y = wa * expert(ga_ref, ua_ref, da_ref)
        y = y + wb * expert(gb_ref, ub_ref, db_ref)
        o_ref[...] = _layer_norm_rows(DEEPNORM_ALPHA * x + y, lg_ref[...], lb_ref[...])

    @pl.when(g >= nu_ref[0])
    def _():
        o_ref[...] = jnp.zeros(o_ref.shape, F32)


def _moe(xs_sorted, plan, layer, w_gate, w_up, w_down, ln_g, ln_b, tm):
    n_tiles = plan["n_tiles"]
    width = xs_sorted.shape[1]
    up_a = pl.BlockSpec((1, 1, D_MODEL, D_EXPERT), lambda g, ta, tb, nu, chg: (layer, ta[g], 0, 0))
    dn_a = pl.BlockSpec((1, 1, D_EXPERT, D_MODEL), lambda g, ta, tb, nu, chg: (layer, ta[g], 0, 0))
    up_b = pl.BlockSpec((1, 1, D_MODEL, D_EXPERT), lambda g, ta, tb, nu, chg: (layer, tb[g], 0, 0))
    dn_b = pl.BlockSpec((1, 1, D_EXPERT, D_MODEL), lambda g, ta, tb, nu, chg: (layer, tb[g], 0, 0))
    vec = pl.BlockSpec((1, D_MODEL), lambda g, ta, tb, nu, chg: (0, 0))
    up_s = pltpu.VMEM((1, D_MODEL, D_EXPERT), BF16)
    dn_s = pltpu.VMEM((1, D_EXPERT, D_MODEL), BF16)
    grid_spec = pltpu.PrefetchScalarGridSpec(
        num_scalar_prefetch=4,
        grid=(n_tiles,),
        in_specs=[pl.BlockSpec((tm, width), lambda g, ta, tb, nu, chg: (g, 0)),
                  up_a, up_a, dn_a, up_b, up_b, dn_b, vec, vec],
        out_specs=pl.BlockSpec((tm, D_MODEL), lambda g, ta, tb, nu, chg: (g, 0)),
        scratch_shapes=[up_s, up_s, dn_s, up_s, up_s, dn_s],
    )
    return pl.pallas_call(
        _moe_kernel,
        grid_spec=grid_spec,
        out_shape=jax.ShapeDtypeStruct((n_tiles * tm, D_MODEL), F32),
        compiler_params=_cparams("arbitrary"),
        name="moe",
    )(plan["ta"], plan["tb"], plan["nu"], plan["chg"], xs_sorted, w_gate, w_up, w_down, w_gate, w_up, w_down,
      ln_g, ln_b)


def _ple_kernel(pos_ref, posn_ref, x2_hbm, pp_ref, ps_ref, wg_ref, wp_ref, op_ref, os_ref, buf, sems, *, n_s):
    i = pl.program_id(0)
    last = pl.num_programs(0) - 1
    tm = buf.shape[1]
    slot = lax.rem(i, 2)

    def issue(idx_ref, s):
        def body(r, carry):
            pltpu.make_async_copy(x2_hbm.at[pl.ds(idx_ref[0, 0, r], 1)], buf.at[s, pl.ds(r, 1)], sems.at[s]).start()
            return carry

        lax.fori_loop(0, tm, body, 0, unroll=8)

    @pl.when(i == 0)
    def _():
        issue(pos_ref, slot)

    @pl.when(i < last)
    def _():
        issue(posn_ref, 1 - slot)

    pltpu.make_async_copy(x2_hbm.at[pl.ds(0, tm)], buf.at[slot], sems.at[slot]).wait()

    def rows(x, p):
        gate = jax.nn.sigmoid(_dot(x.astype(BF16), wg_ref[...]))
        return x + gate * _dot(p.astype(BF16), wp_ref[...])

    @pl.when(i < last)
    def _():
        op_ref[...] = rows(buf[slot], pp_ref[0])

    @pl.when(i == last)
    def _():
        os_ref[...] = rows(buf[slot, 0:n_s, :], ps_ref[0])


def _ple(x2_sorted, pos, layer, p_p, p_s, wg_bf, wp_bf):
    n_p, n_s = p_p.shape[1], p_s.shape[1]
    tm = ROW_TILE
    nb = n_p // tm
    steps = nb + 1
    pos_pad = jnp.zeros((steps * tm,), I32).at[:n_p + n_s].set(pos).reshape(steps, 1, tm)
    full = lambda a: pl.BlockSpec(a.shape, lambda i: (0, 0))
    prow = lambda n: pl.BlockSpec((tm, n), lambda i: (jnp.minimum(i, nb - 1), 0))
    return pl.pallas_call(
        functools.partial(_ple_kernel, n_s=n_s),
        grid=(steps,),
        in_specs=[pl.BlockSpec((1, 1, tm), lambda i: (i, 0, 0), memory_space=pltpu.SMEM),
                  pl.BlockSpec((1, 1, tm), lambda i: (jnp.minimum(i + 1, nb), 0, 0), memory_space=pltpu.SMEM),
                  pl.BlockSpec(memory_space=pl.ANY),
                  pl.BlockSpec((1, tm, D_PLE), lambda i: (layer, jnp.minimum(i, nb - 1), 0)),
                  pl.BlockSpec((1, n_s, D_PLE), lambda i: (layer, 0, 0)), full(wg_bf), full(wp_bf)],
        out_specs=[prow(D_MODEL), pl.BlockSpec((n_s, D_MODEL), lambda i: (0, 0))],
        out_shape=[jax.ShapeDtypeStruct((n_p, D_MODEL), F32), jax.ShapeDtypeStruct((n_s, D_MODEL), F32)],
        scratch_shapes=[pltpu.VMEM((2, tm, D_MODEL), F32), pltpu.SemaphoreType.DMA((2,))],
        compiler_params=_cparams("arbitrary"),
        name="ple",
    )(pos_pad, pos_pad, x2_sorted, p_p, p_s, wg_bf, wp_bf)


def _router_weights(w_group, w_router):
    wr = jnp.zeros((ROUTE_ROWS, D_MODEL), F32)
    wr = wr.at[ROUTE_GROUP_ROW:ROUTE_GROUP_ROW + N_GROUPS, :].set(jnp.transpose(w_group))
    wr = wr.at[ROUTE_EXPERT_ROW:ROUTE_EXPERT_ROW + N_EXPERTS, :].set(jnp.transpose(w_router))
    hi = wr.astype(BF16)
    lo = (wr - hi.astype(F32)).astype(BF16)
    return jnp.concatenate([hi, lo], axis=0)


def _layer_tail(i, acts_p, acts_s, w_list, xp, xs, p_p, p_s, ln_mix_g, ln_mix_b, ln_ffn_g, ln_ffn_b,
                w_group, w_router, w_exp_gate, w_exp_up, w_exp_down, w_ple_proj, w_ple_gate):
    rw = _router_weights(w_group[i], w_router[i])
    x1e, rt = _mix_out(acts_p, acts_s, w_list, xp, xs, ln_mix_g[i][None, :], ln_mix_b[i][None, :], rw)
    plan = _route_plan(rt[:, :x1e.shape[0]], MOE_TILE)
    xs_sorted = _dispatch(x1e, plan, MOE_TILE)
    x2_sorted = _moe(xs_sorted, plan, i, w_exp_gate, w_exp_up, w_exp_down, ln_ffn_g[i][None, :],
                     ln_ffn_b[i][None, :], MOE_TILE)
    return _ple(x2_sorted, plan["pos"], i, p_p, p_s, w_ple_gate[i].astype(BF16), w_ple_proj[i].astype(BF16))


def kernel(x_prompt, x_sample, cache_k, cache_v, page_table, state_conv, state_mlstm_C, state_mlstm_n,
           state_mlstm_m, p_prompt, p_sample, w_in_even, conv_w, lambda_q1, lambda_k1, lambda_q2, lambda_k2,
           subln_w, w_out_even, w_in_odd, b_gates_odd, mh_norm_w, w_out_odd, ln_mix_g, ln_mix_b, ln_ffn_g,
           ln_ffn_b, w_group, w_router, w_exp_gate, w_exp_up, w_exp_down, w_ple_proj, w_ple_gate):
    bp, tp, _ = x_prompt.shape
    bs, ts, _ = x_sample.shape
    assert ts == 1 and tp % ROW_TILE == 0 and tp % ATTN_TILE == 0 and tp % MLSTM_CHUNK == 0
    n_p = bp * tp
    past_len = page_table.shape[1] * cache_k.shape[2]
    xp = x_prompt.reshape(n_p, D_MODEL)
    xs = x_sample.reshape(bs, D_MODEL)
    tail_w = (ln_mix_g, ln_mix_b, ln_ffn_g, ln_ffn_b, w_group, w_router, w_exp_gate, w_exp_up, w_exp_down,
              w_ple_proj, w_ple_gate)
    outs_p, outs_s = {}, {}
    for i in range(DEPTH):
        j = i // 2
        p_p = p_prompt.reshape(DEPTH, n_p, D_PLE)
        p_s = p_sample.reshape(DEPTH, bs, D_PLE)
        if i % 2 == 0:
            lam_init = 0.8 - 0.6 * math.exp(-0.3 * i)
            lam_vecs = jnp.stack([lambda_q1[j], lambda_k1[j], lambda_q2[j], lambda_k2[j]])
            sub = subln_w[j][None, :]
            w_bf = w_in_even[j].astype(BF16)
            tabs_p = _rope_tables(jnp.arange(tp))
            q0, v0 = 3 * D_CONV, 3 * D_CONV + 2 * QK_B
            wqt_bf = jnp.transpose(w_in_even[j][:, q0:q0 + QK_B]).astype(BF16)
            wvt_bf = jnp.transpose(w_in_even[j][:, v0:v0 + V_B]).astype(BF16)
            yc, qt, kf, vf, kb, vt, cst = _even_in_prompt(
                x_prompt if i == 0 else xp.reshape(bp, tp, D_MODEL), w_bf, wqt_bf, wvt_bf, conv_w[j],
                jnp.zeros((bp, CONV_W - 1, D_CONV), F32), tabs_p)
            o_p = _attn_prompt(qt, kb, vt, lam_vecs, sub, lam_init)
            outs_p.setdefault("k", []).append(kf.reshape(bp, tp, 2 * H_B, DH_B))
            outs_p.setdefault("v", []).append(vf.reshape(bp, tp, H_B, 2 * DH_B))
            outs_p.setdefault("c", []).append(cst)
            tabs_s = _rope_tables(jnp.full((1,), past_len, I32))
            prev_t = jnp.swapaxes(state_conv[j], 0, 1)
            yc_s, q_s, kf_s, vf_s, u_s = _even_in_decode(xs, w_bf, conv_w[j], prev_t, tabs_s)
            sub_head = jnp.arange(2 * H_B)
            sub_head = jnp.where(sub_head < H_B, 2 * sub_head, 2 * (sub_head - H_B) + 1)
            lane_head = jnp.arange(QK_B) // DH_B
            qbd = jnp.where(lane_head[None, None, :] == sub_head[None, :, None], q_s[:, None, :],
                            jnp.zeros((), BF16))
            n_pool = cache_k.shape[1]
            pages = cache_k.shape[0] * n_pool
            k_view = jnp.transpose(cache_k, (0, 1, 3, 4, 2)).reshape(pages, QK_B, PAGE_SIZE)
            v_view = cache_v.reshape(pages, PAGE_SIZE * H_B, 2 * DH_B)
            o8 = _attn_decode(qbd, kf_s[:, None, :], vf_s[:, None, :], k_view, v_view,
                              page_table + j * n_pool, lam_vecs, sub, lam_init)
            o_s = o8[:, :H_B, :].reshape(bs, V_B).astype(BF16)
            outs_s.setdefault("k", []).append(kf_s.reshape(bs, ts, 2 * H_B, DH_B))
            outs_s.setdefault("v", []).append(vf_s.reshape(bs, ts, H_B, 2 * DH_B))
            outs_s.setdefault("c", []).append(jnp.stack([state_conv[j][:, 1, :], u_s], axis=1))
            w_out = w_out_even[j].astype(BF16)
            w_list = [w_out[:D_CONV], w_out[D_CONV:]]
            acts_p = [yc.reshape(n_p, D_CONV), o_p.reshape(n_p, V_B)]
            acts_s = [yc_s, o_s]
        else:
            w_in = w_in_odd[j]
            qw, vw = H_C * DK_C, H_C * DV_C
            w_bf = w_in[:, :2 * qw + 2 * vw].astype(BF16)
            wg = jnp.zeros((D_MODEL, LANES), F32).at[:, :2 * H_C].set(w_in[:, 2 * qw + 2 * vw:]).astype(BF16)
            wgt = jnp.transpose(wg[:, :2 * H_C])
            bg = jnp.zeros((1, LANES), F32).at[0, :2 * H_C].set(b_gates_odd[j])
            bgt = jnp.broadcast_to(b_gates_odd[j][:, None], (2 * H_C, LANES))
            nw = mh_norm_w[j][None, :]
            q, k, v, o, gc, gr = _odd_in(xp.reshape(bp, tp, D_MODEL), w_bf, wg, wgt, bg, bgt, decode=False)
            h_p, c_p, n_pp, m_p = _mlstm(q, k, v, o, gc, gr, nw,
                                         jnp.zeros((bp, H_C, DK_C, DV_C), F32), jnp.zeros((bp, H_C, DK_C), F32),
                                         jnp.zeros((bp, H_C, LANES), F32))
            outs_p.setdefault("C", []).append(c_p)
            outs_p.setdefault("n", []).append(n_pp)
            outs_p.setdefault("m", []).append(m_p[:, :, 0])
            q_s, k_s, v_s, o_s2, gc_s, gr_s = _odd_in(xs, w_bf, wg, wgt, bg, bgt, decode=True)
            ch = MLSTM_CHUNK
            pad_rows = lambda a: jnp.zeros((bs, ch, a.shape[1]), a.dtype).at[:, 0, :].set(a)
            lane = jnp.arange(LANES)
            inert_c = jnp.where(lane < H_C, -jnp.inf, 0.0).astype(F32)
            gc_pad = jnp.broadcast_to(inert_c[None, None, :], (bs, ch, LANES)).at[:, 0, :].set(gc_s)
            inert_r = jnp.where(jnp.arange(2 * H_C) < H_C, -jnp.inf, 0.0).astype(F32)
            gr_pad = jnp.broadcast_to(inert_r[None, :, None], (bs, 2 * H_C, ch)).at[:, :, 0].set(gr_s.T)
            m0 = jnp.broadcast_to(state_mlstm_m[j][:, :, None], (bs, H_C, LANES))
            h_s, c_s, n_s, m_s = _mlstm(pad_rows(q_s), pad_rows(k_s), pad_rows(v_s), pad_rows(o_s2), gc_pad,
                                        gr_pad, nw, state_mlstm_C[j], state_mlstm_n[j], m0)
            outs_s.setdefault("C", []).append(c_s)
            outs_s.setdefault("n", []).append(n_s)
            outs_s.setdefault("m", []).append(m_s[:, :, 0])
            w_list = [w_out_odd[j].astype(BF16)]
            acts_p = [h_p.reshape(n_p, vw)]
            acts_s = [h_s[:, 0, :]]
        xp, xs = _layer_tail(i, acts_p, acts_s, w_list, xp, xs, p_p, p_s, *tail_w)
    st = lambda lst: jnp.stack(lst)
    return (xp.reshape(bp, tp, D_MODEL), xs.reshape(bs, ts, D_MODEL),
            st(outs_p["k"]), st(outs_p["v"]), st(outs_p["c"]), st(outs_p["C"]), st(outs_p["n"]), st(outs_p["m"]),
            st(outs_s["k"]), st(outs_s["v"]), st(outs_s["c"]), st(outs_s["C"]), st(outs_s["n"]), st(outs_s["m"]))
```

```python
import functools
import math

import numpy as np
import jax
import jax.numpy as jnp
from jax import lax
from jax.experimental import pallas as pl
from jax.experimental.pallas import tpu as pltpu

F32 = jnp.float32
BF16 = jnp.bfloat16
I32 = jnp.int32

D_MODEL = 1024
DEPTH = 2
PAGE_SIZE = 128
D_CONV = D_MODEL // 2
CONV_W = 3
H_B = 4
DH_B = 64
ROT_DIM = DH_B // 4
ROPE_THETA = 500000.0
H_C = 4
DK_C = (D_MODEL // 2) // H_C
DV_C = D_MODEL // H_C
N_GROUPS = 4
EXP_PER_GROUP = 4
N_EXPERTS = N_GROUPS * EXP_PER_GROUP
D_EXPERT = 512
D_PLE = 256
LN_EPS = 1e-5
LOG2_E = 1.4426950408889634
DEEPNORM_ALPHA = (2 * DEPTH) ** 0.25
QK_B = 2 * H_B * DH_B
V_B = H_B * 2 * DH_B
N_PAIRS = EXP_PER_GROUP * (EXP_PER_GROUP - 1) // 2
N_CLASSES = N_GROUPS * N_PAIRS
PAIR_SEQ = ((0, 1), (0, 2), (1, 2), (1, 3), (0, 3), (2, 3))
assert EXP_PER_GROUP == 4 and len(PAIR_SEQ) == N_PAIRS

LANES = 128
VMEM_LIMIT = 56 * 1024 * 1024
ROW_TILE = 512
ATTN_TILE = 512
ATTN_HEADS_PER_STEP = 2
ONES_ROWS = 16
MLSTM_CHUNK = 128
MOE_TILE = 256
PAGES_PER_STEP = 16
ROUTE_ROWS = 32
ROUTE_GROUP_ROW = 0
ROUTE_EXPERT_ROW = 8


def _cparams(*sem):
    return pltpu.CompilerParams(dimension_semantics=sem, vmem_limit_bytes=VMEM_LIMIT)


def _dot(a, b):
    return jnp.dot(a, b, preferred_element_type=F32)


def _dot_nt(a, b):
    return lax.dot_general(a, b, (((1,), (1,)), ((), ())), preferred_element_type=F32)


def _dot_tn(a, b):
    return lax.dot_general(a, b, (((0,), (0,)), ((), ())), preferred_element_type=F32)


def _layer_norm_rows(z, g, b):
    mu = jnp.mean(z, axis=-1, keepdims=True)
    zc = z - mu
    var = jnp.mean(zc * zc, axis=-1, keepdims=True)
    return zc * lax.rsqrt(var + LN_EPS) * g + b


def _log_sigmoid(x):
    return jnp.minimum(x, 0.0) - jnp.log1p(jnp.exp(-jnp.abs(x)))


def _gated_conv(gate_b, u, um1, um2, cw_ref):
    cw = cw_ref[...]
    conv = um2 * cw[0:1, :] + um1 * cw[1:2, :] + u * cw[2:3, :]
    return (gate_b * conv).astype(BF16)


def _rope(z, cos, sin_lo, sin_hi, axis):
    half = ROT_DIM // 2
    return z * cos + pltpu.roll(z, QK_B - half, axis) * sin_lo + pltpu.roll(z, half, axis) * sin_hi


def _even_in_prompt_kernel(x_ref, w_ref, wqt_ref, wvt_ref, cw_ref, prev_ref, rc_ref, rs1_ref, rs2_ref,
                           rct_ref, rs1t_ref, rs2t_ref,
                           yc_ref, qt_ref, kf_ref, vf_ref, kb_ref, vt_ref, u_ref, carry_ref):
    xb = x_ref[0].astype(BF16)
    tm = xb.shape[0]

    def proj(c0, n):
        return _dot(xb, w_ref[:, c0:c0 + n])

    gate_b = proj(0, D_CONV)
    u = proj(D_CONV, D_CONV) * proj(2 * D_CONV, D_CONV)
    j = pl.program_id(1)

    @pl.when(j == 0)
    def _():
        carry_ref[...] = prev_ref[0]

    row = lax.broadcasted_iota(I32, u.shape, 0)
    c2 = carry_ref[0:1, :]
    c1 = carry_ref[1:2, :]
    um1 = jnp.where(row == 0, c1, pltpu.roll(u, 1, 0))
    um2 = jnp.where(row == 0, c2, jnp.where(row == 1, c1, pltpu.roll(u, 2, 0)))
    carry_ref[...] = u[tm - 2:tm, :]
    u_ref[0] = u[tm - 2:tm, :]
    yc_ref[0] = _gated_conv(gate_b, u, um1, um2, cw_ref)

    reps = QK_B // LANES
    tile = lambda r, ax: jnp.concatenate([r[...]] * reps, axis=ax)
    k = _rope(proj(3 * D_CONV + QK_B, QK_B), tile(rc_ref, 1), tile(rs1_ref, 1), tile(rs2_ref, 1), 1)
    kf_ref[0] = k
    kb_ref[0] = k.astype(BF16)
    v = proj(3 * D_CONV + 2 * QK_B, V_B)
    for h in range(H_B):
        vf_ref[0, pl.ds(h, tm, stride=H_B), :] = v[:, h * 2 * DH_B:(h + 1) * 2 * DH_B]
    qt = _rope(_dot_nt(wqt_ref[...], xb), tile(rct_ref, 0), tile(rs1t_ref, 0), tile(rs2t_ref, 0), 0)
    qt_ref[0] = (qt * (DH_B ** -0.5 * LOG2_E)).astype(BF16)
    vt_ref[0] = _dot_nt(wvt_ref[...], xb).astype(BF16)


def _even_in_decode_kernel(x_ref, w_ref, cw_ref, prev_ref, rc_ref, rs1_ref, rs2_ref,
                           yc_ref, q_ref, kf_ref, vf_ref, u_ref):
    xb = x_ref[...].astype(BF16)

    def proj(c0, n):
        return _dot(xb, w_ref[:, c0:c0 + n])

    gate_b = proj(0, D_CONV)
    u = proj(D_CONV, D_CONV) * proj(2 * D_CONV, D_CONV)
    u_ref[...] = u
    yc_ref[...] = _gated_conv(gate_b, u, prev_ref[1], prev_ref[0], cw_ref)
    reps = QK_B // LANES
    tile = lambda r: jnp.concatenate([r[...]] * reps, axis=1)
    cos, sin_lo, sin_hi = tile(rc_ref), tile(rs1_ref), tile(rs2_ref)
    q_ref[...] = (_rope(proj(3 * D_CONV, QK_B), cos, sin_lo, sin_hi, 1) * (DH_B ** -0.5)).astype(BF16)
    kf_ref[...] = _rope(proj(3 * D_CONV + QK_B, QK_B), cos, sin_lo, sin_hi, 1)
    vf_ref[...] = proj(3 * D_CONV + 2 * QK_B, V_B)


def _rope_tables(pos):
    half = ROT_DIM // 2
    inv = ROPE_THETA ** (-jnp.arange(half, dtype=F32) / half)
    ang = pos.astype(F32)[:, None] * inv[None, :]
    cos, sin = jnp.cos(ang), jnp.sin(ang)
    t = pos.shape[0]
    ones = jnp.ones((t, DH_B - ROT_DIM), F32)
    zeros = jnp.zeros((t, DH_B - ROT_DIM), F32)
    zh = jnp.zeros((t, half), F32)
    c = jnp.concatenate([cos, cos, ones], axis=1)
    s_lo = jnp.concatenate([-sin, zh, zeros], axis=1)
    s_hi = jnp.concatenate([zh, sin, zeros], axis=1)
    tile2 = lambda a: jnp.concatenate([a, a], axis=1)
    return tile2(c), tile2(s_lo), tile2(s_hi)


def _even_in_prompt(x, w_bf, wqt_bf, wvt_bf, conv_w, conv_prev, tables):
    b, t, _ = x.shape
    tm = ROW_TILE
    row3 = lambda n: pl.BlockSpec((1, tm, n), lambda i, j: (i, j, 0))
    col3 = lambda n: pl.BlockSpec((1, n, tm), lambda i, j: (i, 0, j))
    full2 = lambda a: pl.BlockSpec(a.shape, lambda i, j: (0, 0))
    tab = pl.BlockSpec((tm, LANES), lambda i, j: (j, 0))
    tab_t = pl.BlockSpec((LANES, tm), lambda i, j: (0, j))
    st = pl.BlockSpec((1, CONV_W - 1, D_CONV), lambda i, j: (i, 0, 0))
    tables_t = [jnp.transpose(a) for a in tables]
    outs = [jax.ShapeDtypeStruct((b, t, D_CONV), BF16), jax.ShapeDtypeStruct((b, QK_B, t), BF16),
            jax.ShapeDtypeStruct((b, t, QK_B), F32), jax.ShapeDtypeStruct((b, t * H_B, 2 * DH_B), F32),
            jax.ShapeDtypeStruct((b, t, QK_B), BF16), jax.ShapeDtypeStruct((b, V_B, t), BF16),
            jax.ShapeDtypeStruct((b, CONV_W - 1, D_CONV), F32)]
    return pl.pallas_call(
        _even_in_prompt_kernel,
        grid=(b, t // tm),
        in_specs=[row3(D_MODEL), full2(w_bf), full2(wqt_bf), full2(wvt_bf), full2(conv_w), st,
                  tab, tab, tab, tab_t, tab_t, tab_t],
        out_specs=[row3(D_CONV), col3(QK_B), row3(QK_B),
                   pl.BlockSpec((1, tm * H_B, 2 * DH_B), lambda i, j: (i, j, 0)), row3(QK_B), col3(V_B), st],
        out_shape=outs,
        scratch_shapes=[pltpu.VMEM((CONV_W - 1, D_CONV), F32)],
        compiler_params=_cparams("arbitrary", "arbitrary"),
        name="even_in_prompt",
    )(x, w_bf, wqt_bf, wvt_bf, conv_w, conv_prev, *tables, *tables_t)


def _even_in_decode(x, w_bf, conv_w, conv_prev_t, tables):
    n = x.shape[0]
    full = lambda a: pl.BlockSpec(a.shape, lambda i: (0,) * a.ndim)
    o2 = lambda c, dt: jax.ShapeDtypeStruct((n, c), dt)
    outs = [o2(D_CONV, BF16), o2(QK_B, BF16), o2(QK_B, F32), o2(V_B, F32), o2(D_CONV, F32)]
    ins = [x, w_bf, conv_w, conv_prev_t, *tables]
    return pl.pallas_call(
        _even_in_decode_kernel,
        grid=(1,),
        in_specs=[full(a) for a in ins],
        out_specs=[pl.BlockSpec(o.shape, lambda i: (0, 0)) for o in outs],
        out_shape=outs,
        compiler_params=_cparams("arbitrary"),
        name="even_in_decode",
    )(*ins)


def _lambda_value(lam_ref, lam_init):
    lv = lam_ref[...]
    a = jnp.sum(lv[0:1, :] * lv[1:2, :], axis=1, keepdims=True)
    b = jnp.sum(lv[2:3, :] * lv[3:4, :], axis=1, keepdims=True)
    return jnp.exp(a) - jnp.exp(b) + lam_init


def _sub_norm(o, sub_ref, lam_init):
    ms = jnp.mean(o * o, axis=-1, keepdims=True)
    return o * lax.rsqrt(ms + LN_EPS) * sub_ref[...] * (1.0 - lam_init)


def _attn_prompt_kernel(qt_ref, k_ref, vt_ref, lam_ref, sub_ref, o_ref, m_scr, acc_scr, *, lam_init):
    i = pl.program_id(2)
    tq = qt_ref.shape[2]
    tk = tq
    n_heads = qt_ref.shape[1] // LANES
    m_scr[...] = jnp.full(m_scr.shape, -jnp.inf, F32)
    acc_scr[...] = jnp.zeros(acc_scr.shape, F32)
    ones = jnp.ones((ONES_ROWS, tk), BF16)
    qqs = []
    for h in range(n_heads):
        qt = qt_ref[0, h * LANES:(h + 1) * LANES, :].astype(F32)
        feat = lax.broadcasted_iota(I32, qt.shape, 0)
        qqs.append(jnp.concatenate([jnp.where(feat < DH_B, qt, 0.0), jnp.where(feat >= DH_B, qt, 0.0)],
                                   axis=1).astype(BF16))

    def step(j, masked):
        start = pl.multiple_of(j * tk, tk)
        for h in range(n_heads):
            kj = k_ref[0, pl.ds(start, tk), h * LANES:(h + 1) * LANES]
            vtj = jnp.concatenate([vt_ref[0, h * LANES:(h + 1) * LANES, pl.ds(start, tk)], ones], axis=0)
            s = _dot(kj, qqs[h])
            if masked:
                key = lax.broadcasted_iota(I32, s.shape, 0)
                qry = lax.broadcasted_iota(I32, s.shape, 1)
                qry = jnp.where(qry >= tq, qry - tq, qry)
                s = jnp.where(key <= qry, s, -jnp.inf)
            m_prev = m_scr[h]
            m_new = jnp.maximum(m_prev, jnp.max(s, axis=0, keepdims=True))
            alpha = jnp.exp2(m_prev - m_new)
            p = jnp.exp2(s - m_new).astype(BF16)
            acc_scr[h] = alpha * acc_scr[h] + _dot(vtj, p)
            m_scr[h] = m_new

    def body(j, carry):
        step(j, False)
        return carry

    lax.fori_loop(0, i, body, 0)
    step(i, True)
    lam = _lambda_value(lam_ref, lam_init)
    for h in range(n_heads):
        on = acc_scr[h, 0:LANES, :] / acc_scr[h, LANES:LANES + 1, :]
        o = jnp.transpose(on[:, 0:tq] - lam * on[:, tq:2 * tq])
        o_ref[0, :, h * LANES:(h + 1) * LANES] = _sub_norm(o, sub_ref, lam_init).astype(BF16)


def _attn_prompt(qt, k, vt, lam_vecs, subln, lam_init):
    b, t, _ = k.shape
    tq = ATTN_TILE
    nh = ATTN_HEADS_PER_STEP
    hw = nh * LANES
    full = lambda a: pl.BlockSpec(a.shape, lambda bi, h, i: (0, 0))
    return pl.pallas_call(
        functools.partial(_attn_prompt_kernel, lam_init=lam_init),
        grid=(b, H_B // nh, t // tq),
        in_specs=[pl.BlockSpec((1, hw, tq), lambda bi, h, i: (bi, h, i)),
                  pl.BlockSpec((1, t, hw), lambda bi, h, i: (bi, 0, h)),
                  pl.BlockSpec((1, hw, t), lambda bi, h, i: (bi, h, 0)),
                  full(lam_vecs), full(subln)],
        out_specs=pl.BlockSpec((1, tq, hw), lambda bi, h, i: (bi, i, h)),
        out_shape=jax.ShapeDtypeStruct((b, t, V_B), BF16),
        scratch_shapes=[pltpu.VMEM((nh, 1, 2 * tq), F32), pltpu.VMEM((nh, LANES + ONES_ROWS, 2 * tq), F32)],
        compiler_params=_cparams("arbitrary", "arbitrary", "arbitrary"),
        name="attn_prompt",
    )(qt, k, vt, lam_vecs, subln)


def _attn_decode_kernel(pt_ref, qbd_ref, kn_ref, vn_ref, lam_ref, sub_ref, *rest, lam_init, n_pages):
    del pt_ref
    k_refs = rest[:n_pages]
    v_refs = rest[n_pages:2 * n_pages]
    o_ref, m_scr, l_scr, acc_scr = rest[2 * n_pages:]
    j = pl.program_id(1)
    qbd = qbd_ref[0]

    @pl.when(j == 0)
    def _():
        s_new = jnp.sum(qbd.astype(F32) * kn_ref[0], axis=1, keepdims=True)
        m_scr[...] = s_new
        l_scr[...] = jnp.ones(l_scr.shape, F32)
        acc_scr[...] = jnp.broadcast_to(vn_ref[0], acc_scr.shape)

    s = jnp.concatenate([_dot(qbd, k_refs[r][0].astype(BF16)) for r in range(n_pages)], axis=1)
    m_prev = m_scr[...]
    m_new = jnp.maximum(m_prev, jnp.max(s, axis=1, keepdims=True))
    alpha = jnp.exp(m_prev - m_new)
    p = jnp.exp(s - m_new)
    l_scr[...] = alpha * l_scr[...] + jnp.sum(p, axis=1, keepdims=True)

    def head_pv(h):
        acc = None
        for r in range(n_pages):
            vh = v_refs[r][0, pl.ds(h, PAGE_SIZE, stride=H_B), :].astype(BF16)
            term = _dot(p[:, r * PAGE_SIZE:(r + 1) * PAGE_SIZE].astype(BF16), vh)
            acc = term if acc is None else acc + term
        return acc

    pv = jnp.concatenate([head_pv(h) for h in range(H_B)], axis=1)
    acc_scr[...] = alpha * acc_scr[...] + pv
    m_scr[...] = m_new

    @pl.when(j == pl.num_programs(1) - 1)
    def _():
        on = acc_scr[...] / l_scr[...]
        row = lax.broadcasted_iota(I32, (2 * H_B, 2 * DH_B), 0)
        head = jnp.where(row >= H_B, row - H_B, row)
        o8 = jnp.zeros((2 * H_B, 2 * DH_B), F32)
        for c in range(H_B):
            o8 = o8 + jnp.where(head == c, on[:, c * 2 * DH_B:(c + 1) * 2 * DH_B], 0.0)
        lam = _lambda_value(lam_ref, lam_init)
        o = o8 - lam * pltpu.roll(o8, H_B, 0)
        o_ref[0] = _sub_norm(o, sub_ref, lam_init)


def _attn_decode(qbd, k_new, v_new, cache_k, cache_v, page_table, lam_vecs, subln, lam_init):
    n = qbd.shape[0]
    n_pages = page_table.shape[1]
    pp = PAGES_PER_STEP
    width = V_B
    c2 = lambda a: pl.BlockSpec(a.shape, lambda b, j, pt: (0, 0))
    per_b = lambda a: pl.BlockSpec((1,) + a.shape[1:], lambda b, j, pt: (b, 0, 0))

    def page(r, arr):
        return pl.BlockSpec((1,) + arr.shape[1:], lambda b, j, pt: (pt[b, j * pp + r], 0, 0))

    grid_spec = pltpu.PrefetchScalarGridSpec(
        num_scalar_prefetch=1,
        grid=(n, n_pages // pp),
        in_specs=[per_b(qbd), per_b(k_new), per_b(v_new), c2(lam_vecs), c2(subln)]
        + [page(r, cache_k) for r in range(pp)] + [page(r, cache_v) for r in range(pp)],
        out_specs=pl.BlockSpec((1, 2 * H_B, 2 * DH_B), lambda b, j, pt: (b, 0, 0)),
        scratch_shapes=[pltpu.VMEM((2 * H_B, 1), F32), pltpu.VMEM((2 * H_B, 1), F32),
                        pltpu.VMEM((2 * H_B, width), F32)],
    )
    return pl.pallas_call(
        functools.partial(_attn_decode_kernel, lam_init=lam_init, n_pages=pp),
        grid_spec=grid_spec,
        out_shape=jax.ShapeDtypeStruct((n, 2 * H_B, 2 * DH_B), F32),
        compiler_params=_cparams("arbitrary", "arbitrary"),
        name="attn_decode",
    )(page_table, qbd, k_new, v_new, lam_vecs, subln, *([cache_k] * pp), *([cache_v] * pp))


def _odd_in_kernel(x_ref, w_ref, wg_ref, wgt_ref, bg_ref, bgt_ref,
                   q_ref, k_ref, v_ref, o_ref, gc_ref, gr_ref, *, decode):
    xb = x_ref[0].astype(BF16) if not decode else x_ref[...].astype(BF16)
    qw = H_C * DK_C
    vw = H_C * DV_C
    q = _dot(xb, w_ref[:, 0:qw]) * (DK_C ** -0.5)
    k = _dot(xb, w_ref[:, qw:2 * qw])
    v = _dot(xb, w_ref[:, 2 * qw:2 * qw + vw])
    o = _dot(xb, w_ref[:, 2 * qw + vw:2 * qw + 2 * vw])
    g_col = _dot(xb, wg_ref[...]) + bg_ref[...]
    lane = lax.broadcasted_iota(I32, g_col.shape, 1)
    g_col = jnp.where(lane < H_C, g_col, _log_sigmoid(g_col))
    g_row = _dot_nt(wgt_ref[...], xb) + bgt_ref[:, 0:1]
    sub = lax.broadcasted_iota(I32, g_row.shape, 0)
    g_row = jnp.where(sub < H_C, g_row, _log_sigmoid(g_row))
    if decode:
        q_ref[...] = q.astype(BF16)
        k_ref[...] = k.astype(BF16)
        v_ref[...] = v.astype(BF16)
        o_ref[...] = o
        gc_ref[...] = g_col
        gr_ref[...] = g_row
    else:
        q_ref[0] = q.astype(BF16)
        k_ref[0] = k.astype(BF16)
        v_ref[0] = v.astype(BF16)
        o_ref[0] = o
        gc_ref[0] = g_col
        gr_ref[0] = g_row


def _odd_in(x, w_bf, wg, wgt, bg, bgt, decode):
    qw, vw = H_C * DK_C, H_C * DV_C
    if decode:
        n = x.shape[0]
        ins = [x, w_bf, wg, wgt, bg, bgt]
        outs = [jax.ShapeDtypeStruct((n, qw), BF16), jax.ShapeDtypeStruct((n, qw), BF16),
                jax.ShapeDtypeStruct((n, vw), BF16), jax.ShapeDtypeStruct((n, vw), F32),
                jax.ShapeDtypeStruct((n, LANES), F32), jax.ShapeDtypeStruct((2 * H_C, n), F32)]
        return pl.pallas_call(
            functools.partial(_odd_in_kernel, decode=True),
            grid=(1,),
            in_specs=[pl.BlockSpec(a.shape, lambda i: (0, 0)) for a in ins],
            out_specs=[pl.BlockSpec(o.shape, lambda i: (0, 0)) for o in outs],
            out_shape=outs,
            compiler_params=_cparams("arbitrary"),
            name="odd_in_decode",
        )(*ins)
    b, t, _ = x.shape
    tm = ROW_TILE
    row3 = lambda n: pl.BlockSpec((1, tm, n), lambda i, j: (i, j, 0))
    full2 = lambda a: pl.BlockSpec(a.shape, lambda i, j: (0, 0))
    outs = [jax.ShapeDtypeStruct((b, t, qw), BF16), jax.ShapeDtypeStruct((b, t, qw), BF16),
            jax.ShapeDtypeStruct((b, t, vw), BF16), jax.ShapeDtypeStruct((b, t, vw), F32),
            jax.ShapeDtypeStruct((b, t, LANES), F32), jax.ShapeDtypeStruct((b, 2 * H_C, t), F32)]
    return pl.pallas_call(
        functools.partial(_odd_in_kernel, decode=False),
        grid=(b, t // tm),
        in_specs=[row3(D_MODEL), full2(w_bf), full2(wg), full2(wgt), full2(bg), full2(bgt)],
        out_specs=[row3(qw), row3(qw), row3(vw), row3(vw), row3(LANES),
                   pl.BlockSpec((1, 2 * H_C, tm), lambda i, j: (i, 0, j))],
        out_shape=outs,
        compiler_params=_cparams("arbitrary", "arbitrary"),
        name="odd_in_prompt",
    )(x, w_bf, wg, wgt, bg, bgt)


def _mlstm_kernel(q_ref, k_ref, v_ref, o_ref, gc_ref, gr_ref, nw_ref, c0_ref, n0_ref, m0_ref,
                  h_ref, c_out, n_out, m_out, c_scr, n_scr, m_scr):
    ci = pl.program_id(1)
    chunk = q_ref.shape[1]

    @pl.when(ci == 0)
    def _():
        c_scr[...] = c0_ref[0]
        n_scr[...] = n0_ref[0]
        m_scr[...] = m0_ref[0]

    t_idx = lax.broadcasted_iota(I32, (chunk, chunk), 0)
    s_idx = lax.broadcasted_iota(I32, (chunk, chunk), 1)
    causal = s_idx <= t_idx
    for h in range(H_C):
        q = q_ref[0, :, h * DK_C:(h + 1) * DK_C]
        k = k_ref[0, :, h * DK_C:(h + 1) * DK_C]
        v = v_ref[0, :, h * DV_C:(h + 1) * DV_C]
        ig_r = gr_ref[0, h:h + 1, :]
        lf_r = gr_ref[0, H_C + h:H_C + h + 1, :]
        ig_c = gc_ref[0, :, h:h + 1]
        lf_c = gc_ref[0, :, H_C + h:H_C + h + 1]
        bcum_c = jnp.sum(jnp.where(causal, lf_r, 0.0), axis=1, keepdims=True)
        bcum_r = jnp.sum(jnp.where(t_idx <= s_idx, lf_c, 0.0), axis=0, keepdims=True)
        m0 = m_scr[h:h + 1, 0:1]
        dmat = jnp.where(causal, bcum_c - bcum_r + ig_r, -jnp.inf)
        inter = bcum_c + m0
        m = jnp.maximum(inter, jnp.max(dmat, axis=1, keepdims=True))
        w = jnp.exp(dmat - m)
        g = jnp.exp(inter - m)
        s = _dot_nt(q, k) * w
        c0 = c_scr[h]
        n0 = n_scr[h:h + 1, :]
        num = g * _dot(q, c0.astype(BF16)) + _dot(s.astype(BF16), v)
        den = g * jnp.sum(q.astype(F32) * n0, axis=1, keepdims=True) + jnp.sum(s, axis=1, keepdims=True)
        hid = num / jnp.maximum(jnp.abs(den), jnp.exp(-m))
        m_last = m[chunk - 1:chunk, :]
        b_last = bcum_c[chunk - 1:chunk, :]
        w_last = jnp.exp(b_last - bcum_c + ig_c - m_last)
        g_last = jnp.exp(b_last + m0 - m_last)
        kw = k.astype(F32) * w_last
        c_scr[h] = g_last * c0 + _dot_tn(kw.astype(BF16), v)
        n_scr[h:h + 1, :] = g_last * n0 + jnp.sum(kw, axis=0, keepdims=True)
        m_scr[h:h + 1, :] = jnp.broadcast_to(m_last, (1, LANES))
        mu = jnp.mean(hid, axis=1, keepdims=True)
        hc = hid - mu
        var = jnp.mean(hc * hc, axis=1, keepdims=True)
        hn = hc * lax.rsqrt(var + LN_EPS) * nw_ref[:, h * DV_C:(h + 1) * DV_C]
        gate = jax.nn.sigmoid(o_ref[0, :, h * DV_C:(h + 1) * DV_C])
        h_ref[0, :, h * DV_C:(h + 1) * DV_C] = (gate * hn).astype(BF16)

    c_out[0] = c_scr[...]
    n_out[0] = n_scr[...]
    m_out[0] = m_scr[...]


def _mlstm(q, k, v, o, g_col, g_row, norm_w, c0, n0, m0):
    b, t, _ = q.shape
    ch = MLSTM_CHUNK
    qw, vw = H_C * DK_C, H_C * DV_C
    row = lambda n: pl.BlockSpec((1, ch, n), lambda i, j: (i, j, 0))
    st4 = pl.BlockSpec((1, H_C, DK_C, DV_C), lambda i, j: (i, 0, 0, 0))
    st3 = pl.BlockSpec((1, H_C, LANES), lambda i, j: (i, 0, 0))
    outs = [jax.ShapeDtypeStruct((b, t, vw), BF16), jax.ShapeDtypeStruct((b, H_C, DK_C, DV_C), F32),
            jax.ShapeDtypeStruct((b, H_C, DK_C), F32), jax.ShapeDtypeStruct((b, H_C, LANES), F32)]
    return pl.pallas_call(
        _mlstm_kernel,
        grid=(b, t // ch),
        in_specs=[row(qw), row(qw), row(vw), row(vw), row(LANES),
                  pl.BlockSpec((1, 2 * H_C, ch), lambda i, j: (i, 0, j)),
                  pl.BlockSpec(norm_w.shape, lambda i, j: (0, 0)), st4, st3, st3],
        out_specs=[row(vw), st4, st3, st3],
        out_shape=outs,
        scratch_shapes=[pltpu.VMEM((H_C, DK_C, DV_C), F32), pltpu.VMEM((H_C, DK_C), F32),
                        pltpu.VMEM((H_C, LANES), F32)],
        compiler_params=_cparams("arbitrary", "arbitrary"),
        name="mlstm",
    )(q, k, v, o, g_col, g_row, norm_w, c0, n0, m0)


def _mixed_rows(acts, w_refs, x, g_ref, b_ref):
    y = _dot(acts[0], w_refs[0][...])
    for a, w in zip(acts[1:], w_refs[1:]):
        y = y + _dot(a, w[...])
    return _layer_norm_rows(DEEPNORM_ALPHA * x + y, g_ref[...], b_ref[...])


def _route(x1, rw_ref):
    xh = x1.astype(BF16)
    xl = (x1 - xh.astype(F32)).astype(BF16)
    both = _dot_nt(rw_ref[...], xh)
    lg = both[0:ROUTE_ROWS, :] + both[ROUTE_ROWS:2 * ROUTE_ROWS, :] + _dot_nt(rw_ref[0:ROUTE_ROWS, :], xl)
    sub = lax.broadcasted_iota(I32, lg.shape, 0)
    big = jnp.int32(4 * ROUTE_ROWS)
    neg = -jnp.inf
    gl = jnp.where(sub < ROUTE_GROUP_ROW + N_GROUPS, lg, neg)
    g_max = jnp.max(gl, axis=0, keepdims=True)
    g_w = 1.0 / jnp.sum(jnp.exp(gl - g_max), axis=0, keepdims=True)
    g_idx = jnp.min(jnp.where(gl == g_max, sub, big), axis=0, keepdims=True)
    row_group = (sub - ROUTE_EXPERT_ROW) >> 2
    el = jnp.where(row_group == g_idx, lg, neg)
    e1 = jnp.max(el, axis=0, keepdims=True)
    i1 = jnp.min(jnp.where(el == e1, sub, big), axis=0, keepdims=True)
    z = jnp.sum(jnp.exp(el - e1), axis=0, keepdims=True)
    el2 = jnp.where(sub == i1, neg, el)
    e2 = jnp.max(el2, axis=0, keepdims=True)
    i2 = jnp.min(jnp.where(el2 == e2, sub, big), axis=0, keepdims=True)
    p1 = 1.0 / z
    p2 = jnp.exp(e2 - e1) / z
    w1 = p1 / (p1 + p2) * g_w
    w2 = p2 / (p1 + p2) * g_w
    id1 = (i1 - ROUTE_EXPERT_ROW).astype(F32)
    id2 = (i2 - ROUTE_EXPERT_ROW).astype(F32)
    return jnp.where(sub == 0, w1, jnp.where(sub == 1, w2, jnp.where(sub == 2, id1, jnp.where(sub == 3, id2, 0.0))))


def _slab_columns(rt):
    pad = jnp.zeros((LANES - rt.shape[0], rt.shape[1]), F32)
    return jnp.transpose(jnp.concatenate([rt, pad], axis=0))


def _mix_out_kernel(*refs, n_in, n_s):
    ap_refs = refs[:n_in]
    as_refs = refs[n_in:2 * n_in]
    w_refs = refs[2 * n_in:3 * n_in]
    xp_ref, xs_ref, g_ref, b_ref, rw_ref, out_ref, rt_ref = refs[3 * n_in:]
    i = pl.program_id(0)
    last = pl.num_programs(0) - 1

    @pl.when(i < last)
    def _():
        x1 = _mixed_rows([a[...] for a in ap_refs], w_refs, xp_ref[...], g_ref, b_ref)
        rt = _route(x1, rw_ref)
        out_ref[:, 0:D_MODEL] = x1
        out_ref[:, D_MODEL:D_MODEL + LANES] = _slab_columns(rt)
        rt_ref[...] = rt[0:8, :]

    @pl.when(i == last)
    def _():
        x1 = _mixed_rows([a[...] for a in as_refs], w_refs, xs_ref[...], g_ref, b_ref)
        rt = _route(jnp.concatenate([x1, jnp.zeros((LANES - n_s, D_MODEL), F32)], axis=0), rw_ref)
        out_ref[0:n_s, 0:D_MODEL] = x1
        out_ref[0:n_s, D_MODEL:D_MODEL + LANES] = _slab_columns(rt)[0:n_s, :]
        rt_ref[...] = jnp.zeros(rt_ref.shape, F32)
        rt_ref[:, 0:LANES] = rt[0:8, :]


def _mix_out(acts_p, acts_s, weights, xp, xs, ln_g, ln_b, rw):
    n_p, n_s = xp.shape[0], xs.shape[0]
    assert n_s <= LANES
    tm = ROW_TILE
    nb = n_p // tm
    n_in = len(acts_p)
    prow = lambda n: pl.BlockSpec((tm, n), lambda i: (jnp.minimum(i, nb - 1), 0))
    full = lambda a: pl.BlockSpec(a.shape, lambda i: (0, 0))
    width = D_MODEL + LANES
    return pl.pallas_call(
        functools.partial(_mix_out_kernel, n_in=n_in, n_s=n_s),
        grid=(nb + 1,),
        in_specs=[prow(a.shape[1]) for a in acts_p] + [full(a) for a in acts_s] + [full(w) for w in weights]
        + [prow(D_MODEL), full(xs), full(ln_g), full(ln_b), full(rw)],
        out_specs=[pl.BlockSpec((tm, width), lambda i: (i, 0)), pl.BlockSpec((8, tm), lambda i: (0, i))],
        out_shape=[jax.ShapeDtypeStruct((n_p + n_s, width), F32),
                   jax.ShapeDtypeStruct((8, (nb + 1) * tm), F32)],
        compiler_params=_cparams("arbitrary"),
        name="mix_out",
    )(*acts_p, *acts_s, *weights, xp, xs, ln_g, ln_b, rw)


def _route_plan(rt, tm):
    n = rt.shape[1]
    e1, e2 = rt[2, :].astype(I32), rt[3, :].astype(I32)
    ea, eb = jnp.minimum(e1, e2), jnp.maximum(e1, e2)
    la, lb = ea % EXP_PER_GROUP, eb % EXP_PER_GROUP
    pair_rank = sum(k * ((la == a) & (lb == b)).astype(I32) for k, (a, b) in enumerate(PAIR_SEQ))
    cls = (ea // EXP_PER_GROUP) * N_PAIRS + pair_rank
    onehot = (cls[:, None] == jnp.arange(N_CLASSES, dtype=I32)[None, :]).astype(I32)
    csum = jnp.cumsum(onehot, axis=0)
    rank = jnp.sum(onehot * csum, axis=1) - 1
    cnt = csum[-1]
    ntile = (cnt + tm - 1) // tm
    tile_end = jnp.cumsum(ntile)
    tile_start = tile_end - ntile
    n_used = tile_end[-1]
    n_tiles = -(-(n + N_CLASSES * (tm - 1)) // tm)
    pos = (jnp.sum(onehot * tile_start[None, :], axis=1) * tm + rank).astype(I32)
    pair_lo = np.array([a for a, _ in PAIR_SEQ], np.int32)
    pair_hi = np.array([b for _, b in PAIR_SEQ], np.int32)
    cls_ids = np.arange(N_CLASSES)
    cls_a = jnp.asarray((cls_ids // N_PAIRS) * EXP_PER_GROUP + pair_lo[cls_ids % N_PAIRS], I32)
    cls_b = jnp.asarray((cls_ids // N_PAIRS) * EXP_PER_GROUP + pair_hi[cls_ids % N_PAIRS], I32)
    tile_ids = jnp.arange(n_tiles, dtype=I32)
    tile_cls = jnp.sum((tile_end[None, :] <= jnp.minimum(tile_ids, n_used - 1)[:, None]).astype(I32), axis=1)
    tile_cls = jnp.minimum(tile_cls, N_CLASSES - 1)
    onehot_t = (tile_cls[:, None] == jnp.arange(N_CLASSES, dtype=I32)[None, :]).astype(I32)
    last_tile = jnp.where(ntile > 0, tile_end - 1, -1).astype(I32)
    ta = jnp.sum(onehot_t * cls_a[None, :], axis=1)
    tb = jnp.sum(onehot_t * cls_b[None, :], axis=1)
    first = jnp.ones((1,), I32)
    chg = (jnp.concatenate([first, (ta[1:] != ta[:-1]).astype(I32)])
           + 2 * jnp.concatenate([first, (tb[1:] != tb[:-1]).astype(I32)]))
    return dict(pos=pos, ta=ta, tb=tb, nu=n_used.reshape(1).astype(I32), chg=chg, last_tile=last_tile,
                n_tiles=n_tiles)


def _dispatch_kernel(lt_ref, nu_ref, pos_ref, src_ref, dst_hbm, zbuf, zsem, rsem, *, moe_tile, n_tiles):
    i = pl.program_id(0)
    td = pos_ref.shape[2]

    def zero_copy(tile):
        start = pl.multiple_of(tile * moe_tile, moe_tile)
        return pltpu.make_async_copy(zbuf, dst_hbm.at[pl.ds(start, moe_tile)], zsem)

    @pl.when(i == 0)
    def _():
        zbuf[...] = jnp.zeros(zbuf.shape, F32)
        for c in range(N_CLASSES):
            @pl.when(lt_ref[c] >= 0)
            def _(c=c):
                zero_copy(lt_ref[c]).start()
        for c in range(N_CLASSES):
            @pl.when(lt_ref[c] >= 0)
            def _(c=c):
                zero_copy(lt_ref[c]).wait()

        def spare_start(t, carry):
            zero_copy(t).start()
            return carry

        def spare_wait(t, carry):
            zero_copy(t).wait()
            return carry

        lax.fori_loop(nu_ref[0], n_tiles, spare_start, 0)
        lax.fori_loop(nu_ref[0], n_tiles, spare_wait, 0)

    for r in range(td):
        pltpu.make_async_copy(src_ref.at[pl.ds(r, 1)], dst_hbm.at[pl.ds(pos_ref[0, 0, r], 1)], rsem).start()
    pltpu.make_async_copy(src_ref, dst_hbm.at[pl.ds(0, td)], rsem).wait()


def _largest_divisor_tile(n, cap):
    for t in range(cap - cap % 8, 7, -8):
        if n % t == 0:
            return t
    raise ValueError(f"no row tile for {n} rows")


def _dispatch(x1e, plan, tm):
    n, width = x1e.shape
    n_tiles = plan["n_tiles"]
    td = _largest_divisor_tile(n, 1024)
    steps = n // td
    grid_spec = pltpu.PrefetchScalarGridSpec(
        num_scalar_prefetch=2,
        grid=(steps,),
        in_specs=[pl.BlockSpec((1, 1, td), lambda i, lt, nu: (i, 0, 0), memory_space=pltpu.SMEM),
                  pl.BlockSpec((td, width), lambda i, lt, nu: (i, 0))],
        out_specs=pl.BlockSpec(memory_space=pl.ANY),
        scratch_shapes=[pltpu.VMEM((tm, width), F32), pltpu.SemaphoreType.DMA(()),
                        pltpu.SemaphoreType.DMA(())],
    )
    return pl.pallas_call(
        functools.partial(_dispatch_kernel, moe_tile=tm, n_tiles=n_tiles),
        grid_spec=grid_spec,
        out_shape=jax.ShapeDtypeStruct((n_tiles * tm, width), F32),
        compiler_params=_cparams("arbitrary"),
        name="dispatch",
    )(plan["last_tile"], plan["nu"], plan["pos"].reshape(steps, 1, td), x1e)


def _moe_kernel(ta_ref, tb_ref, nu_ref, chg_ref, x_ref, ga_f32, ua_f32, da_f32, gb_f32, ub_f32, db_f32,
                lg_ref, lb_ref, o_ref, ga_ref, ua_ref, da_ref, gb_ref, ub_ref, db_ref):
    del ta_ref, tb_ref
    g = pl.program_id(0)

    @pl.when((chg_ref[g] & 1) == 1)
    def _():
        for src, dst in ((ga_f32, ga_ref), (ua_f32, ua_ref), (da_f32, da_ref)):
            dst[0] = src[0, 0].astype(BF16)

    @pl.when((chg_ref[g] & 2) == 2)
    def _():
        for src, dst in ((gb_f32, gb_ref), (ub_f32, ub_ref), (db_f32, db_ref)):
            dst[0] = src[0, 0].astype(BF16)

    @pl.when(g < nu_ref[0])
    def _():
        x = x_ref[:, 0:D_MODEL]
        slab = x_ref[:, D_MODEL:D_MODEL + LANES]
        w1, w2, e1, e2 = slab[:, 0:1], slab[:, 1:2], slab[:, 2:3], slab[:, 3:4]
        first = e1 < e2
        wa = jnp.where(first, w1, w2)
        wb = jnp.where(first, w2, w1)
        xb = x.astype(BF16)

        def expert(gw, uw, dw):
            hid = jax.nn.silu(_dot(xb, gw[0])) * _dot(xb, uw[0])
            return _dot(hid.astype(BF16), dw[0])

        y = wa * expert(ga_ref, ua_ref, da_ref)
        y = y + wb * expert(gb_ref, ub_ref, db_ref)
        o_ref[...] = _layer_norm_rows(DEEPNORM_ALPHA * x + y, lg_ref[...], lb_ref[...])

    @pl.when(g >= nu_ref[0])
    def _():
        o_ref[...] = jnp.zeros(o_ref.shape, F32)


def _moe(xs_sorted, plan, layer, w_gate, w_up, w_down, ln_g, ln_b, tm):
    n_tiles = plan["n_tiles"]
    width = xs_sorted.shape[1]
    up_a = pl.BlockSpec((1, 1, D_MODEL, D_EXPERT), lambda g, ta, tb, nu, chg: (layer, ta[g], 0, 0))
    dn_a = pl.BlockSpec((1, 1, D_EXPERT, D_MODEL), lambda g, ta, tb, nu, chg: (layer, ta[g], 0, 0))
    up_b = pl.BlockSpec((1, 1, D_MODEL, D_EXPERT), lambda g, ta, tb, nu, chg: (layer, tb[g], 0, 0))
    dn_b = pl.BlockSpec((1, 1, D_EXPERT, D_MODEL), lambda g, ta, tb, nu, chg: (layer, tb[g], 0, 0))
    vec = pl.BlockSpec((1, D_MODEL), lambda g, ta, tb, nu, chg: (0, 0))
    up_s = pltpu.VMEM((1, D_MODEL, D_EXPERT), BF16)
    dn_s = pltpu.VMEM((1, D_EXPERT, D_MODEL), BF16)
    grid_spec = pltpu.PrefetchScalarGridSpec(
        num_scalar_prefetch=4,
        grid=(n_tiles,),
        in_specs=[pl.BlockSpec((tm, width), lambda g, ta, tb, nu, chg: (g, 0)),
                  up_a, up_a, dn_a, up_b, up_b, dn_b, vec, vec],
        out_specs=pl.BlockSpec((tm, D_MODEL), lambda g, ta, tb, nu, chg: (g, 0)),
        scratch_shapes=[up_s, up_s, dn_s, up_s, up_s, dn_s],
    )
    return pl.pallas_call(
        _moe_kernel,
        grid_spec=grid_spec,
        out_shape=jax.ShapeDtypeStruct((n_tiles * tm, D_MODEL), F32),
        compiler_params=_cparams("arbitrary"),
        name="moe",
    )(plan["ta"], plan["tb"], plan["nu"], plan["chg"], xs_sorted, w_gate, w_up, w_down, w_gate, w_up, w_down,
      ln_g, ln_b)


def _ple_kernel(pos_ref, posn_ref, x2_hbm, pp_ref, ps_ref, wg_ref, wp_ref, op_ref, os_ref, buf, sems, *, n_s):
    i = pl.program_id(0)
    last = pl.num_programs(0) - 1
    tm = buf.shape[1]
    slot = lax.rem(i, 2)

    def issue(idx_ref, s):
        for r in range(tm):
            pltpu.make_async_copy(x2_hbm.at[pl.ds(idx_ref[0, 0, r], 1)], buf.at[s, pl.ds(r, 1)], sems.at[s]).start()

    @pl.when(i == 0)
    def _():
        issue(pos_ref, slot)

    @pl.when(i < last)
    def _():
        issue(posn_ref, 1 - slot)

    pltpu.make_async_copy(x2_hbm.at[pl.ds(0, tm)], buf.at[slot], sems.at[slot]).wait()

    def rows(x, p):
        gate = jax.nn.sigmoid(_dot(x.astype(BF16), wg_ref[...]))
        return x + gate * _dot(p.astype(BF16), wp_ref[...])

    @pl.when(i < last)
    def _():
        op_ref[...] = rows(buf[slot], pp_ref[0])

    @pl.when(i == last)
    def _():
        os_ref[...] = rows(buf[slot, 0:n_s, :], ps_ref[0])


def _ple(x2_sorted, pos, layer, p_p, p_s, wg_bf, wp_bf):
    n_p, n_s = p_p.shape[1], p_s.shape[1]
    tm = ROW_TILE
    nb = n_p // tm
    steps = nb + 1
    pos_pad = jnp.zeros((steps * tm,), I32).at[:n_p + n_s].set(pos).reshape(steps, 1, tm)
    full = lambda a: pl.BlockSpec(a.shape, lambda i: (0, 0))
    prow = lambda n: pl.BlockSpec((tm, n), lambda i: (jnp.minimum(i, nb - 1), 0))
    return pl.pallas_call(
        functools.partial(_ple_kernel, n_s=n_s),
        grid=(steps,),
        in_specs=[pl.BlockSpec((1, 1, tm), lambda i: (i, 0, 0), memory_space=pltpu.SMEM),
                  pl.BlockSpec((1, 1, tm), lambda i: (jnp.minimum(i + 1, nb), 0, 0), memory_space=pltpu.SMEM),
                  pl.BlockSpec(memory_space=pl.ANY),
                  pl.BlockSpec((1, tm, D_PLE), lambda i: (layer, jnp.minimum(i, nb - 1), 0)),
                  pl.BlockSpec((1, n_s, D_PLE), lambda i: (layer, 0, 0)), full(wg_bf), full(wp_bf)],
        out_specs=[prow(D_MODEL), pl.BlockSpec((n_s, D_MODEL), lambda i: (0, 0))],
        out_shape=[jax.ShapeDtypeStruct((n_p, D_MODEL), F32), jax.ShapeDtypeStruct((n_s, D_MODEL), F32)],
        scratch_shapes=[pltpu.VMEM((2, tm, D_MODEL), F32), pltpu.SemaphoreType.DMA((2,))],
        compiler_params=_cparams("arbitrary"),
        name="ple",
    )(pos_pad, pos_pad, x2_sorted, p_p, p_s, wg_bf, wp_bf)


def _router_weights(w_group, w_router):
    wr = jnp.zeros((ROUTE_ROWS, D_MODEL), F32)
    wr = wr.at[ROUTE_GROUP_ROW:ROUTE_GROUP_ROW + N_GROUPS, :].set(jnp.transpose(w_group))
    wr = wr.at[ROUTE_EXPERT_ROW:ROUTE_EXPERT_ROW + N_EXPERTS, :].set(jnp.transpose(w_router))
    hi = wr.astype(BF16)
    lo = (wr - hi.astype(F32)).astype(BF16)
    return jnp.concatenate([hi, lo], axis=0)


def _layer_tail(i, acts_p, acts_s, w_list, xp, xs, p_p, p_s, ln_mix_g, ln_mix_b, ln_ffn_g, ln_ffn_b,
                w_group, w_router, w_exp_gate, w_exp_up, w_exp_down, w_ple_proj, w_ple_gate):
    rw = _router_weights(w_group[i], w_router[i])
    x1e, rt = _mix_out(acts_p, acts_s, w_list, xp, xs, ln_mix_g[i][None, :], ln_mix_b[i][None, :], rw)
    plan = _route_plan(rt[:, :x1e.shape[0]], MOE_TILE)
    xs_sorted = _dispatch(x1e, plan, MOE_TILE)
    x2_sorted = _moe(xs_sorted, plan, i, w_exp_gate, w_exp_up, w_exp_down, ln_ffn_g[i][None, :],
                     ln_ffn_b[i][None, :], MOE_TILE)
    return _ple(x2_sorted, plan["pos"], i, p_p, p_s, w_ple_gate[i].astype(BF16), w_ple_proj[i].astype(BF16))


def kernel(x_prompt, x_sample, cache_k, cache_v, page_table, state_conv, state_mlstm_C, state_mlstm_n,
           state_mlstm_m, p_prompt, p_sample, w_in_even, conv_w, lambda_q1, lambda_k1, lambda_q2, lambda_k2,
           subln_w, w_out_even, w_in_odd, b_gates_odd, mh_norm_w, w_out_odd, ln_mix_g, ln_mix_b, ln_ffn_g,
           ln_ffn_b, w_group, w_router, w_exp_gate, w_exp_up, w_exp_down, w_ple_proj, w_ple_gate):
    bp, tp, _ = x_prompt.shape
    bs, ts, _ = x_sample.shape
    assert ts == 1 and tp % ROW_TILE == 0 and tp % ATTN_TILE == 0 and tp % MLSTM_CHUNK == 0
    n_p = bp * tp
    past_len = page_table.shape[1] * cache_k.shape[2]
    xp = x_prompt.reshape(n_p, D_MODEL)
    xs = x_sample.reshape(bs, D_MODEL)
    tail_w = (ln_mix_g, ln_mix_b, ln_ffn_g, ln_ffn_b, w_group, w_router, w_exp_gate, w_exp_up, w_exp_down,
              w_ple_proj, w_ple_gate)
    outs_p, outs_s = {}, {}
    for i in range(DEPTH):
        j = i // 2
        p_p = p_prompt.reshape(DEPTH, n_p, D_PLE)
        p_s = p_sample.reshape(DEPTH, bs, D_PLE)
        if i % 2 == 0:
            lam_init = 0.8 - 0.6 * math.exp(-0.3 * i)
            lam_vecs = jnp.stack([lambda_q1[j], lambda_k1[j], lambda_q2[j], lambda_k2[j]])
            sub = subln_w[j][None, :]
            w_bf = w_in_even[j].astype(BF16)
            tabs_p = _rope_tables(jnp.arange(tp))
            q0, v0 = 3 * D_CONV, 3 * D_CONV + 2 * QK_B
            wqt_bf = jnp.transpose(w_in_even[j][:, q0:q0 + QK_B]).astype(BF16)
            wvt_bf = jnp.transpose(w_in_even[j][:, v0:v0 + V_B]).astype(BF16)
            yc, qt, kf, vf, kb, vt, cst = _even_in_prompt(
                x_prompt if i == 0 else xp.reshape(bp, tp, D_MODEL), w_bf, wqt_bf, wvt_bf, conv_w[j],
                jnp.zeros((bp, CONV_W - 1, D_CONV), F32), tabs_p)
            o_p = _attn_prompt(qt, kb, vt, lam_vecs, sub, lam_init)
            outs_p.setdefault("k", []).append(kf.reshape(bp, tp, 2 * H_B, DH_B))
            outs_p.setdefault("v", []).append(vf.reshape(bp, tp, H_B, 2 * DH_B))
            outs_p.setdefault("c", []).append(cst)
            tabs_s = _rope_tables(jnp.full((1,), past_len, I32))
            prev_t = jnp.swapaxes(state_conv[j], 0, 1)
            yc_s, q_s, kf_s, vf_s, u_s = _even_in_decode(xs, w_bf, conv_w[j], prev_t, tabs_s)
            sub_head = jnp.arange(2 * H_B)
            sub_head = jnp.where(sub_head < H_B, 2 * sub_head, 2 * (sub_head - H_B) + 1)
            lane_head = jnp.arange(QK_B) // DH_B
            qbd = jnp.where(lane_head[None, None, :] == sub_head[None, :, None], q_s[:, None, :],
                            jnp.zeros((), BF16))
            n_pool = cache_k.shape[1]
            pages = cache_k.shape[0] * n_pool
            k_view = jnp.transpose(cache_k, (0, 1, 3, 4, 2)).reshape(pages, QK_B, PAGE_SIZE)
            v_view = cache_v.reshape(pages, PAGE_SIZE * H_B, 2 * DH_B)
            o8 = _attn_decode(qbd, kf_s[:, None, :], vf_s[:, None, :], k_view, v_view,
                              page_table + j * n_pool, lam_vecs, sub, lam_init)
            o_s = o8[:, :H_B, :].reshape(bs, V_B).astype(BF16)
            outs_s.setdefault("k", []).append(kf_s.reshape(bs, ts, 2 * H_B, DH_B))
            outs_s.setdefault("v", []).append(vf_s.reshape(bs, ts, H_B, 2 * DH_B))
            outs_s.setdefault("c", []).append(jnp.stack([state_conv[j][:, 1, :], u_s], axis=1))
            w_out = w_out_even[j].astype(BF16)
            w_list = [w_out[:D_CONV], w_out[D_CONV:]]
            acts_p = [yc.reshape(n_p, D_CONV), o_p.reshape(n_p, V_B)]
            acts_s = [yc_s, o_s]
        else:
            w_in = w_in_odd[j]
            qw, vw = H_C * DK_C, H_C * DV_C
            w_bf = w_in[:, :2 * qw + 2 * vw].astype(BF16)
            wg = jnp.zeros((D_MODEL, LANES), F32).at[:, :2 * H_C].set(w_in[:, 2 * qw + 2 * vw:]).astype(BF16)
            wgt = jnp.transpose(wg[:, :2 * H_C])
            bg = jnp.zeros((1, LANES), F32).at[0, :2 * H_C].set(b_gates_odd[j])
            bgt = jnp.broadcast_to(b_gates_odd[j][:, None], (2 * H_C, LANES))
            nw = mh_norm_w[j][None, :]
            q, k, v, o, gc, gr = _odd_in(xp.reshape(bp, tp, D_MODEL), w_bf, wg, wgt, bg, bgt, decode=False)
            h_p, c_p, n_pp, m_p = _mlstm(q, k, v, o, gc, gr, nw,
                                         jnp.zeros((bp, H_C, DK_C, DV_C), F32), jnp.zeros((bp, H_C, DK_C), F32),
                                         jnp.zeros((bp, H_C, LANES), F32))
            outs_p.setdefault("C", []).append(c_p)
            outs_p.setdefault("n", []).append(n_pp)
            outs_p.setdefault("m", []).append(m_p[:, :, 0])
            q_s, k_s, v_s, o_s2, gc_s, gr_s = _odd_in(xs, w_bf, wg, wgt, bg, bgt, decode=True)
            ch = MLSTM_CHUNK
            pad_rows = lambda a: jnp.zeros((bs, ch, a.shape[1]), a.dtype).at[:, 0, :].set(a)
            lane = jnp.arange(LANES)
            inert_c = jnp.where(lane < H_C, -jnp.inf, 0.0).astype(F32)
            gc_pad = jnp.broadcast_to(inert_c[None, None, :], (bs, ch, LANES)).at[:, 0, :].set(gc_s)
            inert_r = jnp.where(jnp.arange(2 * H_C) < H_C, -jnp.inf, 0.0).astype(F32)
            gr_pad = jnp.broadcast_to(inert_r[None, :, None], (bs, 2 * H_C, ch)).at[:, :, 0].set(gr_s.T)
            m0 = jnp.broadcast_to(state_mlstm_m[j][:, :, None], (bs, H_C, LANES))
            h_s, c_s, n_s, m_s = _mlstm(pad_rows(q_s), pad_rows(k_s), pad_rows(v_s), pad_rows(o_s2), gc_pad,
                                        gr_pad, nw, state_mlstm_C[j], state_mlstm_n[j], m0)
            outs_s.setdefault("C", []).append(c_s)
            outs_s.setdefault("n", []).append(n_s)
            outs_s.setdefault("m", []).append(m_s[:, :, 0])
            w_list = [w_out_odd[j].astype(BF16)]
            acts_p = [h_p.reshape(n_p, vw)]
            acts_s = [h_s[:, 0, :]]
        xp, xs = _layer_tail(i, acts_p, acts_s, w_list, xp, xs, p_p, p_s, *tail_w)
    st = lambda lst: jnp.stack(lst)
    return (xp.reshape(bp, tp, D_MODEL), xs.reshape(bs, ts, D_MODEL),
            st(outs_p["k"]), st(outs_p["v"]), st(outs_p["c"]), st(outs_p["C"]), st(outs_p["n"]), st(outs_p["m"]),
            st(outs_s["k"]), st(outs_s["v"]), st(outs_s["c"]), st(outs_s["C"]), st(outs_s["n"]), st(outs_s["m"]))
```

```python
import functools
import math

import numpy as np
import jax
import jax.numpy as jnp
from jax import lax
from jax.experimental import pallas as pl
from jax.experimental.pallas import tpu as pltpu

F32 = jnp.float32
BF16 = jnp.bfloat16
I32 = jnp.int32

D_MODEL = 1024
DEPTH = 2
PAGE_SIZE = 128
D_CONV = D_MODEL // 2
CONV_W = 3
H_B = 4
DH_B = 64
ROT_DIM = DH_B // 4
ROPE_THETA = 500000.0
H_C = 4
DK_C = (D_MODEL // 2) // H_C
DV_C = D_MODEL // H_C
N_GROUPS = 4
EXP_PER_GROUP = 4
N_EXPERTS = N_GROUPS * EXP_PER_GROUP
D_EXPERT = 512
D_PLE = 256
LN_EPS = 1e-5
LOG2_E = 1.4426950408889634
DEEPNORM_ALPHA = (2 * DEPTH) ** 0.25
QK_B = 2 * H_B * DH_B
V_B = H_B * 2 * DH_B
N_PAIRS = EXP_PER_GROUP * (EXP_PER_GROUP - 1) // 2
N_CLASSES = N_GROUPS * N_PAIRS
PAIR_SEQ = ((0, 1), (0, 2), (1, 2), (1, 3), (0, 3), (2, 3))
assert EXP_PER_GROUP == 4 and len(PAIR_SEQ) == N_PAIRS

LANES = 128
VMEM_LIMIT = 56 * 1024 * 1024
ROW_TILE = 512
ATTN_TILE = 512
ATTN_HEADS_PER_STEP = 4
ONES_ROWS = 16
MLSTM_CHUNK = 128
MLSTM_BATCH_PER_STEP = 1
MOE_TILE = 256
PAGES_PER_STEP = 16
ROUTE_ROWS = 32
ROUTE_GROUP_ROW = 0
ROUTE_EXPERT_ROW = 8


def _cparams(*sem):
    return pltpu.CompilerParams(dimension_semantics=sem, vmem_limit_bytes=VMEM_LIMIT)


def _dot(a, b):
    return jnp.dot(a, b, preferred_element_type=F32)


def _dot_nt(a, b):
    return lax.dot_general(a, b, (((1,), (1,)), ((), ())), preferred_element_type=F32)


def _dot_tn(a, b):
    return lax.dot_general(a, b, (((0,), (0,)), ((), ())), preferred_element_type=F32)


def _layer_norm_rows(z, g, b):
    mu = jnp.mean(z, axis=-1, keepdims=True)
    zc = z - mu
    var = jnp.mean(zc * zc, axis=-1, keepdims=True)
    return zc * lax.rsqrt(var + LN_EPS) * g + b


def _log_sigmoid(x):
    return jnp.minimum(x, 0.0) - jnp.log1p(jnp.exp(-jnp.abs(x)))


def _gated_conv(gate_b, u, um1, um2, cw_ref):
    cw = cw_ref[...]
    conv = um2 * cw[0:1, :] + um1 * cw[1:2, :] + u * cw[2:3, :]
    return (gate_b * conv).astype(BF16)


def _rope(z, cos, sin_lo, sin_hi, axis):
    half = ROT_DIM // 2
    return z * cos + pltpu.roll(z, QK_B - half, axis) * sin_lo + pltpu.roll(z, half, axis) * sin_hi


def _even_in_prompt_kernel(x_ref, w_ref, wqt_ref, wvt_ref, cw_ref, prev_ref, rc_ref, rs1_ref, rs2_ref,
                           rct_ref, rs1t_ref, rs2t_ref,
                           yc_ref, qt_ref, kf_ref, vf_ref, kb_ref, vt_ref, u_ref, carry_ref):
    xb = x_ref[0].astype(BF16)
    tm = xb.shape[0]

    def proj(c0, n):
        return _dot(xb, w_ref[:, c0:c0 + n])

    gate_b = proj(0, D_CONV)
    u = proj(D_CONV, D_CONV) * proj(2 * D_CONV, D_CONV)
    j = pl.program_id(1)

    @pl.when(j == 0)
    def _():
        carry_ref[...] = prev_ref[0]

    row = lax.broadcasted_iota(I32, u.shape, 0)
    c2 = carry_ref[0:1, :]
    c1 = carry_ref[1:2, :]
    um1 = jnp.where(row == 0, c1, pltpu.roll(u, 1, 0))
    um2 = jnp.where(row == 0, c2, jnp.where(row == 1, c1, pltpu.roll(u, 2, 0)))
    carry_ref[...] = u[tm - 2:tm, :]
    u_ref[0] = u[tm - 2:tm, :]
    yc_ref[0] = _gated_conv(gate_b, u, um1, um2, cw_ref)

    reps = QK_B // LANES
    tile = lambda r, ax: jnp.concatenate([r[...]] * reps, axis=ax)
    k = _rope(proj(3 * D_CONV + QK_B, QK_B), tile(rc_ref, 1), tile(rs1_ref, 1), tile(rs2_ref, 1), 1)
    kf_ref[0] = k
    kb_ref[0] = k.astype(BF16)
    v = proj(3 * D_CONV + 2 * QK_B, V_B)
    for h in range(H_B):
        vf_ref[0, pl.ds(h, tm, stride=H_B), :] = v[:, h * 2 * DH_B:(h + 1) * 2 * DH_B]
    qt = _rope(_dot_nt(wqt_ref[...], xb), tile(rct_ref, 0), tile(rs1t_ref, 0), tile(rs2t_ref, 0), 0)
    qt_ref[0] = (qt * (DH_B ** -0.5 * LOG2_E)).astype(BF16)
    vt_ref[0] = _dot_nt(wvt_ref[...], xb).astype(BF16)


def _even_in_decode_kernel(x_ref, w_ref, cw_ref, prev_ref, rc_ref, rs1_ref, rs2_ref,
                           yc_ref, q_ref, kf_ref, vf_ref, u_ref):
    xb = x_ref[...].astype(BF16)

    def proj(c0, n):
        return _dot(xb, w_ref[:, c0:c0 + n])

    gate_b = proj(0, D_CONV)
    u = proj(D_CONV, D_CONV) * proj(2 * D_CONV, D_CONV)
    u_ref[...] = u
    yc_ref[...] = _gated_conv(gate_b, u, prev_ref[1], prev_ref[0], cw_ref)
    reps = QK_B // LANES
    tile = lambda r: jnp.concatenate([r[...]] * reps, axis=1)
    cos, sin_lo, sin_hi = tile(rc_ref), tile(rs1_ref), tile(rs2_ref)
    q_ref[...] = (_rope(proj(3 * D_CONV, QK_B), cos, sin_lo, sin_hi, 1) * (DH_B ** -0.5)).astype(BF16)
    kf_ref[...] = _rope(proj(3 * D_CONV + QK_B, QK_B), cos, sin_lo, sin_hi, 1)
    vf_ref[...] = proj(3 * D_CONV + 2 * QK_B, V_B)


def _rope_tables(pos):
    half = ROT_DIM // 2
    inv = ROPE_THETA ** (-jnp.arange(half, dtype=F32) / half)
    ang = pos.astype(F32)[:, None] * inv[None, :]
    cos, sin = jnp.cos(ang), jnp.sin(ang)
    t = pos.shape[0]
    ones = jnp.ones((t, DH_B - ROT_DIM), F32)
    zeros = jnp.zeros((t, DH_B - ROT_DIM), F32)
    zh = jnp.zeros((t, half), F32)
    c = jnp.concatenate([cos, cos, ones], axis=1)
    s_lo = jnp.concatenate([-sin, zh, zeros], axis=1)
    s_hi = jnp.concatenate([zh, sin, zeros], axis=1)
    tile2 = lambda a: jnp.concatenate([a, a], axis=1)
    return tile2(c), tile2(s_lo), tile2(s_hi)


def _even_in_prompt(x, w_bf, wqt_bf, wvt_bf, conv_w, conv_prev, tables):
    b, t, _ = x.shape
    tm = ROW_TILE
    row3 = lambda n: pl.BlockSpec((1, tm, n), lambda i, j: (i, j, 0))
    col3 = lambda n: pl.BlockSpec((1, n, tm), lambda i, j: (i, 0, j))
    full2 = lambda a: pl.BlockSpec(a.shape, lambda i, j: (0, 0))
    tab = pl.BlockSpec((tm, LANES), lambda i, j: (j, 0))
    tab_t = pl.BlockSpec((LANES, tm), lambda i, j: (0, j))
    st = pl.BlockSpec((1, CONV_W - 1, D_CONV), lambda i, j: (i, 0, 0))
    tables_t = [jnp.transpose(a) for a in tables]
    outs = [jax.ShapeDtypeStruct((b, t, D_CONV), BF16), jax.ShapeDtypeStruct((b, QK_B, t), BF16),
            jax.ShapeDtypeStruct((b, t, QK_B), F32), jax.ShapeDtypeStruct((b, t * H_B, 2 * DH_B), F32),
            jax.ShapeDtypeStruct((b, t, QK_B), BF16), jax.ShapeDtypeStruct((b, V_B, t), BF16),
            jax.ShapeDtypeStruct((b, CONV_W - 1, D_CONV), F32)]
    return pl.pallas_call(
        _even_in_prompt_kernel,
        grid=(b, t // tm),
        in_specs=[row3(D_MODEL), full2(w_bf), full2(wqt_bf), full2(wvt_bf), full2(conv_w), st,
                  tab, tab, tab, tab_t, tab_t, tab_t],
        out_specs=[row3(D_CONV), col3(QK_B), row3(QK_B),
                   pl.BlockSpec((1, tm * H_B, 2 * DH_B), lambda i, j: (i, j, 0)), row3(QK_B), col3(V_B), st],
        out_shape=outs,
        scratch_shapes=[pltpu.VMEM((CONV_W - 1, D_CONV), F32)],
        compiler_params=_cparams("arbitrary", "arbitrary"),
        name="even_in_prompt",
    )(x, w_bf, wqt_bf, wvt_bf, conv_w, conv_prev, *tables, *tables_t)


def _even_in_decode(x, w_bf, conv_w, conv_prev_t, tables):
    n = x.shape[0]
    full = lambda a: pl.BlockSpec(a.shape, lambda i: (0,) * a.ndim)
    o2 = lambda c, dt: jax.ShapeDtypeStruct((n, c), dt)
    outs = [o2(D_CONV, BF16), o2(QK_B, BF16), o2(QK_B, F32), o2(V_B, F32), o2(D_CONV, F32)]
    ins = [x, w_bf, conv_w, conv_prev_t, *tables]
    return pl.pallas_call(
        _even_in_decode_kernel,
        grid=(1,),
        in_specs=[full(a) for a in ins],
        out_specs=[pl.BlockSpec(o.shape, lambda i: (0, 0)) for o in outs],
        out_shape=outs,
        compiler_params=_cparams("arbitrary"),
        name="even_in_decode",
    )(*ins)


def _lambda_value(lam_ref, lam_init):
    lv = lam_ref[...]
    a = jnp.sum(lv[0:1, :] * lv[1:2, :], axis=1, keepdims=True)
    b = jnp.sum(lv[2:3, :] * lv[3:4, :], axis=1, keepdims=True)
    return jnp.exp(a) - jnp.exp(b) + lam_init


def _sub_norm(o, sub_ref, lam_init):
    ms = jnp.mean(o * o, axis=-1, keepdims=True)
    return o * lax.rsqrt(ms + LN_EPS) * sub_ref[...] * (1.0 - lam_init)


def _attn_prompt_kernel(qt_ref, k_ref, vt_ref, lam_ref, sub_ref, o_ref, *scratch, lam_init):
    i = pl.program_id(2)
    tq = qt_ref.shape[2]
    tk = tq
    n_heads = qt_ref.shape[1] // LANES
    m_scrs, acc_scrs = scratch[:n_heads], scratch[n_heads:]
    for m_scr, acc_scr in zip(m_scrs, acc_scrs):
        m_scr[...] = jnp.full(m_scr.shape, -jnp.inf, F32)
        acc_scr[...] = jnp.zeros(acc_scr.shape, F32)
    ones = jnp.ones((ONES_ROWS, tk), BF16)
    qqs = []
    for h in range(n_heads):
        qt = qt_ref[0, h * LANES:(h + 1) * LANES, :].astype(F32)
        feat = lax.broadcasted_iota(I32, qt.shape, 0)
        qqs.append(jnp.concatenate([jnp.where(feat < DH_B, qt, 0.0), jnp.where(feat >= DH_B, qt, 0.0)],
                                   axis=1).astype(BF16))

    def step(j, masked):
        start = pl.multiple_of(j * tk, tk)
        scores = [_dot(k_ref[0, pl.ds(start, tk), h * LANES:(h + 1) * LANES], qqs[h])
                  for h in range(n_heads)]
        probs, alphas = [], []
        for h in range(n_heads):
            s = scores[h]
            if masked:
                key = lax.broadcasted_iota(I32, s.shape, 0)
                qry = lax.broadcasted_iota(I32, s.shape, 1)
                qry = jnp.where(qry >= tq, qry - tq, qry)
                s = jnp.where(key <= qry, s, -jnp.inf)
            m_prev = m_scrs[h][...]
            m_new = jnp.maximum(m_prev, jnp.max(s, axis=0, keepdims=True))
            alphas.append(jnp.exp2(m_prev - m_new))
            probs.append(jnp.exp2(s - m_new).astype(BF16))
            m_scrs[h][...] = m_new
        for h in range(n_heads):
            vtj = jnp.concatenate([vt_ref[0, h * LANES:(h + 1) * LANES, pl.ds(start, tk)], ones], axis=0)
            acc_scrs[h][...] = alphas[h] * acc_scrs[h][...] + _dot(vtj, probs[h])

    def body(j, carry):
        step(j, False)
        return carry

    lax.fori_loop(0, i, body, 0)
    step(i, True)
    lam = _lambda_value(lam_ref, lam_init)
    for h in range(n_heads):
        on = acc_scrs[h][0:LANES, :] / acc_scrs[h][LANES:LANES + 1, :]
        o = jnp.transpose(on[:, 0:tq] - lam * on[:, tq:2 * tq])
        o_ref[0, :, h * LANES:(h + 1) * LANES] = _sub_norm(o, sub_ref, lam_init).astype(BF16)


def _attn_prompt(qt, k, vt, lam_vecs, subln, lam_init):
    b, t, _ = k.shape
    tq = ATTN_TILE
    nh = ATTN_HEADS_PER_STEP
    hw = nh * LANES
    full = lambda a: pl.BlockSpec(a.shape, lambda bi, h, i: (0, 0))
    return pl.pallas_call(
        functools.partial(_attn_prompt_kernel, lam_init=lam_init),
        grid=(b, H_B // nh, t // tq),
        in_specs=[pl.BlockSpec((1, hw, tq), lambda bi, h, i: (bi, h, i)),
                  pl.BlockSpec((1, t, hw), lambda bi, h, i: (bi, 0, h)),
                  pl.BlockSpec((1, hw, t), lambda bi, h, i: (bi, h, 0)),
                  full(lam_vecs), full(subln)],
        out_specs=pl.BlockSpec((1, tq, hw), lambda bi, h, i: (bi, i, h)),
        out_shape=jax.ShapeDtypeStruct((b, t, V_B), BF16),
        scratch_shapes=[pltpu.VMEM((1, 2 * tq), F32)] * nh + [pltpu.VMEM((LANES + ONES_ROWS, 2 * tq), F32)] * nh,
        compiler_params=_cparams("arbitrary", "arbitrary", "arbitrary"),
        name="attn_prompt",
    )(qt, k, vt, lam_vecs, subln)


def _attn_decode_kernel(pt_ref, qbd_ref, kn_ref, vn_ref, lam_ref, sub_ref, *rest, lam_init, n_pages):
    del pt_ref
    k_refs = rest[:n_pages]
    v_refs = rest[n_pages:2 * n_pages]
    o_ref, m_scr, l_scr, acc_scr = rest[2 * n_pages:]
    j = pl.program_id(1)
    qbd = qbd_ref[0]

    @pl.when(j == 0)
    def _():
        s_new = jnp.sum(qbd.astype(F32) * kn_ref[0], axis=1, keepdims=True)
        m_scr[...] = s_new
        l_scr[...] = jnp.ones(l_scr.shape, F32)
        acc_scr[...] = jnp.broadcast_to(vn_ref[0], acc_scr.shape)

    s = jnp.concatenate([_dot(qbd, k_refs[r][0].astype(BF16)) for r in range(n_pages)], axis=1)
    m_prev = m_scr[...]
    m_new = jnp.maximum(m_prev, jnp.max(s, axis=1, keepdims=True))
    alpha = jnp.exp(m_prev - m_new)
    p = jnp.exp(s - m_new)
    l_scr[...] = alpha * l_scr[...] + jnp.sum(p, axis=1, keepdims=True)

    def head_pv(h):
        acc = None
        for r in range(n_pages):
            vh = v_refs[r][0, pl.ds(h, PAGE_SIZE, stride=H_B), :].astype(BF16)
            term = _dot(p[:, r * PAGE_SIZE:(r + 1) * PAGE_SIZE].astype(BF16), vh)
            acc = term if acc is None else acc + term
        return acc

    pv = jnp.concatenate([head_pv(h) for h in range(H_B)], axis=1)
    acc_scr[...] = alpha * acc_scr[...] + pv
    m_scr[...] = m_new

    @pl.when(j == pl.num_programs(1) - 1)
    def _():
        on = acc_scr[...] / l_scr[...]
        row = lax.broadcasted_iota(I32, (2 * H_B, 2 * DH_B), 0)
        head = jnp.where(row >= H_B, row - H_B, row)
        o8 = jnp.zeros((2 * H_B, 2 * DH_B), F32)
        for c in range(H_B):
            o8 = o8 + jnp.where(head == c, on[:, c * 2 * DH_B:(c + 1) * 2 * DH_B], 0.0)
        lam = _lambda_value(lam_ref, lam_init)
        o = o8 - lam * pltpu.roll(o8, H_B, 0)
        o_ref[0] = _sub_norm(o, sub_ref, lam_init)


def _attn_decode(qbd, k_new, v_new, cache_k, cache_v, page_table, lam_vecs, subln, lam_init):
    n = qbd.shape[0]
    n_pages = page_table.shape[1]
    pp = PAGES_PER_STEP
    width = V_B
    c2 = lambda a: pl.BlockSpec(a.shape, lambda b, j, pt: (0, 0))
    per_b = lambda a: pl.BlockSpec((1,) + a.shape[1:], lambda b, j, pt: (b, 0, 0))

    def page(r, arr):
        return pl.BlockSpec((1,) + arr.shape[1:], lambda b, j, pt: (pt[b, j * pp + r], 0, 0))

    grid_spec = pltpu.PrefetchScalarGridSpec(
        num_scalar_prefetch=1,
        grid=(n, n_pages // pp),
        in_specs=[per_b(qbd), per_b(k_new), per_b(v_new), c2(lam_vecs), c2(subln)]
        + [page(r, cache_k) for r in range(pp)] + [page(r, cache_v) for r in range(pp)],
        out_specs=pl.BlockSpec((1, 2 * H_B, 2 * DH_B), lambda b, j, pt: (b, 0, 0)),
        scratch_shapes=[pltpu.VMEM((2 * H_B, 1), F32), pltpu.VMEM((2 * H_B, 1), F32),
                        pltpu.VMEM((2 * H_B, width), F32)],
    )
    return pl.pallas_call(
        functools.partial(_attn_decode_kernel, lam_init=lam_init, n_pages=pp),
        grid_spec=grid_spec,
        out_shape=jax.ShapeDtypeStruct((n, 2 * H_B, 2 * DH_B), F32),
        compiler_params=_cparams("arbitrary", "arbitrary"),
        name="attn_decode",
    )(page_table, qbd, k_new, v_new, lam_vecs, subln, *([cache_k] * pp), *([cache_v] * pp))


def _odd_in_kernel(x_ref, w_ref, wg_ref, wgt_ref, bg_ref, bgt_ref,
                   q_ref, k_ref, v_ref, o_ref, gc_ref, gr_ref, kt_ref, *, decode):
    xb = x_ref[0].astype(BF16) if not decode else x_ref[...].astype(BF16)
    qw = H_C * DK_C
    vw = H_C * DV_C
    q = _dot(xb, w_ref[:, 0:qw]) * (DK_C ** -0.5)
    k = _dot(xb, w_ref[:, qw:2 * qw])
    v = _dot(xb, w_ref[:, 2 * qw:2 * qw + vw])
    o = _dot(xb, w_ref[:, 2 * qw + vw:2 * qw + 2 * vw])
    g_col = _dot(xb, wg_ref[...]) + bg_ref[...]
    lane = lax.broadcasted_iota(I32, g_col.shape, 1)
    g_col = jnp.where(lane < H_C, g_col, _log_sigmoid(g_col))
    t_all = _dot_nt(wgt_ref[...], xb)
    k_t = t_all[0:qw, :].astype(BF16)
    g_row = t_all[qw:qw + 2 * H_C, :] + bgt_ref[:, 0:1]
    sub = lax.broadcasted_iota(I32, g_row.shape, 0)
    g_row = jnp.where(sub < H_C, g_row, _log_sigmoid(g_row))
    if decode:
        kt_ref[...] = k_t
        q_ref[...] = q.astype(BF16)
        k_ref[...] = k.astype(BF16)
        v_ref[...] = v.astype(BF16)
        o_ref[...] = o
        gc_ref[...] = g_col
        gr_ref[...] = g_row
    else:
        q_ref[0] = q.astype(BF16)
        k_ref[0] = k.astype(BF16)
        v_ref[0] = v.astype(BF16)
        o_ref[0] = o
        gc_ref[0] = g_col
        gr_ref[0] = g_row
        kt_ref[0] = k_t


def _odd_in(x, w_bf, wg, wgt, bg, bgt, decode):
    qw, vw = H_C * DK_C, H_C * DV_C
    if decode:
        n = x.shape[0]
        ins = [x, w_bf, wg, wgt, bg, bgt]
        outs = [jax.ShapeDtypeStruct((n, qw), BF16), jax.ShapeDtypeStruct((n, qw), BF16),
                jax.ShapeDtypeStruct((n, vw), BF16), jax.ShapeDtypeStruct((n, vw), F32),
                jax.ShapeDtypeStruct((n, LANES), F32), jax.ShapeDtypeStruct((2 * H_C, n), F32),
                jax.ShapeDtypeStruct((qw, n), BF16)]
        return pl.pallas_call(
            functools.partial(_odd_in_kernel, decode=True),
            grid=(1,),
            in_specs=[pl.BlockSpec(a.shape, lambda i: (0, 0)) for a in ins],
            out_specs=[pl.BlockSpec(o.shape, lambda i: (0, 0)) for o in outs],
            out_shape=outs,
            compiler_params=_cparams("arbitrary"),
            name="odd_in_decode",
        )(*ins)
    b, t, _ = x.shape
    tm = ROW_TILE
    row3 = lambda n: pl.BlockSpec((1, tm, n), lambda i, j: (i, j, 0))
    full2 = lambda a: pl.BlockSpec(a.shape, lambda i, j: (0, 0))
    outs = [jax.ShapeDtypeStruct((b, t, qw), BF16), jax.ShapeDtypeStruct((b, t, qw), BF16),
            jax.ShapeDtypeStruct((b, t, vw), BF16), jax.ShapeDtypeStruct((b, t, vw), F32),
            jax.ShapeDtypeStruct((b, t, LANES), F32), jax.ShapeDtypeStruct((b, 2 * H_C, t), F32),
            jax.ShapeDtypeStruct((b, qw, t), BF16)]
    return pl.pallas_call(
        functools.partial(_odd_in_kernel, decode=False),
        grid=(b, t // tm),
        in_specs=[row3(D_MODEL), full2(w_bf), full2(wg), full2(wgt), full2(bg), full2(bgt)],
        out_specs=[row3(qw), row3(qw), row3(vw), row3(vw), row3(LANES),
                   pl.BlockSpec((1, 2 * H_C, tm), lambda i, j: (i, 0, j)),
                   pl.BlockSpec((1, qw, tm), lambda i, j: (i, 0, j))],
        out_shape=outs,
        compiler_params=_cparams("arbitrary", "arbitrary"),
        name="odd_in_prompt",
    )(x, w_bf, wg, wgt, bg, bgt)


def _mlstm_kernel(q_ref, k_ref, kt_ref, v_ref, o_ref, gc_ref, gr_ref, nw_ref, c0_ref, n0_ref, m0_ref,
                  h_ref, c_out, n_out, m_out, c_scr, n_scr, m_scr):
    ci = pl.program_id(1)
    chunk = q_ref.shape[1]

    @pl.when(ci == 0)
    def _():
        c_scr[...] = c0_ref[...]
        n_scr[...] = n0_ref[...]
        m_scr[...] = m0_ref[...]

    t_idx = lax.broadcasted_iota(I32, (chunk, chunk), 0)
    s_idx = lax.broadcasted_iota(I32, (chunk, chunk), 1)
    causal = s_idx <= t_idx
    for bb, h in [(bb, h) for bb in range(q_ref.shape[0]) for h in range(H_C)]:
        q = q_ref[bb, :, h * DK_C:(h + 1) * DK_C]
        k = k_ref[bb, :, h * DK_C:(h + 1) * DK_C]
        v = v_ref[bb, :, h * DV_C:(h + 1) * DV_C]
        ig_r = gr_ref[bb, h:h + 1, :]
        lf_r = gr_ref[bb, H_C + h:H_C + h + 1, :]
        ig_c = gc_ref[bb, :, h:h + 1]
        lf_c = gc_ref[bb, :, H_C + h:H_C + h + 1]
        bcum_c = jnp.sum(jnp.where(causal, lf_r, 0.0), axis=1, keepdims=True)
        bcum_r = jnp.sum(jnp.where(t_idx <= s_idx, lf_c, 0.0), axis=0, keepdims=True)
        m0 = m_scr[bb, h:h + 1, 0:1]
        dmat = jnp.where(causal, bcum_c - bcum_r + ig_r, -jnp.inf)
        inter = bcum_c + m0
        m = jnp.maximum(inter, jnp.max(dmat, axis=1, keepdims=True))
        w = jnp.exp(dmat - m)
        g = jnp.exp(inter - m)
        s = _dot_nt(q, k) * w
        c0 = c_scr[bb, h]
        n0 = n_scr[bb, h:h + 1, :]
        num = g * _dot(q, c0.astype(BF16)) + _dot(s.astype(BF16), v)
        den = g * jnp.sum(q.astype(F32) * n0, axis=1, keepdims=True) + jnp.sum(s, axis=1, keepdims=True)
        hid = num / jnp.maximum(jnp.abs(den), jnp.exp(-m))
        m_last = m[chunk - 1:chunk, :]
        b_last = bcum_c[chunk - 1:chunk, :]
        w_last = jnp.exp(b_last - bcum_c + ig_c - m_last)
        g_last = jnp.exp(b_last + m0 - m_last)
        kw = k.astype(F32) * w_last
        w_last_r = jnp.exp(b_last - bcum_r + ig_r - m_last)
        kw_t = (kt_ref[bb, h * DK_C:(h + 1) * DK_C, :].astype(F32) * w_last_r).astype(BF16)
        c_scr[bb, h] = g_last * c0 + _dot(kw_t, v)
        n_scr[bb, h:h + 1, :] = g_last * n0 + jnp.sum(kw, axis=0, keepdims=True)
        m_scr[bb, h:h + 1, :] = jnp.broadcast_to(m_last, (1, LANES))
        mu = jnp.mean(hid, axis=1, keepdims=True)
        hc = hid - mu
        var = jnp.mean(hc * hc, axis=1, keepdims=True)
        hn = hc * lax.rsqrt(var + LN_EPS) * nw_ref[:, h * DV_C:(h + 1) * DV_C]
        gate = jax.nn.sigmoid(o_ref[bb, :, h * DV_C:(h + 1) * DV_C])
        h_ref[bb, :, h * DV_C:(h + 1) * DV_C] = (gate * hn).astype(BF16)

    c_out[...] = c_scr[...]
    n_out[...] = n_scr[...]
    m_out[...] = m_scr[...]


def _mlstm(q, k, k_t, v, o, g_col, g_row, norm_w, c0, n0, m0):
    b, t, _ = q.shape
    ch = MLSTM_CHUNK
    nb = MLSTM_BATCH_PER_STEP
    qw, vw = H_C * DK_C, H_C * DV_C
    row = lambda n: pl.BlockSpec((nb, ch, n), lambda i, j: (i, j, 0))
    st4 = pl.BlockSpec((nb, H_C, DK_C, DV_C), lambda i, j: (i, 0, 0, 0))
    st3 = pl.BlockSpec((nb, H_C, LANES), lambda i, j: (i, 0, 0))
    outs = [jax.ShapeDtypeStruct((b, t, vw), BF16), jax.ShapeDtypeStruct((b, H_C, DK_C, DV_C), F32),
            jax.ShapeDtypeStruct((b, H_C, DK_C), F32), jax.ShapeDtypeStruct((b, H_C, LANES), F32)]
    return pl.pallas_call(
        _mlstm_kernel,
        grid=(b // nb, t // ch),
        in_specs=[row(qw), row(qw), pl.BlockSpec((nb, qw, ch), lambda i, j: (i, 0, j)), row(vw), row(vw),
                  row(LANES), pl.BlockSpec((nb, 2 * H_C, ch), lambda i, j: (i, 0, j)),
                  pl.BlockSpec(norm_w.shape, lambda i, j: (0, 0)), st4, st3, st3],
        out_specs=[row(vw), st4, st3, st3],
        out_shape=outs,
        scratch_shapes=[pltpu.VMEM((nb, H_C, DK_C, DV_C), F32), pltpu.VMEM((nb, H_C, DK_C), F32),
                        pltpu.VMEM((nb, H_C, LANES), F32)],
        compiler_params=_cparams("arbitrary", "arbitrary"),
        name="mlstm",
    )(q, k, k_t, v, o, g_col, g_row, norm_w, c0, n0, m0)


def _mixed_rows(acts, w_refs, x, g_ref, b_ref):
    y = _dot(acts[0], w_refs[0][...])
    for a, w in zip(acts[1:], w_refs[1:]):
        y = y + _dot(a, w[...])
    return _layer_norm_rows(DEEPNORM_ALPHA * x + y, g_ref[...], b_ref[...])


def _route(x1, rw_ref):
    xh = x1.astype(BF16)
    xl = (x1 - xh.astype(F32)).astype(BF16)
    both = _dot_nt(rw_ref[...], xh)
    lg = both[0:ROUTE_ROWS, :] + both[ROUTE_ROWS:2 * ROUTE_ROWS, :] + _dot_nt(rw_ref[0:ROUTE_ROWS, :], xl)
    sub = lax.broadcasted_iota(I32, lg.shape, 0)
    big = jnp.int32(4 * ROUTE_ROWS)
    neg = -jnp.inf
    gl = jnp.where(sub < ROUTE_GROUP_ROW + N_GROUPS, lg, neg)
    g_max = jnp.max(gl, axis=0, keepdims=True)
    g_w = 1.0 / jnp.sum(jnp.exp(gl - g_max), axis=0, keepdims=True)
    g_idx = jnp.min(jnp.where(gl == g_max, sub, big), axis=0, keepdims=True)
    row_group = (sub - ROUTE_EXPERT_ROW) >> 2
    el = jnp.where(row_group == g_idx, lg, neg)
    e1 = jnp.max(el, axis=0, keepdims=True)
    i1 = jnp.min(jnp.where(el == e1, sub, big), axis=0, keepdims=True)
    z = jnp.sum(jnp.exp(el - e1), axis=0, keepdims=True)
    el2 = jnp.where(sub == i1, neg, el)
    e2 = jnp.max(el2, axis=0, keepdims=True)
    i2 = jnp.min(jnp.where(el2 == e2, sub, big), axis=0, keepdims=True)
    p1 = 1.0 / z
    p2 = jnp.exp(e2 - e1) / z
    w1 = p1 / (p1 + p2) * g_w
    w2 = p2 / (p1 + p2) * g_w
    id1 = (i1 - ROUTE_EXPERT_ROW).astype(F32)
    id2 = (i2 - ROUTE_EXPERT_ROW).astype(F32)
    return jnp.where(sub == 0, w1, jnp.where(sub == 1, w2, jnp.where(sub == 2, id1, jnp.where(sub == 3, id2, 0.0))))


def _slab_columns(rt):
    pad = jnp.zeros((LANES - rt.shape[0], rt.shape[1]), F32)
    return jnp.transpose(jnp.concatenate([rt, pad], axis=0))


def _mix_out_kernel(*refs, n_in, n_s):
    ap_refs = refs[:n_in]
    as_refs = refs[n_in:2 * n_in]
    w_refs = refs[2 * n_in:3 * n_in]
    xp_ref, xs_ref, g_ref, b_ref, rw_ref, out_ref, rt_ref = refs[3 * n_in:]
    i = pl.program_id(0)
    last = pl.num_programs(0) - 1

    @pl.when(i < last)
    def _():
        x1 = _mixed_rows([a[...] for a in ap_refs], w_refs, xp_ref[...], g_ref, b_ref)
        rt = _route(x1, rw_ref)
        out_ref[:, 0:D_MODEL] = x1
        out_ref[:, D_MODEL:D_MODEL + LANES] = _slab_columns(rt)
        rt_ref[...] = rt[0:8, :]

    @pl.when(i == last)
    def _():
        x1 = _mixed_rows([a[...] for a in as_refs], w_refs, xs_ref[...], g_ref, b_ref)
        rt = _route(jnp.concatenate([x1, jnp.zeros((LANES - n_s, D_MODEL), F32)], axis=0), rw_ref)
        out_ref[0:n_s, 0:D_MODEL] = x1
        out_ref[0:n_s, D_MODEL:D_MODEL + LANES] = _slab_columns(rt)[0:n_s, :]
        rt_ref[...] = jnp.zeros(rt_ref.shape, F32)
        rt_ref[:, 0:LANES] = rt[0:8, :]


def _mix_out(acts_p, acts_s, weights, xp, xs, ln_g, ln_b, rw):
    n_p, n_s = xp.shape[0], xs.shape[0]
    assert n_s <= LANES
    tm = ROW_TILE
    nb = n_p // tm
    n_in = len(acts_p)
    prow = lambda n: pl.BlockSpec((tm, n), lambda i: (jnp.minimum(i, nb - 1), 0))
    full = lambda a: pl.BlockSpec(a.shape, lambda i: (0, 0))
    width = D_MODEL + LANES
    return pl.pallas_call(
        functools.partial(_mix_out_kernel, n_in=n_in, n_s=n_s),
        grid=(nb + 1,),
        in_specs=[prow(a.shape[1]) for a in acts_p] + [full(a) for a in acts_s] + [full(w) for w in weights]
        + [prow(D_MODEL), full(xs), full(ln_g), full(ln_b), full(rw)],
        out_specs=[pl.BlockSpec((tm, width), lambda i: (i, 0)), pl.BlockSpec((8, tm), lambda i: (0, i))],
        out_shape=[jax.ShapeDtypeStruct((n_p + n_s, width), F32),
                   jax.ShapeDtypeStruct((8, (nb + 1) * tm), F32)],
        compiler_params=_cparams("arbitrary"),
        name="mix_out",
    )(*acts_p, *acts_s, *weights, xp, xs, ln_g, ln_b, rw)


def _route_plan(rt, tm):
    n = rt.shape[1]
    e1, e2 = rt[2, :].astype(I32), rt[3, :].astype(I32)
    ea, eb = jnp.minimum(e1, e2), jnp.maximum(e1, e2)
    la, lb = ea % EXP_PER_GROUP, eb % EXP_PER_GROUP
    pair_rank = sum(k * ((la == a) & (lb == b)).astype(I32) for k, (a, b) in enumerate(PAIR_SEQ))
    cls = (ea // EXP_PER_GROUP) * N_PAIRS + pair_rank
    onehot = (cls[:, None] == jnp.arange(N_CLASSES, dtype=I32)[None, :]).astype(I32)
    csum = jnp.cumsum(onehot, axis=0)
    rank = jnp.sum(onehot * csum, axis=1) - 1
    cnt = csum[-1]
    ntile = (cnt + tm - 1) // tm
    tile_end = jnp.cumsum(ntile)
    tile_start = tile_end - ntile
    n_used = tile_end[-1]
    n_tiles = -(-(n + N_CLASSES * (tm - 1)) // tm)
    pos = (jnp.sum(onehot * tile_start[None, :], axis=1) * tm + rank).astype(I32)
    pair_lo = np.array([a for a, _ in PAIR_SEQ], np.int32)
    pair_hi = np.array([b for _, b in PAIR_SEQ], np.int32)
    cls_ids = np.arange(N_CLASSES)
    cls_a = jnp.asarray((cls_ids // N_PAIRS) * EXP_PER_GROUP + pair_lo[cls_ids % N_PAIRS], I32)
    cls_b = jnp.asarray((cls_ids // N_PAIRS) * EXP_PER_GROUP + pair_hi[cls_ids % N_PAIRS], I32)
    tile_ids = jnp.arange(n_tiles, dtype=I32)
    tile_cls = jnp.sum((tile_end[None, :] <= jnp.minimum(tile_ids, n_used - 1)[:, None]).astype(I32), axis=1)
    tile_cls = jnp.minimum(tile_cls, N_CLASSES - 1)
    onehot_t = (tile_cls[:, None] == jnp.arange(N_CLASSES, dtype=I32)[None, :]).astype(I32)
    last_tile = jnp.where(ntile > 0, tile_end - 1, -1).astype(I32)
    ta = jnp.sum(onehot_t * cls_a[None, :], axis=1)
    tb = jnp.sum(onehot_t * cls_b[None, :], axis=1)
    first = jnp.ones((1,), I32)
    chg = (jnp.concatenate([first, (ta[1:] != ta[:-1]).astype(I32)])
           + 2 * jnp.concatenate([first, (tb[1:] != tb[:-1]).astype(I32)]))
    return dict(pos=pos, ta=ta, tb=tb, nu=n_used.reshape(1).astype(I32), chg=chg, last_tile=last_tile,
                n_tiles=n_tiles)


def _dispatch_kernel(lt_ref, nu_ref, pos_ref, src_ref, dst_hbm, zbuf, zsem, rsem, *, moe_tile, n_tiles):
    i = pl.program_id(0)
    td = pos_ref.shape[2]

    def zero_copy(tile):
        start = pl.multiple_of(tile * moe_tile, moe_tile)
        return pltpu.make_async_copy(zbuf, dst_hbm.at[pl.ds(start, moe_tile)], zsem)

    @pl.when(i == 0)
    def _():
        zbuf[...] = jnp.zeros(zbuf.shape, F32)
        for c in range(N_CLASSES):
            @pl.when(lt_ref[c] >= 0)
            def _(c=c):
                zero_copy(lt_ref[c]).start()
        for c in range(N_CLASSES):
            @pl.when(lt_ref[c] >= 0)
            def _(c=c):
                zero_copy(lt_ref[c]).wait()

        def spare_start(t, carry):
            zero_copy(t).start()
            return carry

        def spare_wait(t, carry):
            zero_copy(t).wait()
            return carry

        lax.fori_loop(nu_ref[0], n_tiles, spare_start, 0)
        lax.fori_loop(nu_ref[0], n_tiles, spare_wait, 0)

    for r in range(td):
        pltpu.make_async_copy(src_ref.at[pl.ds(r, 1)], dst_hbm.at[pl.ds(pos_ref[0, 0, r], 1)],
                              rsem).start(priority=r % 2)
    pltpu.make_async_copy(src_ref, dst_hbm.at[pl.ds(0, td)], rsem).wait()


def _largest_divisor_tile(n, cap):
    for t in range(cap - cap % 8, 7, -8):
        if n % t == 0:
            return t
    raise ValueError(f"no row tile for {n} rows")


def _dispatch(x1e, plan, tm):
    n, width = x1e.shape
    n_tiles = plan["n_tiles"]
    td = _largest_divisor_tile(n, 1024)
    steps = n // td
    grid_spec = pltpu.PrefetchScalarGridSpec(
        num_scalar_prefetch=2,
        grid=(steps,),
        in_specs=[pl.BlockSpec((1, 1, td), lambda i, lt, nu: (i, 0, 0), memory_space=pltpu.SMEM),
                  pl.BlockSpec((td, width), lambda i, lt, nu: (i, 0))],
        out_specs=pl.BlockSpec(memory_space=pl.ANY),
        scratch_shapes=[pltpu.VMEM((tm, width), F32), pltpu.SemaphoreType.DMA(()),
                        pltpu.SemaphoreType.DMA(())],
    )
    return pl.pallas_call(
        functools.partial(_dispatch_kernel, moe_tile=tm, n_tiles=n_tiles),
        grid_spec=grid_spec,
        out_shape=jax.ShapeDtypeStruct((n_tiles * tm, width), F32),
        compiler_params=_cparams("arbitrary"),
        name="dispatch",
    )(plan["last_tile"], plan["nu"], plan["pos"].reshape(steps, 1, td), x1e)


def _moe_kernel(ta_ref, tb_ref, nu_ref, chg_ref, x_ref, ga_f32, ua_f32, da_f32, gb_f32, ub_f32, db_f32,
                lg_ref, lb_ref, o_ref, ga_ref, ua_ref, da_ref, gb_ref, ub_ref, db_ref):
    del ta_ref, tb_ref
    g = pl.program_id(0)

    @pl.when((chg_ref[g] & 1) == 1)
    def _():
        for src, dst in ((ga_f32, ga_ref), (ua_f32, ua_ref), (da_f32, da_ref)):
            dst[0] = src[0, 0].astype(BF16)

    @pl.when((chg_ref[g] & 2) == 2)
    def _():
        for src, dst in ((gb_f32, gb_ref), (ub_f32, ub_ref), (db_f32, db_ref)):
            dst[0] = src[0, 0].astype(BF16)

    @pl.when(g < nu_ref[0])
    def _():
        x = x_ref[:, 0:D_MODEL]
        slab = x_ref[:, D_MODEL:D_MODEL + LANES]
        w1, w2, e1, e2 = slab[:, 0:1], slab[:, 1:2], slab[:, 2:3], slab[:, 3:4]
        first = e1 < e2
        wa = jnp.where(first, w1, w2)
        wb = jnp.where(first, w2, w1)
        xb = x.astype(BF16)

        def expert(gw, uw, dw):
            hid = jax.nn.silu(_dot(xb, gw[0])) * _dot(xb, uw[0])
            return _dot(hid.astype(BF16), dw[0])

        y = wa * expert(ga_ref, ua_ref, da_ref)
        y = y + wb * expert(gb_ref, ub_ref, db_ref)
        o_ref[...] = _layer_norm_rows(DEEPNORM_ALPHA * x + y, lg_ref[...], lb_ref[...])

    @pl.when(g >= nu_ref[0])
    def _():
        o_ref[...] = jnp.zeros(o_ref.shape, F32)


def _moe(xs_sorted, plan, layer, w_gate, w_up, w_down, ln_g, ln_b, tm):
    n_tiles = plan["n_tiles"]
    width = xs_sorted.shape[1]
    up_a = pl.BlockSpec((1, 1, D_MODEL, D_EXPERT), lambda g, ta, tb, nu, chg: (layer, ta[g], 0, 0))
    dn_a = pl.BlockSpec((1, 1, D_EXPERT, D_MODEL), lambda g, ta, tb, nu, chg: (layer, ta[g], 0, 0))
    up_b = pl.BlockSpec((1, 1, D_MODEL, D_EXPERT), lambda g, ta, tb, nu, chg: (layer, tb[g], 0, 0))
    dn_b = pl.BlockSpec((1, 1, D_EXPERT, D_MODEL), lambda g, ta, tb, nu, chg: (layer, tb[g], 0, 0))
    vec = pl.BlockSpec((1, D_MODEL), lambda g, ta, tb, nu, chg: (0, 0))
    up_s = pltpu.VMEM((1, D_MODEL, D_EXPERT), BF16)
    dn_s = pltpu.VMEM((1, D_EXPERT, D_MODEL), BF16)
    grid_spec = pltpu.PrefetchScalarGridSpec(
        num_scalar_prefetch=4,
        grid=(n_tiles,),
        in_specs=[pl.BlockSpec((tm, width), lambda g, ta, tb, nu, chg: (g, 0)),
                  up_a, up_a, dn_a, up_b, up_b, dn_b, vec, vec],
        out_specs=pl.BlockSpec((tm, D_MODEL), lambda g, ta, tb, nu, chg: (g, 0)),
        scratch_shapes=[up_s, up_s, dn_s, up_s, up_s, dn_s],
    )
    return pl.pallas_call(
        _moe_kernel,
        grid_spec=grid_spec,
        out_shape=jax.ShapeDtypeStruct((n_tiles * tm, D_MODEL), F32),
        compiler_params=_cparams("arbitrary"),
        name="moe",
    )(plan["ta"], plan["tb"], plan["nu"], plan["chg"], xs_sorted, w_gate, w_up, w_down, w_gate, w_up, w_down,
      ln_g, ln_b)


def _ple_kernel(pos_ref, posn_ref, x2_hbm, pp_ref, ps_ref, wg_ref, wp_ref, op_ref, os_ref, buf, sems, *, n_s):
    i = pl.program_id(0)
    last = pl.num_programs(0) - 1
    tm = buf.shape[1]
    slot = lax.rem(i, 2)

    def issue(idx_ref, s):
        for r in range(tm):
            pltpu.make_async_copy(x2_hbm.at[pl.ds(idx_ref[0, 0, r], 1)], buf.at[s, pl.ds(r, 1)],
                                  sems.at[s]).start(priority=r % 2)

    @pl.when(i == 0)
    def _():
        issue(pos_ref, slot)

    @pl.when(i < last)
    def _():
        issue(posn_ref, 1 - slot)

    pltpu.make_async_copy(x2_hbm.at[pl.ds(0, tm)], buf.at[slot], sems.at[slot]).wait()

    def rows(x, p):
        gate = jax.nn.sigmoid(_dot(x.astype(BF16), wg_ref[...]))
        return x + gate * _dot(p.astype(BF16), wp_ref[...])

    @pl.when(i < last)
    def _():
        op_ref[...] = rows(buf[slot], pp_ref[0])

    @pl.when(i == last)
    def _():
        os_ref[...] = rows(buf[slot, 0:n_s, :], ps_ref[0])


def _ple(x2_sorted, pos, layer, p_p, p_s, wg_bf, wp_bf):
    n_p, n_s = p_p.shape[1], p_s.shape[1]
    tm = ROW_TILE
    nb = n_p // tm
    steps = nb + 1
    pos_pad = jnp.zeros((steps * tm,), I32).at[:n_p + n_s].set(pos).reshape(steps, 1, tm)
    full = lambda a: pl.BlockSpec(a.shape, lambda i: (0, 0))
    prow = lambda n: pl.BlockSpec((tm, n), lambda i: (jnp.minimum(i, nb - 1), 0))
    return pl.pallas_call(
        functools.partial(_ple_kernel, n_s=n_s),
        grid=(steps,),
        in_specs=[pl.BlockSpec((1, 1, tm), lambda i: (i, 0, 0), memory_space=pltpu.SMEM),
                  pl.BlockSpec((1, 1, tm), lambda i: (jnp.minimum(i + 1, nb), 0, 0), memory_space=pltpu.SMEM),
                  pl.BlockSpec(memory_space=pl.ANY),
                  pl.BlockSpec((1, tm, D_PLE), lambda i: (layer, jnp.minimum(i, nb - 1), 0)),
                  pl.BlockSpec((1, n_s, D_PLE), lambda i: (layer, 0, 0)), full(wg_bf), full(wp_bf)],
        out_specs=[prow(D_MODEL), pl.BlockSpec((n_s, D_MODEL), lambda i: (0, 0))],
        out_shape=[jax.ShapeDtypeStruct((n_p, D_MODEL), F32), jax.ShapeDtypeStruct((n_s, D_MODEL), F32)],
        scratch_shapes=[pltpu.VMEM((2, tm, D_MODEL), F32), pltpu.SemaphoreType.DMA((2,))],
        compiler_params=_cparams("arbitrary"),
        name="ple",
    )(pos_pad, pos_pad, x2_sorted, p_p, p_s, wg_bf, wp_bf)


def _router_weights(w_group, w_router):
    wr = jnp.zeros((ROUTE_ROWS, D_MODEL), F32)
    wr = wr.at[ROUTE_GROUP_ROW:ROUTE_GROUP_ROW + N_GROUPS, :].set(jnp.transpose(w_group))
    wr = wr.at[ROUTE_EXPERT_ROW:ROUTE_EXPERT_ROW + N_EXPERTS, :].set(jnp.transpose(w_router))
    hi = wr.astype(BF16)
    lo = (wr - hi.astype(F32)).astype(BF16)
    return jnp.concatenate([hi, lo], axis=0)


def _layer_tail(i, acts_p, acts_s, w_list, xp, xs, p_p, p_s, ln_mix_g, ln_mix_b, ln_ffn_g, ln_ffn_b,
                w_group, w_router, w_exp_gate, w_exp_up, w_exp_down, w_ple_proj, w_ple_gate):
    rw = _router_weights(w_group[i], w_router[i])
    x1e, rt = _mix_out(acts_p, acts_s, w_list, xp, xs, ln_mix_g[i][None, :], ln_mix_b[i][None, :], rw)
    plan = _route_plan(rt[:, :x1e.shape[0]], MOE_TILE)
    xs_sorted = _dispatch(x1e, plan, MOE_TILE)
    x2_sorted = _moe(xs_sorted, plan, i, w_exp_gate, w_exp_up, w_exp_down, ln_ffn_g[i][None, :],
                     ln_ffn_b[i][None, :], MOE_TILE)
    return _ple(x2_sorted, plan["pos"], i, p_p, p_s, w_ple_gate[i].astype(BF16), w_ple_proj[i].astype(BF16))


def kernel(x_prompt, x_sample, cache_k, cache_v, page_table, state_conv, state_mlstm_C, state_mlstm_n,
           state_mlstm_m, p_prompt, p_sample, w_in_even, conv_w, lambda_q1, lambda_k1, lambda_q2, lambda_k2,
           subln_w, w_out_even, w_in_odd, b_gates_odd, mh_norm_w, w_out_odd, ln_mix_g, ln_mix_b, ln_ffn_g,
           ln_ffn_b, w_group, w_router, w_exp_gate, w_exp_up, w_exp_down, w_ple_proj, w_ple_gate):
    bp, tp, _ = x_prompt.shape
    bs, ts, _ = x_sample.shape
    assert ts == 1 and tp % ROW_TILE == 0 and tp % ATTN_TILE == 0 and tp % MLSTM_CHUNK == 0
    n_p = bp * tp
    past_len = page_table.shape[1] * cache_k.shape[2]
    xp = x_prompt.reshape(n_p, D_MODEL)
    xs = x_sample.reshape(bs, D_MODEL)
    tail_w = (ln_mix_g, ln_mix_b, ln_ffn_g, ln_ffn_b, w_group, w_router, w_exp_gate, w_exp_up, w_exp_down,
              w_ple_proj, w_ple_gate)
    outs_p, outs_s = {}, {}
    for i in range(DEPTH):
        j = i // 2
        p_p = p_prompt.reshape(DEPTH, n_p, D_PLE)
        p_s = p_sample.reshape(DEPTH, bs, D_PLE)
        if i % 2 == 0:
            lam_init = 0.8 - 0.6 * math.exp(-0.3 * i)
            lam_vecs = jnp.stack([lambda_q1[j], lambda_k1[j], lambda_q2[j], lambda_k2[j]])
            sub = subln_w[j][None, :]
            w_bf = w_in_even[j].astype(BF16)
            tabs_p = _rope_tables(jnp.arange(tp))
            q0, v0 = 3 * D_CONV, 3 * D_CONV + 2 * QK_B
            wqt_bf = jnp.transpose(w_in_even[j][:, q0:q0 + QK_B]).astype(BF16)
            wvt_bf = jnp.transpose(w_in_even[j][:, v0:v0 + V_B]).astype(BF16)
            yc, qt, kf, vf, kb, vt, cst = _even_in_prompt(
                x_prompt if i == 0 else xp.reshape(bp, tp, D_MODEL), w_bf, wqt_bf, wvt_bf, conv_w[j],
                jnp.zeros((bp, CONV_W - 1, D_CONV), F32), tabs_p)
            o_p = _attn_prompt(qt, kb, vt, lam_vecs, sub, lam_init)
            outs_p.setdefault("k", []).append(kf.reshape(bp, tp, 2 * H_B, DH_B))
            outs_p.setdefault("v", []).append(vf.reshape(bp, tp, H_B, 2 * DH_B))
            outs_p.setdefault("c", []).append(cst)
            tabs_s = _rope_tables(jnp.full((1,), past_len, I32))
            prev_t = jnp.swapaxes(state_conv[j], 0, 1)
            yc_s, q_s, kf_s, vf_s, u_s = _even_in_decode(xs, w_bf, conv_w[j], prev_t, tabs_s)
            sub_head = jnp.arange(2 * H_B)
            sub_head = jnp.where(sub_head < H_B, 2 * sub_head, 2 * (sub_head - H_B) + 1)
            lane_head = jnp.arange(QK_B) // DH_B
            qbd = jnp.where(lane_head[None, None, :] == sub_head[None, :, None], q_s[:, None, :],
                            jnp.zeros((), BF16))
            n_pool = cache_k.shape[1]
            pages = cache_k.shape[0] * n_pool
            k_view = jnp.transpose(cache_k, (0, 1, 3, 4, 2)).reshape(pages, QK_B, PAGE_SIZE)
            v_view = cache_v.reshape(pages, PAGE_SIZE * H_B, 2 * DH_B)
            o8 = _attn_decode(qbd, kf_s[:, None, :], vf_s[:, None, :], k_view, v_view,
                              page_table + j * n_pool, lam_vecs, sub, lam_init)
            o_s = o8[:, :H_B, :].reshape(bs, V_B).astype(BF16)
            outs_s.setdefault("k", []).append(kf_s.reshape(bs, ts, 2 * H_B, DH_B))
            outs_s.setdefault("v", []).append(vf_s.reshape(bs, ts, H_B, 2 * DH_B))
            outs_s.setdefault("c", []).append(jnp.stack([state_conv[j][:, 1, :], u_s], axis=1))
            w_out = w_out_even[j].astype(BF16)
            w_list = [w_out[:D_CONV], w_out[D_CONV:]]
            acts_p = [yc.reshape(n_p, D_CONV), o_p.reshape(n_p, V_B)]
            acts_s = [yc_s, o_s]
        else:
            w_in = w_in_odd[j]
            qw, vw = H_C * DK_C, H_C * DV_C
            w_bf = w_in[:, :2 * qw + 2 * vw].astype(BF16)
            wg = jnp.zeros((D_MODEL, LANES), F32).at[:, :2 * H_C].set(w_in[:, 2 * qw + 2 * vw:]).astype(BF16)
            wgt = jnp.concatenate([jnp.transpose(w_bf[:, qw:2 * qw]), jnp.transpose(wg[:, :2 * H_C])], axis=0)
            bg = jnp.zeros((1, LANES), F32).at[0, :2 * H_C].set(b_gates_odd[j])
            bgt = jnp.broadcast_to(b_gates_odd[j][:, None], (2 * H_C, LANES))
            nw = mh_norm_w[j][None, :]
            q, k, v, o, gc, gr, kt = _odd_in(xp.reshape(bp, tp, D_MODEL), w_bf, wg, wgt, bg, bgt, decode=False)
            h_p, c_p, n_pp, m_p = _mlstm(q, k, kt, v, o, gc, gr, nw,
                                         jnp.zeros((bp, H_C, DK_C, DV_C), F32), jnp.zeros((bp, H_C, DK_C), F32),
                                         jnp.zeros((bp, H_C, LANES), F32))
            outs_p.setdefault("C", []).append(c_p)
            outs_p.setdefault("n", []).append(n_pp)
            outs_p.setdefault("m", []).append(m_p[:, :, 0])
            q_s, k_s, v_s, o_s2, gc_s, gr_s, kt_s = _odd_in(xs, w_bf, wg, wgt, bg, bgt, decode=True)
            ch = MLSTM_CHUNK
            pad_rows = lambda a: jnp.zeros((bs, ch, a.shape[1]), a.dtype).at[:, 0, :].set(a)
            lane = jnp.arange(LANES)
            inert_c = jnp.where(lane < H_C, -jnp.inf, 0.0).astype(F32)
            gc_pad = jnp.broadcast_to(inert_c[None, None, :], (bs, ch, LANES)).at[:, 0, :].set(gc_s)
            inert_r = jnp.where(jnp.arange(2 * H_C) < H_C, -jnp.inf, 0.0).astype(F32)
            gr_pad = jnp.broadcast_to(inert_r[None, :, None], (bs, 2 * H_C, ch)).at[:, :, 0].set(gr_s.T)
            m0 = jnp.broadcast_to(state_mlstm_m[j][:, :, None], (bs, H_C, LANES))
            kt_pad = jnp.zeros((bs, qw, ch), BF16).at[:, :, 0].set(kt_s.T)
            h_s, c_s, n_s, m_s = _mlstm(pad_rows(q_s), pad_rows(k_s), kt_pad, pad_rows(v_s), pad_rows(o_s2),
                                        gc_pad, gr_pad, nw, state_mlstm_C[j], state_mlstm_n[j], m0)
            outs_s.setdefault("C", []).append(c_s)
            outs_s.setdefault("n", []).append(n_s)
            outs_s.setdefault("m", []).append(m_s[:, :, 0])
            w_list = [w_out_odd[j].astype(BF16)]
            acts_p = [h_p.reshape(n_p, vw)]
            acts_s = [h_s[:, 0, :]]
        xp, xs = _layer_tail(i, acts_p, acts_s, w_list, xp, xs, p_p, p_s, *tail_w)
    st = lambda lst: jnp.stack(lst)
    return (xp.reshape(bp, tp, D_MODEL), xs.reshape(bs, ts, D_MODEL),
            st(outs_p["k"]), st(outs_p["v"]), st(outs_p["c"]), st(outs_p["C"]), st(outs_p["n"]), st(outs_p["m"]),
            st(outs_s["k"]), st(outs_s["v"]), st(outs_s["c"]), st(outs_s["C"]), st(outs_s["n"]), st(outs_s["m"]))
```

```python
import functools
import math

import numpy as np
import jax
import jax.numpy as jnp
from jax import lax
from jax.experimental import pallas as pl
from jax.experimental.pallas import tpu as pltpu

F32 = jnp.float32
BF16 = jnp.bfloat16
I32 = jnp.int32

D_MODEL = 1024
DEPTH = 2
PAGE_SIZE = 128
D_CONV = D_MODEL // 2
CONV_W = 3
H_B = 4
DH_B = 64
ROT_DIM = DH_B // 4
ROPE_THETA = 500000.0
H_C = 4
DK_C = (D_MODEL // 2) // H_C
DV_C = D_MODEL // H_C
N_GROUPS = 4
EXP_PER_GROUP = 4
N_EXPERTS = N_GROUPS * EXP_PER_GROUP
D_EXPERT = 512
D_PLE = 256
LN_EPS = 1e-5
LOG2_E = 1.4426950408889634
DEEPNORM_ALPHA = (2 * DEPTH) ** 0.25
QK_B = 2 * H_B * DH_B
V_B = H_B * 2 * DH_B
N_PAIRS = EXP_PER_GROUP * (EXP_PER_GROUP - 1) // 2
N_CLASSES = N_GROUPS * N_PAIRS
PAIR_SEQ = ((0, 1), (0, 2), (1, 2), (1, 3), (0, 3), (2, 3))
assert EXP_PER_GROUP == 4 and len(PAIR_SEQ) == N_PAIRS

LANES = 128
VMEM_LIMIT = 56 * 1024 * 1024
ROW_TILE = 512
ATTN_TILE = 512
ATTN_HEADS_PER_STEP = 4
ONES_ROWS = 16
MLSTM_CHUNK = 128
MLSTM_DECODE_CHUNK = 16
MLSTM_BATCH_PER_STEP = 1
MOE_TILE = 256
PAGES_PER_STEP = 16
ROUTE_ROWS = 32
ROUTE_GROUP_ROW = 0
ROUTE_EXPERT_ROW = 8


def _cparams(*sem):
    return pltpu.CompilerParams(dimension_semantics=sem, vmem_limit_bytes=VMEM_LIMIT)


def _dot(a, b):
    return jnp.dot(a, b, preferred_element_type=F32)


def _dot_nt(a, b):
    return lax.dot_general(a, b, (((1,), (1,)), ((), ())), preferred_element_type=F32)


def _dot_tn(a, b):
    return lax.dot_general(a, b, (((0,), (0,)), ((), ())), preferred_element_type=F32)


def _layer_norm_rows(z, g, b):
    mu = jnp.mean(z, axis=-1, keepdims=True)
    zc = z - mu
    var = jnp.mean(zc * zc, axis=-1, keepdims=True)
    return zc * lax.rsqrt(var + LN_EPS) * g + b


def _log_sigmoid(x):
    return jnp.minimum(x, 0.0) - jnp.log1p(jnp.exp(-jnp.abs(x)))


def _gated_conv(gate_b, u, um1, um2, cw_ref):
    cw = cw_ref[...]
    conv = um2 * cw[0:1, :] + um1 * cw[1:2, :] + u * cw[2:3, :]
    return (gate_b * conv).astype(BF16)


def _rope(z, cos, sin_lo, sin_hi, axis):
    half = ROT_DIM // 2
    return z * cos + pltpu.roll(z, QK_B - half, axis) * sin_lo + pltpu.roll(z, half, axis) * sin_hi


def _even_in_prompt_kernel(x_ref, w_ref, wqt_ref, cw_ref, prev_ref, rc_ref, rs1_ref, rs2_ref,
                           rct_ref, rs1t_ref, rs2t_ref,
                           yc_ref, qt_ref, kf_ref, vf_ref, kb_ref, vt_ref, u_ref, carry_ref):
    xb = x_ref[0].astype(BF16)
    tm = xb.shape[0]

    def proj(c0, n):
        return _dot(xb, w_ref[:, c0:c0 + n])

    gate_b = proj(0, D_CONV)
    u = proj(D_CONV, D_CONV) * proj(2 * D_CONV, D_CONV)
    j = pl.program_id(1)

    @pl.when(j == 0)
    def _():
        carry_ref[...] = prev_ref[0]

    row = lax.broadcasted_iota(I32, u.shape, 0)
    c2 = carry_ref[0:1, :]
    c1 = carry_ref[1:2, :]
    um1 = jnp.where(row == 0, c1, pltpu.roll(u, 1, 0))
    um2 = jnp.where(row == 0, c2, jnp.where(row == 1, c1, pltpu.roll(u, 2, 0)))
    carry_ref[...] = u[tm - 2:tm, :]
    u_ref[0] = u[tm - 2:tm, :]
    yc_ref[0] = _gated_conv(gate_b, u, um1, um2, cw_ref)

    reps = QK_B // LANES
    tile = lambda r, ax: jnp.concatenate([r[...]] * reps, axis=ax)
    k = _rope(proj(3 * D_CONV + QK_B, QK_B), tile(rc_ref, 1), tile(rs1_ref, 1), tile(rs2_ref, 1), 1)
    kf_ref[0] = k
    kb_ref[0] = k.astype(BF16)
    v = proj(3 * D_CONV + 2 * QK_B, V_B)
    for h in range(H_B):
        vf_ref[0, pl.ds(h, tm, stride=H_B), :] = v[:, h * 2 * DH_B:(h + 1) * 2 * DH_B]
    qt = _rope(_dot_nt(wqt_ref[...], xb), tile(rct_ref, 0), tile(rs1t_ref, 0), tile(rs2t_ref, 0), 0)
    qt_ref[0] = (qt * (DH_B ** -0.5 * LOG2_E)).astype(BF16)
    vt_ref[0] = jnp.transpose(v).astype(BF16)


def _even_in_decode_kernel(x_ref, w_ref, cw_ref, prev_ref, rc_ref, rs1_ref, rs2_ref,
                           yc_ref, q_ref, kf_ref, vf_ref, u_ref):
    xb = x_ref[...].astype(BF16)

    def proj(c0, n):
        return _dot(xb, w_ref[:, c0:c0 + n])

    gate_b = proj(0, D_CONV)
    u = proj(D_CONV, D_CONV) * proj(2 * D_CONV, D_CONV)
    u_ref[...] = u
    yc_ref[...] = _gated_conv(gate_b, u, prev_ref[1], prev_ref[0], cw_ref)
    reps = QK_B // LANES
    tile = lambda r: jnp.concatenate([r[...]] * reps, axis=1)
    cos, sin_lo, sin_hi = tile(rc_ref), tile(rs1_ref), tile(rs2_ref)
    q_ref[...] = (_rope(proj(3 * D_CONV, QK_B), cos, sin_lo, sin_hi, 1) * (DH_B ** -0.5)).astype(BF16)
    kf_ref[...] = _rope(proj(3 * D_CONV + QK_B, QK_B), cos, sin_lo, sin_hi, 1)
    vf_ref[...] = proj(3 * D_CONV + 2 * QK_B, V_B)


def _rope_tables(pos):
    half = ROT_DIM // 2
    inv = ROPE_THETA ** (-jnp.arange(half, dtype=F32) / half)
    ang = pos.astype(F32)[:, None] * inv[None, :]
    cos, sin = jnp.cos(ang), jnp.sin(ang)
    t = pos.shape[0]
    ones = jnp.ones((t, DH_B - ROT_DIM), F32)
    zeros = jnp.zeros((t, DH_B - ROT_DIM), F32)
    zh = jnp.zeros((t, half), F32)
    c = jnp.concatenate([cos, cos, ones], axis=1)
    s_lo = jnp.concatenate([-sin, zh, zeros], axis=1)
    s_hi = jnp.concatenate([zh, sin, zeros], axis=1)
    tile2 = lambda a: jnp.concatenate([a, a], axis=1)
    return tile2(c), tile2(s_lo), tile2(s_hi)


def _even_in_prompt(x, w_bf, wqt_bf, conv_w, conv_prev, tables):
    b, t, _ = x.shape
    tm = ROW_TILE
    row3 = lambda n: pl.BlockSpec((1, tm, n), lambda i, j: (i, j, 0))
    col3 = lambda n: pl.BlockSpec((1, n, tm), lambda i, j: (i, 0, j))
    full2 = lambda a: pl.BlockSpec(a.shape, lambda i, j: (0, 0))
    tab = pl.BlockSpec((tm, LANES), lambda i, j: (j, 0))
    tab_t = pl.BlockSpec((LANES, tm), lambda i, j: (0, j))
    st = pl.BlockSpec((1, CONV_W - 1, D_CONV), lambda i, j: (i, 0, 0))
    tables_t = [jnp.transpose(a) for a in tables]
    outs = [jax.ShapeDtypeStruct((b, t, D_CONV), BF16), jax.ShapeDtypeStruct((b, QK_B, t), BF16),
            jax.ShapeDtypeStruct((b, t, QK_B), F32), jax.ShapeDtypeStruct((b, t * H_B, 2 * DH_B), F32),
            jax.ShapeDtypeStruct((b, t, QK_B), BF16), jax.ShapeDtypeStruct((b, V_B, t), BF16),
            jax.ShapeDtypeStruct((b, CONV_W - 1, D_CONV), F32)]
    return pl.pallas_call(
        _even_in_prompt_kernel,
        grid=(b, t // tm),
        in_specs=[row3(D_MODEL), full2(w_bf), full2(wqt_bf), full2(conv_w), st,
                  tab, tab, tab, tab_t, tab_t, tab_t],
        out_specs=[row3(D_CONV), col3(QK_B), row3(QK_B),
                   pl.BlockSpec((1, tm * H_B, 2 * DH_B), lambda i, j: (i, j, 0)), row3(QK_B), col3(V_B), st],
        out_shape=outs,
        scratch_shapes=[pltpu.VMEM((CONV_W - 1, D_CONV), F32)],
        compiler_params=_cparams("arbitrary", "arbitrary"),
        name="even_in_prompt",
    )(x, w_bf, wqt_bf, conv_w, conv_prev, *tables, *tables_t)


def _even_in_decode(x, w_bf, conv_w, conv_prev_t, tables):
    n = x.shape[0]
    full = lambda a: pl.BlockSpec(a.shape, lambda i: (0,) * a.ndim)
    o2 = lambda c, dt: jax.ShapeDtypeStruct((n, c), dt)
    outs = [o2(D_CONV, BF16), o2(QK_B, BF16), o2(QK_B, F32), o2(V_B, F32), o2(D_CONV, F32)]
    ins = [x, w_bf, conv_w, conv_prev_t, *tables]
    return pl.pallas_call(
        _even_in_decode_kernel,
        grid=(1,),
        in_specs=[full(a) for a in ins],
        out_specs=[pl.BlockSpec(o.shape, lambda i: (0, 0)) for o in outs],
        out_shape=outs,
        compiler_params=_cparams("arbitrary"),
        name="even_in_decode",
    )(*ins)


def _lambda_value(lam_ref, lam_init):
    lv = lam_ref[...]
    a = jnp.sum(lv[0:1, :] * lv[1:2, :], axis=1, keepdims=True)
    b = jnp.sum(lv[2:3, :] * lv[3:4, :], axis=1, keepdims=True)
    return jnp.exp(a) - jnp.exp(b) + lam_init


def _sub_norm(o, sub_ref, lam_init):
    ms = jnp.mean(o * o, axis=-1, keepdims=True)
    return o * lax.rsqrt(ms + LN_EPS) * sub_ref[...] * (1.0 - lam_init)


def _attn_prompt_kernel(qt_ref, k_ref, vt_ref, lam_ref, sub_ref, o_ref, *scratch, lam_init):
    i = pl.program_id(2)
    tq = qt_ref.shape[2]
    tk = tq
    n_heads = qt_ref.shape[1] // LANES
    m_scrs, acc_scrs = scratch[:n_heads], scratch[n_heads:]
    for m_scr, acc_scr in zip(m_scrs, acc_scrs):
        m_scr[...] = jnp.full(m_scr.shape, -jnp.inf, F32)
        acc_scr[...] = jnp.zeros(acc_scr.shape, F32)
    ones = jnp.ones((ONES_ROWS, tk), BF16)
    qqs = []
    for h in range(n_heads):
        qt = qt_ref[0, h * LANES:(h + 1) * LANES, :].astype(F32)
        feat = lax.broadcasted_iota(I32, qt.shape, 0)
        qqs.append(jnp.concatenate([jnp.where(feat < DH_B, qt, 0.0), jnp.where(feat >= DH_B, qt, 0.0)],
                                   axis=1).astype(BF16))

    def step(j, masked):
        start = pl.multiple_of(j * tk, tk)
        scores = [_dot(k_ref[0, pl.ds(start, tk), h * LANES:(h + 1) * LANES], qqs[h])
                  for h in range(n_heads)]
        probs, alphas = [], []
        for h in range(n_heads):
            s = scores[h]
            if masked:
                key = lax.broadcasted_iota(I32, s.shape, 0)
                qry = lax.broadcasted_iota(I32, s.shape, 1)
                qry = jnp.where(qry >= tq, qry - tq, qry)
                s = jnp.where(key <= qry, s, -jnp.inf)
            m_prev = m_scrs[h][...]
            m_new = jnp.maximum(m_prev, jnp.max(s, axis=0, keepdims=True))
            alphas.append(jnp.exp2(m_prev - m_new))
            probs.append(jnp.exp2(s - m_new).astype(BF16))
            m_scrs[h][...] = m_new
        for h in range(n_heads):
            vtj = jnp.concatenate([vt_ref[0, h * LANES:(h + 1) * LANES, pl.ds(start, tk)], ones], axis=0)
            acc_scrs[h][...] = alphas[h] * acc_scrs[h][...] + _dot(vtj, probs[h])

    def body(j, carry):
        step(j, False)
        return carry

    lax.fori_loop(0, i, body, 0)
    step(i, True)
    lam = _lambda_value(lam_ref, lam_init)
    for h in range(n_heads):
        on = acc_scrs[h][0:LANES, :] / acc_scrs[h][LANES:LANES + 1, :]
        o = jnp.transpose(on[:, 0:tq] - lam * on[:, tq:2 * tq])
        o_ref[0, :, h * LANES:(h + 1) * LANES] = _sub_norm(o, sub_ref, lam_init).astype(BF16)


def _attn_prompt(qt, k, vt, lam_vecs, subln, lam_init):
    b, t, _ = k.shape
    tq = ATTN_TILE
    nh = ATTN_HEADS_PER_STEP
    hw = nh * LANES
    full = lambda a: pl.BlockSpec(a.shape, lambda bi, h, i: (0, 0))
    return pl.pallas_call(
        functools.partial(_attn_prompt_kernel, lam_init=lam_init),
        grid=(b, H_B // nh, t // tq),
        in_specs=[pl.BlockSpec((1, hw, tq), lambda bi, h, i: (bi, h, i)),
                  pl.BlockSpec((1, t, hw), lambda bi, h, i: (bi, 0, h)),
                  pl.BlockSpec((1, hw, t), lambda bi, h, i: (bi, h, 0)),
                  full(lam_vecs), full(subln)],
        out_specs=pl.BlockSpec((1, tq, hw), lambda bi, h, i: (bi, i, h)),
        out_shape=jax.ShapeDtypeStruct((b, t, V_B), BF16),
        scratch_shapes=[pltpu.VMEM((1, 2 * tq), F32)] * nh + [pltpu.VMEM((LANES + ONES_ROWS, 2 * tq), F32)] * nh,
        compiler_params=_cparams("arbitrary", "arbitrary", "arbitrary"),
        name="attn_prompt",
    )(qt, k, vt, lam_vecs, subln)


def _attn_decode_kernel(pt_ref, qbd_ref, kn_ref, vn_ref, lam_ref, sub_ref, *rest, lam_init, n_pages):
    del pt_ref
    k_refs = rest[:n_pages]
    v_refs = rest[n_pages:2 * n_pages]
    o_ref, m_scr, l_scr, acc_scr = rest[2 * n_pages:]
    j = pl.program_id(1)
    qbd = qbd_ref[0]

    @pl.when(j == 0)
    def _():
        s_new = jnp.sum(qbd.astype(F32) * kn_ref[0], axis=1, keepdims=True)
        m_scr[...] = s_new
        l_scr[...] = jnp.ones(l_scr.shape, F32)
        acc_scr[...] = jnp.broadcast_to(vn_ref[0], acc_scr.shape)

    s = jnp.concatenate([_dot(qbd, k_refs[r][0].astype(BF16)) for r in range(n_pages)], axis=1)
    m_prev = m_scr[...]
    m_new = jnp.maximum(m_prev, jnp.max(s, axis=1, keepdims=True))
    alpha = jnp.exp(m_prev - m_new)
    p = jnp.exp(s - m_new)
    l_scr[...] = alpha * l_scr[...] + jnp.sum(p, axis=1, keepdims=True)

    def head_pv(h):
        acc = None
        for r in range(n_pages):
            vh = v_refs[r][0, pl.ds(h, PAGE_SIZE, stride=H_B), :].astype(BF16)
            term = _dot(p[:, r * PAGE_SIZE:(r + 1) * PAGE_SIZE].astype(BF16), vh)
            acc = term if acc is None else acc + term
        return acc

    pv = jnp.concatenate([head_pv(h) for h in range(H_B)], axis=1)
    acc_scr[...] = alpha * acc_scr[...] + pv
    m_scr[...] = m_new

    @pl.when(j == pl.num_programs(1) - 1)
    def _():
        on = acc_scr[...] / l_scr[...]
        row = lax.broadcasted_iota(I32, (2 * H_B, 2 * DH_B), 0)
        head = jnp.where(row >= H_B, row - H_B, row)
        o8 = jnp.zeros((2 * H_B, 2 * DH_B), F32)
        for c in range(H_B):
            o8 = o8 + jnp.where(head == c, on[:, c * 2 * DH_B:(c + 1) * 2 * DH_B], 0.0)
        lam = _lambda_value(lam_ref, lam_init)
        o = o8 - lam * pltpu.roll(o8, H_B, 0)
        o_ref[0] = _sub_norm(o, sub_ref, lam_init)


def _attn_decode(qbd, k_new, v_new, cache_k, cache_v, page_table, lam_vecs, subln, lam_init):
    n = qbd.shape[0]
    n_pages = page_table.shape[1]
    pp = PAGES_PER_STEP
    width = V_B
    c2 = lambda a: pl.BlockSpec(a.shape, lambda b, j, pt: (0, 0))
    per_b = lambda a: pl.BlockSpec((1,) + a.shape[1:], lambda b, j, pt: (b, 0, 0))

    def page(r, arr):
        return pl.BlockSpec((1,) + arr.shape[1:], lambda b, j, pt: (pt[b, j * pp + r], 0, 0))

    grid_spec = pltpu.PrefetchScalarGridSpec(
        num_scalar_prefetch=1,
        grid=(n, n_pages // pp),
        in_specs=[per_b(qbd), per_b(k_new), per_b(v_new), c2(lam_vecs), c2(subln)]
        + [page(r, cache_k) for r in range(pp)] + [page(r, cache_v) for r in range(pp)],
        out_specs=pl.BlockSpec((1, 2 * H_B, 2 * DH_B), lambda b, j, pt: (b, 0, 0)),
        scratch_shapes=[pltpu.VMEM((2 * H_B, 1), F32), pltpu.VMEM((2 * H_B, 1), F32),
                        pltpu.VMEM((2 * H_B, width), F32)],
    )
    return pl.pallas_call(
        functools.partial(_attn_decode_kernel, lam_init=lam_init, n_pages=pp),
        grid_spec=grid_spec,
        out_shape=jax.ShapeDtypeStruct((n, 2 * H_B, 2 * DH_B), F32),
        compiler_params=_cparams("arbitrary", "arbitrary"),
        name="attn_decode",
    )(page_table, qbd, k_new, v_new, lam_vecs, subln, *([cache_k] * pp), *([cache_v] * pp))


def _odd_in_kernel(x_ref, w_ref, wg_ref, wgt_ref, bg_ref, bgt_ref,
                   q_ref, k_ref, v_ref, o_ref, gc_ref, gr_ref, kt_ref, *, decode):
    xb = x_ref[0].astype(BF16) if not decode else x_ref[...].astype(BF16)
    qw = H_C * DK_C
    vw = H_C * DV_C
    q = _dot(xb, w_ref[:, 0:qw]) * (DK_C ** -0.5)
    k = _dot(xb, w_ref[:, qw:2 * qw])
    v = _dot(xb, w_ref[:, 2 * qw:2 * qw + vw])
    o = _dot(xb, w_ref[:, 2 * qw + vw:2 * qw + 2 * vw])
    g_col = _dot(xb, wg_ref[...]) + bg_ref[...]
    lane = lax.broadcasted_iota(I32, g_col.shape, 1)
    g_col = jnp.where(lane < H_C, g_col, _log_sigmoid(g_col))
    k_t = jnp.transpose(k).astype(BF16)
    g_row = _dot_nt(wgt_ref[...], xb) + bgt_ref[:, 0:1]
    sub = lax.broadcasted_iota(I32, g_row.shape, 0)
    g_row = jnp.where(sub < H_C, g_row, _log_sigmoid(g_row))
    if decode:
        kt_ref[...] = k_t
        q_ref[...] = q.astype(BF16)
        k_ref[...] = k.astype(BF16)
        v_ref[...] = v.astype(BF16)
        o_ref[...] = o
        gc_ref[...] = g_col
        gr_ref[...] = g_row
    else:
        q_ref[0] = q.astype(BF16)
        k_ref[0] = k.astype(BF16)
        v_ref[0] = v.astype(BF16)
        o_ref[0] = o
        gc_ref[0] = g_col
        gr_ref[0] = g_row
        kt_ref[0] = k_t


def _odd_in(x, w_bf, wg, wgt, bg, bgt, decode):
    qw, vw = H_C * DK_C, H_C * DV_C
    if decode:
        n = x.shape[0]
        ins = [x, w_bf, wg, wgt, bg, bgt]
        outs = [jax.ShapeDtypeStruct((n, qw), BF16), jax.ShapeDtypeStruct((n, qw), BF16),
                jax.ShapeDtypeStruct((n, vw), BF16), jax.ShapeDtypeStruct((n, vw), F32),
                jax.ShapeDtypeStruct((n, LANES), F32), jax.ShapeDtypeStruct((2 * H_C, n), F32),
                jax.ShapeDtypeStruct((qw, n), BF16)]
        return pl.pallas_call(
            functools.partial(_odd_in_kernel, decode=True),
            grid=(1,),
            in_specs=[pl.BlockSpec(a.shape, lambda i: (0, 0)) for a in ins],
            out_specs=[pl.BlockSpec(o.shape, lambda i: (0, 0)) for o in outs],
            out_shape=outs,
            compiler_params=_cparams("arbitrary"),
            name="odd_in_decode",
        )(*ins)
    b, t, _ = x.shape
    tm = ROW_TILE
    row3 = lambda n: pl.BlockSpec((1, tm, n), lambda i, j: (i, j, 0))
    full2 = lambda a: pl.BlockSpec(a.shape, lambda i, j: (0, 0))
    outs = [jax.ShapeDtypeStruct((b, t, qw), BF16), jax.ShapeDtypeStruct((b, t, qw), BF16),
            jax.ShapeDtypeStruct((b, t, vw), BF16), jax.ShapeDtypeStruct((b, t, vw), F32),
            jax.ShapeDtypeStruct((b, t, LANES), F32), jax.ShapeDtypeStruct((b, 2 * H_C, t), F32),
            jax.ShapeDtypeStruct((b, qw, t), BF16)]
    return pl.pallas_call(
        functools.partial(_odd_in_kernel, decode=False),
        grid=(b, t // tm),
        in_specs=[row3(D_MODEL), full2(w_bf), full2(wg), full2(wgt), full2(bg), full2(bgt)],
        out_specs=[row3(qw), row3(qw), row3(vw), row3(vw), row3(LANES),
                   pl.BlockSpec((1, 2 * H_C, tm), lambda i, j: (i, 0, j)),
                   pl.BlockSpec((1, qw, tm), lambda i, j: (i, 0, j))],
        out_shape=outs,
        compiler_params=_cparams("arbitrary", "arbitrary"),
        name="odd_in_prompt",
    )(x, w_bf, wg, wgt, bg, bgt)


def _mlstm_kernel(q_ref, k_ref, kt_ref, v_ref, o_ref, gc_ref, gr_ref, nw_ref, c0_ref, n0_ref, m0_ref,
                  h_ref, c_out, n_out, m_out, c_scr, n_scr, m_scr):
    ci = pl.program_id(1)
    chunk = q_ref.shape[1]

    @pl.when(ci == 0)
    def _():
        c_scr[...] = c0_ref[...]
        n_scr[...] = n0_ref[...]
        m_scr[...] = m0_ref[...]

    t_idx = lax.broadcasted_iota(I32, (chunk, chunk), 0)
    s_idx = lax.broadcasted_iota(I32, (chunk, chunk), 1)
    causal = s_idx <= t_idx
    for bb, h in [(bb, h) for bb in range(q_ref.shape[0]) for h in range(H_C)]:
        q = q_ref[bb, :, h * DK_C:(h + 1) * DK_C]
        k = k_ref[bb, :, h * DK_C:(h + 1) * DK_C]
        v = v_ref[bb, :, h * DV_C:(h + 1) * DV_C]
        ig_r = gr_ref[bb, h:h + 1, :]
        lf_r = gr_ref[bb, H_C + h:H_C + h + 1, :]
        ig_c = gc_ref[bb, :, h:h + 1]
        lf_c = gc_ref[bb, :, H_C + h:H_C + h + 1]
        bcum_c = jnp.sum(jnp.where(causal, lf_r, 0.0), axis=1, keepdims=True)
        bcum_r = jnp.sum(jnp.where(t_idx <= s_idx, lf_c, 0.0), axis=0, keepdims=True)
        m0 = m_scr[bb, h:h + 1, 0:1]
        dmat = jnp.where(causal, bcum_c - bcum_r + ig_r, -jnp.inf)
        inter = bcum_c + m0
        m = jnp.maximum(inter, jnp.max(dmat, axis=1, keepdims=True))
        w = jnp.exp(dmat - m)
        g = jnp.exp(inter - m)
        s = _dot_nt(q, k) * w
        c0 = c_scr[bb, h]
        n0 = n_scr[bb, h:h + 1, :]
        num = g * _dot(q, c0.astype(BF16)) + _dot(s.astype(BF16), v)
        den = g * jnp.sum(q.astype(F32) * n0, axis=1, keepdims=True) + jnp.sum(s, axis=1, keepdims=True)
        hid = num / jnp.maximum(jnp.abs(den), jnp.exp(-m))
        m_last = m[chunk - 1:chunk, :]
        b_last = bcum_c[chunk - 1:chunk, :]
        w_last = jnp.exp(b_last - bcum_c + ig_c - m_last)
        g_last = jnp.exp(b_last + m0 - m_last)
        kw = k.astype(F32) * w_last
        w_last_r = jnp.exp(b_last - bcum_r + ig_r - m_last)
        kw_t = (kt_ref[bb, h * DK_C:(h + 1) * DK_C, :].astype(F32) * w_last_r).astype(BF16)
        c_scr[bb, h] = g_last * c0 + _dot(kw_t, v)
        n_scr[bb, h:h + 1, :] = g_last * n0 + jnp.sum(kw, axis=0, keepdims=True)
        m_scr[bb, h:h + 1, :] = jnp.broadcast_to(m_last, (1, LANES))
        mu = jnp.mean(hid, axis=1, keepdims=True)
        hc = hid - mu
        var = jnp.mean(hc * hc, axis=1, keepdims=True)
        hn = hc * lax.rsqrt(var + LN_EPS) * nw_ref[:, h * DV_C:(h + 1) * DV_C]
        gate = jax.nn.sigmoid(o_ref[bb, :, h * DV_C:(h + 1) * DV_C])
        h_ref[bb, :, h * DV_C:(h + 1) * DV_C] = (gate * hn).astype(BF16)

    c_out[...] = c_scr[...]
    n_out[...] = n_scr[...]
    m_out[...] = m_scr[...]


def _mlstm(q, k, k_t, v, o, g_col, g_row, norm_w, c0, n0, m0, ch):
    b, t, _ = q.shape
    nb = MLSTM_BATCH_PER_STEP
    qw, vw = H_C * DK_C, H_C * DV_C
    row = lambda n: pl.BlockSpec((nb, ch, n), lambda i, j: (i, j, 0))
    st4 = pl.BlockSpec((nb, H_C, DK_C, DV_C), lambda i, j: (i, 0, 0, 0))
    st3 = pl.BlockSpec((nb, H_C, LANES), lambda i, j: (i, 0, 0))
    outs = [jax.ShapeDtypeStruct((b, t, vw), BF16), jax.ShapeDtypeStruct((b, H_C, DK_C, DV_C), F32),
            jax.ShapeDtypeStruct((b, H_C, DK_C), F32), jax.ShapeDtypeStruct((b, H_C, LANES), F32)]
    return pl.pallas_call(
        _mlstm_kernel,
        grid=(b // nb, t // ch),
        in_specs=[row(qw), row(qw), pl.BlockSpec((nb, qw, ch), lambda i, j: (i, 0, j)), row(vw), row(vw),
                  row(LANES), pl.BlockSpec((nb, 2 * H_C, ch), lambda i, j: (i, 0, j)),
                  pl.BlockSpec(norm_w.shape, lambda i, j: (0, 0)), st4, st3, st3],
        out_specs=[row(vw), st4, st3, st3],
        out_shape=outs,
        scratch_shapes=[pltpu.VMEM((nb, H_C, DK_C, DV_C), F32), pltpu.VMEM((nb, H_C, DK_C), F32),
                        pltpu.VMEM((nb, H_C, LANES), F32)],
        compiler_params=_cparams("arbitrary", "arbitrary"),
        name="mlstm",
    )(q, k, k_t, v, o, g_col, g_row, norm_w, c0, n0, m0)


def _mix_project(acts, w_refs):
    y = _dot(acts[0], w_refs[0][...])
    for a, w in zip(acts[1:], w_refs[1:]):
        y = y + _dot(a, w[...])
    return y


def _mixed_rows(acts, w_refs, x, g_ref, b_ref):
    return _layer_norm_rows(DEEPNORM_ALPHA * x + _mix_project(acts, w_refs), g_ref[...], b_ref[...])


def _route_logits(x1, rw_ref):
    xh = x1.astype(BF16)
    xl = (x1 - xh.astype(F32)).astype(BF16)
    both = _dot_nt(rw_ref[...], xh)
    return both[0:ROUTE_ROWS, :] + both[ROUTE_ROWS:2 * ROUTE_ROWS, :] + _dot_nt(rw_ref[0:ROUTE_ROWS, :], xl)


def _route(x1, rw_ref):
    return _route_decide(_route_logits(x1, rw_ref))


def _route_decide(lg):
    sub = lax.broadcasted_iota(I32, lg.shape, 0)
    big = jnp.int32(4 * ROUTE_ROWS)
    neg = -jnp.inf
    gl = jnp.where(sub < ROUTE_GROUP_ROW + N_GROUPS, lg, neg)
    g_max = jnp.max(gl, axis=0, keepdims=True)
    g_w = 1.0 / jnp.sum(jnp.exp(gl - g_max), axis=0, keepdims=True)
    g_idx = jnp.min(jnp.where(gl == g_max, sub, big), axis=0, keepdims=True)
    row_group = (sub - ROUTE_EXPERT_ROW) >> 2
    el = jnp.where(row_group == g_idx, lg, neg)
    e1 = jnp.max(el, axis=0, keepdims=True)
    i1 = jnp.min(jnp.where(el == e1, sub, big), axis=0, keepdims=True)
    z = jnp.sum(jnp.exp(el - e1), axis=0, keepdims=True)
    el2 = jnp.where(sub == i1, neg, el)
    e2 = jnp.max(el2, axis=0, keepdims=True)
    i2 = jnp.min(jnp.where(el2 == e2, sub, big), axis=0, keepdims=True)
    p1 = 1.0 / z
    p2 = jnp.exp(e2 - e1) / z
    w1 = p1 / (p1 + p2) * g_w
    w2 = p2 / (p1 + p2) * g_w
    id1 = (i1 - ROUTE_EXPERT_ROW).astype(F32)
    id2 = (i2 - ROUTE_EXPERT_ROW).astype(F32)
    return jnp.where(sub == 0, w1, jnp.where(sub == 1, w2, jnp.where(sub == 2, id1, jnp.where(sub == 3, id2, 0.0))))


def _slab_columns(rt):
    pad = jnp.zeros((LANES - rt.shape[0], rt.shape[1]), F32)
    return jnp.transpose(jnp.concatenate([rt, pad], axis=0))


def _mix_out_kernel(*refs, n_in, n_s):
    ap_refs = refs[:n_in]
    as_refs = refs[n_in:2 * n_in]
    w_refs = refs[2 * n_in:3 * n_in]
    xp_ref, xs_ref, g_ref, b_ref, rw_ref, out_ref, rt_ref = refs[3 * n_in:]
    i = pl.program_id(0)
    last = pl.num_programs(0) - 1

    @pl.when(i < last)
    def _():
        x1 = _mixed_rows([a[...] for a in ap_refs], w_refs, xp_ref[...], g_ref, b_ref)
        rt = _route(x1, rw_ref)
        out_ref[:, 0:D_MODEL] = x1
        out_ref[:, D_MODEL:D_MODEL + LANES] = _slab_columns(rt)
        rt_ref[...] = rt[0:8, :]

    @pl.when(i == last)
    def _():
        x1 = _mixed_rows([a[...] for a in as_refs], w_refs, xs_ref[...], g_ref, b_ref)
        rt = _route(jnp.concatenate([x1, jnp.zeros((LANES - n_s, D_MODEL), F32)], axis=0), rw_ref)
        out_ref[0:n_s, 0:D_MODEL] = x1
        out_ref[0:n_s, D_MODEL:D_MODEL + LANES] = _slab_columns(rt)[0:n_s, :]
        rt_ref[...] = jnp.zeros(rt_ref.shape, F32)
        rt_ref[:, 0:LANES] = rt[0:8, :]


def _mix_out(acts_p, acts_s, weights, xp, xs, ln_g, ln_b, rw):
    n_p, n_s = xp.shape[0], xs.shape[0]
    assert n_s <= LANES
    tm = ROW_TILE
    nb = n_p // tm
    n_in = len(acts_p)
    prow = lambda n: pl.BlockSpec((tm, n), lambda i: (jnp.minimum(i, nb - 1), 0))
    full = lambda a: pl.BlockSpec(a.shape, lambda i: (0, 0))
    width = D_MODEL + LANES
    return pl.pallas_call(
        functools.partial(_mix_out_kernel, n_in=n_in, n_s=n_s),
        grid=(nb + 1,),
        in_specs=[prow(a.shape[1]) for a in acts_p] + [full(a) for a in acts_s] + [full(w) for w in weights]
        + [prow(D_MODEL), full(xs), full(ln_g), full(ln_b), full(rw)],
        out_specs=[pl.BlockSpec((tm, width), lambda i: (i, 0)), pl.BlockSpec((8, tm), lambda i: (0, i))],
        out_shape=[jax.ShapeDtypeStruct((n_p + n_s, width), F32),
                   jax.ShapeDtypeStruct((8, (nb + 1) * tm), F32)],
        compiler_params=_cparams("arbitrary"),
        name="mix_out",
    )(*acts_p, *acts_s, *weights, xp, xs, ln_g, ln_b, rw)


def _route_plan(rt, tm):
    n = rt.shape[1]
    e1, e2 = rt[2, :].astype(I32), rt[3, :].astype(I32)
    ea, eb = jnp.minimum(e1, e2), jnp.maximum(e1, e2)
    la, lb = ea % EXP_PER_GROUP, eb % EXP_PER_GROUP
    pair_rank = sum(k * ((la == a) & (lb == b)).astype(I32) for k, (a, b) in enumerate(PAIR_SEQ))
    cls = (ea // EXP_PER_GROUP) * N_PAIRS + pair_rank
    onehot = (cls[:, None] == jnp.arange(N_CLASSES, dtype=I32)[None, :]).astype(I32)
    csum = jnp.cumsum(onehot, axis=0)
    rank = jnp.sum(onehot * csum, axis=1) - 1
    cnt = csum[-1]
    ntile = (cnt + tm - 1) // tm
    tile_end = jnp.cumsum(ntile)
    tile_start = tile_end - ntile
    n_used = tile_end[-1]
    n_tiles = -(-(n + N_CLASSES * (tm - 1)) // tm)
    pos = (jnp.sum(onehot * tile_start[None, :], axis=1) * tm + rank).astype(I32)
    pair_lo = np.array([a for a, _ in PAIR_SEQ], np.int32)
    pair_hi = np.array([b for _, b in PAIR_SEQ], np.int32)
    cls_ids = np.arange(N_CLASSES)
    cls_a = jnp.asarray((cls_ids // N_PAIRS) * EXP_PER_GROUP + pair_lo[cls_ids % N_PAIRS], I32)
    cls_b = jnp.asarray((cls_ids // N_PAIRS) * EXP_PER_GROUP + pair_hi[cls_ids % N_PAIRS], I32)
    tile_ids = jnp.arange(n_tiles, dtype=I32)
    tile_cls = jnp.sum((tile_end[None, :] <= jnp.minimum(tile_ids, n_used - 1)[:, None]).astype(I32), axis=1)
    tile_cls = jnp.minimum(tile_cls, N_CLASSES - 1)
    onehot_t = (tile_cls[:, None] == jnp.arange(N_CLASSES, dtype=I32)[None, :]).astype(I32)
    last_tile = jnp.where(ntile > 0, tile_end - 1, -1).astype(I32)
    ta = jnp.sum(onehot_t * cls_a[None, :], axis=1)
    tb = jnp.sum(onehot_t * cls_b[None, :], axis=1)
    first = jnp.ones((1,), I32)
    chg = (jnp.concatenate([first, (ta[1:] != ta[:-1]).astype(I32)])
           + 2 * jnp.concatenate([first, (tb[1:] != tb[:-1]).astype(I32)]))
    return dict(pos=pos, ta=ta, tb=tb, nu=n_used.reshape(1).astype(I32), chg=chg, last_tile=last_tile,
                n_tiles=n_tiles)


def _dispatch_kernel(lt_ref, nu_ref, pos_ref, src_ref, dst_hbm, zbuf, zsem, rsem, *, moe_tile, n_tiles):
    i = pl.program_id(0)
    td = pos_ref.shape[2]

    def zero_copy(tile):
        start = pl.multiple_of(tile * moe_tile, moe_tile)
        return pltpu.make_async_copy(zbuf, dst_hbm.at[pl.ds(start, moe_tile)], zsem)

    @pl.when(i == 0)
    def _():
        zbuf[...] = jnp.zeros(zbuf.shape, F32)
        for c in range(N_CLASSES):
            @pl.when(lt_ref[c] >= 0)
            def _(c=c):
                zero_copy(lt_ref[c]).start()
        for c in range(N_CLASSES):
            @pl.when(lt_ref[c] >= 0)
            def _(c=c):
                zero_copy(lt_ref[c]).wait()

        def spare_start(t, carry):
            zero_copy(t).start()
            return carry

        def spare_wait(t, carry):
            zero_copy(t).wait()
            return carry

        lax.fori_loop(nu_ref[0], n_tiles, spare_start, 0)
        lax.fori_loop(nu_ref[0], n_tiles, spare_wait, 0)

    for r in range(td):
        pltpu.make_async_copy(src_ref.at[pl.ds(r, 1)], dst_hbm.at[pl.ds(pos_ref[0, 0, r], 1)],
                              rsem).start(priority=r % 2)
    pltpu.make_async_copy(src_ref, dst_hbm.at[pl.ds(0, td)], rsem).wait()


def _largest_divisor_tile(n, cap):
    for t in range(cap - cap % 8, 7, -8):
        if n % t == 0:
            return t
    raise ValueError(f"no row tile for {n} rows")


def _dispatch(x1e, plan, tm):
    n, width = x1e.shape
    n_tiles = plan["n_tiles"]
    td = _largest_divisor_tile(n, 1024)
    steps = n // td
    grid_spec = pltpu.PrefetchScalarGridSpec(
        num_scalar_prefetch=2,
        grid=(steps,),
        in_specs=[pl.BlockSpec((1, 1, td), lambda i, lt, nu: (i, 0, 0), memory_space=pltpu.SMEM),
                  pl.BlockSpec((td, width), lambda i, lt, nu: (i, 0))],
        out_specs=pl.BlockSpec(memory_space=pl.ANY),
        scratch_shapes=[pltpu.VMEM((tm, width), F32), pltpu.SemaphoreType.DMA(()),
                        pltpu.SemaphoreType.DMA(())],
    )
    return pl.pallas_call(
        functools.partial(_dispatch_kernel, moe_tile=tm, n_tiles=n_tiles),
        grid_spec=grid_spec,
        out_shape=jax.ShapeDtypeStruct((n_tiles * tm, width), F32),
        compiler_params=_cparams("arbitrary"),
        name="dispatch",
    )(plan["last_tile"], plan["nu"], plan["pos"].reshape(steps, 1, td), x1e)


def _moe_kernel(ta_ref, tb_ref, nu_ref, chg_ref, x_ref, ga_f32, ua_f32, da_f32, gb_f32, ub_f32, db_f32,
                lg_ref, lb_ref, o_ref, ga_ref, ua_ref, da_ref, gb_ref, ub_ref, db_ref):
    del ta_ref, tb_ref
    g = pl.program_id(0)

    @pl.when((chg_ref[g] & 1) == 1)
    def _():
        for src, dst in ((ga_f32, ga_ref), (ua_f32, ua_ref), (da_f32, da_ref)):
            dst[0] = src[0, 0].astype(BF16)

    @pl.when((chg_ref[g] & 2) == 2)
    def _():
        for src, dst in ((gb_f32, gb_ref), (ub_f32, ub_ref), (db_f32, db_ref)):
            dst[0] = src[0, 0].astype(BF16)

    @pl.when(g < nu_ref[0])
    def _():
        x = x_ref[:, 0:D_MODEL]
        slab = x_ref[:, D_MODEL:D_MODEL + LANES]
        w1, w2, e1, e2 = slab[:, 0:1], slab[:, 1:2], slab[:, 2:3], slab[:, 3:4]
        first = e1 < e2
        wa = jnp.where(first, w1, w2)
        wb = jnp.where(first, w2, w1)
        xb = x.astype(BF16)

        gate_a, up_a = _dot(xb, ga_ref[0]), _dot(xb, ua_ref[0])
        gate_b, up_b = _dot(xb, gb_ref[0]), _dot(xb, ub_ref[0])
        y = wa * _dot((jax.nn.silu(gate_a) * up_a).astype(BF16), da_ref[0])
        y = y + wb * _dot((jax.nn.silu(gate_b) * up_b).astype(BF16), db_ref[0])
        o_ref[...] = _layer_norm_rows(DEEPNORM_ALPHA * x + y, lg_ref[...], lb_ref[...])

    @pl.when(g >= nu_ref[0])
    def _():
        o_ref[...] = jnp.zeros(o_ref.shape, F32)


def _moe(xs_sorted, plan, layer, w_gate, w_up, w_down, ln_g, ln_b, tm):
    n_tiles = plan["n_tiles"]
    width = xs_sorted.shape[1]
    up_a = pl.BlockSpec((1, 1, D_MODEL, D_EXPERT), lambda g, ta, tb, nu, chg: (layer, ta[g], 0, 0))
    dn_a = pl.BlockSpec((1, 1, D_EXPERT, D_MODEL), lambda g, ta, tb, nu, chg: (layer, ta[g], 0, 0))
    up_b = pl.BlockSpec((1, 1, D_MODEL, D_EXPERT), lambda g, ta, tb, nu, chg: (layer, tb[g], 0, 0))
    dn_b = pl.BlockSpec((1, 1, D_EXPERT, D_MODEL), lambda g, ta, tb, nu, chg: (layer, tb[g], 0, 0))
    vec = pl.BlockSpec((1, D_MODEL), lambda g, ta, tb, nu, chg: (0, 0))
    up_s = pltpu.VMEM((1, D_MODEL, D_EXPERT), BF16)
    dn_s = pltpu.VMEM((1, D_EXPERT, D_MODEL), BF16)
    grid_spec = pltpu.PrefetchScalarGridSpec(
        num_scalar_prefetch=4,
        grid=(n_tiles,),
        in_specs=[pl.BlockSpec((tm, width), lambda g, ta, tb, nu, chg: (g, 0)),
                  up_a, up_a, dn_a, up_b, up_b, dn_b, vec, vec],
        out_specs=pl.BlockSpec((tm, D_MODEL), lambda g, ta, tb, nu, chg: (g, 0)),
        scratch_shapes=[up_s, up_s, dn_s, up_s, up_s, dn_s],
    )
    return pl.pallas_call(
        _moe_kernel,
        grid_spec=grid_spec,
        out_shape=jax.ShapeDtypeStruct((n_tiles * tm, D_MODEL), F32),
        compiler_params=_cparams("arbitrary"),
        name="moe",
    )(plan["ta"], plan["tb"], plan["nu"], plan["chg"], xs_sorted, w_gate, w_up, w_down, w_gate, w_up, w_down,
      ln_g, ln_b)


def _ple_kernel(pos_ref, posn_ref, x2_hbm, pp_ref, ps_ref, wg_ref, wp_ref, op_ref, os_ref, buf, sems, *, n_s):
    i = pl.program_id(0)
    last = pl.num_programs(0) - 1
    tm = buf.shape[1]

    def issue(idx_ref, s):
        for r in range(tm):
            pltpu.make_async_copy(x2_hbm.at[pl.ds(idx_ref[0, 0, r], 1)], buf.at[s, pl.ds(r, 1)],
                                  sems.at[s]).start(priority=r % 2)

    def rows(x, p):
        gate = jax.nn.sigmoid(_dot(x.astype(BF16), wg_ref[...]))
        return x + gate * _dot(p.astype(BF16), wp_ref[...])

    def step(slot):
        if slot == 0:
            @pl.when(i == 0)
            def _():
                issue(pos_ref, 0)

        @pl.when(i < last)
        def _():
            issue(posn_ref, 1 - slot)

        pltpu.make_async_copy(x2_hbm.at[pl.ds(0, tm)], buf.at[slot], sems.at[slot]).wait()

        @pl.when(i < last)
        def _():
            op_ref[...] = rows(buf[slot], pp_ref[0])

        @pl.when(i == last)
        def _():
            os_ref[...] = rows(buf[slot, 0:n_s, :], ps_ref[0])

    for slot in range(2):
        @pl.when(lax.rem(i, 2) == slot)
        def _(slot=slot):
            step(slot)


def _ple(x2_sorted, pos, layer, p_p, p_s, wg_bf, wp_bf):
    n_p, n_s = p_p.shape[1], p_s.shape[1]
    tm = ROW_TILE
    nb = n_p // tm
    steps = nb + 1
    pos_pad = jnp.zeros((steps * tm,), I32).at[:n_p + n_s].set(pos).reshape(steps, 1, tm)
    full = lambda a: pl.BlockSpec(a.shape, lambda i: (0, 0))
    prow = lambda n: pl.BlockSpec((tm, n), lambda i: (jnp.minimum(i, nb - 1), 0))
    return pl.pallas_call(
        functools.partial(_ple_kernel, n_s=n_s),
        grid=(steps,),
        in_specs=[pl.BlockSpec((1, 1, tm), lambda i: (i, 0, 0), memory_space=pltpu.SMEM),
                  pl.BlockSpec((1, 1, tm), lambda i: (jnp.minimum(i + 1, nb), 0, 0), memory_space=pltpu.SMEM),
                  pl.BlockSpec(memory_space=pl.ANY),
                  pl.BlockSpec((1, tm, D_PLE), lambda i: (layer, jnp.minimum(i, nb - 1), 0)),
                  pl.BlockSpec((1, n_s, D_PLE), lambda i: (layer, 0, 0)), full(wg_bf), full(wp_bf)],
        out_specs=[prow(D_MODEL), pl.BlockSpec((n_s, D_MODEL), lambda i: (0, 0))],
        out_shape=[jax.ShapeDtypeStruct((n_p, D_MODEL), F32), jax.ShapeDtypeStruct((n_s, D_MODEL), F32)],
        scratch_shapes=[pltpu.VMEM((2, tm, D_MODEL), F32), pltpu.SemaphoreType.DMA((2,))],
        compiler_params=_cparams("arbitrary"),
        name="ple",
    )(pos_pad, pos_pad, x2_sorted, p_p, p_s, wg_bf, wp_bf)


def _router_weights(w_group, w_router):
    wr = jnp.zeros((ROUTE_ROWS, D_MODEL), F32)
    wr = wr.at[ROUTE_GROUP_ROW:ROUTE_GROUP_ROW + N_GROUPS, :].set(jnp.transpose(w_group))
    wr = wr.at[ROUTE_EXPERT_ROW:ROUTE_EXPERT_ROW + N_EXPERTS, :].set(jnp.transpose(w_router))
    hi = wr.astype(BF16)
    lo = (wr - hi.astype(F32)).astype(BF16)
    return jnp.concatenate([hi, lo], axis=0)


def _layer_tail(i, acts_p, acts_s, w_list, xp, xs, p_p, p_s, ln_mix_g, ln_mix_b, ln_ffn_g, ln_ffn_b,
                w_group, w_router, w_exp_gate, w_exp_up, w_exp_down, w_ple_proj, w_ple_gate):
    rw = _router_weights(w_group[i], w_router[i])
    x1e, rt = _mix_out(acts_p, acts_s, w_list, xp, xs, ln_mix_g[i][None, :], ln_mix_b[i][None, :], rw)
    plan = _route_plan(rt[:, :x1e.shape[0]], MOE_TILE)
    xs_sorted = _dispatch(x1e, plan, MOE_TILE)
    x2_sorted = _moe(xs_sorted, plan, i, w_exp_gate, w_exp_up, w_exp_down, ln_ffn_g[i][None, :],
                     ln_ffn_b[i][None, :], MOE_TILE)
    return _ple(x2_sorted, plan["pos"], i, p_p, p_s, w_ple_gate[i].astype(BF16), w_ple_proj[i].astype(BF16))


def kernel(x_prompt, x_sample, cache_k, cache_v, page_table, state_conv, state_mlstm_C, state_mlstm_n,
           state_mlstm_m, p_prompt, p_sample, w_in_even, conv_w, lambda_q1, lambda_k1, lambda_q2, lambda_k2,
           subln_w, w_out_even, w_in_odd, b_gates_odd, mh_norm_w, w_out_odd, ln_mix_g, ln_mix_b, ln_ffn_g,
           ln_ffn_b, w_group, w_router, w_exp_gate, w_exp_up, w_exp_down, w_ple_proj, w_ple_gate):
    bp, tp, _ = x_prompt.shape
    bs, ts, _ = x_sample.shape
    assert ts == 1 and tp % ROW_TILE == 0 and tp % ATTN_TILE == 0 and tp % MLSTM_CHUNK == 0
    n_p = bp * tp
    past_len = page_table.shape[1] * cache_k.shape[2]
    xp = x_prompt.reshape(n_p, D_MODEL)
    xs = x_sample.reshape(bs, D_MODEL)
    tail_w = (ln_mix_g, ln_mix_b, ln_ffn_g, ln_ffn_b, w_group, w_router, w_exp_gate, w_exp_up, w_exp_down,
              w_ple_proj, w_ple_gate)
    outs_p, outs_s = {}, {}
    for i in range(DEPTH):
        j = i // 2
        p_p = p_prompt.reshape(DEPTH, n_p, D_PLE)
        p_s = p_sample.reshape(DEPTH, bs, D_PLE)
        if i % 2 == 0:
            lam_init = 0.8 - 0.6 * math.exp(-0.3 * i)
            lam_vecs = jnp.stack([lambda_q1[j], lambda_k1[j], lambda_q2[j], lambda_k2[j]])
            sub = subln_w[j][None, :]
            w_bf = w_in_even[j].astype(BF16)
            tabs_p = _rope_tables(jnp.arange(tp))
            q0 = 3 * D_CONV
            wqt_bf = jnp.transpose(w_in_even[j][:, q0:q0 + QK_B]).astype(BF16)
            yc, qt, kf, vf, kb, vt, cst = _even_in_prompt(
                x_prompt if i == 0 else xp.reshape(bp, tp, D_MODEL), w_bf, wqt_bf, conv_w[j],
                jnp.zeros((bp, CONV_W - 1, D_CONV), F32), tabs_p)
            o_p = _attn_prompt(qt, kb, vt, lam_vecs, sub, lam_init)
            outs_p.setdefault("k", []).append(kf.reshape(bp, tp, 2 * H_B, DH_B))
            outs_p.setdefault("v", []).append(vf.reshape(bp, tp, H_B, 2 * DH_B))
            outs_p.setdefault("c", []).append(cst)
            tabs_s = _rope_tables(jnp.full((1,), past_len, I32))
            prev_t = jnp.swapaxes(state_conv[j], 0, 1)
            yc_s, q_s, kf_s, vf_s, u_s = _even_in_decode(xs, w_bf, conv_w[j], prev_t, tabs_s)
            sub_head = jnp.arange(2 * H_B)
            sub_head = jnp.where(sub_head < H_B, 2 * sub_head, 2 * (sub_head - H_B) + 1)
            lane_head = jnp.arange(QK_B) // DH_B
            qbd = jnp.where(lane_head[None, None, :] == sub_head[None, :, None], q_s[:, None, :],
                            jnp.zeros((), BF16))
            n_pool = cache_k.shape[1]
            pages = cache_k.shape[0] * n_pool
            k_view = jnp.transpose(cache_k, (0, 1, 3, 4, 2)).reshape(pages, QK_B, PAGE_SIZE)
            v_view = cache_v.reshape(pages, PAGE_SIZE * H_B, 2 * DH_B)
            o8 = _attn_decode(qbd, kf_s[:, None, :], vf_s[:, None, :], k_view, v_view,
                              page_table + j * n_pool, lam_vecs, sub, lam_init)
            o_s = o8[:, :H_B, :].reshape(bs, V_B).astype(BF16)
            outs_s.setdefault("k", []).append(kf_s.reshape(bs, ts, 2 * H_B, DH_B))
            outs_s.setdefault("v", []).append(vf_s.reshape(bs, ts, H_B, 2 * DH_B))
            outs_s.setdefault("c", []).append(jnp.stack([state_conv[j][:, 1, :], u_s], axis=1))
            w_out = w_out_even[j].astype(BF16)
            w_list = [w_out[:D_CONV], w_out[D_CONV:]]
            acts_p = [yc.reshape(n_p, D_CONV), o_p.reshape(n_p, V_B)]
            acts_s = [yc_s, o_s]
        else:
            w_in = w_in_odd[j]
            qw, vw = H_C * DK_C, H_C * DV_C
            w_bf = w_in[:, :2 * qw + 2 * vw].astype(BF16)
            wg = jnp.zeros((D_MODEL, LANES), F32).at[:, :2 * H_C].set(w_in[:, 2 * qw + 2 * vw:]).astype(BF16)
            wgt = jnp.transpose(wg[:, :2 * H_C])
            bg = jnp.zeros((1, LANES), F32).at[0, :2 * H_C].set(b_gates_odd[j])
            bgt = jnp.broadcast_to(b_gates_odd[j][:, None], (2 * H_C, LANES))
            nw = mh_norm_w[j][None, :]
            q, k, v, o, gc, gr, kt = _odd_in(xp.reshape(bp, tp, D_MODEL), w_bf, wg, wgt, bg, bgt, decode=False)
            h_p, c_p, n_pp, m_p = _mlstm(q, k, kt, v, o, gc, gr, nw,
                                         jnp.zeros((bp, H_C, DK_C, DV_C), F32), jnp.zeros((bp, H_C, DK_C), F32),
                                         jnp.zeros((bp, H_C, LANES), F32), MLSTM_CHUNK)
            outs_p.setdefault("C", []).append(c_p)
            outs_p.setdefault("n", []).append(n_pp)
            outs_p.setdefault("m", []).append(m_p[:, :, 0])
            q_s, k_s, v_s, o_s2, gc_s, gr_s, kt_s = _odd_in(xs, w_bf, wg, wgt, bg, bgt, decode=True)
            ch = MLSTM_DECODE_CHUNK
            pad_rows = lambda a: jnp.zeros((bs, ch, a.shape[1]), a.dtype).at[:, 0, :].set(a)
            lane = jnp.arange(LANES)
            inert_c = jnp.where(lane < H_C, -jnp.inf, 0.0).astype(F32)
            gc_pad = jnp.broadcast_to(inert_c[None, None, :], (bs, ch, LANES)).at[:, 0, :].set(gc_s)
            inert_r = jnp.where(jnp.arange(2 * H_C) < H_C, -jnp.inf, 0.0).astype(F32)
            gr_pad = jnp.broadcast_to(inert_r[None, :, None], (bs, 2 * H_C, ch)).at[:, :, 0].set(gr_s.T)
            m0 = jnp.broadcast_to(state_mlstm_m[j][:, :, None], (bs, H_C, LANES))
            kt_pad = jnp.zeros((bs, qw, ch), BF16).at[:, :, 0].set(kt_s.T)
            h_s, c_s, n_s, m_s = _mlstm(pad_rows(q_s), pad_rows(k_s), kt_pad, pad_rows(v_s), pad_rows(o_s2),
                                        gc_pad, gr_pad, nw, state_mlstm_C[j], state_mlstm_n[j], m0, ch)
            outs_s.setdefault("C", []).append(c_s)
            outs_s.setdefault("n", []).append(n_s)
            outs_s.setdefault("m", []).append(m_s[:, :, 0])
            w_list = [w_out_odd[j].astype(BF16)]
            acts_p = [h_p.reshape(n_p, vw)]
            acts_s = [h_s[:, 0, :]]
        xp, xs = _layer_tail(i, acts_p, acts_s, w_list, xp, xs, p_p, p_s, *tail_w)
    st = lambda lst: jnp.stack(lst)
    return (xp.reshape(bp, tp, D_MODEL), xs.reshape(bs, ts, D_MODEL),
            st(outs_p["k"]), st(outs_p["v"]), st(outs_p["c"]), st(outs_p["C"]), st(outs_p["n"]), st(outs_p["m"]),
            st(outs_s["k"]), st(outs_s["v"]), st(outs_s["c"]), st(outs_s["C"]), st(outs_s["n"]), st(outs_s["m"]))
```

```python
import functools
import math

import numpy as np
import jax
import jax.numpy as jnp
from jax import lax
from jax.experimental import pallas as pl
from jax.experimental.pallas import tpu as pltpu

F32 = jnp.float32
BF16 = jnp.bfloat16
I32 = jnp.int32

D_MODEL = 1024
DEPTH = 2
PAGE_SIZE = 128
D_CONV = D_MODEL // 2
CONV_W = 3
H_B = 4
DH_B = 64
ROT_DIM = DH_B // 4
ROPE_THETA = 500000.0
H_C = 4
DK_C = (D_MODEL // 2) // H_C
DV_C = D_MODEL // H_C
N_GROUPS = 4
EXP_PER_GROUP = 4
N_EXPERTS = N_GROUPS * EXP_PER_GROUP
D_EXPERT = 512
D_PLE = 256
LN_EPS = 1e-5
LOG2_E = 1.4426950408889634
DEEPNORM_ALPHA = (2 * DEPTH) ** 0.25
QK_B = 2 * H_B * DH_B
V_B = H_B * 2 * DH_B
N_PAIRS = EXP_PER_GROUP * (EXP_PER_GROUP - 1) // 2
N_CLASSES = N_GROUPS * N_PAIRS
PAIR_SEQ = ((0, 1), (0, 2), (1, 2), (1, 3), (0, 3), (2, 3))
assert EXP_PER_GROUP == 4 and len(PAIR_SEQ) == N_PAIRS

LANES = 128
VMEM_LIMIT = 56 * 1024 * 1024
ROW_TILE = 512
ATTN_TILE = 512
ATTN_HEADS_PER_STEP = 4
ONES_ROWS = 16
MLSTM_CHUNK = 128
MLSTM_DECODE_CHUNK = 16
MLSTM_BATCH_PER_STEP = 1
MOE_TILE = 256
PAGES_PER_STEP = 32
ROUTE_ROWS = 32
ROUTE_GROUP_ROW = 0
ROUTE_EXPERT_ROW = 8


def _cparams(*sem):
    return pltpu.CompilerParams(dimension_semantics=sem, vmem_limit_bytes=VMEM_LIMIT)


def _dot(a, b):
    return jnp.dot(a, b, preferred_element_type=F32)


def _dot_nt(a, b):
    return lax.dot_general(a, b, (((1,), (1,)), ((), ())), preferred_element_type=F32)


def _dot_tn(a, b):
    return lax.dot_general(a, b, (((0,), (0,)), ((), ())), preferred_element_type=F32)


def _layer_norm_rows(z, g, b):
    mu = jnp.mean(z, axis=-1, keepdims=True)
    zc = z - mu
    var = jnp.mean(zc * zc, axis=-1, keepdims=True)
    return zc * lax.rsqrt(var + LN_EPS) * g + b


def _log_sigmoid(x):
    return jnp.minimum(x, 0.0) - jnp.log1p(jnp.exp(-jnp.abs(x)))


def _gated_conv(gate_b, u, um1, um2, cw_ref):
    cw = cw_ref[...]
    conv = um2 * cw[0:1, :] + um1 * cw[1:2, :] + u * cw[2:3, :]
    return (gate_b * conv).astype(BF16)


def _rope(z, cos, sin_lo, sin_hi, axis):
    half = ROT_DIM // 2
    return z * cos + pltpu.roll(z, QK_B - half, axis) * sin_lo + pltpu.roll(z, half, axis) * sin_hi


def _even_in_prompt_kernel(x_ref, w_ref, wqt_ref, cw_ref, prev_ref, rc_ref, rs1_ref, rs2_ref,
                           rct_ref, rs1t_ref, rs2t_ref,
                           yc_ref, qt_ref, kf_ref, vf_ref, kb_ref, vt_ref, u_ref, carry_ref):
    xb = x_ref[0].astype(BF16)
    tm = xb.shape[0]

    def proj(c0, n):
        return _dot(xb, w_ref[:, c0:c0 + n])

    gate_b = proj(0, D_CONV)
    u = proj(D_CONV, D_CONV) * proj(2 * D_CONV, D_CONV)
    j = pl.program_id(1)

    @pl.when(j == 0)
    def _():
        carry_ref[...] = prev_ref[0]

    row = lax.broadcasted_iota(I32, u.shape, 0)
    c2 = carry_ref[0:1, :]
    c1 = carry_ref[1:2, :]
    um1 = jnp.where(row == 0, c1, pltpu.roll(u, 1, 0))
    um2 = jnp.where(row == 0, c2, jnp.where(row == 1, c1, pltpu.roll(u, 2, 0)))
    carry_ref[...] = u[tm - 2:tm, :]
    u_ref[0] = u[tm - 2:tm, :]
    yc_ref[0] = _gated_conv(gate_b, u, um1, um2, cw_ref)

    reps = QK_B // LANES
    tile = lambda r, ax: jnp.concatenate([r[...]] * reps, axis=ax)
    k = _rope(proj(3 * D_CONV + QK_B, QK_B), tile(rc_ref, 1), tile(rs1_ref, 1), tile(rs2_ref, 1), 1)
    kf_ref[0] = k
    kb_ref[0] = k.astype(BF16)
    v = proj(3 * D_CONV + 2 * QK_B, V_B)
    for h in range(H_B):
        vf_ref[0, pl.ds(h, tm, stride=H_B), :] = v[:, h * 2 * DH_B:(h + 1) * 2 * DH_B]
    qt = _rope(_dot_nt(wqt_ref[...], xb), tile(rct_ref, 0), tile(rs1t_ref, 0), tile(rs2t_ref, 0), 0)
    qt_ref[0] = (qt * (DH_B ** -0.5 * LOG2_E)).astype(BF16)
    vt_ref[0] = jnp.transpose(v).astype(BF16)


def _even_in_decode_kernel(x_ref, w_ref, cw_ref, prev_ref, rc_ref, rs1_ref, rs2_ref,
                           yc_ref, q_ref, kf_ref, vf_ref, u_ref):
    xb = x_ref[...].astype(BF16)

    def proj(c0, n):
        return _dot(xb, w_ref[:, c0:c0 + n])

    gate_b = proj(0, D_CONV)
    u = proj(D_CONV, D_CONV) * proj(2 * D_CONV, D_CONV)
    u_ref[...] = u
    yc_ref[...] = _gated_conv(gate_b, u, prev_ref[1], prev_ref[0], cw_ref)
    reps = QK_B // LANES
    tile = lambda r: jnp.concatenate([r[...]] * reps, axis=1)
    cos, sin_lo, sin_hi = tile(rc_ref), tile(rs1_ref), tile(rs2_ref)
    q_ref[...] = (_rope(proj(3 * D_CONV, QK_B), cos, sin_lo, sin_hi, 1) * (DH_B ** -0.5)).astype(BF16)
    kf_ref[...] = _rope(proj(3 * D_CONV + QK_B, QK_B), cos, sin_lo, sin_hi, 1)
    vf_ref[...] = proj(3 * D_CONV + 2 * QK_B, V_B)


def _rope_tables(pos):
    half = ROT_DIM // 2
    inv = ROPE_THETA ** (-jnp.arange(half, dtype=F32) / half)
    ang = pos.astype(F32)[:, None] * inv[None, :]
    cos, sin = jnp.cos(ang), jnp.sin(ang)
    t = pos.shape[0]
    ones = jnp.ones((t, DH_B - ROT_DIM), F32)
    zeros = jnp.zeros((t, DH_B - ROT_DIM), F32)
    zh = jnp.zeros((t, half), F32)
    c = jnp.concatenate([cos, cos, ones], axis=1)
    s_lo = jnp.concatenate([-sin, zh, zeros], axis=1)
    s_hi = jnp.concatenate([zh, sin, zeros], axis=1)
    tile2 = lambda a: jnp.concatenate([a, a], axis=1)
    return tile2(c), tile2(s_lo), tile2(s_hi)


def _even_in_prompt(x, w_bf, wqt_bf, conv_w, conv_prev, tables):
    b, t, _ = x.shape
    tm = ROW_TILE
    row3 = lambda n: pl.BlockSpec((1, tm, n), lambda i, j: (i, j, 0))
    col3 = lambda n: pl.BlockSpec((1, n, tm), lambda i, j: (i, 0, j))
    full2 = lambda a: pl.BlockSpec(a.shape, lambda i, j: (0, 0))
    tab = pl.BlockSpec((tm, LANES), lambda i, j: (j, 0))
    tab_t = pl.BlockSpec((LANES, tm), lambda i, j: (0, j))
    st = pl.BlockSpec((1, CONV_W - 1, D_CONV), lambda i, j: (i, 0, 0))
    tables_t = [jnp.transpose(a) for a in tables]
    outs = [jax.ShapeDtypeStruct((b, t, D_CONV), BF16), jax.ShapeDtypeStruct((b, QK_B, t), BF16),
            jax.ShapeDtypeStruct((b, t, QK_B), F32), jax.ShapeDtypeStruct((b, t * H_B, 2 * DH_B), F32),
            jax.ShapeDtypeStruct((b, t, QK_B), BF16), jax.ShapeDtypeStruct((b, V_B, t), BF16),
            jax.ShapeDtypeStruct((b, CONV_W - 1, D_CONV), F32)]
    return pl.pallas_call(
        _even_in_prompt_kernel,
        grid=(b, t // tm),
        in_specs=[row3(D_MODEL), full2(w_bf), full2(wqt_bf), full2(conv_w), st,
                  tab, tab, tab, tab_t, tab_t, tab_t],
        out_specs=[row3(D_CONV), col3(QK_B), row3(QK_B),
                   pl.BlockSpec((1, tm * H_B, 2 * DH_B), lambda i, j: (i, j, 0)), row3(QK_B), col3(V_B), st],
        out_shape=outs,
        scratch_shapes=[pltpu.VMEM((CONV_W - 1, D_CONV), F32)],
        compiler_params=_cparams("arbitrary", "arbitrary"),
        name="even_in_prompt",
    )(x, w_bf, wqt_bf, conv_w, conv_prev, *tables, *tables_t)


def _even_in_decode(x, w_bf, conv_w, conv_prev_t, tables):
    n = x.shape[0]
    full = lambda a: pl.BlockSpec(a.shape, lambda i: (0,) * a.ndim)
    o2 = lambda c, dt: jax.ShapeDtypeStruct((n, c), dt)
    outs = [o2(D_CONV, BF16), o2(QK_B, BF16), o2(QK_B, F32), o2(V_B, F32), o2(D_CONV, F32)]
    ins = [x, w_bf, conv_w, conv_prev_t, *tables]
    return pl.pallas_call(
        _even_in_decode_kernel,
        grid=(1,),
        in_specs=[full(a) for a in ins],
        out_specs=[pl.BlockSpec(o.shape, lambda i: (0, 0)) for o in outs],
        out_shape=outs,
        compiler_params=_cparams("arbitrary"),
        name="even_in_decode",
    )(*ins)


def _lambda_value(lam_ref, lam_init):
    lv = lam_ref[...]
    a = jnp.sum(lv[0:1, :] * lv[1:2, :], axis=1, keepdims=True)
    b = jnp.sum(lv[2:3, :] * lv[3:4, :], axis=1, keepdims=True)
    return jnp.exp(a) - jnp.exp(b) + lam_init


def _sub_norm(o, sub_ref, lam_init):
    ms = jnp.mean(o * o, axis=-1, keepdims=True)
    return o * lax.rsqrt(ms + LN_EPS) * sub_ref[...] * (1.0 - lam_init)


def _attn_prompt_kernel(qt_ref, k_ref, vt_ref, lam_ref, sub_ref, o_ref, *scratch, lam_init):
    i = pl.program_id(2)
    tq = qt_ref.shape[2]
    tk = tq
    n_heads = qt_ref.shape[1] // LANES
    m_scrs, acc_scrs = scratch[:n_heads], scratch[n_heads:]
    for m_scr, acc_scr in zip(m_scrs, acc_scrs):
        m_scr[...] = jnp.full(m_scr.shape, -jnp.inf, F32)
        acc_scr[...] = jnp.zeros(acc_scr.shape, F32)
    ones = jnp.ones((ONES_ROWS, tk), BF16)
    qqs = []
    for h in range(n_heads):
        qt = qt_ref[0, h * LANES:(h + 1) * LANES, :].astype(F32)
        feat = lax.broadcasted_iota(I32, qt.shape, 0)
        qqs.append(jnp.concatenate([jnp.where(feat < DH_B, qt, 0.0), jnp.where(feat >= DH_B, qt, 0.0)],
                                   axis=1).astype(BF16))

    def step(j, masked):
        start = pl.multiple_of(j * tk, tk)
        scores = [_dot(k_ref[0, pl.ds(start, tk), h * LANES:(h + 1) * LANES], qqs[h])
                  for h in range(n_heads)]
        probs, alphas = [], []
        for h in range(n_heads):
            s = scores[h]
            if masked:
                key = lax.broadcasted_iota(I32, s.shape, 0)
                qry = lax.broadcasted_iota(I32, s.shape, 1)
                qry = jnp.where(qry >= tq, qry - tq, qry)
                s = jnp.where(key <= qry, s, -jnp.inf)
            m_prev = m_scrs[h][...]
            m_new = jnp.maximum(m_prev, jnp.max(s, axis=0, keepdims=True))
            alphas.append(jnp.exp2(m_prev - m_new))
            probs.append(jnp.exp2(s - m_new).astype(BF16))
            m_scrs[h][...] = m_new
        for h in range(n_heads):
            vtj = jnp.concatenate([vt_ref[0, h * LANES:(h + 1) * LANES, pl.ds(start, tk)], ones], axis=0)
            acc_scrs[h][...] = alphas[h] * acc_scrs[h][...] + _dot(vtj, probs[h])

    def body(j, carry):
        step(j, False)
        return carry

    lax.fori_loop(0, i, body, 0)
    step(i, True)
    lam = _lambda_value(lam_ref, lam_init)
    for h in range(n_heads):
        on = acc_scrs[h][0:LANES, :] / acc_scrs[h][LANES:LANES + 1, :]
        o = jnp.transpose(on[:, 0:tq] - lam * on[:, tq:2 * tq])
        o_ref[0, :, h * LANES:(h + 1) * LANES] = _sub_norm(o, sub_ref, lam_init).astype(BF16)


def _attn_prompt(qt, k, vt, lam_vecs, subln, lam_init):
    b, t, _ = k.shape
    tq = ATTN_TILE
    nh = ATTN_HEADS_PER_STEP
    hw = nh * LANES
    full = lambda a: pl.BlockSpec(a.shape, lambda bi, h, i: (0, 0))
    return pl.pallas_call(
        functools.partial(_attn_prompt_kernel, lam_init=lam_init),
        grid=(b, H_B // nh, t // tq),
        in_specs=[pl.BlockSpec((1, hw, tq), lambda bi, h, i: (bi, h, i)),
                  pl.BlockSpec((1, t, hw), lambda bi, h, i: (bi, 0, h)),
                  pl.BlockSpec((1, hw, t), lambda bi, h, i: (bi, h, 0)),
                  full(lam_vecs), full(subln)],
        out_specs=pl.BlockSpec((1, tq, hw), lambda bi, h, i: (bi, i, h)),
        out_shape=jax.ShapeDtypeStruct((b, t, V_B), BF16),
        scratch_shapes=[pltpu.VMEM((1, 2 * tq), F32)] * nh + [pltpu.VMEM((LANES + ONES_ROWS, 2 * tq), F32)] * nh,
        compiler_params=_cparams("arbitrary", "arbitrary", "arbitrary"),
        name="attn_prompt",
    )(qt, k, vt, lam_vecs, subln)


def _attn_decode_kernel(pt_ref, qbd_ref, kn_ref, vn_ref, lam_ref, sub_ref, *rest, lam_init, n_pages):
    del pt_ref
    k_refs = rest[:n_pages]
    v_refs = rest[n_pages:2 * n_pages]
    o_ref, m_scr, l_scr, acc_scr = rest[2 * n_pages:]
    j = pl.program_id(1)
    qbd = qbd_ref[0]

    @pl.when(j == 0)
    def _():
        s_new = jnp.sum(qbd.astype(F32) * kn_ref[0], axis=1, keepdims=True)
        m_scr[...] = s_new
        l_scr[...] = jnp.ones(l_scr.shape, F32)
        acc_scr[...] = jnp.broadcast_to(vn_ref[0], acc_scr.shape)

    s = jnp.concatenate([_dot(qbd, k_refs[r][0].astype(BF16)) for r in range(n_pages)], axis=1)
    m_prev = m_scr[...]
    m_new = jnp.maximum(m_prev, jnp.max(s, axis=1, keepdims=True))
    alpha = jnp.exp(m_prev - m_new)
    p = jnp.exp(s - m_new)
    l_scr[...] = alpha * l_scr[...] + jnp.sum(p, axis=1, keepdims=True)

    def head_pv(h):
        acc = None
        for r in range(n_pages):
            vh = v_refs[r][0, pl.ds(h, PAGE_SIZE, stride=H_B), :].astype(BF16)
            term = _dot(p[:, r * PAGE_SIZE:(r + 1) * PAGE_SIZE].astype(BF16), vh)
            acc = term if acc is None else acc + term
        return acc

    pv = jnp.concatenate([head_pv(h) for h in range(H_B)], axis=1)
    acc_scr[...] = alpha * acc_scr[...] + pv
    m_scr[...] = m_new

    @pl.when(j == pl.num_programs(1) - 1)
    def _():
        on = acc_scr[...] / l_scr[...]
        row = lax.broadcasted_iota(I32, (2 * H_B, 2 * DH_B), 0)
        head = jnp.where(row >= H_B, row - H_B, row)
        o8 = jnp.zeros((2 * H_B, 2 * DH_B), F32)
        for c in range(H_B):
            o8 = o8 + jnp.where(head == c, on[:, c * 2 * DH_B:(c + 1) * 2 * DH_B], 0.0)
        lam = _lambda_value(lam_ref, lam_init)
        o = o8 - lam * pltpu.roll(o8, H_B, 0)
        o_ref[0] = _sub_norm(o, sub_ref, lam_init)


def _attn_decode(qbd, k_new, v_new, cache_k, cache_v, page_table, lam_vecs, subln, lam_init):
    n = qbd.shape[0]
    n_pages = page_table.shape[1]
    pp = PAGES_PER_STEP
    width = V_B
    c2 = lambda a: pl.BlockSpec(a.shape, lambda b, j, pt: (0, 0))
    per_b = lambda a: pl.BlockSpec((1,) + a.shape[1:], lambda b, j, pt: (b, 0, 0))

    def page(r, arr):
        return pl.BlockSpec((1,) + arr.shape[1:], lambda b, j, pt: (pt[b, j * pp + r], 0, 0))

    grid_spec = pltpu.PrefetchScalarGridSpec(
        num_scalar_prefetch=1,
        grid=(n, n_pages // pp),
        in_specs=[per_b(qbd), per_b(k_new), per_b(v_new), c2(lam_vecs), c2(subln)]
        + [page(r, cache_k) for r in range(pp)] + [page(r, cache_v) for r in range(pp)],
        out_specs=pl.BlockSpec((1, 2 * H_B, 2 * DH_B), lambda b, j, pt: (b, 0, 0)),
        scratch_shapes=[pltpu.VMEM((2 * H_B, 1), F32), pltpu.VMEM((2 * H_B, 1), F32),
                        pltpu.VMEM((2 * H_B, width), F32)],
    )
    return pl.pallas_call(
        functools.partial(_attn_decode_kernel, lam_init=lam_init, n_pages=pp),
        grid_spec=grid_spec,
        out_shape=jax.ShapeDtypeStruct((n, 2 * H_B, 2 * DH_B), F32),
        compiler_params=_cparams("arbitrary", "arbitrary"),
        name="attn_decode",
    )(page_table, qbd, k_new, v_new, lam_vecs, subln, *([cache_k] * pp), *([cache_v] * pp))


def _odd_in_kernel(x_ref, w_ref, wg_ref, wgt_ref, bg_ref, bgt_ref,
                   q_ref, k_ref, v_ref, o_ref, gc_ref, gr_ref, kt_ref, *, decode):
    xb = x_ref[0].astype(BF16) if not decode else x_ref[...].astype(BF16)
    qw = H_C * DK_C
    vw = H_C * DV_C
    q = _dot(xb, w_ref[:, 0:qw]) * (DK_C ** -0.5)
    k = _dot(xb, w_ref[:, qw:2 * qw])
    v = _dot(xb, w_ref[:, 2 * qw:2 * qw + vw])
    o = _dot(xb, w_ref[:, 2 * qw + vw:2 * qw + 2 * vw])
    g_col = _dot(xb, wg_ref[...]) + bg_ref[...]
    lane = lax.broadcasted_iota(I32, g_col.shape, 1)
    g_col = jnp.where(lane < H_C, g_col, _log_sigmoid(g_col))
    k_t = jnp.transpose(k).astype(BF16)
    g_row = _dot_nt(wgt_ref[...], xb) + bgt_ref[:, 0:1]
    sub = lax.broadcasted_iota(I32, g_row.shape, 0)
    g_row = jnp.where(sub < H_C, g_row, _log_sigmoid(g_row))
    if decode:
        kt_ref[...] = k_t
        q_ref[...] = q.astype(BF16)
        k_ref[...] = k.astype(BF16)
        v_ref[...] = v.astype(BF16)
        o_ref[...] = o
        gc_ref[...] = g_col
        gr_ref[...] = g_row
    else:
        q_ref[0] = q.astype(BF16)
        k_ref[0] = k.astype(BF16)
        v_ref[0] = v.astype(BF16)
        o_ref[0] = o
        gc_ref[0] = g_col
        gr_ref[0] = g_row
        kt_ref[0] = k_t


def _odd_in(x, w_bf, wg, wgt, bg, bgt, decode):
    qw, vw = H_C * DK_C, H_C * DV_C
    if decode:
        n = x.shape[0]
        ins = [x, w_bf, wg, wgt, bg, bgt]
        outs = [jax.ShapeDtypeStruct((n, qw), BF16), jax.ShapeDtypeStruct((n, qw), BF16),
                jax.ShapeDtypeStruct((n, vw), BF16), jax.ShapeDtypeStruct((n, vw), F32),
                jax.ShapeDtypeStruct((n, LANES), F32), jax.ShapeDtypeStruct((2 * H_C, n), F32),
                jax.ShapeDtypeStruct((qw, n), BF16)]
        return pl.pallas_call(
            functools.partial(_odd_in_kernel, decode=True),
            grid=(1,),
            in_specs=[pl.BlockSpec(a.shape, lambda i: (0, 0)) for a in ins],
            out_specs=[pl.BlockSpec(o.shape, lambda i: (0, 0)) for o in outs],
            out_shape=outs,
            compiler_params=_cparams("arbitrary"),
            name="odd_in_decode",
        )(*ins)
    b, t, _ = x.shape
    tm = ROW_TILE
    row3 = lambda n: pl.BlockSpec((1, tm, n), lambda i, j: (i, j, 0))
    full2 = lambda a: pl.BlockSpec(a.shape, lambda i, j: (0, 0))
    outs = [jax.ShapeDtypeStruct((b, t, qw), BF16), jax.ShapeDtypeStruct((b, t, qw), BF16),
            jax.ShapeDtypeStruct((b, t, vw), BF16), jax.ShapeDtypeStruct((b, t, vw), F32),
            jax.ShapeDtypeStruct((b, t, LANES), F32), jax.ShapeDtypeStruct((b, 2 * H_C, t), F32),
            jax.ShapeDtypeStruct((b, qw, t), BF16)]
    return pl.pallas_call(
        functools.partial(_odd_in_kernel, decode=False),
        grid=(b, t // tm),
        in_specs=[row3(D_MODEL), full2(w_bf), full2(wg), full2(wgt), full2(bg), full2(bgt)],
        out_specs=[row3(qw), row3(qw), row3(vw), row3(vw), row3(LANES),
                   pl.BlockSpec((1, 2 * H_C, tm), lambda i, j: (i, 0, j)),
                   pl.BlockSpec((1, qw, tm), lambda i, j: (i, 0, j))],
        out_shape=outs,
        compiler_params=_cparams("arbitrary", "arbitrary"),
        name="odd_in_prompt",
    )(x, w_bf, wg, wgt, bg, bgt)


def _mlstm_kernel(q_ref, k_ref, kt_ref, v_ref, o_ref, gc_ref, gr_ref, nw_ref, c0_ref, n0_ref, m0_ref,
                  h_ref, c_out, n_out, m_out, c_scr, n_scr, m_scr):
    ci = pl.program_id(1)
    chunk = q_ref.shape[1]

    @pl.when(ci == 0)
    def _():
        c_scr[...] = c0_ref[...]
        n_scr[...] = n0_ref[...]
        m_scr[...] = m0_ref[...]

    t_idx = lax.broadcasted_iota(I32, (chunk, chunk), 0)
    s_idx = lax.broadcasted_iota(I32, (chunk, chunk), 1)
    causal = s_idx <= t_idx
    for bb, h in [(bb, h) for bb in range(q_ref.shape[0]) for h in range(H_C)]:
        q = q_ref[bb, :, h * DK_C:(h + 1) * DK_C]
        k = k_ref[bb, :, h * DK_C:(h + 1) * DK_C]
        v = v_ref[bb, :, h * DV_C:(h + 1) * DV_C]
        ig_r = gr_ref[bb, h:h + 1, :]
        lf_r = gr_ref[bb, H_C + h:H_C + h + 1, :]
        ig_c = gc_ref[bb, :, h:h + 1]
        lf_c = gc_ref[bb, :, H_C + h:H_C + h + 1]
        bcum_c = jnp.sum(jnp.where(causal, lf_r, 0.0), axis=1, keepdims=True)
        bcum_r = jnp.sum(jnp.where(t_idx <= s_idx, lf_c, 0.0), axis=0, keepdims=True)
        m0 = m_scr[bb, h:h + 1, 0:1]
        dmat = jnp.where(causal, bcum_c - bcum_r + ig_r, -jnp.inf)
        inter = bcum_c + m0
        m = jnp.maximum(inter, jnp.max(dmat, axis=1, keepdims=True))
        w = jnp.exp(dmat - m)
        g = jnp.exp(inter - m)
        s = _dot_nt(q, k) * w
        c0 = c_scr[bb, h]
        n0 = n_scr[bb, h:h + 1, :]
        num = g * _dot(q, c0.astype(BF16)) + _dot(s.astype(BF16), v)
        den = g * jnp.sum(q.astype(F32) * n0, axis=1, keepdims=True) + jnp.sum(s, axis=1, keepdims=True)
        hid = num / jnp.maximum(jnp.abs(den), jnp.exp(-m))
        m_last = m[chunk - 1:chunk, :]
        b_last = bcum_c[chunk - 1:chunk, :]
        w_last = jnp.exp(b_last - bcum_c + ig_c - m_last)
        g_last = jnp.exp(b_last + m0 - m_last)
        kw = k.astype(F32) * w_last
        w_last_r = jnp.exp(b_last - bcum_r + ig_r - m_last)
        kw_t = (kt_ref[bb, h * DK_C:(h + 1) * DK_C, :].astype(F32) * w_last_r).astype(BF16)
        c_scr[bb, h] = g_last * c0 + _dot(kw_t, v)
        n_scr[bb, h:h + 1, :] = g_last * n0 + jnp.sum(kw, axis=0, keepdims=True)
        m_scr[bb, h:h + 1, :] = jnp.broadcast_to(m_last, (1, LANES))
        mu = jnp.mean(hid, axis=1, keepdims=True)
        hc = hid - mu
        var = jnp.mean(hc * hc, axis=1, keepdims=True)
        hn = hc * lax.rsqrt(var + LN_EPS) * nw_ref[:, h * DV_C:(h + 1) * DV_C]
        gate = jax.nn.sigmoid(o_ref[bb, :, h * DV_C:(h + 1) * DV_C])
        h_ref[bb, :, h * DV_C:(h + 1) * DV_C] = (gate * hn).astype(BF16)

    c_out[...] = c_scr[...]
    n_out[...] = n_scr[...]
    m_out[...] = m_scr[...]


def _mlstm(q, k, k_t, v, o, g_col, g_row, norm_w, c0, n0, m0, ch):
    b, t, _ = q.shape
    nb = MLSTM_BATCH_PER_STEP
    qw, vw = H_C * DK_C, H_C * DV_C
    row = lambda n: pl.BlockSpec((nb, ch, n), lambda i, j: (i, j, 0))
    st4 = pl.BlockSpec((nb, H_C, DK_C, DV_C), lambda i, j: (i, 0, 0, 0))
    st3 = pl.BlockSpec((nb, H_C, LANES), lambda i, j: (i, 0, 0))
    outs = [jax.ShapeDtypeStruct((b, t, vw), BF16), jax.ShapeDtypeStruct((b, H_C, DK_C, DV_C), F32),
            jax.ShapeDtypeStruct((b, H_C, DK_C), F32), jax.ShapeDtypeStruct((b, H_C, LANES), F32)]
    return pl.pallas_call(
        _mlstm_kernel,
        grid=(b // nb, t // ch),
        in_specs=[row(qw), row(qw), pl.BlockSpec((nb, qw, ch), lambda i, j: (i, 0, j)), row(vw), row(vw),
                  row(LANES), pl.BlockSpec((nb, 2 * H_C, ch), lambda i, j: (i, 0, j)),
                  pl.BlockSpec(norm_w.shape, lambda i, j: (0, 0)), st4, st3, st3],
        out_specs=[row(vw), st4, st3, st3],
        out_shape=outs,
        scratch_shapes=[pltpu.VMEM((nb, H_C, DK_C, DV_C), F32), pltpu.VMEM((nb, H_C, DK_C), F32),
                        pltpu.VMEM((nb, H_C, LANES), F32)],
        compiler_params=_cparams("arbitrary", "arbitrary"),
        name="mlstm",
    )(q, k, k_t, v, o, g_col, g_row, norm_w, c0, n0, m0)


def _mix_project(acts, w_refs):
    y = _dot(acts[0], w_refs[0][...])
    for a, w in zip(acts[1:], w_refs[1:]):
        y = y + _dot(a, w[...])
    return y


def _mixed_rows(acts, w_refs, x, g_ref, b_ref):
    return _layer_norm_rows(DEEPNORM_ALPHA * x + _mix_project(acts, w_refs), g_ref[...], b_ref[...])


def _route_logits(x1, rw_ref):
    xh = x1.astype(BF16)
    xl = (x1 - xh.astype(F32)).astype(BF16)
    both = _dot_nt(rw_ref[...], xh)
    return both[0:ROUTE_ROWS, :] + both[ROUTE_ROWS:2 * ROUTE_ROWS, :] + _dot_nt(rw_ref[0:ROUTE_ROWS, :], xl)


def _route(x1, rw_ref):
    return _route_decide(_route_logits(x1, rw_ref))


def _route_decide(lg):
    sub = lax.broadcasted_iota(I32, lg.shape, 0)
    big = jnp.int32(4 * ROUTE_ROWS)
    neg = -jnp.inf
    gl = jnp.where(sub < ROUTE_GROUP_ROW + N_GROUPS, lg, neg)
    g_max = jnp.max(gl, axis=0, keepdims=True)
    g_w = 1.0 / jnp.sum(jnp.exp(gl - g_max), axis=0, keepdims=True)
    g_idx = jnp.min(jnp.where(gl == g_max, sub, big), axis=0, keepdims=True)
    row_group = (sub - ROUTE_EXPERT_ROW) >> 2
    el = jnp.where(row_group == g_idx, lg, neg)
    e1 = jnp.max(el, axis=0, keepdims=True)
    i1 = jnp.min(jnp.where(el == e1, sub, big), axis=0, keepdims=True)
    z = jnp.sum(jnp.exp(el - e1), axis=0, keepdims=True)
    el2 = jnp.where(sub == i1, neg, el)
    e2 = jnp.max(el2, axis=0, keepdims=True)
    i2 = jnp.min(jnp.where(el2 == e2, sub, big), axis=0, keepdims=True)
    p1 = 1.0 / z
    p2 = jnp.exp(e2 - e1) / z
    w1 = p1 / (p1 + p2) * g_w
    w2 = p2 / (p1 + p2) * g_w
    id1 = (i1 - ROUTE_EXPERT_ROW).astype(F32)
    id2 = (i2 - ROUTE_EXPERT_ROW).astype(F32)
    return jnp.where(sub == 0, w1, jnp.where(sub == 1, w2, jnp.where(sub == 2, id1, jnp.where(sub == 3, id2, 0.0))))


def _slab_columns(rt):
    pad = jnp.zeros((LANES - rt.shape[0], rt.shape[1]), F32)
    return jnp.transpose(jnp.concatenate([rt, pad], axis=0))


def _mix_out_kernel(*refs, n_in, n_s):
    ap_refs = refs[:n_in]
    as_refs = refs[n_in:2 * n_in]
    w_refs = refs[2 * n_in:3 * n_in]
    xp_ref, xs_ref, g_ref, b_ref, rw_ref, out_ref, rt_ref = refs[3 * n_in:]
    i = pl.program_id(0)
    last = pl.num_programs(0) - 1

    @pl.when(i < last)
    def _():
        x1 = _mixed_rows([a[...] for a in ap_refs], w_refs, xp_ref[...], g_ref, b_ref)
        rt = _route(x1, rw_ref)
        out_ref[:, 0:D_MODEL] = x1
        out_ref[:, D_MODEL:D_MODEL + LANES] = _slab_columns(rt)
        rt_ref[...] = rt[0:8, :]

    @pl.when(i == last)
    def _():
        x1 = _mixed_rows([a[...] for a in as_refs], w_refs, xs_ref[...], g_ref, b_ref)
        rt = _route(jnp.concatenate([x1, jnp.zeros((LANES - n_s, D_MODEL), F32)], axis=0), rw_ref)
        out_ref[0:n_s, 0:D_MODEL] = x1
        out_ref[0:n_s, D_MODEL:D_MODEL + LANES] = _slab_columns(rt)[0:n_s, :]
        rt_ref[...] = jnp.zeros(rt_ref.shape, F32)
        rt_ref[:, 0:LANES] = rt[0:8, :]


def _mix_out(acts_p, acts_s, weights, xp, xs, ln_g, ln_b, rw):
    n_p, n_s = xp.shape[0], xs.shape[0]
    assert n_s <= LANES
    tm = ROW_TILE
    nb = n_p // tm
    n_in = len(acts_p)
    prow = lambda n: pl.BlockSpec((tm, n), lambda i: (jnp.minimum(i, nb - 1), 0))
    full = lambda a: pl.BlockSpec(a.shape, lambda i: (0, 0))
    width = D_MODEL + LANES
    return pl.pallas_call(
        functools.partial(_mix_out_kernel, n_in=n_in, n_s=n_s),
        grid=(nb + 1,),
        in_specs=[prow(a.shape[1]) for a in acts_p] + [full(a) for a in acts_s] + [full(w) for w in weights]
        + [prow(D_MODEL), full(xs), full(ln_g), full(ln_b), full(rw)],
        out_specs=[pl.BlockSpec((tm, width), lambda i: (i, 0)), pl.BlockSpec((8, tm), lambda i: (0, i))],
        out_shape=[jax.ShapeDtypeStruct((n_p + n_s, width), F32),
                   jax.ShapeDtypeStruct((8, (nb + 1) * tm), F32)],
        compiler_params=_cparams("arbitrary"),
        name="mix_out",
    )(*acts_p, *acts_s, *weights, xp, xs, ln_g, ln_b, rw)


def _route_plan(rt, tm):
    n = rt.shape[1]
    e1, e2 = rt[2, :].astype(I32), rt[3, :].astype(I32)
    ea, eb = jnp.minimum(e1, e2), jnp.maximum(e1, e2)
    la, lb = ea % EXP_PER_GROUP, eb % EXP_PER_GROUP
    pair_rank = sum(k * ((la == a) & (lb == b)).astype(I32) for k, (a, b) in enumerate(PAIR_SEQ))
    cls = (ea // EXP_PER_GROUP) * N_PAIRS + pair_rank
    onehot = (cls[:, None] == jnp.arange(N_CLASSES, dtype=I32)[None, :]).astype(I32)
    csum = jnp.cumsum(onehot, axis=0)
    rank = jnp.sum(onehot * csum, axis=1) - 1
    cnt = csum[-1]
    ntile = (cnt + tm - 1) // tm
    tile_end = jnp.cumsum(ntile)
    tile_start = tile_end - ntile
    n_used = tile_end[-1]
    n_tiles = -(-(n + N_CLASSES * (tm - 1)) // tm)
    pos = (jnp.sum(onehot * tile_start[None, :], axis=1) * tm + rank).astype(I32)
    pair_lo = np.array([a for a, _ in PAIR_SEQ], np.int32)
    pair_hi = np.array([b for _, b in PAIR_SEQ], np.int32)
    cls_ids = np.arange(N_CLASSES)
    cls_a = jnp.asarray((cls_ids // N_PAIRS) * EXP_PER_GROUP + pair_lo[cls_ids % N_PAIRS], I32)
    cls_b = jnp.asarray((cls_ids // N_PAIRS) * EXP_PER_GROUP + pair_hi[cls_ids % N_PAIRS], I32)
    tile_ids = jnp.arange(n_tiles, dtype=I32)
    tile_cls = jnp.sum((tile_end[None, :] <= jnp.minimum(tile_ids, n_used - 1)[:, None]).astype(I32), axis=1)
    tile_cls = jnp.minimum(tile_cls, N_CLASSES - 1)
    onehot_t = (tile_cls[:, None] == jnp.arange(N_CLASSES, dtype=I32)[None, :]).astype(I32)
    last_tile = jnp.where(ntile > 0, tile_end - 1, -1).astype(I32)
    ta = jnp.sum(onehot_t * cls_a[None, :], axis=1)
    tb = jnp.sum(onehot_t * cls_b[None, :], axis=1)
    first = jnp.ones((1,), I32)
    chg = (jnp.concatenate([first, (ta[1:] != ta[:-1]).astype(I32)])
           + 2 * jnp.concatenate([first, (tb[1:] != tb[:-1]).astype(I32)]))
    return dict(pos=pos, ta=ta, tb=tb, nu=n_used.reshape(1).astype(I32), chg=chg, last_tile=last_tile,
                n_tiles=n_tiles)


def _dispatch_kernel(lt_ref, nu_ref, pos_ref, src_ref, dst_hbm, zbuf, zsem, rsem, *, moe_tile, n_tiles):
    i = pl.program_id(0)
    td = pos_ref.shape[2]

    def zero_copy(tile):
        start = pl.multiple_of(tile * moe_tile, moe_tile)
        return pltpu.make_async_copy(zbuf, dst_hbm.at[pl.ds(start, moe_tile)], zsem)

    @pl.when(i == 0)
    def _():
        zbuf[...] = jnp.zeros(zbuf.shape, F32)
        for c in range(N_CLASSES):
            @pl.when(lt_ref[c] >= 0)
            def _(c=c):
                zero_copy(lt_ref[c]).start()
        for c in range(N_CLASSES):
            @pl.when(lt_ref[c] >= 0)
            def _(c=c):
                zero_copy(lt_ref[c]).wait()

        def spare_start(t, carry):
            zero_copy(t).start()
            return carry

        def spare_wait(t, carry):
            zero_copy(t).wait()
            return carry

        lax.fori_loop(nu_ref[0], n_tiles, spare_start, 0)
        lax.fori_loop(nu_ref[0], n_tiles, spare_wait, 0)

    for r in range(td):
        pltpu.make_async_copy(src_ref.at[pl.ds(r, 1)], dst_hbm.at[pl.ds(pos_ref[0, 0, r], 1)],
                              rsem).start(priority=r % 2)
    pltpu.make_async_copy(src_ref, dst_hbm.at[pl.ds(0, td)], rsem).wait()


def _largest_divisor_tile(n, cap):
    for t in range(cap - cap % 8, 7, -8):
        if n % t == 0:
            return t
    raise ValueError(f"no row tile for {n} rows")


def _dispatch(x1e, plan, tm):
    n, width = x1e.shape
    n_tiles = plan["n_tiles"]
    td = _largest_divisor_tile(n, 1024)
    steps = n // td
    grid_spec = pltpu.PrefetchScalarGridSpec(
        num_scalar_prefetch=2,
        grid=(steps,),
        in_specs=[pl.BlockSpec((1, 1, td), lambda i, lt, nu: (i, 0, 0), memory_space=pltpu.SMEM),
                  pl.BlockSpec((td, width), lambda i, lt, nu: (i, 0))],
        out_specs=pl.BlockSpec(memory_space=pl.ANY),
        scratch_shapes=[pltpu.VMEM((tm, width), F32), pltpu.SemaphoreType.DMA(()),
                        pltpu.SemaphoreType.DMA(())],
    )
    return pl.pallas_call(
        functools.partial(_dispatch_kernel, moe_tile=tm, n_tiles=n_tiles),
        grid_spec=grid_spec,
        out_shape=jax.ShapeDtypeStruct((n_tiles * tm, width), F32),
        compiler_params=_cparams("arbitrary"),
        name="dispatch",
    )(plan["last_tile"], plan["nu"], plan["pos"].reshape(steps, 1, td), x1e)


def _moe_kernel(ta_ref, tb_ref, nu_ref, chg_ref, x_ref, ga_f32, ua_f32, da_f32, gb_f32, ub_f32, db_f32,
                lg_ref, lb_ref, o_ref, ga_ref, ua_ref, da_ref, gb_ref, ub_ref, db_ref):
    del ta_ref, tb_ref
    g = pl.program_id(0)

    @pl.when((chg_ref[g] & 1) == 1)
    def _():
        for src, dst in ((ga_f32, ga_ref), (ua_f32, ua_ref), (da_f32, da_ref)):
            dst[0] = src[0, 0].astype(BF16)

    @pl.when((chg_ref[g] & 2) == 2)
    def _():
        for src, dst in ((gb_f32, gb_ref), (ub_f32, ub_ref), (db_f32, db_ref)):
            dst[0] = src[0, 0].astype(BF16)

    @pl.when(g < nu_ref[0])
    def _():
        x = x_ref[:, 0:D_MODEL]
        slab = x_ref[:, D_MODEL:D_MODEL + LANES]
        w1, w2, e1, e2 = slab[:, 0:1], slab[:, 1:2], slab[:, 2:3], slab[:, 3:4]
        first = e1 < e2
        wa = jnp.where(first, w1, w2)
        wb = jnp.where(first, w2, w1)
        xb = x.astype(BF16)

        gate_a, up_a = _dot(xb, ga_ref[0]), _dot(xb, ua_ref[0])
        gate_b, up_b = _dot(xb, gb_ref[0]), _dot(xb, ub_ref[0])
        y = wa * _dot((jax.nn.silu(gate_a) * up_a).astype(BF16), da_ref[0])
        y = y + wb * _dot((jax.nn.silu(gate_b) * up_b).astype(BF16), db_ref[0])
        o_ref[...] = _layer_norm_rows(DEEPNORM_ALPHA * x + y, lg_ref[...], lb_ref[...])

    @pl.when(g >= nu_ref[0])
    def _():
        o_ref[...] = jnp.zeros(o_ref.shape, F32)


def _moe(xs_sorted, plan, layer, w_gate, w_up, w_down, ln_g, ln_b, tm):
    n_tiles = plan["n_tiles"]
    width = xs_sorted.shape[1]
    up_a = pl.BlockSpec((1, 1, D_MODEL, D_EXPERT), lambda g, ta, tb, nu, chg: (layer, ta[g], 0, 0))
    dn_a = pl.BlockSpec((1, 1, D_EXPERT, D_MODEL), lambda g, ta, tb, nu, chg: (layer, ta[g], 0, 0))
    up_b = pl.BlockSpec((1, 1, D_MODEL, D_EXPERT), lambda g, ta, tb, nu, chg: (layer, tb[g], 0, 0))
    dn_b = pl.BlockSpec((1, 1, D_EXPERT, D_MODEL), lambda g, ta, tb, nu, chg: (layer, tb[g], 0, 0))
    vec = pl.BlockSpec((1, D_MODEL), lambda g, ta, tb, nu, chg: (0, 0))
    up_s = pltpu.VMEM((1, D_MODEL, D_EXPERT), BF16)
    dn_s = pltpu.VMEM((1, D_EXPERT, D_MODEL), BF16)
    grid_spec = pltpu.PrefetchScalarGridSpec(
        num_scalar_prefetch=4,
        grid=(n_tiles,),
        in_specs=[pl.BlockSpec((tm, width), lambda g, ta, tb, nu, chg: (g, 0)),
                  up_a, up_a, dn_a, up_b, up_b, dn_b, vec, vec],
        out_specs=pl.BlockSpec((tm, D_MODEL), lambda g, ta, tb, nu, chg: (g, 0)),
        scratch_shapes=[up_s, up_s, dn_s, up_s, up_s, dn_s],
    )
    return pl.pallas_call(
        _moe_kernel,
        grid_spec=grid_spec,
        out_shape=jax.ShapeDtypeStruct((n_tiles * tm, D_MODEL), F32),
        compiler_params=_cparams("arbitrary"),
        name="moe",
    )(plan["ta"], plan["tb"], plan["nu"], plan["chg"], xs_sorted, w_gate, w_up, w_down, w_gate, w_up, w_down,
      ln_g, ln_b)


def _ple_kernel(pos_ref, posn_ref, x2_hbm, pp_ref, ps_ref, wg_ref, wp_ref, op_ref, os_ref, buf, sems, *, n_s):
    i = pl.program_id(0)
    last = pl.num_programs(0) - 1
    tm = buf.shape[1]

    def issue(idx_ref, s):
        for r in range(tm):
            pltpu.make_async_copy(x2_hbm.at[pl.ds(idx_ref[0, 0, r], 1)], buf.at[s, pl.ds(r, 1)],
                                  sems.at[s]).start(priority=r % 2)

    def rows(x, p):
        gate = jax.nn.sigmoid(_dot(x.astype(BF16), wg_ref[...]))
        return x + gate * _dot(p.astype(BF16), wp_ref[...])

    def step(slot):
        if slot == 0:
            @pl.when(i == 0)
            def _():
                issue(pos_ref, 0)

        @pl.when(i < last)
        def _():
            issue(posn_ref, 1 - slot)

        pltpu.make_async_copy(x2_hbm.at[pl.ds(0, tm)], buf.at[slot], sems.at[slot]).wait()

        @pl.when(i < last)
        def _():
            op_ref[...] = rows(buf[slot], pp_ref[0])

        @pl.when(i == last)
        def _():
            os_ref[...] = rows(buf[slot, 0:n_s, :], ps_ref[0])

    for slot in range(2):
        @pl.when(lax.rem(i, 2) == slot)
        def _(slot=slot):
            step(slot)


def _ple(x2_sorted, pos, layer, p_p, p_s, wg_bf, wp_bf):
    n_p, n_s = p_p.shape[1], p_s.shape[1]
    tm = ROW_TILE
    nb = n_p // tm
    steps = nb + 1
    pos_pad = jnp.zeros((steps * tm,), I32).at[:n_p + n_s].set(pos).reshape(steps, 1, tm)
    full = lambda a: pl.BlockSpec(a.shape, lambda i: (0, 0))
    prow = lambda n: pl.BlockSpec((tm, n), lambda i: (jnp.minimum(i, nb - 1), 0))
    return pl.pallas_call(
        functools.partial(_ple_kernel, n_s=n_s),
        grid=(steps,),
        in_specs=[pl.BlockSpec((1, 1, tm), lambda i: (i, 0, 0), memory_space=pltpu.SMEM),
                  pl.BlockSpec((1, 1, tm), lambda i: (jnp.minimum(i + 1, nb), 0, 0), memory_space=pltpu.SMEM),
                  pl.BlockSpec(memory_space=pl.ANY),
                  pl.BlockSpec((1, tm, D_PLE), lambda i: (layer, jnp.minimum(i, nb - 1), 0)),
                  pl.BlockSpec((1, n_s, D_PLE), lambda i: (layer, 0, 0)), full(wg_bf), full(wp_bf)],
        out_specs=[prow(D_MODEL), pl.BlockSpec((n_s, D_MODEL), lambda i: (0, 0))],
        out_shape=[jax.ShapeDtypeStruct((n_p, D_MODEL), F32), jax.ShapeDtypeStruct((n_s, D_MODEL), F32)],
        scratch_shapes=[pltpu.VMEM((2, tm, D_MODEL), F32), pltpu.SemaphoreType.DMA((2,))],
        compiler_params=_cparams("arbitrary"),
        name="ple",
    )(pos_pad, pos_pad, x2_sorted, p_p, p_s, wg_bf, wp_bf)


def _router_weights(w_group, w_router):
    wr = jnp.zeros((ROUTE_ROWS, D_MODEL), F32)
    wr = wr.at[ROUTE_GROUP_ROW:ROUTE_GROUP_ROW + N_GROUPS, :].set(jnp.transpose(w_group))
    wr = wr.at[ROUTE_EXPERT_ROW:ROUTE_EXPERT_ROW + N_EXPERTS, :].set(jnp.transpose(w_router))
    hi = wr.astype(BF16)
    lo = (wr - hi.astype(F32)).astype(BF16)
    return jnp.concatenate([hi, lo], axis=0)


def _layer_tail(i, acts_p, acts_s, w_list, xp, xs, p_p, p_s, ln_mix_g, ln_mix_b, ln_ffn_g, ln_ffn_b,
                w_group, w_router, w_exp_gate, w_exp_up, w_exp_down, w_ple_proj, w_ple_gate):
    rw = _router_weights(w_group[i], w_router[i])
    x1e, rt = _mix_out(acts_p, acts_s, w_list, xp, xs, ln_mix_g[i][None, :], ln_mix_b[i][None, :], rw)
    plan = _route_plan(rt[:, :x1e.shape[0]], MOE_TILE)
    xs_sorted = _dispatch(x1e, plan, MOE_TILE)
    x2_sorted = _moe(xs_sorted, plan, i, w_exp_gate, w_exp_up, w_exp_down, ln_ffn_g[i][None, :],
                     ln_ffn_b[i][None, :], MOE_TILE)
    return _ple(x2_sorted, plan["pos"], i, p_p, p_s, w_ple_gate[i].astype(BF16), w_ple_proj[i].astype(BF16))


def kernel(x_prompt, x_sample, cache_k, cache_v, page_table, state_conv, state_mlstm_C, state_mlstm_n,
           state_mlstm_m, p_prompt, p_sample, w_in_even, conv_w, lambda_q1, lambda_k1, lambda_q2, lambda_k2,
           subln_w, w_out_even, w_in_odd, b_gates_odd, mh_norm_w, w_out_odd, ln_mix_g, ln_mix_b, ln_ffn_g,
           ln_ffn_b, w_group, w_router, w_exp_gate, w_exp_up, w_exp_down, w_ple_proj, w_ple_gate):
    bp, tp, _ = x_prompt.shape
    bs, ts, _ = x_sample.shape
    assert ts == 1 and tp % ROW_TILE == 0 and tp % ATTN_TILE == 0 and tp % MLSTM_CHUNK == 0
    n_p = bp * tp
    past_len = page_table.shape[1] * cache_k.shape[2]
    xp = x_prompt.reshape(n_p, D_MODEL)
    xs = x_sample.reshape(bs, D_MODEL)
    tail_w = (ln_mix_g, ln_mix_b, ln_ffn_g, ln_ffn_b, w_group, w_router, w_exp_gate, w_exp_up, w_exp_down,
              w_ple_proj, w_ple_gate)
    outs_p, outs_s = {}, {}
    for i in range(DEPTH):
        j = i // 2
        p_p = p_prompt.reshape(DEPTH, n_p, D_PLE)
        p_s = p_sample.reshape(DEPTH, bs, D_PLE)
        if i % 2 == 0:
            lam_init = 0.8 - 0.6 * math.exp(-0.3 * i)
            lam_vecs = jnp.stack([lambda_q1[j], lambda_k1[j], lambda_q2[j], lambda_k2[j]])
            sub = subln_w[j][None, :]
            w_bf = w_in_even[j].astype(BF16)
            tabs_p = _rope_tables(jnp.arange(tp))
            q0 = 3 * D_CONV
            wqt_bf = jnp.transpose(w_in_even[j][:, q0:q0 + QK_B]).astype(BF16)
            yc, qt, kf, vf, kb, vt, cst = _even_in_prompt(
                x_prompt if i == 0 else xp.reshape(bp, tp, D_MODEL), w_bf, wqt_bf, conv_w[j],
                jnp.zeros((bp, CONV_W - 1, D_CONV), F32), tabs_p)
            o_p = _attn_prompt(qt, kb, vt, lam_vecs, sub, lam_init)
            outs_p.setdefault("k", []).append(kf.reshape(bp, tp, 2 * H_B, DH_B))
            outs_p.setdefault("v", []).append(vf.reshape(bp, tp, H_B, 2 * DH_B))
            outs_p.setdefault("c", []).append(cst)
            tabs_s = _rope_tables(jnp.full((1,), past_len, I32))
            prev_t = jnp.swapaxes(state_conv[j], 0, 1)
            yc_s, q_s, kf_s, vf_s, u_s = _even_in_decode(xs, w_bf, conv_w[j], prev_t, tabs_s)
            sub_head = jnp.arange(2 * H_B)
            sub_head = jnp.where(sub_head < H_B, 2 * sub_head, 2 * (sub_head - H_B) + 1)
            lane_head = jnp.arange(QK_B) // DH_B
            qbd = jnp.where(lane_head[None, None, :] == sub_head[None, :, None], q_s[:, None, :],
                            jnp.zeros((), BF16))
            n_pool = cache_k.shape[1]
            pages = cache_k.shape[0] * n_pool
            k_view = jnp.transpose(cache_k, (0, 1, 3, 4, 2)).reshape(pages, QK_B, PAGE_SIZE)
            v_view = cache_v.reshape(pages, PAGE_SIZE * H_B, 2 * DH_B)
            o8 = _attn_decode(qbd, kf_s[:, None, :], vf_s[:, None, :], k_view, v_view,
                              page_table + j * n_pool, lam_vecs, sub, lam_init)
            o_s = o8[:, :H_B, :].reshape(bs, V_B).astype(BF16)
            outs_s.setdefault("k", []).append(kf_s.reshape(bs, ts, 2 * H_B, DH_B))
            outs_s.setdefault("v", []).append(vf_s.reshape(bs, ts, H_B, 2 * DH_B))
            outs_s.setdefault("c", []).append(jnp.stack([state_conv[j][:, 1, :], u_s], axis=1))
            w_out = w_out_even[j].astype(BF16)
            w_list = [w_out[:D_CONV], w_out[D_CONV:]]
            acts_p = [yc.reshape(n_p, D_CONV), o_p.reshape(n_p, V_B)]
            acts_s = [yc_s, o_s]
        else:
            w_in = w_in_odd[j]
            qw, vw = H_C * DK_C, H_C * DV_C
            w_bf = w_in[:, :2 * qw + 2 * vw].astype(BF16)
            wg = jnp.zeros((D_MODEL, LANES), F32).at[:, :2 * H_C].set(w_in[:, 2 * qw + 2 * vw:]).astype(BF16)
            wgt = jnp.transpose(wg[:, :2 * H_C])
            bg = jnp.zeros((1, LANES), F32).at[0, :2 * H_C].set(b_gates_odd[j])
            bgt = jnp.broadcast_to(b_gates_odd[j][:, None], (2 * H_C, LANES))
            nw = mh_norm_w[j][None, :]
            q, k, v, o, gc, gr, kt = _odd_in(xp.reshape(bp, tp, D_MODEL), w_bf, wg, wgt, bg, bgt, decode=False)
            h_p, c_p, n_pp, m_p = _mlstm(q, k, kt, v, o, gc, gr, nw,
                                         jnp.zeros((bp, H_C, DK_C, DV_C), F32), jnp.zeros((bp, H_C, DK_C), F32),
                                         jnp.zeros((bp, H_C, LANES), F32), MLSTM_CHUNK)
            outs_p.setdefault("C", []).append(c_p)
            outs_p.setdefault("n", []).append(n_pp)
            outs_p.setdefault("m", []).append(m_p[:, :, 0])
            q_s, k_s, v_s, o_s2, gc_s, gr_s, kt_s = _odd_in(xs, w_bf, wg, wgt, bg, bgt, decode=True)
            ch = MLSTM_DECODE_CHUNK
            pad_rows = lambda a: jnp.zeros((bs, ch, a.shape[1]), a.dtype).at[:, 0, :].set(a)
            lane = jnp.arange(LANES)
            inert_c = jnp.where(lane < H_C, -jnp.inf, 0.0).astype(F32)
            gc_pad = jnp.broadcast_to(inert_c[None, None, :], (bs, ch, LANES)).at[:, 0, :].set(gc_s)
            inert_r = jnp.where(jnp.arange(2 * H_C) < H_C, -jnp.inf, 0.0).astype(F32)
            gr_pad = jnp.broadcast_to(inert_r[None, :, None], (bs, 2 * H_C, ch)).at[:, :, 0].set(gr_s.T)
            m0 = jnp.broadcast_to(state_mlstm_m[j][:, :, None], (bs, H_C, LANES))
            kt_pad = jnp.zeros((bs, qw, ch), BF16).at[:, :, 0].set(kt_s.T)
            h_s, c_s, n_s, m_s = _mlstm(pad_rows(q_s), pad_rows(k_s), kt_pad, pad_rows(v_s), pad_rows(o_s2),
                                        gc_pad, gr_pad, nw, state_mlstm_C[j], state_mlstm_n[j], m0, ch)
            outs_s.setdefault("C", []).append(c_s)
            outs_s.setdefault("n", []).append(n_s)
            outs_s.setdefault("m", []).append(m_s[:, :, 0])
            w_list = [w_out_odd[j].astype(BF16)]
            acts_p = [h_p.reshape(n_p, vw)]
            acts_s = [h_s[:, 0, :]]
        xp, xs = _layer_tail(i, acts_p, acts_s, w_list, xp, xs, p_p, p_s, *tail_w)
    st = lambda lst: jnp.stack(lst)
    return (xp.reshape(bp, tp, D_MODEL), xs.reshape(bs, ts, D_MODEL),
            st(outs_p["k"]), st(outs_p["v"]), st(outs_p["c"]), st(outs_p["C"]), st(outs_p["n"]), st(outs_p["m"]),
            st(outs_s["k"]), st(outs_s["v"]), st(outs_s["c"]), st(outs_s["C"]), st(outs_s["n"]), st(outs_s["m"]))
```

```python
import functools
import math

import numpy as np
import jax
import jax.numpy as jnp
from jax import lax
from jax.experimental import pallas as pl
from jax.experimental.pallas import tpu as pltpu

F32 = jnp.float32
BF16 = jnp.bfloat16
I32 = jnp.int32

D_MODEL = 1024
DEPTH = 2
PAGE_SIZE = 128
D_CONV = D_MODEL // 2
CONV_W = 3
H_B = 4
DH_B = 64
ROT_DIM = DH_B // 4
ROPE_THETA = 500000.0
H_C = 4
DK_C = (D_MODEL // 2) // H_C
DV_C = D_MODEL // H_C
N_GROUPS = 4
EXP_PER_GROUP = 4
N_EXPERTS = N_GROUPS * EXP_PER_GROUP
D_EXPERT = 512
D_PLE = 256
LN_EPS = 1e-5
LOG2_E = 1.4426950408889634
DEEPNORM_ALPHA = (2 * DEPTH) ** 0.25
QK_B = 2 * H_B * DH_B
V_B = H_B * 2 * DH_B
N_PAIRS = EXP_PER_GROUP * (EXP_PER_GROUP - 1) // 2
N_CLASSES = N_GROUPS * N_PAIRS
PAIR_SEQ = ((0, 1), (0, 2), (1, 2), (1, 3), (0, 3), (2, 3))
assert EXP_PER_GROUP == 4 and len(PAIR_SEQ) == N_PAIRS

LANES = 128
VMEM_LIMIT = 56 * 1024 * 1024
ROW_TILE = 1024
GATHER_TILE = 512
ATTN_TILE = 512
ATTN_HEADS_PER_STEP = 4
ONES_ROWS = 16
MLSTM_CHUNK = 128
MLSTM_DECODE_CHUNK = 16
MLSTM_BATCH_PER_STEP = 1
DISPATCH_TILE_CAP = 1024
MOE_TILE = 256
PAGES_PER_STEP = 32
ROUTE_ROWS = 32
ROUTE_GROUP_ROW = 0
ROUTE_EXPERT_ROW = 8


def _cparams(*sem):
    return pltpu.CompilerParams(dimension_semantics=sem, vmem_limit_bytes=VMEM_LIMIT)


def _dot(a, b):
    return jnp.dot(a, b, preferred_element_type=F32)


def _dot_nt(a, b):
    return lax.dot_general(a, b, (((1,), (1,)), ((), ())), preferred_element_type=F32)


def _layer_norm_rows(z, g, b):
    mu = jnp.mean(z, axis=-1, keepdims=True)
    zc = z - mu
    var = jnp.mean(zc * zc, axis=-1, keepdims=True)
    return zc * lax.rsqrt(var + LN_EPS) * g + b


def _log_sigmoid(x):
    return jnp.minimum(x, 0.0) - jnp.log1p(jnp.exp(-jnp.abs(x)))


def _gated_conv(gate_b, u, um1, um2, cw_ref):
    cw = cw_ref[...]
    conv = um2 * cw[0:1, :] + um1 * cw[1:2, :] + u * cw[2:3, :]
    return (gate_b * conv).astype(BF16)


def _rope(z, cos, sin_lo, sin_hi, axis):
    half = ROT_DIM // 2
    return z * cos + pltpu.roll(z, QK_B - half, axis) * sin_lo + pltpu.roll(z, half, axis) * sin_hi


def _even_in_prompt_kernel(x_ref, w_ref, wqt_ref, cw_ref, prev_ref, rc_ref, rs1_ref, rs2_ref,
                           rct_ref, rs1t_ref, rs2t_ref,
                           yc_ref, qt_ref, kf_ref, vf_ref, kb_ref, vt_ref, u_ref, carry_ref):
    xb = x_ref[0].astype(BF16)
    tm = xb.shape[0]

    def proj(c0, n):
        return _dot(xb, w_ref[:, c0:c0 + n])

    gate_b = proj(0, D_CONV)
    u = proj(D_CONV, D_CONV) * proj(2 * D_CONV, D_CONV)
    j = pl.program_id(1)

    @pl.when(j == 0)
    def _():
        carry_ref[...] = prev_ref[0]

    row = lax.broadcasted_iota(I32, u.shape, 0)
    c2 = carry_ref[0:1, :]
    c1 = carry_ref[1:2, :]
    um1 = jnp.where(row == 0, c1, pltpu.roll(u, 1, 0))
    um2 = jnp.where(row == 0, c2, jnp.where(row == 1, c1, pltpu.roll(u, 2, 0)))
    carry_ref[...] = u[tm - 2:tm, :]
    u_ref[0] = u[tm - 2:tm, :]
    yc_ref[0] = _gated_conv(gate_b, u, um1, um2, cw_ref)

    reps = QK_B // LANES
    tile = lambda r, ax: jnp.concatenate([r[...]] * reps, axis=ax)
    k = _rope(proj(3 * D_CONV + QK_B, QK_B), tile(rc_ref, 1), tile(rs1_ref, 1), tile(rs2_ref, 1), 1)
    kf_ref[0] = k
    kb_ref[0] = k.astype(BF16)
    v = proj(3 * D_CONV + 2 * QK_B, V_B)
    for h in range(H_B):
        vf_ref[0, pl.ds(h, tm, stride=H_B), :] = v[:, h * 2 * DH_B:(h + 1) * 2 * DH_B]
    qt = _rope(_dot_nt(wqt_ref[...], xb), tile(rct_ref, 0), tile(rs1t_ref, 0), tile(rs2t_ref, 0), 0)
    qt_ref[0] = (qt * (DH_B ** -0.5 * LOG2_E)).astype(BF16)
    vt_ref[0] = jnp.transpose(v).astype(BF16)


def _even_in_decode_kernel(x_ref, w_ref, cw_ref, prev_ref, rc_ref, rs1_ref, rs2_ref,
                           yc_ref, q_ref, kf_ref, vf_ref, u_ref):
    xb = x_ref[...].astype(BF16)

    def proj(c0, n):
        return _dot(xb, w_ref[:, c0:c0 + n])

    gate_b = proj(0, D_CONV)
    u = proj(D_CONV, D_CONV) * proj(2 * D_CONV, D_CONV)
    u_ref[...] = u
    yc_ref[...] = _gated_conv(gate_b, u, prev_ref[1], prev_ref[0], cw_ref)
    reps = QK_B // LANES
    tile = lambda r: jnp.concatenate([r[...]] * reps, axis=1)
    cos, sin_lo, sin_hi = tile(rc_ref), tile(rs1_ref), tile(rs2_ref)
    q_ref[...] = (_rope(proj(3 * D_CONV, QK_B), cos, sin_lo, sin_hi, 1) * (DH_B ** -0.5)).astype(BF16)
    kf_ref[...] = _rope(proj(3 * D_CONV + QK_B, QK_B), cos, sin_lo, sin_hi, 1)
    vf_ref[...] = proj(3 * D_CONV + 2 * QK_B, V_B)


def _rope_tables(pos):
    half = ROT_DIM // 2
    inv = ROPE_THETA ** (-jnp.arange(half, dtype=F32) / half)
    ang = pos.astype(F32)[:, None] * inv[None, :]
    cos, sin = jnp.cos(ang), jnp.sin(ang)
    t = pos.shape[0]
    ones = jnp.ones((t, DH_B - ROT_DIM), F32)
    zeros = jnp.zeros((t, DH_B - ROT_DIM), F32)
    zh = jnp.zeros((t, half), F32)
    c = jnp.concatenate([cos, cos, ones], axis=1)
    s_lo = jnp.concatenate([-sin, zh, zeros], axis=1)
    s_hi = jnp.concatenate([zh, sin, zeros], axis=1)
    tile2 = lambda a: jnp.concatenate([a, a], axis=1)
    return tile2(c), tile2(s_lo), tile2(s_hi)


def _even_in_prompt(x, w_bf, wqt_bf, conv_w, conv_prev, tables):
    b, t, _ = x.shape
    tm = ROW_TILE
    row3 = lambda n: pl.BlockSpec((1, tm, n), lambda i, j: (i, j, 0))
    col3 = lambda n: pl.BlockSpec((1, n, tm), lambda i, j: (i, 0, j))
    full2 = lambda a: pl.BlockSpec(a.shape, lambda i, j: (0, 0))
    tab = pl.BlockSpec((tm, LANES), lambda i, j: (j, 0))
    tab_t = pl.BlockSpec((LANES, tm), lambda i, j: (0, j))
    st = pl.BlockSpec((1, CONV_W - 1, D_CONV), lambda i, j: (i, 0, 0))
    tables_t = [jnp.transpose(a) for a in tables]
    outs = [jax.ShapeDtypeStruct((b, t, D_CONV), BF16), jax.ShapeDtypeStruct((b, QK_B, t), BF16),
            jax.ShapeDtypeStruct((b, t, QK_B), F32), jax.ShapeDtypeStruct((b, t * H_B, 2 * DH_B), F32),
            jax.ShapeDtypeStruct((b, t, QK_B), BF16), jax.ShapeDtypeStruct((b, V_B, t), BF16),
            jax.ShapeDtypeStruct((b, CONV_W - 1, D_CONV), F32)]
    return pl.pallas_call(
        _even_in_prompt_kernel,
        grid=(b, t // tm),
        in_specs=[row3(D_MODEL), full2(w_bf), full2(wqt_bf), full2(conv_w), st,
                  tab, tab, tab, tab_t, tab_t, tab_t],
        out_specs=[row3(D_CONV), col3(QK_B), row3(QK_B),
                   pl.BlockSpec((1, tm * H_B, 2 * DH_B), lambda i, j: (i, j, 0)), row3(QK_B), col3(V_B), st],
        out_shape=outs,
        scratch_shapes=[pltpu.VMEM((CONV_W - 1, D_CONV), F32)],
        compiler_params=_cparams("arbitrary", "arbitrary"),
        name="even_in_prompt",
    )(x, w_bf, wqt_bf, conv_w, conv_prev, *tables, *tables_t)


def _even_in_decode(x, w_bf, conv_w, conv_prev_t, tables):
    n = x.shape[0]
    full = lambda a: pl.BlockSpec(a.shape, lambda i: (0,) * a.ndim)
    o2 = lambda c, dt: jax.ShapeDtypeStruct((n, c), dt)
    outs = [o2(D_CONV, BF16), o2(QK_B, BF16), o2(QK_B, F32), o2(V_B, F32), o2(D_CONV, F32)]
    ins = [x, w_bf, conv_w, conv_prev_t, *tables]
    return pl.pallas_call(
        _even_in_decode_kernel,
        grid=(1,),
        in_specs=[full(a) for a in ins],
        out_specs=[pl.BlockSpec(o.shape, lambda i: (0, 0)) for o in outs],
        out_shape=outs,
        compiler_params=_cparams("arbitrary"),
        name="even_in_decode",
    )(*ins)


def _lambda_value(lam_ref, lam_init):
    lv = lam_ref[...]
    a = jnp.sum(lv[0:1, :] * lv[1:2, :], axis=1, keepdims=True)
    b = jnp.sum(lv[2:3, :] * lv[3:4, :], axis=1, keepdims=True)
    return jnp.exp(a) - jnp.exp(b) + lam_init


def _sub_norm(o, sub_ref, lam_init):
    ms = jnp.mean(o * o, axis=-1, keepdims=True)
    return o * lax.rsqrt(ms + LN_EPS) * sub_ref[...] * (1.0 - lam_init)


def _attn_prompt_kernel(qt_ref, k_ref, vt_ref, lam_ref, sub_ref, o_ref, *scratch, lam_init):
    i = pl.program_id(2)
    tq = qt_ref.shape[2]
    tk = tq
    n_heads = qt_ref.shape[1] // LANES
    m_scrs, acc_scrs = scratch[:n_heads], scratch[n_heads:]
    for m_scr, acc_scr in zip(m_scrs, acc_scrs):
        m_scr[...] = jnp.full(m_scr.shape, -jnp.inf, F32)
        acc_scr[...] = jnp.zeros(acc_scr.shape, F32)
    ones = jnp.ones((ONES_ROWS, tk), BF16)
    qqs = []
    for h in range(n_heads):
        qt = qt_ref[0, h * LANES:(h + 1) * LANES, :].astype(F32)
        feat = lax.broadcasted_iota(I32, qt.shape, 0)
        qqs.append(jnp.concatenate([jnp.where(feat < DH_B, qt, 0.0), jnp.where(feat >= DH_B, qt, 0.0)],
                                   axis=1).astype(BF16))

    def step(j, masked):
        start = pl.multiple_of(j * tk, tk)
        scores = [_dot(k_ref[0, pl.ds(start, tk), h * LANES:(h + 1) * LANES], qqs[h])
                  for h in range(n_heads)]
        probs, alphas = [], []
        for h in range(n_heads):
            s = scores[h]
            if masked:
                key = lax.broadcasted_iota(I32, s.shape, 0)
                qry = lax.broadcasted_iota(I32, s.shape, 1)
                qry = jnp.where(qry >= tq, qry - tq, qry)
                s = jnp.where(key <= qry, s, -jnp.inf)
            m_prev = m_scrs[h][...]
            m_new = jnp.maximum(m_prev, jnp.max(s, axis=0, keepdims=True))
            alphas.append(jnp.exp2(m_prev - m_new))
            probs.append(jnp.exp2(s - m_new).astype(BF16))
            m_scrs[h][...] = m_new
        for h in range(n_heads):
            vtj = jnp.concatenate([vt_ref[0, h * LANES:(h + 1) * LANES, pl.ds(start, tk)], ones], axis=0)
            acc_scrs[h][...] = alphas[h] * acc_scrs[h][...] + _dot(vtj, probs[h])

    def body(j, carry):
        step(j, False)
        return carry

    lax.fori_loop(0, i, body, 0)
    step(i, True)
    lam = _lambda_value(lam_ref, lam_init)
    for h in range(n_heads):
        on = acc_scrs[h][0:LANES, :] / acc_scrs[h][LANES:LANES + 1, :]
        o = jnp.transpose(on[:, 0:tq] - lam * on[:, tq:2 * tq])
        o_ref[0, :, h * LANES:(h + 1) * LANES] = _sub_norm(o, sub_ref, lam_init).astype(BF16)


def _attn_prompt(qt, k, vt, lam_vecs, subln, lam_init):
    b, t, _ = k.shape
    tq = ATTN_TILE
    nh = ATTN_HEADS_PER_STEP
    hw = nh * LANES
    full = lambda a: pl.BlockSpec(a.shape, lambda bi, h, i: (0, 0))
    return pl.pallas_call(
        functools.partial(_attn_prompt_kernel, lam_init=lam_init),
        grid=(b, H_B // nh, t // tq),
        in_specs=[pl.BlockSpec((1, hw, tq), lambda bi, h, i: (bi, h, i)),
                  pl.BlockSpec((1, t, hw), lambda bi, h, i: (bi, 0, h)),
                  pl.BlockSpec((1, hw, t), lambda bi, h, i: (bi, h, 0)),
                  full(lam_vecs), full(subln)],
        out_specs=pl.BlockSpec((1, tq, hw), lambda bi, h, i: (bi, i, h)),
        out_shape=jax.ShapeDtypeStruct((b, t, V_B), BF16),
        scratch_shapes=[pltpu.VMEM((1, 2 * tq), F32)] * nh + [pltpu.VMEM((LANES + ONES_ROWS, 2 * tq), F32)] * nh,
        compiler_params=_cparams("arbitrary", "arbitrary", "arbitrary"),
        name="attn_prompt",
    )(qt, k, vt, lam_vecs, subln)


def _attn_decode_kernel(pt_ref, qbd_ref, kn_ref, vn_ref, lam_ref, sub_ref, *rest, lam_init, n_pages):
    del pt_ref
    k_refs = rest[:n_pages]
    v_refs = rest[n_pages:2 * n_pages]
    o_ref, m_scr, l_scr, acc_scr = rest[2 * n_pages:]
    j = pl.program_id(1)
    qbd = qbd_ref[0]

    @pl.when(j == 0)
    def _():
        s_new = jnp.sum(qbd.astype(F32) * kn_ref[0], axis=1, keepdims=True)
        m_scr[...] = s_new
        l_scr[...] = jnp.ones(l_scr.shape, F32)
        acc_scr[...] = jnp.broadcast_to(vn_ref[0], acc_scr.shape)

    s = jnp.concatenate([_dot(qbd, k_refs[r][0].astype(BF16)) for r in range(n_pages)], axis=1)
    m_prev = m_scr[...]
    m_new = jnp.maximum(m_prev, jnp.max(s, axis=1, keepdims=True))
    alpha = jnp.exp(m_prev - m_new)
    p = jnp.exp(s - m_new)
    l_scr[...] = alpha * l_scr[...] + jnp.sum(p, axis=1, keepdims=True)

    def head_pv(h):
        acc = None
        for r in range(n_pages):
            vh = v_refs[r][0, pl.ds(h, PAGE_SIZE, stride=H_B), :].astype(BF16)
            term = _dot(p[:, r * PAGE_SIZE:(r + 1) * PAGE_SIZE].astype(BF16), vh)
            acc = term if acc is None else acc + term
        return acc

    pv = jnp.concatenate([head_pv(h) for h in range(H_B)], axis=1)
    acc_scr[...] = alpha * acc_scr[...] + pv
    m_scr[...] = m_new

    @pl.when(j == pl.num_programs(1) - 1)
    def _():
        on = acc_scr[...] / l_scr[...]
        row = lax.broadcasted_iota(I32, (2 * H_B, 2 * DH_B), 0)
        head = jnp.where(row >= H_B, row - H_B, row)
        o8 = jnp.zeros((2 * H_B, 2 * DH_B), F32)
        for c in range(H_B):
            o8 = o8 + jnp.where(head == c, on[:, c * 2 * DH_B:(c + 1) * 2 * DH_B], 0.0)
        lam = _lambda_value(lam_ref, lam_init)
        o = o8 - lam * pltpu.roll(o8, H_B, 0)
        o_ref[0] = _sub_norm(o, sub_ref, lam_init)


def _attn_decode(qbd, k_new, v_new, cache_k, cache_v, page_table, lam_vecs, subln, lam_init):
    n = qbd.shape[0]
    n_pages = page_table.shape[1]
    pp = PAGES_PER_STEP
    width = V_B
    c2 = lambda a: pl.BlockSpec(a.shape, lambda b, j, pt: (0, 0))
    per_b = lambda a: pl.BlockSpec((1,) + a.shape[1:], lambda b, j, pt: (b, 0, 0))

    def page(r, arr):
        return pl.BlockSpec((1,) + arr.shape[1:], lambda b, j, pt: (pt[b, j * pp + r], 0, 0))

    grid_spec = pltpu.PrefetchScalarGridSpec(
        num_scalar_prefetch=1,
        grid=(n, n_pages // pp),
        in_specs=[per_b(qbd), per_b(k_new), per_b(v_new), c2(lam_vecs), c2(subln)]
        + [page(r, cache_k) for r in range(pp)] + [page(r, cache_v) for r in range(pp)],
        out_specs=pl.BlockSpec((1, 2 * H_B, 2 * DH_B), lambda b, j, pt: (b, 0, 0)),
        scratch_shapes=[pltpu.VMEM((2 * H_B, 1), F32), pltpu.VMEM((2 * H_B, 1), F32),
                        pltpu.VMEM((2 * H_B, width), F32)],
    )
    return pl.pallas_call(
        functools.partial(_attn_decode_kernel, lam_init=lam_init, n_pages=pp),
        grid_spec=grid_spec,
        out_shape=jax.ShapeDtypeStruct((n, 2 * H_B, 2 * DH_B), F32),
        compiler_params=_cparams("arbitrary", "arbitrary"),
        name="attn_decode",
    )(page_table, qbd, k_new, v_new, lam_vecs, subln, *([cache_k] * pp), *([cache_v] * pp))


def _odd_in_kernel(x_ref, w_ref, wg_ref, wgt_ref, bg_ref, bgt_ref,
                   q_ref, k_ref, v_ref, o_ref, gc_ref, gr_ref, kt_ref, *, decode):
    xb = x_ref[0].astype(BF16) if not decode else x_ref[...].astype(BF16)
    qw = H_C * DK_C
    vw = H_C * DV_C
    q = _dot(xb, w_ref[:, 0:qw]) * (DK_C ** -0.5)
    k = _dot(xb, w_ref[:, qw:2 * qw])
    v = _dot(xb, w_ref[:, 2 * qw:2 * qw + vw])
    o = _dot(xb, w_ref[:, 2 * qw + vw:2 * qw + 2 * vw])
    g_col = _dot(xb, wg_ref[...]) + bg_ref[...]
    lane = lax.broadcasted_iota(I32, g_col.shape, 1)
    g_col = jnp.where(lane < H_C, g_col, _log_sigmoid(g_col))
    k_t = jnp.transpose(k).astype(BF16)
    g_row = _dot_nt(wgt_ref[...], xb) + bgt_ref[:, 0:1]
    sub = lax.broadcasted_iota(I32, g_row.shape, 0)
    g_row = jnp.where(sub < H_C, g_row, _log_sigmoid(g_row))
    if decode:
        kt_ref[...] = k_t
        q_ref[...] = q.astype(BF16)
        k_ref[...] = k.astype(BF16)
        v_ref[...] = v.astype(BF16)
        o_ref[...] = o
        gc_ref[...] = g_col
        gr_ref[...] = g_row
    else:
        q_ref[0] = q.astype(BF16)
        k_ref[0] = k.astype(BF16)
        v_ref[0] = v.astype(BF16)
        o_ref[0] = o
        gc_ref[0] = g_col
        gr_ref[0] = g_row
        kt_ref[0] = k_t


def _odd_in(x, w_bf, wg, wgt, bg, bgt, decode):
    qw, vw = H_C * DK_C, H_C * DV_C
    if decode:
        n = x.shape[0]
        ins = [x, w_bf, wg, wgt, bg, bgt]
        outs = [jax.ShapeDtypeStruct((n, qw), BF16), jax.ShapeDtypeStruct((n, qw), BF16),
                jax.ShapeDtypeStruct((n, vw), BF16), jax.ShapeDtypeStruct((n, vw), F32),
                jax.ShapeDtypeStruct((n, LANES), F32), jax.ShapeDtypeStruct((2 * H_C, n), F32),
                jax.ShapeDtypeStruct((qw, n), BF16)]
        return pl.pallas_call(
            functools.partial(_odd_in_kernel, decode=True),
            grid=(1,),
            in_specs=[pl.BlockSpec(a.shape, lambda i: (0, 0)) for a in ins],
            out_specs=[pl.BlockSpec(o.shape, lambda i: (0, 0)) for o in outs],
            out_shape=outs,
            compiler_params=_cparams("arbitrary"),
            name="odd_in_decode",
        )(*ins)
    b, t, _ = x.shape
    tm = ROW_TILE
    row3 = lambda n: pl.BlockSpec((1, tm, n), lambda i, j: (i, j, 0))
    full2 = lambda a: pl.BlockSpec(a.shape, lambda i, j: (0, 0))
    outs = [jax.ShapeDtypeStruct((b, t, qw), BF16), jax.ShapeDtypeStruct((b, t, qw), BF16),
            jax.ShapeDtypeStruct((b, t, vw), BF16), jax.ShapeDtypeStruct((b, t, vw), F32),
            jax.ShapeDtypeStruct((b, t, LANES), F32), jax.ShapeDtypeStruct((b, 2 * H_C, t), F32),
            jax.ShapeDtypeStruct((b, qw, t), BF16)]
    return pl.pallas_call(
        functools.partial(_odd_in_kernel, decode=False),
        grid=(b, t // tm),
        in_specs=[row3(D_MODEL), full2(w_bf), full2(wg), full2(wgt), full2(bg), full2(bgt)],
        out_specs=[row3(qw), row3(qw), row3(vw), row3(vw), row3(LANES),
                   pl.BlockSpec((1, 2 * H_C, tm), lambda i, j: (i, 0, j)),
                   pl.BlockSpec((1, qw, tm), lambda i, j: (i, 0, j))],
        out_shape=outs,
        compiler_params=_cparams("arbitrary", "arbitrary"),
        name="odd_in_prompt",
    )(x, w_bf, wg, wgt, bg, bgt)


def _mlstm_kernel(q_ref, k_ref, kt_ref, v_ref, o_ref, gc_ref, gr_ref, nw_ref, c0_ref, n0_ref, m0_ref,
                  h_ref, c_out, n_out, m_out, c_scr, n_scr, m_scr):
    ci = pl.program_id(1)
    chunk = q_ref.shape[1]

    @pl.when(ci == 0)
    def _():
        c_scr[...] = c0_ref[...]
        n_scr[...] = n0_ref[...]
        m_scr[...] = m0_ref[...]

    t_idx = lax.broadcasted_iota(I32, (chunk, chunk), 0)
    s_idx = lax.broadcasted_iota(I32, (chunk, chunk), 1)
    causal = s_idx <= t_idx
    for bb, h in [(bb, h) for bb in range(q_ref.shape[0]) for h in range(H_C)]:
        q = q_ref[bb, :, h * DK_C:(h + 1) * DK_C]
        k = k_ref[bb, :, h * DK_C:(h + 1) * DK_C]
        v = v_ref[bb, :, h * DV_C:(h + 1) * DV_C]
        ig_r = gr_ref[bb, h:h + 1, :]
        lf_r = gr_ref[bb, H_C + h:H_C + h + 1, :]
        ig_c = gc_ref[bb, :, h:h + 1]
        lf_c = gc_ref[bb, :, H_C + h:H_C + h + 1]
        bcum_c = jnp.sum(jnp.where(causal, lf_r, 0.0), axis=1, keepdims=True)
        bcum_r = jnp.sum(jnp.where(t_idx <= s_idx, lf_c, 0.0), axis=0, keepdims=True)
        m0 = m_scr[bb, h:h + 1, 0:1]
        dmat = jnp.where(causal, bcum_c - bcum_r + ig_r, -jnp.inf)
        inter = bcum_c + m0
        m = jnp.maximum(inter, jnp.max(dmat, axis=1, keepdims=True))
        w = jnp.exp(dmat - m)
        g = jnp.exp(inter - m)
        s = _dot_nt(q, k) * w
        c0 = c_scr[bb, h]
        n0 = n_scr[bb, h:h + 1, :]
        num = g * _dot(q, c0.astype(BF16)) + _dot(s.astype(BF16), v)
        den = g * jnp.sum(q.astype(F32) * n0, axis=1, keepdims=True) + jnp.sum(s, axis=1, keepdims=True)
        hid = num / jnp.maximum(jnp.abs(den), jnp.exp(-m))
        m_last = m[chunk - 1:chunk, :]
        b_last = bcum_c[chunk - 1:chunk, :]
        w_last = jnp.exp(b_last - bcum_c + ig_c - m_last)
        g_last = jnp.exp(b_last + m0 - m_last)
        kw = k.astype(F32) * w_last
        w_last_r = jnp.exp(b_last - bcum_r + ig_r - m_last)
        kw_t = (kt_ref[bb, h * DK_C:(h + 1) * DK_C, :].astype(F32) * w_last_r).astype(BF16)
        c_scr[bb, h] = g_last * c0 + _dot(kw_t, v)
        n_scr[bb, h:h + 1, :] = g_last * n0 + jnp.sum(kw, axis=0, keepdims=True)
        m_scr[bb, h:h + 1, :] = jnp.broadcast_to(m_last, (1, LANES))
        mu = jnp.mean(hid, axis=1, keepdims=True)
        hc = hid - mu
        var = jnp.mean(hc * hc, axis=1, keepdims=True)
        hn = hc * lax.rsqrt(var + LN_EPS) * nw_ref[:, h * DV_C:(h + 1) * DV_C]
        gate = jax.nn.sigmoid(o_ref[bb, :, h * DV_C:(h + 1) * DV_C])
        h_ref[bb, :, h * DV_C:(h + 1) * DV_C] = (gate * hn).astype(BF16)

    c_out[...] = c_scr[...]
    n_out[...] = n_scr[...]
    m_out[...] = m_scr[...]


def _mlstm(q, k, k_t, v, o, g_col, g_row, norm_w, c0, n0, m0, ch):
    b, t, _ = q.shape
    nb = MLSTM_BATCH_PER_STEP
    qw, vw = H_C * DK_C, H_C * DV_C
    row = lambda n: pl.BlockSpec((nb, ch, n), lambda i, j: (i, j, 0))
    st4 = pl.BlockSpec((nb, H_C, DK_C, DV_C), lambda i, j: (i, 0, 0, 0))
    st3 = pl.BlockSpec((nb, H_C, LANES), lambda i, j: (i, 0, 0))
    outs = [jax.ShapeDtypeStruct((b, t, vw), BF16), jax.ShapeDtypeStruct((b, H_C, DK_C, DV_C), F32),
            jax.ShapeDtypeStruct((b, H_C, DK_C), F32), jax.ShapeDtypeStruct((b, H_C, LANES), F32)]
    return pl.pallas_call(
        _mlstm_kernel,
        grid=(b // nb, t // ch),
        in_specs=[row(qw), row(qw), pl.BlockSpec((nb, qw, ch), lambda i, j: (i, 0, j)), row(vw), row(vw),
                  row(LANES), pl.BlockSpec((nb, 2 * H_C, ch), lambda i, j: (i, 0, j)),
                  pl.BlockSpec(norm_w.shape, lambda i, j: (0, 0)), st4, st3, st3],
        out_specs=[row(vw), st4, st3, st3],
        out_shape=outs,
        scratch_shapes=[pltpu.VMEM((nb, H_C, DK_C, DV_C), F32), pltpu.VMEM((nb, H_C, DK_C), F32),
                        pltpu.VMEM((nb, H_C, LANES), F32)],
        compiler_params=_cparams("arbitrary", "arbitrary"),
        name="mlstm",
    )(q, k, k_t, v, o, g_col, g_row, norm_w, c0, n0, m0)


def _mix_project(acts, w_refs):
    y = _dot(acts[0], w_refs[0][...])
    for a, w in zip(acts[1:], w_refs[1:]):
        y = y + _dot(a, w[...])
    return y


def _mixed_rows(acts, w_refs, x, g_ref, b_ref):
    return _layer_norm_rows(DEEPNORM_ALPHA * x + _mix_project(acts, w_refs), g_ref[...], b_ref[...])


def _route_logits(x1, rw_ref):
    xh = x1.astype(BF16)
    xl = (x1 - xh.astype(F32)).astype(BF16)
    both = _dot_nt(rw_ref[...], xh)
    return both[0:ROUTE_ROWS, :] + both[ROUTE_ROWS:2 * ROUTE_ROWS, :] + _dot_nt(rw_ref[0:ROUTE_ROWS, :], xl)


def _route(x1, rw_ref):
    return _route_decide(_route_logits(x1, rw_ref))


def _route_decide(lg):
    sub = lax.broadcasted_iota(I32, lg.shape, 0)
    big = jnp.int32(4 * ROUTE_ROWS)
    neg = -jnp.inf
    gl = jnp.where(sub < ROUTE_GROUP_ROW + N_GROUPS, lg, neg)
    g_max = jnp.max(gl, axis=0, keepdims=True)
    g_w = 1.0 / jnp.sum(jnp.exp(gl - g_max), axis=0, keepdims=True)
    g_idx = jnp.min(jnp.where(gl == g_max, sub, big), axis=0, keepdims=True)
    row_group = (sub - ROUTE_EXPERT_ROW) >> 2
    el = jnp.where(row_group == g_idx, lg, neg)
    e1 = jnp.max(el, axis=0, keepdims=True)
    i1 = jnp.min(jnp.where(el == e1, sub, big), axis=0, keepdims=True)
    z = jnp.sum(jnp.exp(el - e1), axis=0, keepdims=True)
    el2 = jnp.where(sub == i1, neg, el)
    e2 = jnp.max(el2, axis=0, keepdims=True)
    i2 = jnp.min(jnp.where(el2 == e2, sub, big), axis=0, keepdims=True)
    p1 = 1.0 / z
    p2 = jnp.exp(e2 - e1) / z
    w1 = p1 / (p1 + p2) * g_w
    w2 = p2 / (p1 + p2) * g_w
    id1 = (i1 - ROUTE_EXPERT_ROW).astype(F32)
    id2 = (i2 - ROUTE_EXPERT_ROW).astype(F32)
    return jnp.where(sub == 0, w1, jnp.where(sub == 1, w2, jnp.where(sub == 2, id1, jnp.where(sub == 3, id2, 0.0))))


def _slab_columns(rt):
    pad = jnp.zeros((LANES - rt.shape[0], rt.shape[1]), F32)
    return jnp.transpose(jnp.concatenate([rt, pad], axis=0))


def _mix_out_kernel(*refs, n_in, n_s):
    ap_refs = refs[:n_in]
    as_refs = refs[n_in:2 * n_in]
    w_refs = refs[2 * n_in:3 * n_in]
    xp_ref, xs_ref, g_ref, b_ref, rw_ref, out_ref, rt_ref = refs[3 * n_in:]
    i = pl.program_id(0)
    last = pl.num_programs(0) - 1

    @pl.when(i < last)
    def _():
        x1 = _mixed_rows([a[...] for a in ap_refs], w_refs, xp_ref[...], g_ref, b_ref)
        rt = _route(x1, rw_ref)
        out_ref[:, 0:D_MODEL] = x1
        out_ref[:, D_MODEL:D_MODEL + LANES] = _slab_columns(rt)
        rt_ref[...] = rt[0:8, :]

    @pl.when(i == last)
    def _():
        x1 = _mixed_rows([a[...] for a in as_refs], w_refs, xs_ref[...], g_ref, b_ref)
        rt = _route(jnp.concatenate([x1, jnp.zeros((LANES - n_s, D_MODEL), F32)], axis=0), rw_ref)
        out_ref[0:n_s, 0:D_MODEL] = x1
        out_ref[0:n_s, D_MODEL:D_MODEL + LANES] = _slab_columns(rt)[0:n_s, :]
        rt_ref[...] = jnp.zeros(rt_ref.shape, F32)
        rt_ref[:, 0:LANES] = rt[0:8, :]


def _mix_out(acts_p, acts_s, weights, xp, xs, ln_g, ln_b, rw):
    n_p, n_s = xp.shape[0], xs.shape[0]
    assert n_s <= LANES
    tm = ROW_TILE
    nb = n_p // tm
    n_in = len(acts_p)
    prow = lambda n: pl.BlockSpec((tm, n), lambda i: (jnp.minimum(i, nb - 1), 0))
    full = lambda a: pl.BlockSpec(a.shape, lambda i: (0, 0))
    width = D_MODEL + LANES
    return pl.pallas_call(
        functools.partial(_mix_out_kernel, n_in=n_in, n_s=n_s),
        grid=(nb + 1,),
        in_specs=[prow(a.shape[1]) for a in acts_p] + [full(a) for a in acts_s] + [full(w) for w in weights]
        + [prow(D_MODEL), full(xs), full(ln_g), full(ln_b), full(rw)],
        out_specs=[pl.BlockSpec((tm, width), lambda i: (i, 0)), pl.BlockSpec((8, tm), lambda i: (0, i))],
        out_shape=[jax.ShapeDtypeStruct((n_p + n_s, width), F32),
                   jax.ShapeDtypeStruct((8, (nb + 1) * tm), F32)],
        compiler_params=_cparams("arbitrary"),
        name="mix_out",
    )(*acts_p, *acts_s, *weights, xp, xs, ln_g, ln_b, rw)


def _route_plan(rt, tm):
    n = rt.shape[1]
    e1, e2 = rt[2, :].astype(I32), rt[3, :].astype(I32)
    ea, eb = jnp.minimum(e1, e2), jnp.maximum(e1, e2)
    la, lb = ea % EXP_PER_GROUP, eb % EXP_PER_GROUP
    pair_rank = sum(k * ((la == a) & (lb == b)).astype(I32) for k, (a, b) in enumerate(PAIR_SEQ))
    cls = (ea // EXP_PER_GROUP) * N_PAIRS + pair_rank
    onehot = (cls[:, None] == jnp.arange(N_CLASSES, dtype=I32)[None, :]).astype(I32)
    csum = jnp.cumsum(onehot, axis=0)
    rank = jnp.sum(onehot * csum, axis=1) - 1
    cnt = csum[-1]
    ntile = (cnt + tm - 1) // tm
    tile_end = jnp.cumsum(ntile)
    tile_start = tile_end - ntile
    n_used = tile_end[-1]
    n_tiles = -(-(n + N_CLASSES * (tm - 1)) // tm)
    pos = (jnp.sum(onehot * tile_start[None, :], axis=1) * tm + rank).astype(I32)
    pair_lo = np.array([a for a, _ in PAIR_SEQ], np.int32)
    pair_hi = np.array([b for _, b in PAIR_SEQ], np.int32)
    cls_ids = np.arange(N_CLASSES)
    cls_a = jnp.asarray((cls_ids // N_PAIRS) * EXP_PER_GROUP + pair_lo[cls_ids % N_PAIRS], I32)
    cls_b = jnp.asarray((cls_ids // N_PAIRS) * EXP_PER_GROUP + pair_hi[cls_ids % N_PAIRS], I32)
    tile_ids = jnp.arange(n_tiles, dtype=I32)
    tile_cls = jnp.sum((tile_end[None, :] <= jnp.minimum(tile_ids, n_used - 1)[:, None]).astype(I32), axis=1)
    tile_cls = jnp.minimum(tile_cls, N_CLASSES - 1)
    onehot_t = (tile_cls[:, None] == jnp.arange(N_CLASSES, dtype=I32)[None, :]).astype(I32)
    last_tile = jnp.where(ntile > 0, tile_end - 1, -1).astype(I32)
    ta = jnp.sum(onehot_t * cls_a[None, :], axis=1)
    tb = jnp.sum(onehot_t * cls_b[None, :], axis=1)
    first = jnp.ones((1,), I32)
    chg = (jnp.concatenate([first, (ta[1:] != ta[:-1]).astype(I32)])
           + 2 * jnp.concatenate([first, (tb[1:] != tb[:-1]).astype(I32)]))
    return dict(pos=pos, ta=ta, tb=tb, nu=n_used.reshape(1).astype(I32), chg=chg, last_tile=last_tile,
                n_tiles=n_tiles)


def _dispatch_kernel(lt_ref, nu_ref, pos_ref, src_ref, dst_hbm, zbuf, zsem, rsem, *, moe_tile, n_tiles):
    i = pl.program_id(0)
    td = pos_ref.shape[2]

    def zero_copy(tile):
        start = pl.multiple_of(tile * moe_tile, moe_tile)
        return pltpu.make_async_copy(zbuf, dst_hbm.at[pl.ds(start, moe_tile)], zsem)

    @pl.when(i == 0)
    def _():
        zbuf[...] = jnp.zeros(zbuf.shape, F32)
        for c in range(N_CLASSES):
            @pl.when(lt_ref[c] >= 0)
            def _(c=c):
                zero_copy(lt_ref[c]).start()
        for c in range(N_CLASSES):
            @pl.when(lt_ref[c] >= 0)
            def _(c=c):
                zero_copy(lt_ref[c]).wait()

        def spare_start(t, carry):
            zero_copy(t).start()
            return carry

        def spare_wait(t, carry):
            zero_copy(t).wait()
            return carry

        lax.fori_loop(nu_ref[0], n_tiles, spare_start, 0)
        lax.fori_loop(nu_ref[0], n_tiles, spare_wait, 0)

    for r in range(td):
        pltpu.make_async_copy(src_ref.at[pl.ds(r, 1)], dst_hbm.at[pl.ds(pos_ref[0, 0, r], 1)],
                              rsem).start(priority=r % 2)
    pltpu.make_async_copy(src_ref, dst_hbm.at[pl.ds(0, td)], rsem).wait()


def _largest_divisor_tile(n, cap):
    for t in range(cap - cap % 8, 7, -8):
        if n % t == 0:
            return t
    raise ValueError(f"no row tile for {n} rows")


def _dispatch(x1e, plan, tm):
    n, width = x1e.shape
    n_tiles = plan["n_tiles"]
    td = _largest_divisor_tile(n, DISPATCH_TILE_CAP)
    steps = n // td
    grid_spec = pltpu.PrefetchScalarGridSpec(
        num_scalar_prefetch=2,
        grid=(steps,),
        in_specs=[pl.BlockSpec((1, 1, td), lambda i, lt, nu: (i, 0, 0), memory_space=pltpu.SMEM),
                  pl.BlockSpec((td, width), lambda i, lt, nu: (i, 0))],
        out_specs=pl.BlockSpec(memory_space=pl.ANY),
        scratch_shapes=[pltpu.VMEM((tm, width), F32), pltpu.SemaphoreType.DMA(()),
                        pltpu.SemaphoreType.DMA(())],
    )
    return pl.pallas_call(
        functools.partial(_dispatch_kernel, moe_tile=tm, n_tiles=n_tiles),
        grid_spec=grid_spec,
        out_shape=jax.ShapeDtypeStruct((n_tiles * tm, width), F32),
        compiler_params=_cparams("arbitrary"),
        name="dispatch",
    )(plan["last_tile"], plan["nu"], plan["pos"].reshape(steps, 1, td), x1e)


def _moe_kernel(ta_ref, tb_ref, nu_ref, chg_ref, x_ref, ga_f32, ua_f32, da_f32, gb_f32, ub_f32, db_f32,
                lg_ref, lb_ref, o_ref, ga_ref, ua_ref, da_ref, gb_ref, ub_ref, db_ref):
    del ta_ref, tb_ref
    g = pl.program_id(0)

    @pl.when((chg_ref[g] & 1) == 1)
    def _():
        for src, dst in ((ga_f32, ga_ref), (ua_f32, ua_ref), (da_f32, da_ref)):
            dst[0] = src[0, 0].astype(BF16)

    @pl.when((chg_ref[g] & 2) == 2)
    def _():
        for src, dst in ((gb_f32, gb_ref), (ub_f32, ub_ref), (db_f32, db_ref)):
            dst[0] = src[0, 0].astype(BF16)

    @pl.when(g < nu_ref[0])
    def _():
        x = x_ref[:, 0:D_MODEL]
        slab = x_ref[:, D_MODEL:D_MODEL + LANES]
        w1, w2, e1, e2 = slab[:, 0:1], slab[:, 1:2], slab[:, 2:3], slab[:, 3:4]
        first = e1 < e2
        wa = jnp.where(first, w1, w2)
        wb = jnp.where(first, w2, w1)
        xb = x.astype(BF16)

        gate_a, up_a = _dot(xb, ga_ref[0]), _dot(xb, ua_ref[0])
        gate_b, up_b = _dot(xb, gb_ref[0]), _dot(xb, ub_ref[0])
        y = wa * _dot((jax.nn.silu(gate_a) * up_a).astype(BF16), da_ref[0])
        y = y + wb * _dot((jax.nn.silu(gate_b) * up_b).astype(BF16), db_ref[0])
        o_ref[...] = _layer_norm_rows(DEEPNORM_ALPHA * x + y, lg_ref[...], lb_ref[...])

    @pl.when(g >= nu_ref[0])
    def _():
        o_ref[...] = jnp.zeros(o_ref.shape, F32)


def _moe(xs_sorted, plan, layer, w_gate, w_up, w_down, ln_g, ln_b, tm):
    n_tiles = plan["n_tiles"]
    width = xs_sorted.shape[1]
    up_a = pl.BlockSpec((1, 1, D_MODEL, D_EXPERT), lambda g, ta, tb, nu, chg: (layer, ta[g], 0, 0))
    dn_a = pl.BlockSpec((1, 1, D_EXPERT, D_MODEL), lambda g, ta, tb, nu, chg: (layer, ta[g], 0, 0))
    up_b = pl.BlockSpec((1, 1, D_MODEL, D_EXPERT), lambda g, ta, tb, nu, chg: (layer, tb[g], 0, 0))
    dn_b = pl.BlockSpec((1, 1, D_EXPERT, D_MODEL), lambda g, ta, tb, nu, chg: (layer, tb[g], 0, 0))
    vec = pl.BlockSpec((1, D_MODEL), lambda g, ta, tb, nu, chg: (0, 0))
    up_s = pltpu.VMEM((1, D_MODEL, D_EXPERT), BF16)
    dn_s = pltpu.VMEM((1, D_EXPERT, D_MODEL), BF16)
    grid_spec = pltpu.PrefetchScalarGridSpec(
        num_scalar_prefetch=4,
        grid=(n_tiles,),
        in_specs=[pl.BlockSpec((tm, width), lambda g, ta, tb, nu, chg: (g, 0)),
                  up_a, up_a, dn_a, up_b, up_b, dn_b, vec, vec],
        out_specs=pl.BlockSpec((tm, D_MODEL), lambda g, ta, tb, nu, chg: (g, 0)),
        scratch_shapes=[up_s, up_s, dn_s, up_s, up_s, dn_s],
    )
    return pl.pallas_call(
        _moe_kernel,
        grid_spec=grid_spec,
        out_shape=jax.ShapeDtypeStruct((n_tiles * tm, D_MODEL), F32),
        compiler_params=_cparams("arbitrary"),
        name="moe",
    )(plan["ta"], plan["tb"], plan["nu"], plan["chg"], xs_sorted, w_gate, w_up, w_down, w_gate, w_up, w_down,
      ln_g, ln_b)


def _ple_kernel(pos_ref, posn_ref, x2_hbm, pp_ref, ps_ref, wg_ref, wp_ref, op_ref, os_ref, buf, sems, *, n_s):
    i = pl.program_id(0)
    last = pl.num_programs(0) - 1
    tm = buf.shape[1]

    def issue(idx_ref, s):
        for r in range(tm):
            pltpu.make_async_copy(x2_hbm.at[pl.ds(idx_ref[0, 0, r], 1)], buf.at[s, pl.ds(r, 1)],
                                  sems.at[s]).start(priority=r % 2)

    def rows(x, p):
        gate = jax.nn.sigmoid(_dot(x.astype(BF16), wg_ref[...]))
        return x + gate * _dot(p.astype(BF16), wp_ref[...])

    def step(slot):
        if slot == 0:
            @pl.when(i == 0)
            def _():
                issue(pos_ref, 0)

        @pl.when(i < last)
        def _():
            issue(posn_ref, 1 - slot)

        pltpu.make_async_copy(x2_hbm.at[pl.ds(0, tm)], buf.at[slot], sems.at[slot]).wait()

        @pl.when(i < last)
        def _():
            op_ref[...] = rows(buf[slot], pp_ref[0])

        @pl.when(i == last)
        def _():
            os_ref[...] = rows(buf[slot, 0:n_s, :], ps_ref[0])

    for slot in range(2):
        @pl.when(lax.rem(i, 2) == slot)
        def _(slot=slot):
            step(slot)


def _ple(x2_sorted, pos, layer, p_p, p_s, wg_bf, wp_bf):
    n_p, n_s = p_p.shape[1], p_s.shape[1]
    tm = GATHER_TILE
    nb = n_p // tm
    steps = nb + 1
    pos_pad = jnp.zeros((steps * tm,), I32).at[:n_p + n_s].set(pos).reshape(steps, 1, tm)
    full = lambda a: pl.BlockSpec(a.shape, lambda i: (0, 0))
    prow = lambda n: pl.BlockSpec((tm, n), lambda i: (jnp.minimum(i, nb - 1), 0))
    return pl.pallas_call(
        functools.partial(_ple_kernel, n_s=n_s),
        grid=(steps,),
        in_specs=[pl.BlockSpec((1, 1, tm), lambda i: (i, 0, 0), memory_space=pltpu.SMEM),
                  pl.BlockSpec((1, 1, tm), lambda i: (jnp.minimum(i + 1, nb), 0, 0), memory_space=pltpu.SMEM),
                  pl.BlockSpec(memory_space=pl.ANY),
                  pl.BlockSpec((1, tm, D_PLE), lambda i: (layer, jnp.minimum(i, nb - 1), 0)),
                  pl.BlockSpec((1, n_s, D_PLE), lambda i: (layer, 0, 0)), full(wg_bf), full(wp_bf)],
        out_specs=[prow(D_MODEL), pl.BlockSpec((n_s, D_MODEL), lambda i: (0, 0))],
        out_shape=[jax.ShapeDtypeStruct((n_p, D_MODEL), F32), jax.ShapeDtypeStruct((n_s, D_MODEL), F32)],
        scratch_shapes=[pltpu.VMEM((2, tm, D_MODEL), F32), pltpu.SemaphoreType.DMA((2,))],
        compiler_params=_cparams("arbitrary"),
        name="ple",
    )(pos_pad, pos_pad, x2_sorted, p_p, p_s, wg_bf, wp_bf)


def _router_weights(w_group, w_router):
    wr = jnp.zeros((ROUTE_ROWS, D_MODEL), F32)
    wr = wr.at[ROUTE_GROUP_ROW:ROUTE_GROUP_ROW + N_GROUPS, :].set(jnp.transpose(w_group))
    wr = wr.at[ROUTE_EXPERT_ROW:ROUTE_EXPERT_ROW + N_EXPERTS, :].set(jnp.transpose(w_router))
    hi = wr.astype(BF16)
    lo = (wr - hi.astype(F32)).astype(BF16)
    return jnp.concatenate([hi, lo], axis=0)


def _layer_tail(i, acts_p, acts_s, w_list, xp, xs, p_p, p_s, ln_mix_g, ln_mix_b, ln_ffn_g, ln_ffn_b,
                w_group, w_router, w_exp_gate, w_exp_up, w_exp_down, w_ple_proj, w_ple_gate):
    rw = _router_weights(w_group[i], w_router[i])
    x1e, rt = _mix_out(acts_p, acts_s, w_list, xp, xs, ln_mix_g[i][None, :], ln_mix_b[i][None, :], rw)
    plan = _route_plan(rt[:, :x1e.shape[0]], MOE_TILE)
    xs_sorted = _dispatch(x1e, plan, MOE_TILE)
    x2_sorted = _moe(xs_sorted, plan, i, w_exp_gate, w_exp_up, w_exp_down, ln_ffn_g[i][None, :],
                     ln_ffn_b[i][None, :], MOE_TILE)
    return _ple(x2_sorted, plan["pos"], i, p_p, p_s, w_ple_gate[i].astype(BF16), w_ple_proj[i].astype(BF16))


def kernel(x_prompt, x_sample, cache_k, cache_v, page_table, state_conv, state_mlstm_C, state_mlstm_n,
           state_mlstm_m, p_prompt, p_sample, w_in_even, conv_w, lambda_q1, lambda_k1, lambda_q2, lambda_k2,
           subln_w, w_out_even, w_in_odd, b_gates_odd, mh_norm_w, w_out_odd, ln_mix_g, ln_mix_b, ln_ffn_g,
           ln_ffn_b, w_group, w_router, w_exp_gate, w_exp_up, w_exp_down, w_ple_proj, w_ple_gate):
    bp, tp, _ = x_prompt.shape
    bs, ts, _ = x_sample.shape
    assert ts == 1 and tp % ROW_TILE == 0 and tp % ATTN_TILE == 0 and tp % MLSTM_CHUNK == 0
    assert (bp * tp) % GATHER_TILE == 0
    n_p = bp * tp
    past_len = page_table.shape[1] * cache_k.shape[2]
    xp = x_prompt.reshape(n_p, D_MODEL)
    xs = x_sample.reshape(bs, D_MODEL)
    tail_w = (ln_mix_g, ln_mix_b, ln_ffn_g, ln_ffn_b, w_group, w_router, w_exp_gate, w_exp_up, w_exp_down,
              w_ple_proj, w_ple_gate)
    outs_p, outs_s = {}, {}
    for i in range(DEPTH):
        j = i // 2
        p_p = p_prompt.reshape(DEPTH, n_p, D_PLE)
        p_s = p_sample.reshape(DEPTH, bs, D_PLE)
        if i % 2 == 0:
            lam_init = 0.8 - 0.6 * math.exp(-0.3 * i)
            lam_vecs = jnp.stack([lambda_q1[j], lambda_k1[j], lambda_q2[j], lambda_k2[j]])
            sub = subln_w[j][None, :]
            w_bf = w_in_even[j].astype(BF16)
            tabs_p = _rope_tables(jnp.arange(tp))
            q0 = 3 * D_CONV
            wqt_bf = jnp.transpose(w_in_even[j][:, q0:q0 + QK_B]).astype(BF16)
            yc, qt, kf, vf, kb, vt, cst = _even_in_prompt(
                x_prompt if i == 0 else xp.reshape(bp, tp, D_MODEL), w_bf, wqt_bf, conv_w[j],
                jnp.zeros((bp, CONV_W - 1, D_CONV), F32), tabs_p)
            o_p = _attn_prompt(qt, kb, vt, lam_vecs, sub, lam_init)
            outs_p.setdefault("k", []).append(kf.reshape(bp, tp, 2 * H_B, DH_B))
            outs_p.setdefault("v", []).append(vf.reshape(bp, tp, H_B, 2 * DH_B))
            outs_p.setdefault("c", []).append(cst)
            tabs_s = _rope_tables(jnp.full((1,), past_len, I32))
            prev_t = jnp.swapaxes(state_conv[j], 0, 1)
            yc_s, q_s, kf_s, vf_s, u_s = _even_in_decode(xs, w_bf, conv_w[j], prev_t, tabs_s)
            sub_head = jnp.arange(2 * H_B)
            sub_head = jnp.where(sub_head < H_B, 2 * sub_head, 2 * (sub_head - H_B) + 1)
            lane_head = jnp.arange(QK_B) // DH_B
            qbd = jnp.where(lane_head[None, None, :] == sub_head[None, :, None], q_s[:, None, :],
                            jnp.zeros((), BF16))
            n_pool = cache_k.shape[1]
            pages = cache_k.shape[0] * n_pool
            k_view = jnp.transpose(cache_k, (0, 1, 3, 4, 2)).reshape(pages, QK_B, PAGE_SIZE)
            v_view = cache_v.reshape(pages, PAGE_SIZE * H_B, 2 * DH_B)
            o8 = _attn_decode(qbd, kf_s[:, None, :], vf_s[:, None, :], k_view, v_view,
                              page_table + j * n_pool, lam_vecs, sub, lam_init)
            o_s = o8[:, :H_B, :].reshape(bs, V_B).astype(BF16)
            outs_s.setdefault("k", []).append(kf_s.reshape(bs, ts, 2 * H_B, DH_B))
            outs_s.setdefault("v", []).append(vf_s.reshape(bs, ts, H_B, 2 * DH_B))
            outs_s.setdefault("c", []).append(jnp.stack([state_conv[j][:, 1, :], u_s], axis=1))
            w_out = w_out_even[j].astype(BF16)
            w_list = [w_out[:D_CONV], w_out[D_CONV:]]
            acts_p = [yc.reshape(n_p, D_CONV), o_p.reshape(n_p, V_B)]
            acts_s = [yc_s, o_s]
        else:
            w_in = w_in_odd[j]
            qw, vw = H_C * DK_C, H_C * DV_C
            w_bf = w_in[:, :2 * qw + 2 * vw].astype(BF16)
            wg = jnp.zeros((D_MODEL, LANES), F32).at[:, :2 * H_C].set(w_in[:, 2 * qw + 2 * vw:]).astype(BF16)
            wgt = jnp.transpose(wg[:, :2 * H_C])
            bg = jnp.zeros((1, LANES), F32).at[0, :2 * H_C].set(b_gates_odd[j])
            bgt = jnp.broadcast_to(b_gates_odd[j][:, None], (2 * H_C, LANES))
            nw = mh_norm_w[j][None, :]
            q, k, v, o, gc, gr, kt = _odd_in(xp.reshape(bp, tp, D_MODEL), w_bf, wg, wgt, bg, bgt, decode=False)
            h_p, c_p, n_pp, m_p = _mlstm(q, k, kt, v, o, gc, gr, nw,
                                         jnp.zeros((bp, H_C, DK_C, DV_C), F32), jnp.zeros((bp, H_C, DK_C), F32),
                                         jnp.zeros((bp, H_C, LANES), F32), MLSTM_CHUNK)
            outs_p.setdefault("C", []).append(c_p)
            outs_p.setdefault("n", []).append(n_pp)
            outs_p.setdefault("m", []).append(m_p[:, :, 0])
            q_s, k_s, v_s, o_s2, gc_s, gr_s, kt_s = _odd_in(xs, w_bf, wg, wgt, bg, bgt, decode=True)
            ch = MLSTM_DECODE_CHUNK
            pad_rows = lambda a: jnp.zeros((bs, ch, a.shape[1]), a.dtype).at[:, 0, :].set(a)
            lane = jnp.arange(LANES)
            inert_c = jnp.where(lane < H_C, -jnp.inf, 0.0).astype(F32)
            gc_pad = jnp.broadcast_to(inert_c[None, None, :], (bs, ch, LANES)).at[:, 0, :].set(gc_s)
            inert_r = jnp.where(jnp.arange(2 * H_C) < H_C, -jnp.inf, 0.0).astype(F32)
            gr_pad = jnp.broadcast_to(inert_r[None, :, None], (bs, 2 * H_C, ch)).at[:, :, 0].set(gr_s.T)
            m0 = jnp.broadcast_to(state_mlstm_m[j][:, :, None], (bs, H_C, LANES))
            kt_pad = jnp.zeros((bs, qw, ch), BF16).at[:, :, 0].set(kt_s.T)
            h_s, c_s, n_s, m_s = _mlstm(pad_rows(q_s), pad_rows(k_s), kt_pad, pad_rows(v_s), pad_rows(o_s2),
                                        gc_pad, gr_pad, nw, state_mlstm_C[j], state_mlstm_n[j], m0, ch)
            outs_s.setdefault("C", []).append(c_s)
            outs_s.setdefault("n", []).append(n_s)
            outs_s.setdefault("m", []).append(m_s[:, :, 0])
            w_list = [w_out_odd[j].astype(BF16)]
            acts_p = [h_p.reshape(n_p, vw)]
            acts_s = [h_s[:, 0, :]]
        xp, xs = _layer_tail(i, acts_p, acts_s, w_list, xp, xs, p_p, p_s, *tail_w)
    st = lambda lst: jnp.stack(lst)
    return (xp.reshape(bp, tp, D_MODEL), xs.reshape(bs, ts, D_MODEL),
            st(outs_p["k"]), st(outs_p["v"]), st(outs_p["c"]), st(outs_p["C"]), st(outs_p["n"]), st(outs_p["m"]),
            st(outs_s["k"]), st(outs_s["v"]), st(outs_s["c"]), st(outs_s["C"]), st(outs_s["n"]), st(outs_s["m"]))
```

```python
import functools
import math

import numpy as np
import jax
import jax.numpy as jnp
from jax import lax
from jax.experimental import pallas as pl
from jax.experimental.pallas import tpu as pltpu

F32 = jnp.float32
BF16 = jnp.bfloat16
I32 = jnp.int32

D_MODEL = 1024
DEPTH = 2
PAGE_SIZE = 128
D_CONV = D_MODEL // 2
CONV_W = 3
H_B = 4
DH_B = 64
ROT_DIM = DH_B // 4
ROPE_THETA = 500000.0
H_C = 4
DK_C = (D_MODEL // 2) // H_C
DV_C = D_MODEL // H_C
N_GROUPS = 4
EXP_PER_GROUP = 4
N_EXPERTS = N_GROUPS * EXP_PER_GROUP
D_EXPERT = 512
D_PLE = 256
LN_EPS = 1e-5
LOG2_E = 1.4426950408889634
DEEPNORM_ALPHA = (2 * DEPTH) ** 0.25
QK_B = 2 * H_B * DH_B
V_B = H_B * 2 * DH_B
N_PAIRS = EXP_PER_GROUP * (EXP_PER_GROUP - 1) // 2
N_CLASSES = N_GROUPS * N_PAIRS
PAIR_SEQ = ((0, 1), (0, 2), (1, 2), (1, 3), (0, 3), (2, 3))
assert EXP_PER_GROUP == 4 and len(PAIR_SEQ) == N_PAIRS

LANES = 128
VMEM_LIMIT = 56 * 1024 * 1024
ROW_TILE = 1024
GATHER_TILE = 512
ATTN_TILE = 512
ATTN_HEADS_PER_STEP = 4
ONES_ROWS = 16
MLSTM_CHUNK = 128
MLSTM_DECODE_CHUNK = 16
MLSTM_BATCH_PER_STEP = 1
DISPATCH_TILE_CAP = 1024
MOE_TILE = 256
PAGES_PER_STEP = 32
ROUTE_ROWS = 32
ROUTE_GROUP_ROW = 0
ROUTE_EXPERT_ROW = 8


def _cparams(*sem):
    return pltpu.CompilerParams(dimension_semantics=sem, vmem_limit_bytes=VMEM_LIMIT)


def _dot(a, b):
    return jnp.dot(a, b, preferred_element_type=F32)


def _dot_nt(a, b):
    return lax.dot_general(a, b, (((1,), (1,)), ((), ())), preferred_element_type=F32)


def _layer_norm_rows(z, g, b):
    mu = jnp.mean(z, axis=-1, keepdims=True)
    zc = z - mu
    var = jnp.mean(zc * zc, axis=-1, keepdims=True)
    return zc * lax.rsqrt(var + LN_EPS) * g + b


def _log_sigmoid(x):
    return jnp.minimum(x, 0.0) - jnp.log1p(jnp.exp(-jnp.abs(x)))


def _gated_conv(gate_b, u, um1, um2, cw_ref):
    cw = cw_ref[...]
    conv = um2 * cw[0:1, :] + um1 * cw[1:2, :] + u * cw[2:3, :]
    return (gate_b * conv).astype(BF16)


def _rope(z, cos, sin_lo, sin_hi, axis):
    half = ROT_DIM // 2
    return z * cos + pltpu.roll(z, QK_B - half, axis) * sin_lo + pltpu.roll(z, half, axis) * sin_hi


def _even_in_prompt_kernel(x_ref, w_ref, wqt_ref, cw_ref, prev_ref, rc_ref, rs1_ref, rs2_ref,
                           rct_ref, rs1t_ref, rs2t_ref,
                           yc_ref, qt_ref, kf_ref, vf_ref, kb_ref, vt_ref, u_ref, carry_ref):
    xb = x_ref[0].astype(BF16)
    tm = xb.shape[0]

    def proj(c0, n):
        return _dot(xb, w_ref[:, c0:c0 + n])

    gate_b = proj(0, D_CONV)
    u = proj(D_CONV, D_CONV) * proj(2 * D_CONV, D_CONV)
    j = pl.program_id(1)

    @pl.when(j == 0)
    def _():
        carry_ref[...] = prev_ref[0]

    row = lax.broadcasted_iota(I32, u.shape, 0)
    c2 = carry_ref[0:1, :]
    c1 = carry_ref[1:2, :]
    um1 = jnp.where(row == 0, c1, pltpu.roll(u, 1, 0))
    um2 = jnp.where(row == 0, c2, jnp.where(row == 1, c1, pltpu.roll(u, 2, 0)))
    carry_ref[...] = u[tm - 2:tm, :]
    u_ref[0] = u[tm - 2:tm, :]
    yc_ref[0] = _gated_conv(gate_b, u, um1, um2, cw_ref)

    reps = QK_B // LANES
    tile = lambda r, ax: jnp.concatenate([r[...]] * reps, axis=ax)
    k = _rope(proj(3 * D_CONV + QK_B, QK_B), tile(rc_ref, 1), tile(rs1_ref, 1), tile(rs2_ref, 1), 1)
    kf_ref[0] = k
    kb_ref[0] = k.astype(BF16)
    v = proj(3 * D_CONV + 2 * QK_B, V_B)
    for h in range(H_B):
        vf_ref[0, pl.ds(h, tm, stride=H_B), :] = v[:, h * 2 * DH_B:(h + 1) * 2 * DH_B]
    qt = _rope(_dot_nt(wqt_ref[...], xb), tile(rct_ref, 0), tile(rs1t_ref, 0), tile(rs2t_ref, 0), 0)
    qt_ref[0] = (qt * (DH_B ** -0.5 * LOG2_E)).astype(BF16)
    vt_ref[0] = jnp.transpose(v).astype(BF16)


def _even_in_decode_kernel(x_ref, w_ref, cw_ref, prev_ref, rc_ref, rs1_ref, rs2_ref,
                           yc_ref, q_ref, kf_ref, vf_ref, u_ref):
    xb = x_ref[...].astype(BF16)

    def proj(c0, n):
        return _dot(xb, w_ref[:, c0:c0 + n])

    gate_b = proj(0, D_CONV)
    u = proj(D_CONV, D_CONV) * proj(2 * D_CONV, D_CONV)
    u_ref[...] = u
    yc_ref[...] = _gated_conv(gate_b, u, prev_ref[1], prev_ref[0], cw_ref)
    reps = QK_B // LANES
    tile = lambda r: jnp.concatenate([r[...]] * reps, axis=1)
    cos, sin_lo, sin_hi = tile(rc_ref), tile(rs1_ref), tile(rs2_ref)
    q_ref[...] = (_rope(proj(3 * D_CONV, QK_B), cos, sin_lo, sin_hi, 1) * (DH_B ** -0.5)).astype(BF16)
    kf_ref[...] = _rope(proj(3 * D_CONV + QK_B, QK_B), cos, sin_lo, sin_hi, 1)
    vf_ref[...] = proj(3 * D_CONV + 2 * QK_B, V_B)


def _rope_tables(pos):
    half = ROT_DIM // 2
    inv = ROPE_THETA ** (-jnp.arange(half, dtype=F32) / half)
    ang = pos.astype(F32)[:, None] * inv[None, :]
    cos, sin = jnp.cos(ang), jnp.sin(ang)
    t = pos.shape[0]
    ones = jnp.ones((t, DH_B - ROT_DIM), F32)
    zeros = jnp.zeros((t, DH_B - ROT_DIM), F32)
    zh = jnp.zeros((t, half), F32)
    c = jnp.concatenate([cos, cos, ones], axis=1)
    s_lo = jnp.concatenate([-sin, zh, zeros], axis=1)
    s_hi = jnp.concatenate([zh, sin, zeros], axis=1)
    tile2 = lambda a: jnp.concatenate([a, a], axis=1)
    return tile2(c), tile2(s_lo), tile2(s_hi)


def _even_in_prompt(x, w_bf, wqt_bf, conv_w, conv_prev, tables):
    b, t, _ = x.shape
    tm = ROW_TILE
    row3 = lambda n: pl.BlockSpec((1, tm, n), lambda i, j: (i, j, 0))
    col3 = lambda n: pl.BlockSpec((1, n, tm), lambda i, j: (i, 0, j))
    full2 = lambda a: pl.BlockSpec(a.shape, lambda i, j: (0, 0))
    tab = pl.BlockSpec((tm, LANES), lambda i, j: (j, 0))
    tab_t = pl.BlockSpec((LANES, tm), lambda i, j: (0, j))
    st = pl.BlockSpec((1, CONV_W - 1, D_CONV), lambda i, j: (i, 0, 0))
    tables_t = [jnp.transpose(a) for a in tables]
    outs = [jax.ShapeDtypeStruct((b, t, D_CONV), BF16), jax.ShapeDtypeStruct((b, QK_B, t), BF16),
            jax.ShapeDtypeStruct((b, t, QK_B), F32), jax.ShapeDtypeStruct((b, t * H_B, 2 * DH_B), F32),
            jax.ShapeDtypeStruct((b, t, QK_B), BF16), jax.ShapeDtypeStruct((b, V_B, t), BF16),
            jax.ShapeDtypeStruct((b, CONV_W - 1, D_CONV), F32)]
    return pl.pallas_call(
        _even_in_prompt_kernel,
        grid=(b, t // tm),
        in_specs=[row3(D_MODEL), full2(w_bf), full2(wqt_bf), full2(conv_w), st,
                  tab, tab, tab, tab_t, tab_t, tab_t],
        out_specs=[row3(D_CONV), col3(QK_B), row3(QK_B),
                   pl.BlockSpec((1, tm * H_B, 2 * DH_B), lambda i, j: (i, j, 0)), row3(QK_B), col3(V_B), st],
        out_shape=outs,
        scratch_shapes=[pltpu.VMEM((CONV_W - 1, D_CONV), F32)],
        compiler_params=_cparams("arbitrary", "arbitrary"),
        name="even_in_prompt",
    )(x, w_bf, wqt_bf, conv_w, conv_prev, *tables, *tables_t)


def _even_in_decode(x, w_bf, conv_w, conv_prev_t, tables):
    n = x.shape[0]
    full = lambda a: pl.BlockSpec(a.shape, lambda i: (0,) * a.ndim)
    o2 = lambda c, dt: jax.ShapeDtypeStruct((n, c), dt)
    outs = [o2(D_CONV, BF16), o2(QK_B, BF16), o2(QK_B, F32), o2(V_B, F32), o2(D_CONV, F32)]
    ins = [x, w_bf, conv_w, conv_prev_t, *tables]
    return pl.pallas_call(
        _even_in_decode_kernel,
        grid=(1,),
        in_specs=[full(a) for a in ins],
        out_specs=[pl.BlockSpec(o.shape, lambda i: (0, 0)) for o in outs],
        out_shape=outs,
        compiler_params=_cparams("arbitrary"),
        name="even_in_decode",
    )(*ins)


def _lambda_value(lam_ref, lam_init):
    lv = lam_ref[...]
    a = jnp.sum(lv[0:1, :] * lv[1:2, :], axis=1, keepdims=True)
    b = jnp.sum(lv[2:3, :] * lv[3:4, :], axis=1, keepdims=True)
    return jnp.exp(a) - jnp.exp(b) + lam_init


def _sub_norm(o, sub_ref, lam_init):
    ms = jnp.mean(o * o, axis=-1, keepdims=True)
    return o * lax.rsqrt(ms + LN_EPS) * sub_ref[...] * (1.0 - lam_init)


def _attn_prompt_kernel(qt_ref, k_ref, vt_ref, lam_ref, sub_ref, o_ref, *scratch, lam_init):
    i = pl.program_id(2)
    tq = qt_ref.shape[2]
    tk = tq
    n_heads = qt_ref.shape[1] // LANES
    m_scrs, acc_scrs = scratch[:n_heads], scratch[n_heads:]
    for m_scr, acc_scr in zip(m_scrs, acc_scrs):
        m_scr[...] = jnp.full(m_scr.shape, -jnp.inf, F32)
        acc_scr[...] = jnp.zeros(acc_scr.shape, F32)
    ones = jnp.ones((ONES_ROWS, tk), BF16)
    qqs = []
    for h in range(n_heads):
        qt = qt_ref[0, h * LANES:(h + 1) * LANES, :].astype(F32)
        feat = lax.broadcasted_iota(I32, qt.shape, 0)
        qqs.append(jnp.concatenate([jnp.where(feat < DH_B, qt, 0.0), jnp.where(feat >= DH_B, qt, 0.0)],
                                   axis=1).astype(BF16))

    def step(j, masked):
        start = pl.multiple_of(j * tk, tk)
        scores = [_dot(k_ref[0, pl.ds(start, tk), h * LANES:(h + 1) * LANES], qqs[h])
                  for h in range(n_heads)]
        probs, alphas = [], []
        for h in range(n_heads):
            s = scores[h]
            if masked:
                key = lax.broadcasted_iota(I32, s.shape, 0)
                qry = lax.broadcasted_iota(I32, s.shape, 1)
                qry = jnp.where(qry >= tq, qry - tq, qry)
                s = jnp.where(key <= qry, s, -jnp.inf)
            m_prev = m_scrs[h][...]
            m_new = jnp.maximum(m_prev, jnp.max(s, axis=0, keepdims=True))
            alphas.append(jnp.exp2(m_prev - m_new))
            probs.append(jnp.exp2(s - m_new).astype(BF16))
            m_scrs[h][...] = m_new
        for h in range(n_heads):
            vtj = jnp.concatenate([vt_ref[0, h * LANES:(h + 1) * LANES, pl.ds(start, tk)], ones], axis=0)
            acc_scrs[h][...] = alphas[h] * acc_scrs[h][...] + _dot(vtj, probs[h])

    def body(j, carry):
        step(j, False)
        return carry

    lax.fori_loop(0, i, body, 0)
    step(i, True)
    lam = _lambda_value(lam_ref, lam_init)
    for h in range(n_heads):
        on = acc_scrs[h][0:LANES, :] / acc_scrs[h][LANES:LANES + 1, :]
        o = jnp.transpose(on[:, 0:tq] - lam * on[:, tq:2 * tq])
        o_ref[0, :, h * LANES:(h + 1) * LANES] = _sub_norm(o, sub_ref, lam_init).astype(BF16)


def _attn_prompt(qt, k, vt, lam_vecs, subln, lam_init):
    b, t, _ = k.shape
    tq = ATTN_TILE
    nh = ATTN_HEADS_PER_STEP
    hw = nh * LANES
    full = lambda a: pl.BlockSpec(a.shape, lambda bi, h, i: (0, 0))
    return pl.pallas_call(
        functools.partial(_attn_prompt_kernel, lam_init=lam_init),
        grid=(b, H_B // nh, t // tq),
        in_specs=[pl.BlockSpec((1, hw, tq), lambda bi, h, i: (bi, h, i)),
                  pl.BlockSpec((1, t, hw), lambda bi, h, i: (bi, 0, h)),
                  pl.BlockSpec((1, hw, t), lambda bi, h, i: (bi, h, 0)),
                  full(lam_vecs), full(subln)],
        out_specs=pl.BlockSpec((1, tq, hw), lambda bi, h, i: (bi, i, h)),
        out_shape=jax.ShapeDtypeStruct((b, t, V_B), BF16),
        scratch_shapes=[pltpu.VMEM((1, 2 * tq), F32)] * nh + [pltpu.VMEM((LANES + ONES_ROWS, 2 * tq), F32)] * nh,
        compiler_params=_cparams("arbitrary", "arbitrary", "arbitrary"),
        name="attn_prompt",
    )(qt, k, vt, lam_vecs, subln)


def _attn_decode_kernel(pt_ref, qbd_ref, kn_ref, vn_ref, lam_ref, sub_ref, *rest, lam_init, n_pages):
    del pt_ref
    k_refs = rest[:n_pages]
    v_refs = rest[n_pages:2 * n_pages]
    o_ref, m_scr, l_scr, acc_scr = rest[2 * n_pages:]
    j = pl.program_id(1)
    qbd = qbd_ref[0]

    @pl.when(j == 0)
    def _():
        s_new = jnp.sum(qbd.astype(F32) * kn_ref[0], axis=1, keepdims=True)
        m_scr[...] = s_new
        l_scr[...] = jnp.ones(l_scr.shape, F32)
        acc_scr[...] = jnp.broadcast_to(vn_ref[0], acc_scr.shape)

    s = jnp.concatenate([_dot(qbd, k_refs[r][0].astype(BF16)) for r in range(n_pages)], axis=1)
    m_prev = m_scr[...]
    m_new = jnp.maximum(m_prev, jnp.max(s, axis=1, keepdims=True))
    alpha = jnp.exp(m_prev - m_new)
    p = jnp.exp(s - m_new)
    l_scr[...] = alpha * l_scr[...] + jnp.sum(p, axis=1, keepdims=True)

    def head_pv(h):
        acc = None
        for r in range(n_pages):
            vh = v_refs[r][0, pl.ds(h, PAGE_SIZE, stride=H_B), :].astype(BF16)
            term = _dot(p[:, r * PAGE_SIZE:(r + 1) * PAGE_SIZE].astype(BF16), vh)
            acc = term if acc is None else acc + term
        return acc

    pv = jnp.concatenate([head_pv(h) for h in range(H_B)], axis=1)
    acc_scr[...] = alpha * acc_scr[...] + pv
    m_scr[...] = m_new

    @pl.when(j == pl.num_programs(1) - 1)
    def _():
        on = acc_scr[...] / l_scr[...]
        row = lax.broadcasted_iota(I32, (2 * H_B, 2 * DH_B), 0)
        head = jnp.where(row >= H_B, row - H_B, row)
        o8 = jnp.zeros((2 * H_B, 2 * DH_B), F32)
        for c in range(H_B):
            o8 = o8 + jnp.where(head == c, on[:, c * 2 * DH_B:(c + 1) * 2 * DH_B], 0.0)
        lam = _lambda_value(lam_ref, lam_init)
        o = o8 - lam * pltpu.roll(o8, H_B, 0)
        o_ref[0] = _sub_norm(o, sub_ref, lam_init)


def _attn_decode(qbd, k_new, v_new, cache_k, cache_v, page_table, lam_vecs, subln, lam_init):
    n = qbd.shape[0]
    n_pages = page_table.shape[1]
    pp = PAGES_PER_STEP
    width = V_B
    c2 = lambda a: pl.BlockSpec(a.shape, lambda b, j, pt: (0, 0))
    per_b = lambda a: pl.BlockSpec((1,) + a.shape[1:], lambda b, j, pt: (b, 0, 0))

    def page(r, arr):
        return pl.BlockSpec((1,) + arr.shape[1:], lambda b, j, pt: (pt[b, j * pp + r], 0, 0))

    grid_spec = pltpu.PrefetchScalarGridSpec(
        num_scalar_prefetch=1,
        grid=(n, n_pages // pp),
        in_specs=[per_b(qbd), per_b(k_new), per_b(v_new), c2(lam_vecs), c2(subln)]
        + [page(r, cache_k) for r in range(pp)] + [page(r, cache_v) for r in range(pp)],
        out_specs=pl.BlockSpec((1, 2 * H_B, 2 * DH_B), lambda b, j, pt: (b, 0, 0)),
        scratch_shapes=[pltpu.VMEM((2 * H_B, 1), F32), pltpu.VMEM((2 * H_B, 1), F32),
                        pltpu.VMEM((2 * H_B, width), F32)],
    )
    return pl.pallas_call(
        functools.partial(_attn_decode_kernel, lam_init=lam_init, n_pages=pp),
        grid_spec=grid_spec,
        out_shape=jax.ShapeDtypeStruct((n, 2 * H_B, 2 * DH_B), F32),
        compiler_params=_cparams("arbitrary", "arbitrary"),
        name="attn_decode",
    )(page_table, qbd, k_new, v_new, lam_vecs, subln, *([cache_k] * pp), *([cache_v] * pp))


def _odd_in_kernel(x_ref, w_ref, wg_ref, wgt_ref, bg_ref, bgt_ref,
                   q_ref, k_ref, v_ref, o_ref, gc_ref, gr_ref, kt_ref, *, decode):
    xb = x_ref[0].astype(BF16) if not decode else x_ref[...].astype(BF16)
    qw = H_C * DK_C
    vw = H_C * DV_C
    q = _dot(xb, w_ref[:, 0:qw]) * (DK_C ** -0.5)
    k = _dot(xb, w_ref[:, qw:2 * qw])
    v = _dot(xb, w_ref[:, 2 * qw:2 * qw + vw])
    o = _dot(xb, w_ref[:, 2 * qw + vw:2 * qw + 2 * vw])
    g_col = _dot(xb, wg_ref[...]) + bg_ref[...]
    lane = lax.broadcasted_iota(I32, g_col.shape, 1)
    g_col = jnp.where(lane < H_C, g_col, _log_sigmoid(g_col))
    k_t = jnp.transpose(k).astype(BF16)
    g_row = _dot_nt(wgt_ref[...], xb) + bgt_ref[:, 0:1]
    sub = lax.broadcasted_iota(I32, g_row.shape, 0)
    g_row = jnp.where(sub < H_C, g_row, _log_sigmoid(g_row))
    if decode:
        kt_ref[...] = k_t
        q_ref[...] = q.astype(BF16)
        k_ref[...] = k.astype(BF16)
        v_ref[...] = v.astype(BF16)
        o_ref[...] = o
        gc_ref[...] = g_col
        gr_ref[...] = g_row
    else:
        q_ref[0] = q.astype(BF16)
        k_ref[0] = k.astype(BF16)
        v_ref[0] = v.astype(BF16)
        o_ref[0] = o
        gc_ref[0] = g_col
        gr_ref[0] = g_row
        kt_ref[0] = k_t


def _odd_in(x, w_bf, wg, wgt, bg, bgt, decode):
    qw, vw = H_C * DK_C, H_C * DV_C
    if decode:
        n = x.shape[0]
        ins = [x, w_bf, wg, wgt, bg, bgt]
        outs = [jax.ShapeDtypeStruct((n, qw), BF16), jax.ShapeDtypeStruct((n, qw), BF16),
                jax.ShapeDtypeStruct((n, vw), BF16), jax.ShapeDtypeStruct((n, vw), F32),
                jax.ShapeDtypeStruct((n, LANES), F32), jax.ShapeDtypeStruct((2 * H_C, n), F32),
                jax.ShapeDtypeStruct((qw, n), BF16)]
        return pl.pallas_call(
            functools.partial(_odd_in_kernel, decode=True),
            grid=(1,),
            in_specs=[pl.BlockSpec(a.shape, lambda i: (0, 0)) for a in ins],
            out_specs=[pl.BlockSpec(o.shape, lambda i: (0, 0)) for o in outs],
            out_shape=outs,
            compiler_params=_cparams("arbitrary"),
            name="odd_in_decode",
        )(*ins)
    b, t, _ = x.shape
    tm = ROW_TILE
    row3 = lambda n: pl.BlockSpec((1, tm, n), lambda i, j: (i, j, 0))
    full2 = lambda a: pl.BlockSpec(a.shape, lambda i, j: (0, 0))
    outs = [jax.ShapeDtypeStruct((b, t, qw), BF16), jax.ShapeDtypeStruct((b, t, qw), BF16),
            jax.ShapeDtypeStruct((b, t, vw), BF16), jax.ShapeDtypeStruct((b, t, vw), F32),
            jax.ShapeDtypeStruct((b, t, LANES), F32), jax.ShapeDtypeStruct((b, 2 * H_C, t), F32),
            jax.ShapeDtypeStruct((b, qw, t), BF16)]
    return pl.pallas_call(
        functools.partial(_odd_in_kernel, decode=False),
        grid=(b, t // tm),
        in_specs=[row3(D_MODEL), full2(w_bf), full2(wg), full2(wgt), full2(bg), full2(bgt)],
        out_specs=[row3(qw), row3(qw), row3(vw), row3(vw), row3(LANES),
                   pl.BlockSpec((1, 2 * H_C, tm), lambda i, j: (i, 0, j)),
                   pl.BlockSpec((1, qw, tm), lambda i, j: (i, 0, j))],
        out_shape=outs,
        compiler_params=_cparams("arbitrary", "arbitrary"),
        name="odd_in_prompt",
    )(x, w_bf, wg, wgt, bg, bgt)


def _mlstm_kernel(q_ref, k_ref, kt_ref, v_ref, o_ref, gc_ref, gr_ref, nw_ref, c0_ref, n0_ref, m0_ref,
                  h_ref, c_out, n_out, m_out, c_scr, n_scr, m_scr):
    ci = pl.program_id(1)
    chunk = q_ref.shape[1]

    @pl.when(ci == 0)
    def _():
        c_scr[...] = c0_ref[...]
        n_scr[...] = n0_ref[...]
        m_scr[...] = m0_ref[...]

    t_idx = lax.broadcasted_iota(I32, (chunk, chunk), 0)
    s_idx = lax.broadcasted_iota(I32, (chunk, chunk), 1)
    causal = s_idx <= t_idx
    for bb, h in [(bb, h) for bb in range(q_ref.shape[0]) for h in range(H_C)]:
        q = q_ref[bb, :, h * DK_C:(h + 1) * DK_C]
        k = k_ref[bb, :, h * DK_C:(h + 1) * DK_C]
        v = v_ref[bb, :, h * DV_C:(h + 1) * DV_C]
        ig_r = gr_ref[bb, h:h + 1, :]
        lf_r = gr_ref[bb, H_C + h:H_C + h + 1, :]
        ig_c = gc_ref[bb, :, h:h + 1]
        lf_c = gc_ref[bb, :, H_C + h:H_C + h + 1]
        bcum_c = jnp.sum(jnp.where(causal, lf_r, 0.0), axis=1, keepdims=True)
        bcum_r = jnp.sum(jnp.where(t_idx <= s_idx, lf_c, 0.0), axis=0, keepdims=True)
        m0 = m_scr[bb, h:h + 1, 0:1]
        dmat = jnp.where(causal, bcum_c - bcum_r + ig_r, -jnp.inf)
        inter = bcum_c + m0
        m = jnp.maximum(inter, jnp.max(dmat, axis=1, keepdims=True))
        w = jnp.exp(dmat - m)
        g = jnp.exp(inter - m)
        s = _dot_nt(q, k) * w
        c0 = c_scr[bb, h]
        n0 = n_scr[bb, h:h + 1, :]
        num = g * _dot(q, c0.astype(BF16)) + _dot(s.astype(BF16), v)
        den = g * jnp.sum(q.astype(F32) * n0, axis=1, keepdims=True) + jnp.sum(s, axis=1, keepdims=True)
        hid = num / jnp.maximum(jnp.abs(den), jnp.exp(-m))
        m_last = m[chunk - 1:chunk, :]
        b_last = bcum_c[chunk - 1:chunk, :]
        w_last = jnp.exp(b_last - bcum_c + ig_c - m_last)
        g_last = jnp.exp(b_last + m0 - m_last)
        kw = k.astype(F32) * w_last
        w_last_r = jnp.exp(b_last - bcum_r + ig_r - m_last)
        kw_t = (kt_ref[bb, h * DK_C:(h + 1) * DK_C, :].astype(F32) * w_last_r).astype(BF16)
        c_scr[bb, h] = g_last * c0 + _dot(kw_t, v)
        n_scr[bb, h:h + 1, :] = g_last * n0 + jnp.sum(kw, axis=0, keepdims=True)
        m_scr[bb, h:h + 1, :] = jnp.broadcast_to(m_last, (1, LANES))
        mu = jnp.mean(hid, axis=1, keepdims=True)
        hc = hid - mu
        var = jnp.mean(hc * hc, axis=1, keepdims=True)
        hn = hc * lax.rsqrt(var + LN_EPS) * nw_ref[:, h * DV_C:(h + 1) * DV_C]
        gate = jax.nn.sigmoid(o_ref[bb, :, h * DV_C:(h + 1) * DV_C])
        h_ref[bb, :, h * DV_C:(h + 1) * DV_C] = (gate * hn).astype(BF16)

    c_out[...] = c_scr[...]
    n_out[...] = n_scr[...]
    m_out[...] = m_scr[...]


def _mlstm(q, k, k_t, v, o, g_col, g_row, norm_w, c0, n0, m0, ch):
    b, t, _ = q.shape
    nb = MLSTM_BATCH_PER_STEP
    qw, vw = H_C * DK_C, H_C * DV_C
    row = lambda n: pl.BlockSpec((nb, ch, n), lambda i, j: (i, j, 0))
    st4 = pl.BlockSpec((nb, H_C, DK_C, DV_C), lambda i, j: (i, 0, 0, 0))
    st3 = pl.BlockSpec((nb, H_C, LANES), lambda i, j: (i, 0, 0))
    outs = [jax.ShapeDtypeStruct((b, t, vw), BF16), jax.ShapeDtypeStruct((b, H_C, DK_C, DV_C), F32),
            jax.ShapeDtypeStruct((b, H_C, DK_C), F32), jax.ShapeDtypeStruct((b, H_C, LANES), F32)]
    return pl.pallas_call(
        _mlstm_kernel,
        grid=(b // nb, t // ch),
        in_specs=[row(qw), row(qw), pl.BlockSpec((nb, qw, ch), lambda i, j: (i, 0, j)), row(vw), row(vw),
                  row(LANES), pl.BlockSpec((nb, 2 * H_C, ch), lambda i, j: (i, 0, j)),
                  pl.BlockSpec(norm_w.shape, lambda i, j: (0, 0)), st4, st3, st3],
        out_specs=[row(vw), st4, st3, st3],
        out_shape=outs,
        scratch_shapes=[pltpu.VMEM((nb, H_C, DK_C, DV_C), F32), pltpu.VMEM((nb, H_C, DK_C), F32),
                        pltpu.VMEM((nb, H_C, LANES), F32)],
        compiler_params=_cparams("arbitrary", "arbitrary"),
        name="mlstm",
    )(q, k, k_t, v, o, g_col, g_row, norm_w, c0, n0, m0)


def _mix_project(acts, w_refs):
    y = _dot(acts[0], w_refs[0][...])
    for a, w in zip(acts[1:], w_refs[1:]):
        y = y + _dot(a, w[...])
    return y


def _mixed_rows(acts, w_refs, x, g_ref, b_ref):
    return _layer_norm_rows(DEEPNORM_ALPHA * x + _mix_project(acts, w_refs), g_ref[...], b_ref[...])


def _route_logits(x1, rw_ref):
    xh = x1.astype(BF16)
    xl = (x1 - xh.astype(F32)).astype(BF16)
    both = _dot_nt(rw_ref[...], xh)
    return both[0:ROUTE_ROWS, :] + both[ROUTE_ROWS:2 * ROUTE_ROWS, :] + _dot_nt(rw_ref[0:ROUTE_ROWS, :], xl)


def _route(x1, rw_ref):
    return _route_decide(_route_logits(x1, rw_ref))


def _route_decide(lg):
    sub = lax.broadcasted_iota(I32, lg.shape, 0)
    big = jnp.int32(4 * ROUTE_ROWS)
    neg = -jnp.inf
    gl = jnp.where(sub < ROUTE_GROUP_ROW + N_GROUPS, lg, neg)
    g_max = jnp.max(gl, axis=0, keepdims=True)
    g_w = 1.0 / jnp.sum(jnp.exp(gl - g_max), axis=0, keepdims=True)
    g_idx = jnp.min(jnp.where(gl == g_max, sub, big), axis=0, keepdims=True)
    row_group = (sub - ROUTE_EXPERT_ROW) >> 2
    el = jnp.where(row_group == g_idx, lg, neg)
    e1 = jnp.max(el, axis=0, keepdims=True)
    i1 = jnp.min(jnp.where(el == e1, sub, big), axis=0, keepdims=True)
    z = jnp.sum(jnp.exp(el - e1), axis=0, keepdims=True)
    el2 = jnp.where(sub == i1, neg, el)
    e2 = jnp.max(el2, axis=0, keepdims=True)
    i2 = jnp.min(jnp.where(el2 == e2, sub, big), axis=0, keepdims=True)
    p1 = 1.0 / z
    p2 = jnp.exp(e2 - e1) / z
    w1 = p1 / (p1 + p2) * g_w
    w2 = p2 / (p1 + p2) * g_w
    id1 = (i1 - ROUTE_EXPERT_ROW).astype(F32)
    id2 = (i2 - ROUTE_EXPERT_ROW).astype(F32)
    return jnp.where(sub == 0, w1, jnp.where(sub == 1, w2, jnp.where(sub == 2, id1, jnp.where(sub == 3, id2, 0.0))))


def _slab_columns(rt):
    pad = jnp.zeros((LANES - rt.shape[0], rt.shape[1]), F32)
    return jnp.transpose(jnp.concatenate([rt, pad], axis=0))


def _mix_out_kernel(*refs, n_in, n_s):
    ap_refs = refs[:n_in]
    as_refs = refs[n_in:2 * n_in]
    w_refs = refs[2 * n_in:3 * n_in]
    xp_ref, xs_ref, g_ref, b_ref, rw_ref, out_ref, rt_ref = refs[3 * n_in:]
    i = pl.program_id(0)
    last = pl.num_programs(0) - 1

    @pl.when(i < last)
    def _():
        x1 = _mixed_rows([a[...] for a in ap_refs], w_refs, xp_ref[...], g_ref, b_ref)
        rt = _route(x1, rw_ref)
        out_ref[:, 0:D_MODEL] = x1
        out_ref[:, D_MODEL:D_MODEL + LANES] = _slab_columns(rt)
        rt_ref[...] = rt[0:8, :]

    @pl.when(i == last)
    def _():
        x1 = _mixed_rows([a[...] for a in as_refs], w_refs, xs_ref[...], g_ref, b_ref)
        rt = _route(jnp.concatenate([x1, jnp.zeros((LANES - n_s, D_MODEL), F32)], axis=0), rw_ref)
        out_ref[0:n_s, 0:D_MODEL] = x1
        out_ref[0:n_s, D_MODEL:D_MODEL + LANES] = _slab_columns(rt)[0:n_s, :]
        rt_ref[...] = jnp.zeros(rt_ref.shape, F32)
        rt_ref[:, 0:LANES] = rt[0:8, :]


def _mix_out(acts_p, acts_s, weights, xp, xs, ln_g, ln_b, rw):
    n_p, n_s = xp.shape[0], xs.shape[0]
    assert n_s <= LANES
    tm = ROW_TILE
    nb = n_p // tm
    n_in = len(acts_p)
    prow = lambda n: pl.BlockSpec((tm, n), lambda i: (jnp.minimum(i, nb - 1), 0))
    full = lambda a: pl.BlockSpec(a.shape, lambda i: (0, 0))
    width = D_MODEL + LANES
    return pl.pallas_call(
        functools.partial(_mix_out_kernel, n_in=n_in, n_s=n_s),
        grid=(nb + 1,),
        in_specs=[prow(a.shape[1]) for a in acts_p] + [full(a) for a in acts_s] + [full(w) for w in weights]
        + [prow(D_MODEL), full(xs), full(ln_g), full(ln_b), full(rw)],
        out_specs=[pl.BlockSpec((tm, width), lambda i: (i, 0)), pl.BlockSpec((8, tm), lambda i: (0, i))],
        out_shape=[jax.ShapeDtypeStruct((n_p + n_s, width), F32),
                   jax.ShapeDtypeStruct((8, (nb + 1) * tm), F32)],
        compiler_params=_cparams("arbitrary"),
        name="mix_out",
    )(*acts_p, *acts_s, *weights, xp, xs, ln_g, ln_b, rw)


def _route_plan(rt, tm):
    n = rt.shape[1]
    e1, e2 = rt[2, :].astype(I32), rt[3, :].astype(I32)
    ea, eb = jnp.minimum(e1, e2), jnp.maximum(e1, e2)
    la, lb = ea % EXP_PER_GROUP, eb % EXP_PER_GROUP
    pair_rank = sum(k * ((la == a) & (lb == b)).astype(I32) for k, (a, b) in enumerate(PAIR_SEQ))
    cls = (ea // EXP_PER_GROUP) * N_PAIRS + pair_rank
    onehot = (cls[:, None] == jnp.arange(N_CLASSES, dtype=I32)[None, :]).astype(I32)
    csum = jnp.cumsum(onehot, axis=0)
    rank = jnp.sum(onehot * csum, axis=1) - 1
    cnt = csum[-1]
    ntile = (cnt + tm - 1) // tm
    tile_end = jnp.cumsum(ntile)
    tile_start = tile_end - ntile
    n_used = tile_end[-1]
    n_tiles = -(-(n + N_CLASSES * (tm - 1)) // tm)
    pos = (jnp.sum(onehot * tile_start[None, :], axis=1) * tm + rank).astype(I32)
    pair_lo = np.array([a for a, _ in PAIR_SEQ], np.int32)
    pair_hi = np.array([b for _, b in PAIR_SEQ], np.int32)
    cls_ids = np.arange(N_CLASSES)
    cls_a = jnp.asarray((cls_ids // N_PAIRS) * EXP_PER_GROUP + pair_lo[cls_ids % N_PAIRS], I32)
    cls_b = jnp.asarray((cls_ids // N_PAIRS) * EXP_PER_GROUP + pair_hi[cls_ids % N_PAIRS], I32)
    tile_ids = jnp.arange(n_tiles, dtype=I32)
    tile_cls = jnp.sum((tile_end[None, :] <= jnp.minimum(tile_ids, n_used - 1)[:, None]).astype(I32), axis=1)
    tile_cls = jnp.minimum(tile_cls, N_CLASSES - 1)
    onehot_t = (tile_cls[:, None] == jnp.arange(N_CLASSES, dtype=I32)[None, :]).astype(I32)
    last_tile = jnp.where(ntile > 0, tile_end - 1, -1).astype(I32)
    ta = jnp.sum(onehot_t * cls_a[None, :], axis=1)
    tb = jnp.sum(onehot_t * cls_b[None, :], axis=1)
    first = jnp.ones((1,), I32)
    last = jnp.zeros((1,), I32)
    chg = (jnp.concatenate([first, (ta[1:] != ta[:-1]).astype(I32), last])
           + 2 * jnp.concatenate([first, (tb[1:] != tb[:-1]).astype(I32), last]))
    return dict(pos=pos, ta=ta, tb=tb, nu=n_used.reshape(1).astype(I32), chg=chg, last_tile=last_tile,
                n_tiles=n_tiles)


def _dispatch_kernel(lt_ref, nu_ref, pos_ref, src_ref, dst_hbm, zbuf, zsem, rsem, *, moe_tile, n_tiles):
    i = pl.program_id(0)
    td = pos_ref.shape[2]

    def zero_copy(tile):
        start = pl.multiple_of(tile * moe_tile, moe_tile)
        return pltpu.make_async_copy(zbuf, dst_hbm.at[pl.ds(start, moe_tile)], zsem)

    @pl.when(i == 0)
    def _():
        zbuf[...] = jnp.zeros(zbuf.shape, F32)
        for c in range(N_CLASSES):
            @pl.when(lt_ref[c] >= 0)
            def _(c=c):
                zero_copy(lt_ref[c]).start()
        for c in range(N_CLASSES):
            @pl.when(lt_ref[c] >= 0)
            def _(c=c):
                zero_copy(lt_ref[c]).wait()

        def spare_start(t, carry):
            zero_copy(t).start()
            return carry

        def spare_wait(t, carry):
            zero_copy(t).wait()
            return carry

        lax.fori_loop(nu_ref[0], n_tiles, spare_start, 0)
        lax.fori_loop(nu_ref[0], n_tiles, spare_wait, 0)

    for r in range(td):
        pltpu.make_async_copy(src_ref.at[pl.ds(r, 1)], dst_hbm.at[pl.ds(pos_ref[0, 0, r], 1)],
                              rsem).start(priority=r % 2)
    pltpu.make_async_copy(src_ref, dst_hbm.at[pl.ds(0, td)], rsem).wait()


def _largest_divisor_tile(n, cap):
    for t in range(cap - cap % 8, 7, -8):
        if n % t == 0:
            return t
    raise ValueError(f"no row tile for {n} rows")


def _dispatch(x1e, plan, tm):
    n, width = x1e.shape
    n_tiles = plan["n_tiles"]
    td = _largest_divisor_tile(n, DISPATCH_TILE_CAP)
    steps = n // td
    grid_spec = pltpu.PrefetchScalarGridSpec(
        num_scalar_prefetch=2,
        grid=(steps,),
        in_specs=[pl.BlockSpec((1, 1, td), lambda i, lt, nu: (i, 0, 0), memory_space=pltpu.SMEM),
                  pl.BlockSpec((td, width), lambda i, lt, nu: (i, 0))],
        out_specs=pl.BlockSpec(memory_space=pl.ANY),
        scratch_shapes=[pltpu.VMEM((tm, width), F32), pltpu.SemaphoreType.DMA(()),
                        pltpu.SemaphoreType.DMA(())],
    )
    return pl.pallas_call(
        functools.partial(_dispatch_kernel, moe_tile=tm, n_tiles=n_tiles),
        grid_spec=grid_spec,
        out_shape=jax.ShapeDtypeStruct((n_tiles * tm, width), F32),
        compiler_params=_cparams("arbitrary"),
        name="dispatch",
    )(plan["last_tile"], plan["nu"], plan["pos"].reshape(steps, 1, td), x1e)


def _moe_kernel(ta_ref, tb_ref, nu_ref, chg_ref, x_ref, ga_f32, ua_f32, da_f32, gb_f32, ub_f32, db_f32,
                lg_ref, lb_ref, o_ref, ga_ref, ua_ref, da_ref, gb_ref, ub_ref, db_ref, z_scr):
    del ta_ref, tb_ref
    g = pl.program_id(0)
    n_used = nu_ref[0]

    @pl.when((chg_ref[g] & 1) == 1)
    def _():
        for src, dst in ((ga_f32, ga_ref), (ua_f32, ua_ref), (da_f32, da_ref)):
            dst[0] = src[0, 0].astype(BF16)

    @pl.when((chg_ref[g] & 2) == 2)
    def _():
        for src, dst in ((gb_f32, gb_ref), (ub_f32, ub_ref), (db_f32, db_ref)):
            dst[0] = src[0, 0].astype(BF16)

    def norm_previous():
        o_ref[...] = _layer_norm_rows(z_scr[...], lg_ref[...], lb_ref[...])

    def experts():
        x = x_ref[:, 0:D_MODEL]
        slab = x_ref[:, D_MODEL:D_MODEL + LANES]
        w1, w2, e1, e2 = slab[:, 0:1], slab[:, 1:2], slab[:, 2:3], slab[:, 3:4]
        first = e1 < e2
        wa = jnp.where(first, w1, w2)
        wb = jnp.where(first, w2, w1)
        xb = x.astype(BF16)
        gate_a, up_a = _dot(xb, ga_ref[0]), _dot(xb, ua_ref[0])
        gate_b, up_b = _dot(xb, gb_ref[0]), _dot(xb, ub_ref[0])
        y = wa * _dot((jax.nn.silu(gate_a) * up_a).astype(BF16), da_ref[0])
        y = y + wb * _dot((jax.nn.silu(gate_b) * up_b).astype(BF16), db_ref[0])
        z_scr[...] = DEEPNORM_ALPHA * x + y

    @pl.when(jnp.logical_and(g >= 1, g < n_used))
    def _():
        norm_previous()
        experts()

    @pl.when(g == 0)
    def _():
        experts()

    @pl.when(jnp.logical_and(g >= 1, g == n_used))
    def _():
        norm_previous()

    @pl.when(g > n_used)
    def _():
        o_ref[...] = jnp.zeros(o_ref.shape, F32)


def _moe(xs_sorted, plan, layer, w_gate, w_up, w_down, ln_g, ln_b, tm):
    n_tiles = plan["n_tiles"]
    width = xs_sorted.shape[1]
    tile = lambda g: jnp.minimum(g, n_tiles - 1)
    up_a = pl.BlockSpec((1, 1, D_MODEL, D_EXPERT), lambda g, ta, tb, nu, chg: (layer, ta[tile(g)], 0, 0))
    dn_a = pl.BlockSpec((1, 1, D_EXPERT, D_MODEL), lambda g, ta, tb, nu, chg: (layer, ta[tile(g)], 0, 0))
    up_b = pl.BlockSpec((1, 1, D_MODEL, D_EXPERT), lambda g, ta, tb, nu, chg: (layer, tb[tile(g)], 0, 0))
    dn_b = pl.BlockSpec((1, 1, D_EXPERT, D_MODEL), lambda g, ta, tb, nu, chg: (layer, tb[tile(g)], 0, 0))
    vec = pl.BlockSpec((1, D_MODEL), lambda g, ta, tb, nu, chg: (0, 0))
    up_s = pltpu.VMEM((1, D_MODEL, D_EXPERT), BF16)
    dn_s = pltpu.VMEM((1, D_EXPERT, D_MODEL), BF16)
    grid_spec = pltpu.PrefetchScalarGridSpec(
        num_scalar_prefetch=4,
        grid=(n_tiles + 1,),
        in_specs=[pl.BlockSpec((tm, width), lambda g, ta, tb, nu, chg: (tile(g), 0)),
                  up_a, up_a, dn_a, up_b, up_b, dn_b, vec, vec],
        out_specs=pl.BlockSpec((tm, D_MODEL), lambda g, ta, tb, nu, chg: (jnp.maximum(g - 1, 0), 0)),
        scratch_shapes=[up_s, up_s, dn_s, up_s, up_s, dn_s, pltpu.VMEM((tm, D_MODEL), F32)],
    )
    return pl.pallas_call(
        _moe_kernel,
        grid_spec=grid_spec,
        out_shape=jax.ShapeDtypeStruct((n_tiles * tm, D_MODEL), F32),
        compiler_params=_cparams("arbitrary"),
        name="moe",
    )(plan["ta"], plan["tb"], plan["nu"], plan["chg"], xs_sorted, w_gate, w_up, w_down, w_gate, w_up, w_down,
      ln_g, ln_b)


def _ple_kernel(pos_ref, posn_ref, x2_hbm, pp_ref, ps_ref, wg_ref, wp_ref, op_ref, os_ref, buf, sems, *, n_s):
    i = pl.program_id(0)
    last = pl.num_programs(0) - 1
    tm = buf.shape[1]

    def issue(idx_ref, s):
        for r in range(tm):
            pltpu.make_async_copy(x2_hbm.at[pl.ds(idx_ref[0, 0, r], 1)], buf.at[s, pl.ds(r, 1)],
                                  sems.at[s]).start(priority=r % 2)

    def rows(x, p):
        gate = jax.nn.sigmoid(_dot(x.astype(BF16), wg_ref[...]))
        return x + gate * _dot(p.astype(BF16), wp_ref[...])

    def step(slot):
        if slot == 0:
            @pl.when(i == 0)
            def _():
                issue(pos_ref, 0)

        @pl.when(i < last)
        def _():
            issue(posn_ref, 1 - slot)

        pltpu.make_async_copy(x2_hbm.at[pl.ds(0, tm)], buf.at[slot], sems.at[slot]).wait()

        @pl.when(i < last)
        def _():
            op_ref[...] = rows(buf[slot], pp_ref[0])

        @pl.when(i == last)
        def _():
            os_ref[...] = rows(buf[slot, 0:n_s, :], ps_ref[0])

    for slot in range(2):
        @pl.when(lax.rem(i, 2) == slot)
        def _(slot=slot):
            step(slot)


def _ple(x2_sorted, pos, layer, p_p, p_s, wg_bf, wp_bf):
    n_p, n_s = p_p.shape[1], p_s.shape[1]
    tm = GATHER_TILE
    nb = n_p // tm
    steps = nb + 1
    pos_pad = jnp.zeros((steps * tm,), I32).at[:n_p + n_s].set(pos).reshape(steps, 1, tm)
    full = lambda a: pl.BlockSpec(a.shape, lambda i: (0, 0))
    prow = lambda n: pl.BlockSpec((tm, n), lambda i: (jnp.minimum(i, nb - 1), 0))
    return pl.pallas_call(
        functools.partial(_ple_kernel, n_s=n_s),
        grid=(steps,),
        in_specs=[pl.BlockSpec((1, 1, tm), lambda i: (i, 0, 0), memory_space=pltpu.SMEM),
                  pl.BlockSpec((1, 1, tm), lambda i: (jnp.minimum(i + 1, nb), 0, 0), memory_space=pltpu.SMEM),
                  pl.BlockSpec(memory_space=pl.ANY),
                  pl.BlockSpec((1, tm, D_PLE), lambda i: (layer, jnp.minimum(i, nb - 1), 0)),
                  pl.BlockSpec((1, n_s, D_PLE), lambda i: (layer, 0, 0)), full(wg_bf), full(wp_bf)],
        out_specs=[prow(D_MODEL), pl.BlockSpec((n_s, D_MODEL), lambda i: (0, 0))],
        out_shape=[jax.ShapeDtypeStruct((n_p, D_MODEL), F32), jax.ShapeDtypeStruct((n_s, D_MODEL), F32)],
        scratch_shapes=[pltpu.VMEM((2, tm, D_MODEL), F32), pltpu.SemaphoreType.DMA((2,))],
        compiler_params=_cparams("arbitrary"),
        name="ple",
    )(pos_pad, pos_pad, x2_sorted, p_p, p_s, wg_bf, wp_bf)


def _router_weights(w_group, w_router):
    wr = jnp.zeros((ROUTE_ROWS, D_MODEL), F32)
    wr = wr.at[ROUTE_GROUP_ROW:ROUTE_GROUP_ROW + N_GROUPS, :].set(jnp.transpose(w_group))
    wr = wr.at[ROUTE_EXPERT_ROW:ROUTE_EXPERT_ROW + N_EXPERTS, :].set(jnp.transpose(w_router))
    hi = wr.astype(BF16)
    lo = (wr - hi.astype(F32)).astype(BF16)
    return jnp.concatenate([hi, lo], axis=0)


def _layer_tail(i, acts_p, acts_s, w_list, xp, xs, p_p, p_s, ln_mix_g, ln_mix_b, ln_ffn_g, ln_ffn_b,
                w_group, w_router, w_exp_gate, w_exp_up, w_exp_down, w_ple_proj, w_ple_gate):
    rw = _router_weights(w_group[i], w_router[i])
    x1e, rt = _mix_out(acts_p, acts_s, w_list, xp, xs, ln_mix_g[i][None, :], ln_mix_b[i][None, :], rw)
    plan = _route_plan(rt[:, :x1e.shape[0]], MOE_TILE)
    xs_sorted = _dispatch(x1e, plan, MOE_TILE)
    x2_sorted = _moe(xs_sorted, plan, i, w_exp_gate, w_exp_up, w_exp_down, ln_ffn_g[i][None, :],
                     ln_ffn_b[i][None, :], MOE_TILE)
    return _ple(x2_sorted, plan["pos"], i, p_p, p_s, w_ple_gate[i].astype(BF16), w_ple_proj[i].astype(BF16))


def kernel(x_prompt, x_sample, cache_k, cache_v, page_table, state_conv, state_mlstm_C, state_mlstm_n,
           state_mlstm_m, p_prompt, p_sample, w_in_even, conv_w, lambda_q1, lambda_k1, lambda_q2, lambda_k2,
           subln_w, w_out_even, w_in_odd, b_gates_odd, mh_norm_w, w_out_odd, ln_mix_g, ln_mix_b, ln_ffn_g,
           ln_ffn_b, w_group, w_router, w_exp_gate, w_exp_up, w_exp_down, w_ple_proj, w_ple_gate):
    bp, tp, _ = x_prompt.shape
    bs, ts, _ = x_sample.shape
    assert ts == 1 and tp % ROW_TILE == 0 and tp % ATTN_TILE == 0 and tp % MLSTM_CHUNK == 0
    assert (bp * tp) % GATHER_TILE == 0
    n_p = bp * tp
    past_len = page_table.shape[1] * cache_k.shape[2]
    xp = x_prompt.reshape(n_p, D_MODEL)
    xs = x_sample.reshape(bs, D_MODEL)
    tail_w = (ln_mix_g, ln_mix_b, ln_ffn_g, ln_ffn_b, w_group, w_router, w_exp_gate, w_exp_up, w_exp_down,
              w_ple_proj, w_ple_gate)
    outs_p, outs_s = {}, {}
    for i in range(DEPTH):
        j = i // 2
        p_p = p_prompt.reshape(DEPTH, n_p, D_PLE)
        p_s = p_sample.reshape(DEPTH, bs, D_PLE)
        if i % 2 == 0:
            lam_init = 0.8 - 0.6 * math.exp(-0.3 * i)
            lam_vecs = jnp.stack([lambda_q1[j], lambda_k1[j], lambda_q2[j], lambda_k2[j]])
            sub = subln_w[j][None, :]
            w_bf = w_in_even[j].astype(BF16)
            tabs_p = _rope_tables(jnp.arange(tp))
            q0 = 3 * D_CONV
            wqt_bf = jnp.transpose(w_in_even[j][:, q0:q0 + QK_B]).astype(BF16)
            yc, qt, kf, vf, kb, vt, cst = _even_in_prompt(
                x_prompt if i == 0 else xp.reshape(bp, tp, D_MODEL), w_bf, wqt_bf, conv_w[j],
                jnp.zeros((bp, CONV_W - 1, D_CONV), F32), tabs_p)
            o_p = _attn_prompt(qt, kb, vt, lam_vecs, sub, lam_init)
            outs_p.setdefault("k", []).append(kf.reshape(bp, tp, 2 * H_B, DH_B))
            outs_p.setdefault("v", []).append(vf.reshape(bp, tp, H_B, 2 * DH_B))
            outs_p.setdefault("c", []).append(cst)
            tabs_s = _rope_tables(jnp.full((1,), past_len, I32))
            prev_t = jnp.swapaxes(state_conv[j], 0, 1)
            yc_s, q_s, kf_s, vf_s, u_s = _even_in_decode(xs, w_bf, conv_w[j], prev_t, tabs_s)
            sub_head = jnp.arange(2 * H_B)
            sub_head = jnp.where(sub_head < H_B, 2 * sub_head, 2 * (sub_head - H_B) + 1)
            lane_head = jnp.arange(QK_B) // DH_B
            qbd = jnp.where(lane_head[None, None, :] == sub_head[None, :, None], q_s[:, None, :],
                            jnp.zeros((), BF16))
            n_pool = cache_k.shape[1]
            pages = cache_k.shape[0] * n_pool
            k_view = jnp.transpose(cache_k, (0, 1, 3, 4, 2)).reshape(pages, QK_B, PAGE_SIZE)
            v_view = cache_v.reshape(pages, PAGE_SIZE * H_B, 2 * DH_B)
            o8 = _attn_decode(qbd, kf_s[:, None, :], vf_s[:, None, :], k_view, v_view,
                              page_table + j * n_pool, lam_vecs, sub, lam_init)
            o_s = o8[:, :H_B, :].reshape(bs, V_B).astype(BF16)
            outs_s.setdefault("k", []).append(kf_s.reshape(bs, ts, 2 * H_B, DH_B))
            outs_s.setdefault("v", []).append(vf_s.reshape(bs, ts, H_B, 2 * DH_B))
            outs_s.setdefault("c", []).append(jnp.stack([state_conv[j][:, 1, :], u_s], axis=1))
            w_out = w_out_even[j].astype(BF16)
            w_list = [w_out[:D_CONV], w_out[D_CONV:]]
            acts_p = [yc.reshape(n_p, D_CONV), o_p.reshape(n_p, V_B)]
            acts_s = [yc_s, o_s]
        else:
            w_in = w_in_odd[j]
            qw, vw = H_C * DK_C, H_C * DV_C
            w_bf = w_in[:, :2 * qw + 2 * vw].astype(BF16)
            wg = jnp.zeros((D_MODEL, LANES), F32).at[:, :2 * H_C].set(w_in[:, 2 * qw + 2 * vw:]).astype(BF16)
            wgt = jnp.transpose(wg[:, :2 * H_C])
            bg = jnp.zeros((1, LANES), F32).at[0, :2 * H_C].set(b_gates_odd[j])
            bgt = jnp.broadcast_to(b_gates_odd[j][:, None], (2 * H_C, LANES))
            nw = mh_norm_w[j][None, :]
            q, k, v, o, gc, gr, kt = _odd_in(xp.reshape(bp, tp, D_MODEL), w_bf, wg, wgt, bg, bgt, decode=False)
            h_p, c_p, n_pp, m_p = _mlstm(q, k, kt, v, o, gc, gr, nw,
                                         jnp.zeros((bp, H_C, DK_C, DV_C), F32), jnp.zeros((bp, H_C, DK_C), F32),
                                         jnp.zeros((bp, H_C, LANES), F32), MLSTM_CHUNK)
            outs_p.setdefault("C", []).append(c_p)
            outs_p.setdefault("n", []).append(n_pp)
            outs_p.setdefault("m", []).append(m_p[:, :, 0])
            q_s, k_s, v_s, o_s2, gc_s, gr_s, kt_s = _odd_in(xs, w_bf, wg, wgt, bg, bgt, decode=True)
            ch = MLSTM_DECODE_CHUNK
            pad_rows = lambda a: jnp.zeros((bs, ch, a.shape[1]), a.dtype).at[:, 0, :].set(a)
            lane = jnp.arange(LANES)
            inert_c = jnp.where(lane < H_C, -jnp.inf, 0.0).astype(F32)
            gc_pad = jnp.broadcast_to(inert_c[None, None, :], (bs, ch, LANES)).at[:, 0, :].set(gc_s)
            inert_r = jnp.where(jnp.arange(2 * H_C) < H_C, -jnp.inf, 0.0).astype(F32)
            gr_pad = jnp.broadcast_to(inert_r[None, :, None], (bs, 2 * H_C, ch)).at[:, :, 0].set(gr_s.T)
            m0 = jnp.broadcast_to(state_mlstm_m[j][:, :, None], (bs, H_C, LANES))
            kt_pad = jnp.zeros((bs, qw, ch), BF16).at[:, :, 0].set(kt_s.T)
            h_s, c_s, n_s, m_s = _mlstm(pad_rows(q_s), pad_rows(k_s), kt_pad, pad_rows(v_s), pad_rows(o_s2),
                                        gc_pad, gr_pad, nw, state_mlstm_C[j], state_mlstm_n[j], m0, ch)
            outs_s.setdefault("C", []).append(c_s)
            outs_s.setdefault("n", []).append(n_s)
            outs_s.setdefault("m", []).append(m_s[:, :, 0])
            w_list = [w_out_odd[j].astype(BF16)]
            acts_p = [h_p.reshape(n_p, vw)]
            acts_s = [h_s[:, 0, :]]
        xp, xs = _layer_tail(i, acts_p, acts_s, w_list, xp, xs, p_p, p_s, *tail_w)
    st = lambda lst: jnp.stack(lst)
    return (xp.reshape(bp, tp, D_MODEL), xs.reshape(bs, ts, D_MODEL),
            st(outs_p["k"]), st(outs_p["v"]), st(outs_p["c"]), st(outs_p["C"]), st(outs_p["n"]), st(outs_p["m"]),
            st(outs_s["k"]), st(outs_s["v"]), st(outs_s["c"]), st(outs_s["C"]), st(outs_s["n"]), st(outs_s["m"]))
```

```python
import functools
import math

import numpy as np
import jax
import jax.numpy as jnp
from jax import lax
from jax.experimental import pallas as pl
from jax.experimental.pallas import tpu as pltpu

F32 = jnp.float32
BF16 = jnp.bfloat16
I32 = jnp.int32

D_MODEL = 1024
DEPTH = 2
PAGE_SIZE = 128
D_CONV = D_MODEL // 2
CONV_W = 3
H_B = 4
DH_B = 64
ROT_DIM = DH_B // 4
ROPE_THETA = 500000.0
H_C = 4
DK_C = (D_MODEL // 2) // H_C
DV_C = D_MODEL // H_C
N_GROUPS = 4
EXP_PER_GROUP = 4
N_EXPERTS = N_GROUPS * EXP_PER_GROUP
D_EXPERT = 512
D_PLE = 256
LN_EPS = 1e-5
LOG2_E = 1.4426950408889634
DEEPNORM_ALPHA = (2 * DEPTH) ** 0.25
QK_B = 2 * H_B * DH_B
V_B = H_B * 2 * DH_B
N_PAIRS = EXP_PER_GROUP * (EXP_PER_GROUP - 1) // 2
N_CLASSES = N_GROUPS * N_PAIRS
PAIR_SEQ = ((0, 1), (0, 2), (1, 2), (1, 3), (0, 3), (2, 3))
assert EXP_PER_GROUP == 4 and len(PAIR_SEQ) == N_PAIRS

LANES = 128
VMEM_LIMIT = 56 * 1024 * 1024
ROW_TILE = 1024
GATHER_TILE = 512
ATTN_TILE = 512
ATTN_HEADS_PER_STEP = 4
ONES_ROWS = 16
MLSTM_CHUNK = 128
MLSTM_DECODE_CHUNK = 16
MLSTM_BATCH_PER_STEP = 1
DISPATCH_TILE_CAP = 1024
MOE_TILE = 256
PAGES_PER_STEP = 32
ROUTE_ROWS = 32
ROUTE_GROUP_ROW = 0
ROUTE_EXPERT_ROW = 8


def _cparams(*sem):
    return pltpu.CompilerParams(dimension_semantics=sem, vmem_limit_bytes=VMEM_LIMIT)


def _dot(a, b):
    return jnp.dot(a, b, preferred_element_type=F32)


def _dot_nt(a, b):
    return lax.dot_general(a, b, (((1,), (1,)), ((), ())), preferred_element_type=F32)


def _layer_norm_rows(z, g, b):
    mu = jnp.mean(z, axis=-1, keepdims=True)
    zc = z - mu
    var = jnp.mean(zc * zc, axis=-1, keepdims=True)
    return zc * lax.rsqrt(var + LN_EPS) * g + b


def _log_sigmoid(x):
    return jnp.minimum(x, 0.0) - jnp.log1p(jnp.exp(-jnp.abs(x)))


def _gated_conv(gate_b, u, um1, um2, cw_ref):
    cw = cw_ref[...]
    conv = um2 * cw[0:1, :] + um1 * cw[1:2, :] + u * cw[2:3, :]
    return (gate_b * conv).astype(BF16)


def _rope(z, cos, sin_lo, sin_hi, axis):
    half = ROT_DIM // 2
    return z * cos + pltpu.roll(z, QK_B - half, axis) * sin_lo + pltpu.roll(z, half, axis) * sin_hi


def _even_in_prompt_kernel(x_ref, w_ref, wqt_ref, cw_ref, prev_ref, rc_ref, rs1_ref, rs2_ref,
                           rct_ref, rs1t_ref, rs2t_ref,
                           yc_ref, qt_ref, kf_ref, vf_ref, kb_ref, vt_ref, u_ref, carry_ref):
    xb = x_ref[0].astype(BF16)
    tm = xb.shape[0]

    def proj(c0, n):
        return _dot(xb, w_ref[:, c0:c0 + n])

    gate_b = proj(0, D_CONV)
    u = proj(D_CONV, D_CONV) * proj(2 * D_CONV, D_CONV)
    j = pl.program_id(1)

    @pl.when(j == 0)
    def _():
        carry_ref[...] = prev_ref[0]

    row = lax.broadcasted_iota(I32, u.shape, 0)
    c2 = carry_ref[0:1, :]
    c1 = carry_ref[1:2, :]
    um1 = jnp.where(row == 0, c1, pltpu.roll(u, 1, 0))
    um2 = jnp.where(row == 0, c2, jnp.where(row == 1, c1, pltpu.roll(u, 2, 0)))
    carry_ref[...] = u[tm - 2:tm, :]
    u_ref[0] = u[tm - 2:tm, :]
    yc_ref[0] = _gated_conv(gate_b, u, um1, um2, cw_ref)

    reps = QK_B // LANES
    tile = lambda r, ax: jnp.concatenate([r[...]] * reps, axis=ax)
    k = _rope(proj(3 * D_CONV + QK_B, QK_B), tile(rc_ref, 1), tile(rs1_ref, 1), tile(rs2_ref, 1), 1)
    kf_ref[0] = k
    kb_ref[0] = k.astype(BF16)
    v = proj(3 * D_CONV + 2 * QK_B, V_B)
    for h in range(H_B):
        vf_ref[0, pl.ds(h, tm, stride=H_B), :] = v[:, h * 2 * DH_B:(h + 1) * 2 * DH_B]
    qt = _rope(_dot_nt(wqt_ref[...], xb), tile(rct_ref, 0), tile(rs1t_ref, 0), tile(rs2t_ref, 0), 0)
    qt_ref[0] = (qt * (DH_B ** -0.5 * LOG2_E)).astype(BF16)
    vt_ref[0] = jnp.transpose(v).astype(BF16)


def _even_in_decode_kernel(x_ref, w_ref, cw_ref, prev_ref, rc_ref, rs1_ref, rs2_ref,
                           yc_ref, q_ref, kf_ref, vf_ref, u_ref):
    xb = x_ref[...].astype(BF16)

    def proj(c0, n):
        return _dot(xb, w_ref[:, c0:c0 + n])

    gate_b = proj(0, D_CONV)
    u = proj(D_CONV, D_CONV) * proj(2 * D_CONV, D_CONV)
    u_ref[...] = u
    yc_ref[...] = _gated_conv(gate_b, u, prev_ref[1], prev_ref[0], cw_ref)
    reps = QK_B // LANES
    tile = lambda r: jnp.concatenate([r[...]] * reps, axis=1)
    cos, sin_lo, sin_hi = tile(rc_ref), tile(rs1_ref), tile(rs2_ref)
    q_ref[...] = (_rope(proj(3 * D_CONV, QK_B), cos, sin_lo, sin_hi, 1) * (DH_B ** -0.5)).astype(BF16)
    kf_ref[...] = _rope(proj(3 * D_CONV + QK_B, QK_B), cos, sin_lo, sin_hi, 1)
    vf_ref[...] = proj(3 * D_CONV + 2 * QK_B, V_B)


def _rope_tables(pos):
    half = ROT_DIM // 2
    inv = ROPE_THETA ** (-jnp.arange(half, dtype=F32) / half)
    ang = pos.astype(F32)[:, None] * inv[None, :]
    cos, sin = jnp.cos(ang), jnp.sin(ang)
    t = pos.shape[0]
    ones = jnp.ones((t, DH_B - ROT_DIM), F32)
    zeros = jnp.zeros((t, DH_B - ROT_DIM), F32)
    zh = jnp.zeros((t, half), F32)
    c = jnp.concatenate([cos, cos, ones], axis=1)
    s_lo = jnp.concatenate([-sin, zh, zeros], axis=1)
    s_hi = jnp.concatenate([zh, sin, zeros], axis=1)
    tile2 = lambda a: jnp.concatenate([a, a], axis=1)
    return tile2(c), tile2(s_lo), tile2(s_hi)


def _even_in_prompt(x, w_bf, wqt_bf, conv_w, conv_prev, tables):
    b, t, _ = x.shape
    tm = ROW_TILE
    row3 = lambda n: pl.BlockSpec((1, tm, n), lambda i, j: (i, j, 0))
    col3 = lambda n: pl.BlockSpec((1, n, tm), lambda i, j: (i, 0, j))
    full2 = lambda a: pl.BlockSpec(a.shape, lambda i, j: (0, 0))
    tab = pl.BlockSpec((tm, LANES), lambda i, j: (j, 0))
    tab_t = pl.BlockSpec((LANES, tm), lambda i, j: (0, j))
    st = pl.BlockSpec((1, CONV_W - 1, D_CONV), lambda i, j: (i, 0, 0))
    tables_t = [jnp.transpose(a) for a in tables]
    outs = [jax.ShapeDtypeStruct((b, t, D_CONV), BF16), jax.ShapeDtypeStruct((b, QK_B, t), BF16),
            jax.ShapeDtypeStruct((b, t, QK_B), F32), jax.ShapeDtypeStruct((b, t * H_B, 2 * DH_B), F32),
            jax.ShapeDtypeStruct((b, t, QK_B), BF16), jax.ShapeDtypeStruct((b, V_B, t), BF16),
            jax.ShapeDtypeStruct((b, CONV_W - 1, D_CONV), F32)]
    return pl.pallas_call(
        _even_in_prompt_kernel,
        grid=(b, t // tm),
        in_specs=[row3(D_MODEL), full2(w_bf), full2(wqt_bf), full2(conv_w), st,
                  tab, tab, tab, tab_t, tab_t, tab_t],
        out_specs=[row3(D_CONV), col3(QK_B), row3(QK_B),
                   pl.BlockSpec((1, tm * H_B, 2 * DH_B), lambda i, j: (i, j, 0)), row3(QK_B), col3(V_B), st],
        out_shape=outs,
        scratch_shapes=[pltpu.VMEM((CONV_W - 1, D_CONV), F32)],
        compiler_params=_cparams("arbitrary", "arbitrary"),
        name="even_in_prompt",
    )(x, w_bf, wqt_bf, conv_w, conv_prev, *tables, *tables_t)


def _even_in_decode(x, w_bf, conv_w, conv_prev_t, tables):
    n = x.shape[0]
    full = lambda a: pl.BlockSpec(a.shape, lambda i: (0,) * a.ndim)
    o2 = lambda c, dt: jax.ShapeDtypeStruct((n, c), dt)
    outs = [o2(D_CONV, BF16), o2(QK_B, BF16), o2(QK_B, F32), o2(V_B, F32), o2(D_CONV, F32)]
    ins = [x, w_bf, conv_w, conv_prev_t, *tables]
    return pl.pallas_call(
        _even_in_decode_kernel,
        grid=(1,),
        in_specs=[full(a) for a in ins],
        out_specs=[pl.BlockSpec(o.shape, lambda i: (0, 0)) for o in outs],
        out_shape=outs,
        compiler_params=_cparams("arbitrary"),
        name="even_in_decode",
    )(*ins)


def _lambda_value(lam_ref, lam_init):
    lv = lam_ref[...]
    a = jnp.sum(lv[0:1, :] * lv[1:2, :], axis=1, keepdims=True)
    b = jnp.sum(lv[2:3, :] * lv[3:4, :], axis=1, keepdims=True)
    return jnp.exp(a) - jnp.exp(b) + lam_init


def _sub_norm(o, sub_ref, lam_init):
    ms = jnp.mean(o * o, axis=-1, keepdims=True)
    return o * lax.rsqrt(ms + LN_EPS) * sub_ref[...] * (1.0 - lam_init)


def _attn_prompt_kernel(qt_ref, k_ref, vt_ref, lam_ref, sub_ref, o_ref, *scratch, lam_init):
    i = pl.program_id(2)
    tq = qt_ref.shape[2]
    tk = tq
    n_heads = qt_ref.shape[1] // LANES
    m_scrs, acc_scrs = scratch[:n_heads], scratch[n_heads:]
    for m_scr, acc_scr in zip(m_scrs, acc_scrs):
        m_scr[...] = jnp.full(m_scr.shape, -jnp.inf, F32)
        acc_scr[...] = jnp.zeros(acc_scr.shape, F32)
    ones = jnp.ones((ONES_ROWS, tk), BF16)
    qqs = []
    for h in range(n_heads):
        qt = qt_ref[0, h * LANES:(h + 1) * LANES, :].astype(F32)
        feat = lax.broadcasted_iota(I32, qt.shape, 0)
        qqs.append(jnp.concatenate([jnp.where(feat < DH_B, qt, 0.0), jnp.where(feat >= DH_B, qt, 0.0)],
                                   axis=1).astype(BF16))

    def step(j, masked):
        start = pl.multiple_of(j * tk, tk)
        scores = [_dot(k_ref[0, pl.ds(start, tk), h * LANES:(h + 1) * LANES], qqs[h])
                  for h in range(n_heads)]
        probs, alphas = [], []
        for h in range(n_heads):
            s = scores[h]
            if masked:
                key = lax.broadcasted_iota(I32, s.shape, 0)
                qry = lax.broadcasted_iota(I32, s.shape, 1)
                qry = jnp.where(qry >= tq, qry - tq, qry)
                s = jnp.where(key <= qry, s, -jnp.inf)
            m_prev = m_scrs[h][...]
            m_new = jnp.maximum(m_prev, jnp.max(s, axis=0, keepdims=True))
            alphas.append(jnp.exp2(m_prev - m_new))
            probs.append(jnp.exp2(s - m_new).astype(BF16))
            m_scrs[h][...] = m_new
        for h in range(n_heads):
            vtj = jnp.concatenate([vt_ref[0, h * LANES:(h + 1) * LANES, pl.ds(start, tk)], ones], axis=0)
            acc_scrs[h][...] = alphas[h] * acc_scrs[h][...] + _dot(vtj, probs[h])

    def body(j, carry):
        step(j, False)
        return carry

    lax.fori_loop(0, i, body, 0)
    step(i, True)
    lam = _lambda_value(lam_ref, lam_init)
    for h in range(n_heads):
        on = acc_scrs[h][0:LANES, :] / acc_scrs[h][LANES:LANES + 1, :]
        o = jnp.transpose(on[:, 0:tq] - lam * on[:, tq:2 * tq])
        o_ref[0, :, h * LANES:(h + 1) * LANES] = _sub_norm(o, sub_ref, lam_init).astype(BF16)


def _attn_prompt(qt, k, vt, lam_vecs, subln, lam_init):
    b, t, _ = k.shape
    tq = ATTN_TILE
    nh = ATTN_HEADS_PER_STEP
    hw = nh * LANES
    full = lambda a: pl.BlockSpec(a.shape, lambda bi, h, i: (0, 0))
    return pl.pallas_call(
        functools.partial(_attn_prompt_kernel, lam_init=lam_init),
        grid=(b, H_B // nh, t // tq),
        in_specs=[pl.BlockSpec((1, hw, tq), lambda bi, h, i: (bi, h, i)),
                  pl.BlockSpec((1, t, hw), lambda bi, h, i: (bi, 0, h)),
                  pl.BlockSpec((1, hw, t), lambda bi, h, i: (bi, h, 0)),
                  full(lam_vecs), full(subln)],
        out_specs=pl.BlockSpec((1, tq, hw), lambda bi, h, i: (bi, i, h)),
        out_shape=jax.ShapeDtypeStruct((b, t, V_B), BF16),
        scratch_shapes=[pltpu.VMEM((1, 2 * tq), F32)] * nh + [pltpu.VMEM((LANES + ONES_ROWS, 2 * tq), F32)] * nh,
        compiler_params=_cparams("arbitrary", "arbitrary", "arbitrary"),
        name="attn_prompt",
    )(qt, k, vt, lam_vecs, subln)


def _attn_decode_kernel(pt_ref, qbd_ref, kn_ref, vn_ref, lam_ref, sub_ref, *rest, lam_init, n_pages):
    del pt_ref
    k_refs = rest[:n_pages]
    v_refs = rest[n_pages:2 * n_pages]
    o_ref, m_scr, l_scr, acc_scr = rest[2 * n_pages:]
    j = pl.program_id(1)
    qbd = qbd_ref[0]

    @pl.when(j == 0)
    def _():
        s_new = jnp.sum(qbd.astype(F32) * kn_ref[0], axis=1, keepdims=True)
        m_scr[...] = s_new
        l_scr[...] = jnp.ones(l_scr.shape, F32)
        acc_scr[...] = jnp.broadcast_to(vn_ref[0], acc_scr.shape)

    s = jnp.concatenate([_dot(qbd, k_refs[r][0].astype(BF16)) for r in range(n_pages)], axis=1)
    m_prev = m_scr[...]
    m_new = jnp.maximum(m_prev, jnp.max(s, axis=1, keepdims=True))
    alpha = jnp.exp(m_prev - m_new)
    p = jnp.exp(s - m_new)
    l_scr[...] = alpha * l_scr[...] + jnp.sum(p, axis=1, keepdims=True)

    def head_pv(h):
        acc = None
        for r in range(n_pages):
            vh = v_refs[r][0, pl.ds(h, PAGE_SIZE, stride=H_B), :].astype(BF16)
            term = _dot(p[:, r * PAGE_SIZE:(r + 1) * PAGE_SIZE].astype(BF16), vh)
            acc = term if acc is None else acc + term
        return acc

    pv = jnp.concatenate([head_pv(h) for h in range(H_B)], axis=1)
    acc_scr[...] = alpha * acc_scr[...] + pv
    m_scr[...] = m_new

    @pl.when(j == pl.num_programs(1) - 1)
    def _():
        on = acc_scr[...] / l_scr[...]
        row = lax.broadcasted_iota(I32, (2 * H_B, 2 * DH_B), 0)
        head = jnp.where(row >= H_B, row - H_B, row)
        o8 = jnp.zeros((2 * H_B, 2 * DH_B), F32)
        for c in range(H_B):
            o8 = o8 + jnp.where(head == c, on[:, c * 2 * DH_B:(c + 1) * 2 * DH_B], 0.0)
        lam = _lambda_value(lam_ref, lam_init)
        o = o8 - lam * pltpu.roll(o8, H_B, 0)
        o_ref[0] = _sub_norm(o, sub_ref, lam_init)


def _attn_decode(qbd, k_new, v_new, cache_k, cache_v, page_table, lam_vecs, subln, lam_init):
    n = qbd.shape[0]
    n_pages = page_table.shape[1]
    pp = PAGES_PER_STEP
    width = V_B
    c2 = lambda a: pl.BlockSpec(a.shape, lambda b, j, pt: (0, 0))
    per_b = lambda a: pl.BlockSpec((1,) + a.shape[1:], lambda b, j, pt: (b, 0, 0))

    def page(r, arr):
        return pl.BlockSpec((1,) + arr.shape[1:], lambda b, j, pt: (pt[b, j * pp + r], 0, 0))

    grid_spec = pltpu.PrefetchScalarGridSpec(
        num_scalar_prefetch=1,
        grid=(n, n_pages // pp),
        in_specs=[per_b(qbd), per_b(k_new), per_b(v_new), c2(lam_vecs), c2(subln)]
        + [page(r, cache_k) for r in range(pp)] + [page(r, cache_v) for r in range(pp)],
        out_specs=pl.BlockSpec((1, 2 * H_B, 2 * DH_B), lambda b, j, pt: (b, 0, 0)),
        scratch_shapes=[pltpu.VMEM((2 * H_B, 1), F32), pltpu.VMEM((2 * H_B, 1), F32),
                        pltpu.VMEM((2 * H_B, width), F32)],
    )
    return pl.pallas_call(
        functools.partial(_attn_decode_kernel, lam_init=lam_init, n_pages=pp),
        grid_spec=grid_spec,
        out_shape=jax.ShapeDtypeStruct((n, 2 * H_B, 2 * DH_B), F32),
        compiler_params=_cparams("arbitrary", "arbitrary"),
        name="attn_decode",
    )(page_table, qbd, k_new, v_new, lam_vecs, subln, *([cache_k] * pp), *([cache_v] * pp))


def _odd_in_kernel(x_ref, w_ref, wg_ref, wgt_ref, bg_ref, bgt_ref,
                   q_ref, k_ref, v_ref, o_ref, gc_ref, gr_ref, kt_ref, *, decode):
    xb = x_ref[0].astype(BF16) if not decode else x_ref[...].astype(BF16)
    qw = H_C * DK_C
    vw = H_C * DV_C
    q = _dot(xb, w_ref[:, 0:qw]) * (DK_C ** -0.5)
    k = _dot(xb, w_ref[:, qw:2 * qw])
    v = _dot(xb, w_ref[:, 2 * qw:2 * qw + vw])
    o = _dot(xb, w_ref[:, 2 * qw + vw:2 * qw + 2 * vw])
    g_col = _dot(xb, wg_ref[...]) + bg_ref[...]
    lane = lax.broadcasted_iota(I32, g_col.shape, 1)
    g_col = jnp.where(lane < H_C, g_col, _log_sigmoid(g_col))
    k_t = jnp.transpose(k).astype(BF16)
    g_row = _dot_nt(wgt_ref[...], xb) + bgt_ref[:, 0:1]
    sub = lax.broadcasted_iota(I32, g_row.shape, 0)
    g_row = jnp.where(sub < H_C, g_row, _log_sigmoid(g_row))
    if decode:
        kt_ref[...] = k_t
        q_ref[...] = q.astype(BF16)
        k_ref[...] = k.astype(BF16)
        v_ref[...] = v.astype(BF16)
        o_ref[...] = o
        gc_ref[...] = g_col
        gr_ref[...] = g_row
    else:
        q_ref[0] = q.astype(BF16)
        k_ref[0] = k.astype(BF16)
        v_ref[0] = v.astype(BF16)
        o_ref[0] = o
        gc_ref[0] = g_col
        gr_ref[0] = g_row
        kt_ref[0] = k_t


def _odd_in(x, w_bf, wg, wgt, bg, bgt, decode):
    qw, vw = H_C * DK_C, H_C * DV_C
    if decode:
        n = x.shape[0]
        ins = [x, w_bf, wg, wgt, bg, bgt]
        outs = [jax.ShapeDtypeStruct((n, qw), BF16), jax.ShapeDtypeStruct((n, qw), BF16),
                jax.ShapeDtypeStruct((n, vw), BF16), jax.ShapeDtypeStruct((n, vw), F32),
                jax.ShapeDtypeStruct((n, LANES), F32), jax.ShapeDtypeStruct((2 * H_C, n), F32),
                jax.ShapeDtypeStruct((qw, n), BF16)]
        return pl.pallas_call(
            functools.partial(_odd_in_kernel, decode=True),
            grid=(1,),
            in_specs=[pl.BlockSpec(a.shape, lambda i: (0, 0)) for a in ins],
            out_specs=[pl.BlockSpec(o.shape, lambda i: (0, 0)) for o in outs],
            out_shape=outs,
            compiler_params=_cparams("arbitrary"),
            name="odd_in_decode",
        )(*ins)
    b, t, _ = x.shape
    tm = ROW_TILE
    row3 = lambda n: pl.BlockSpec((1, tm, n), lambda i, j: (i, j, 0))
    full2 = lambda a: pl.BlockSpec(a.shape, lambda i, j: (0, 0))
    outs = [jax.ShapeDtypeStruct((b, t, qw), BF16), jax.ShapeDtypeStruct((b, t, qw), BF16),
            jax.ShapeDtypeStruct((b, t, vw), BF16), jax.ShapeDtypeStruct((b, t, vw), F32),
            jax.ShapeDtypeStruct((b, t, LANES), F32), jax.ShapeDtypeStruct((b, 2 * H_C, t), F32),
            jax.ShapeDtypeStruct((b, qw, t), BF16)]
    return pl.pallas_call(
        functools.partial(_odd_in_kernel, decode=False),
        grid=(b, t // tm),
        in_specs=[row3(D_MODEL), full2(w_bf), full2(wg), full2(wgt), full2(bg), full2(bgt)],
        out_specs=[row3(qw), row3(qw), row3(vw), row3(vw), row3(LANES),
                   pl.BlockSpec((1, 2 * H_C, tm), lambda i, j: (i, 0, j)),
                   pl.BlockSpec((1, qw, tm), lambda i, j: (i, 0, j))],
        out_shape=outs,
        compiler_params=_cparams("arbitrary", "arbitrary"),
        name="odd_in_prompt",
    )(x, w_bf, wg, wgt, bg, bgt)


def _mlstm_kernel(q_ref, k_ref, kt_ref, v_ref, o_ref, gc_ref, gr_ref, nw_ref, c0_ref, n0_ref, m0_ref,
                  h_ref, c_out, n_out, m_out, c_scr, n_scr, m_scr):
    ci = pl.program_id(1)
    chunk = q_ref.shape[1]

    @pl.when(ci == 0)
    def _():
        c_scr[...] = c0_ref[...]
        n_scr[...] = n0_ref[...]
        m_scr[...] = m0_ref[...]

    t_idx = lax.broadcasted_iota(I32, (chunk, chunk), 0)
    s_idx = lax.broadcasted_iota(I32, (chunk, chunk), 1)
    causal = s_idx <= t_idx
    for bb, h in [(bb, h) for bb in range(q_ref.shape[0]) for h in range(H_C)]:
        q = q_ref[bb, :, h * DK_C:(h + 1) * DK_C]
        k = k_ref[bb, :, h * DK_C:(h + 1) * DK_C]
        v = v_ref[bb, :, h * DV_C:(h + 1) * DV_C]
        ig_r = gr_ref[bb, h:h + 1, :]
        lf_r = gr_ref[bb, H_C + h:H_C + h + 1, :]
        ig_c = gc_ref[bb, :, h:h + 1]
        lf_c = gc_ref[bb, :, H_C + h:H_C + h + 1]
        bcum_c = jnp.sum(jnp.where(causal, lf_r, 0.0), axis=1, keepdims=True)
        bcum_r = jnp.sum(jnp.where(t_idx <= s_idx, lf_c, 0.0), axis=0, keepdims=True)
        m0 = m_scr[bb, h:h + 1, 0:1]
        dmat = jnp.where(causal, bcum_c - bcum_r + ig_r, -jnp.inf)
        inter = bcum_c + m0
        m = jnp.maximum(inter, jnp.max(dmat, axis=1, keepdims=True))
        w = jnp.exp(dmat - m)
        g = jnp.exp(inter - m)
        s = _dot_nt(q, k) * w
        c0 = c_scr[bb, h]
        n0 = n_scr[bb, h:h + 1, :]
        num = g * _dot(q, c0.astype(BF16)) + _dot(s.astype(BF16), v)
        den = g * jnp.sum(q.astype(F32) * n0, axis=1, keepdims=True) + jnp.sum(s, axis=1, keepdims=True)
        hid = num / jnp.maximum(jnp.abs(den), jnp.exp(-m))
        m_last = m[chunk - 1:chunk, :]
        b_last = bcum_c[chunk - 1:chunk, :]
        w_last = jnp.exp(b_last - bcum_c + ig_c - m_last)
        g_last = jnp.exp(b_last + m0 - m_last)
        kw = k.astype(F32) * w_last
        w_last_r = jnp.exp(b_last - bcum_r + ig_r - m_last)
        kw_t = (kt_ref[bb, h * DK_C:(h + 1) * DK_C, :].astype(F32) * w_last_r).astype(BF16)
        c_scr[bb, h] = g_last * c0 + _dot(kw_t, v)
        n_scr[bb, h:h + 1, :] = g_last * n0 + jnp.sum(kw, axis=0, keepdims=True)
        m_scr[bb, h:h + 1, :] = jnp.broadcast_to(m_last, (1, LANES))
        mu = jnp.mean(hid, axis=1, keepdims=True)
        hc = hid - mu
        var = jnp.mean(hc * hc, axis=1, keepdims=True)
        hn = hc * lax.rsqrt(var + LN_EPS) * nw_ref[:, h * DV_C:(h + 1) * DV_C]
        gate = jax.nn.sigmoid(o_ref[bb, :, h * DV_C:(h + 1) * DV_C])
        h_ref[bb, :, h * DV_C:(h + 1) * DV_C] = (gate * hn).astype(BF16)

    c_out[...] = c_scr[...]
    n_out[...] = n_scr[...]
    m_out[...] = m_scr[...]


def _mlstm(q, k, k_t, v, o, g_col, g_row, norm_w, c0, n0, m0, ch):
    b, t, _ = q.shape
    nb = MLSTM_BATCH_PER_STEP
    qw, vw = H_C * DK_C, H_C * DV_C
    row = lambda n: pl.BlockSpec((nb, ch, n), lambda i, j: (i, j, 0))
    st4 = pl.BlockSpec((nb, H_C, DK_C, DV_C), lambda i, j: (i, 0, 0, 0))
    st3 = pl.BlockSpec((nb, H_C, LANES), lambda i, j: (i, 0, 0))
    outs = [jax.ShapeDtypeStruct((b, t, vw), BF16), jax.ShapeDtypeStruct((b, H_C, DK_C, DV_C), F32),
            jax.ShapeDtypeStruct((b, H_C, DK_C), F32), jax.ShapeDtypeStruct((b, H_C, LANES), F32)]
    return pl.pallas_call(
        _mlstm_kernel,
        grid=(b // nb, t // ch),
        in_specs=[row(qw), row(qw), pl.BlockSpec((nb, qw, ch), lambda i, j: (i, 0, j)), row(vw), row(vw),
                  row(LANES), pl.BlockSpec((nb, 2 * H_C, ch), lambda i, j: (i, 0, j)),
                  pl.BlockSpec(norm_w.shape, lambda i, j: (0, 0)), st4, st3, st3],
        out_specs=[row(vw), st4, st3, st3],
        out_shape=outs,
        scratch_shapes=[pltpu.VMEM((nb, H_C, DK_C, DV_C), F32), pltpu.VMEM((nb, H_C, DK_C), F32),
                        pltpu.VMEM((nb, H_C, LANES), F32)],
        compiler_params=_cparams("arbitrary", "arbitrary"),
        name="mlstm",
    )(q, k, k_t, v, o, g_col, g_row, norm_w, c0, n0, m0)


def _mix_project(acts, w_refs):
    y = _dot(acts[0], w_refs[0][...])
    for a, w in zip(acts[1:], w_refs[1:]):
        y = y + _dot(a, w[...])
    return y


def _mixed_rows(acts, w_refs, x, g_ref, b_ref):
    return _layer_norm_rows(DEEPNORM_ALPHA * x + _mix_project(acts, w_refs), g_ref[...], b_ref[...])


def _route_logits(x1, rw_ref):
    xh = x1.astype(BF16)
    xl = (x1 - xh.astype(F32)).astype(BF16)
    both = _dot_nt(rw_ref[...], xh)
    return both[0:ROUTE_ROWS, :] + both[ROUTE_ROWS:2 * ROUTE_ROWS, :] + _dot_nt(rw_ref[0:ROUTE_ROWS, :], xl)


def _route(x1, rw_ref):
    return _route_decide(_route_logits(x1, rw_ref))


def _route_decide(lg):
    sub = lax.broadcasted_iota(I32, lg.shape, 0)
    big = jnp.int32(4 * ROUTE_ROWS)
    neg = -jnp.inf
    gl = jnp.where(sub < ROUTE_GROUP_ROW + N_GROUPS, lg, neg)
    g_max = jnp.max(gl, axis=0, keepdims=True)
    g_w = 1.0 / jnp.sum(jnp.exp(gl - g_max), axis=0, keepdims=True)
    g_idx = jnp.min(jnp.where(gl == g_max, sub, big), axis=0, keepdims=True)
    row_group = (sub - ROUTE_EXPERT_ROW) >> 2
    el = jnp.where(row_group == g_idx, lg, neg)
    e1 = jnp.max(el, axis=0, keepdims=True)
    i1 = jnp.min(jnp.where(el == e1, sub, big), axis=0, keepdims=True)
    z = jnp.sum(jnp.exp(el - e1), axis=0, keepdims=True)
    el2 = jnp.where(sub == i1, neg, el)
    e2 = jnp.max(el2, axis=0, keepdims=True)
    i2 = jnp.min(jnp.where(el2 == e2, sub, big), axis=0, keepdims=True)
    p1 = 1.0 / z
    p2 = jnp.exp(e2 - e1) / z
    w1 = p1 / (p1 + p2) * g_w
    w2 = p2 / (p1 + p2) * g_w
    id1 = (i1 - ROUTE_EXPERT_ROW).astype(F32)
    id2 = (i2 - ROUTE_EXPERT_ROW).astype(F32)
    return jnp.where(sub == 0, w1, jnp.where(sub == 1, w2, jnp.where(sub == 2, id1, jnp.where(sub == 3, id2, 0.0))))


def _slab_columns(rt):
    pad = jnp.zeros((LANES - rt.shape[0], rt.shape[1]), F32)
    return jnp.transpose(jnp.concatenate([rt, pad], axis=0))


def _mix_out_kernel(*refs, n_in, n_s):
    ap_refs = refs[:n_in]
    as_refs = refs[n_in:2 * n_in]
    w_refs = refs[2 * n_in:3 * n_in]
    xp_ref, xs_ref, g_ref, b_ref, rw_ref, out_ref, rt_ref = refs[3 * n_in:]
    i = pl.program_id(0)
    last = pl.num_programs(0) - 1

    @pl.when(i < last)
    def _():
        x1 = _mixed_rows([a[...] for a in ap_refs], w_refs, xp_ref[...], g_ref, b_ref)
        rt = _route(x1, rw_ref)
        out_ref[:, 0:D_MODEL] = x1
        out_ref[:, D_MODEL:D_MODEL + LANES] = _slab_columns(rt)
        rt_ref[...] = rt[0:8, :]

    @pl.when(i == last)
    def _():
        x1 = _mixed_rows([a[...] for a in as_refs], w_refs, xs_ref[...], g_ref, b_ref)
        rt = _route(jnp.concatenate([x1, jnp.zeros((LANES - n_s, D_MODEL), F32)], axis=0), rw_ref)
        out_ref[0:n_s, 0:D_MODEL] = x1
        out_ref[0:n_s, D_MODEL:D_MODEL + LANES] = _slab_columns(rt)[0:n_s, :]
        rt_ref[...] = jnp.zeros(rt_ref.shape, F32)
        rt_ref[:, 0:LANES] = rt[0:8, :]


def _mix_out(acts_p, acts_s, weights, xp, xs, ln_g, ln_b, rw):
    n_p, n_s = xp.shape[0], xs.shape[0]
    assert n_s <= LANES
    tm = ROW_TILE
    nb = n_p // tm
    n_in = len(acts_p)
    prow = lambda n: pl.BlockSpec((tm, n), lambda i: (jnp.minimum(i, nb - 1), 0))
    full = lambda a: pl.BlockSpec(a.shape, lambda i: (0, 0))
    width = D_MODEL + LANES
    return pl.pallas_call(
        functools.partial(_mix_out_kernel, n_in=n_in, n_s=n_s),
        grid=(nb + 1,),
        in_specs=[prow(a.shape[1]) for a in acts_p] + [full(a) for a in acts_s] + [full(w) for w in weights]
        + [prow(D_MODEL), full(xs), full(ln_g), full(ln_b), full(rw)],
        out_specs=[pl.BlockSpec((tm, width), lambda i: (i, 0)), pl.BlockSpec((8, tm), lambda i: (0, i))],
        out_shape=[jax.ShapeDtypeStruct((n_p + n_s, width), F32),
                   jax.ShapeDtypeStruct((8, (nb + 1) * tm), F32)],
        compiler_params=_cparams("arbitrary"),
        name="mix_out",
    )(*acts_p, *acts_s, *weights, xp, xs, ln_g, ln_b, rw)


def _route_plan(rt, tm):
    n = rt.shape[1]
    e1, e2 = rt[2, :].astype(I32), rt[3, :].astype(I32)
    ea, eb = jnp.minimum(e1, e2), jnp.maximum(e1, e2)
    la, lb = ea % EXP_PER_GROUP, eb % EXP_PER_GROUP
    pair_rank = sum(k * ((la == a) & (lb == b)).astype(I32) for k, (a, b) in enumerate(PAIR_SEQ))
    cls = (ea // EXP_PER_GROUP) * N_PAIRS + pair_rank
    onehot = (cls[:, None] == jnp.arange(N_CLASSES, dtype=I32)[None, :]).astype(I32)
    csum = jnp.cumsum(onehot, axis=0)
    rank = jnp.sum(onehot * csum, axis=1) - 1
    cnt = csum[-1]
    ntile = (cnt + tm - 1) // tm
    tile_end = jnp.cumsum(ntile)
    tile_start = tile_end - ntile
    n_used = tile_end[-1]
    n_tiles = -(-(n + N_CLASSES * (tm - 1)) // tm)
    pos = (jnp.sum(onehot * tile_start[None, :], axis=1) * tm + rank).astype(I32)
    pair_lo = np.array([a for a, _ in PAIR_SEQ], np.int32)
    pair_hi = np.array([b for _, b in PAIR_SEQ], np.int32)
    cls_ids = np.arange(N_CLASSES)
    cls_a = jnp.asarray((cls_ids // N_PAIRS) * EXP_PER_GROUP + pair_lo[cls_ids % N_PAIRS], I32)
    cls_b = jnp.asarray((cls_ids // N_PAIRS) * EXP_PER_GROUP + pair_hi[cls_ids % N_PAIRS], I32)
    tile_ids = jnp.arange(n_tiles, dtype=I32)
    tile_cls = jnp.sum((tile_end[None, :] <= jnp.minimum(tile_ids, n_used - 1)[:, None]).astype(I32), axis=1)
    tile_cls = jnp.minimum(tile_cls, N_CLASSES - 1)
    onehot_t = (tile_cls[:, None] == jnp.arange(N_CLASSES, dtype=I32)[None, :]).astype(I32)
    last_tile = jnp.where(ntile > 0, tile_end - 1, -1).astype(I32)
    ta = jnp.sum(onehot_t * cls_a[None, :], axis=1)
    tb = jnp.sum(onehot_t * cls_b[None, :], axis=1)
    return dict(pos=pos, ta=ta, tb=tb, nu=n_used.reshape(1).astype(I32), last_tile=last_tile, n_tiles=n_tiles)


def _dispatch_kernel(lt_ref, nu_ref, pos_ref, src_ref, wg_ref, wu_ref, wd_ref, dst_hbm, wg_out, wu_out, wd_out,
                     zbuf, zsem, rsem, *, moe_tile, n_tiles):
    i = pl.program_id(0)
    td = pos_ref.shape[2]

    def zero_copy(tile):
        start = pl.multiple_of(tile * moe_tile, moe_tile)
        return pltpu.make_async_copy(zbuf, dst_hbm.at[pl.ds(start, moe_tile)], zsem)

    @pl.when(i == 0)
    def _():
        zbuf[...] = jnp.zeros(zbuf.shape, F32)
        for c in range(N_CLASSES):
            @pl.when(lt_ref[c] >= 0)
            def _(c=c):
                zero_copy(lt_ref[c]).start()
        for c in range(N_CLASSES):
            @pl.when(lt_ref[c] >= 0)
            def _(c=c):
                zero_copy(lt_ref[c]).wait()

        def spare_start(t, carry):
            zero_copy(t).start()
            return carry

        def spare_wait(t, carry):
            zero_copy(t).wait()
            return carry

        lax.fori_loop(nu_ref[0], n_tiles, spare_start, 0)
        lax.fori_loop(nu_ref[0], n_tiles, spare_wait, 0)

    for r in range(td):
        pltpu.make_async_copy(src_ref.at[pl.ds(r, 1)], dst_hbm.at[pl.ds(pos_ref[0, 0, r], 1)],
                              rsem).start(priority=r % 2)

    @pl.when(i < N_EXPERTS)
    def _():
        wg_out[0] = wg_ref[0, 0].astype(BF16)
        wu_out[0] = wu_ref[0, 0].astype(BF16)
        wd_out[0] = wd_ref[0, 0].astype(BF16)

    pltpu.make_async_copy(src_ref, dst_hbm.at[pl.ds(0, td)], rsem).wait()


def _largest_divisor_tile(n, cap):
    for t in range(cap - cap % 8, 7, -8):
        if n % t == 0:
            return t
    raise ValueError(f"no row tile for {n} rows")


def _dispatch(x1e, plan, layer, w_gate, w_up, w_down, tm):
    n, width = x1e.shape
    n_tiles = plan["n_tiles"]
    td = _largest_divisor_tile(n, DISPATCH_TILE_CAP)
    steps = n // td
    assert steps >= N_EXPERTS
    expert = lambda i: jnp.minimum(i, N_EXPERTS - 1)
    w_in = lambda a: pl.BlockSpec((1, 1) + a.shape[2:], lambda i, lt, nu: (layer, expert(i), 0, 0))
    w_out = lambda a: pl.BlockSpec((1,) + a.shape[2:], lambda i, lt, nu: (expert(i), 0, 0))
    grid_spec = pltpu.PrefetchScalarGridSpec(
        num_scalar_prefetch=2,
        grid=(steps,),
        in_specs=[pl.BlockSpec((1, 1, td), lambda i, lt, nu: (i, 0, 0), memory_space=pltpu.SMEM),
                  pl.BlockSpec((td, width), lambda i, lt, nu: (i, 0)), w_in(w_gate), w_in(w_up), w_in(w_down)],
        out_specs=[pl.BlockSpec(memory_space=pl.ANY), w_out(w_gate), w_out(w_up), w_out(w_down)],
        scratch_shapes=[pltpu.VMEM((tm, width), F32), pltpu.SemaphoreType.DMA(()),
                        pltpu.SemaphoreType.DMA(())],
    )
    return pl.pallas_call(
        functools.partial(_dispatch_kernel, moe_tile=tm, n_tiles=n_tiles),
        grid_spec=grid_spec,
        out_shape=[jax.ShapeDtypeStruct((n_tiles * tm, width), F32)]
        + [jax.ShapeDtypeStruct(a.shape[1:], BF16) for a in (w_gate, w_up, w_down)],
        compiler_params=_cparams("arbitrary"),
        name="dispatch",
    )(plan["last_tile"], plan["nu"], plan["pos"].reshape(steps, 1, td), x1e, w_gate, w_up, w_down)


def _moe_kernel(ta_ref, tb_ref, nu_ref, x_ref, ga_ref, ua_ref, da_ref, gb_ref, ub_ref, db_ref, lg_ref, lb_ref,
                o_ref):
    del ta_ref, tb_ref
    g = pl.program_id(0)

    @pl.when(g < nu_ref[0])
    def _():
        x = x_ref[:, 0:D_MODEL]
        slab = x_ref[:, D_MODEL:D_MODEL + LANES]
        w1, w2, e1, e2 = slab[:, 0:1], slab[:, 1:2], slab[:, 2:3], slab[:, 3:4]
        first = e1 < e2
        wa = jnp.where(first, w1, w2)
        wb = jnp.where(first, w2, w1)
        xb = x.astype(BF16)
        gate_a, up_a = _dot(xb, ga_ref[0]), _dot(xb, ua_ref[0])
        gate_b, up_b = _dot(xb, gb_ref[0]), _dot(xb, ub_ref[0])
        y = wa * _dot((jax.nn.silu(gate_a) * up_a).astype(BF16), da_ref[0])
        y = y + wb * _dot((jax.nn.silu(gate_b) * up_b).astype(BF16), db_ref[0])
        o_ref[...] = _layer_norm_rows(DEEPNORM_ALPHA * x + y, lg_ref[...], lb_ref[...])

    @pl.when(g >= nu_ref[0])
    def _():
        o_ref[...] = jnp.zeros(o_ref.shape, F32)


def _moe(xs_sorted, plan, wg_bf, wu_bf, wd_bf, ln_g, ln_b, tm):
    n_tiles = plan["n_tiles"]
    width = xs_sorted.shape[1]
    up_a = pl.BlockSpec((1, D_MODEL, D_EXPERT), lambda g, ta, tb, nu: (ta[g], 0, 0))
    dn_a = pl.BlockSpec((1, D_EXPERT, D_MODEL), lambda g, ta, tb, nu: (ta[g], 0, 0))
    up_b = pl.BlockSpec((1, D_MODEL, D_EXPERT), lambda g, ta, tb, nu: (tb[g], 0, 0))
    dn_b = pl.BlockSpec((1, D_EXPERT, D_MODEL), lambda g, ta, tb, nu: (tb[g], 0, 0))
    vec = pl.BlockSpec((1, D_MODEL), lambda g, ta, tb, nu: (0, 0))
    grid_spec = pltpu.PrefetchScalarGridSpec(
        num_scalar_prefetch=3,
        grid=(n_tiles,),
        in_specs=[pl.BlockSpec((tm, width), lambda g, ta, tb, nu: (g, 0)),
                  up_a, up_a, dn_a, up_b, up_b, dn_b, vec, vec],
        out_specs=pl.BlockSpec((tm, D_MODEL), lambda g, ta, tb, nu: (g, 0)),
    )
    return pl.pallas_call(
        _moe_kernel,
        grid_spec=grid_spec,
        out_shape=jax.ShapeDtypeStruct((n_tiles * tm, D_MODEL), F32),
        compiler_params=_cparams("arbitrary"),
        name="moe",
    )(plan["ta"], plan["tb"], plan["nu"], xs_sorted, wg_bf, wu_bf, wd_bf, wg_bf, wu_bf, wd_bf, ln_g, ln_b)


def _ple_kernel(pos_ref, posn_ref, x2_hbm, pp_ref, ps_ref, wg_ref, wp_ref, op_ref, os_ref, buf, sems, *, n_s):
    i = pl.program_id(0)
    last = pl.num_programs(0) - 1
    tm = buf.shape[1]

    def issue(idx_ref, s):
        for r in range(tm):
            pltpu.make_async_copy(x2_hbm.at[pl.ds(idx_ref[0, 0, r], 1)], buf.at[s, pl.ds(r, 1)],
                                  sems.at[s]).start(priority=r % 2)

    def rows(x, p):
        gate = jax.nn.sigmoid(_dot(x.astype(BF16), wg_ref[...]))
        return x + gate * _dot(p.astype(BF16), wp_ref[...])

    def step(slot):
        if slot == 0:
            @pl.when(i == 0)
            def _():
                issue(pos_ref, 0)

        @pl.when(i < last)
        def _():
            issue(posn_ref, 1 - slot)

        pltpu.make_async_copy(x2_hbm.at[pl.ds(0, tm)], buf.at[slot], sems.at[slot]).wait()

        @pl.when(i < last)
        def _():
            op_ref[...] = rows(buf[slot], pp_ref[0])

        @pl.when(i == last)
        def _():
            os_ref[...] = rows(buf[slot, 0:n_s, :], ps_ref[0])

    for slot in range(2):
        @pl.when(lax.rem(i, 2) == slot)
        def _(slot=slot):
            step(slot)


def _ple(x2_sorted, pos, layer, p_p, p_s, wg_bf, wp_bf):
    n_p, n_s = p_p.shape[1], p_s.shape[1]
    tm = GATHER_TILE
    nb = n_p // tm
    steps = nb + 1
    pos_pad = jnp.zeros((steps * tm,), I32).at[:n_p + n_s].set(pos).reshape(steps, 1, tm)
    full = lambda a: pl.BlockSpec(a.shape, lambda i: (0, 0))
    prow = lambda n: pl.BlockSpec((tm, n), lambda i: (jnp.minimum(i, nb - 1), 0))
    return pl.pallas_call(
        functools.partial(_ple_kernel, n_s=n_s),
        grid=(steps,),
        in_specs=[pl.BlockSpec((1, 1, tm), lambda i: (i, 0, 0), memory_space=pltpu.SMEM),
                  pl.BlockSpec((1, 1, tm), lambda i: (jnp.minimum(i + 1, nb), 0, 0), memory_space=pltpu.SMEM),
                  pl.BlockSpec(memory_space=pl.ANY),
                  pl.BlockSpec((1, tm, D_PLE), lambda i: (layer, jnp.minimum(i, nb - 1), 0)),
                  pl.BlockSpec((1, n_s, D_PLE), lambda i: (layer, 0, 0)), full(wg_bf), full(wp_bf)],
        out_specs=[prow(D_MODEL), pl.BlockSpec((n_s, D_MODEL), lambda i: (0, 0))],
        out_shape=[jax.ShapeDtypeStruct((n_p, D_MODEL), F32), jax.ShapeDtypeStruct((n_s, D_MODEL), F32)],
        scratch_shapes=[pltpu.VMEM((2, tm, D_MODEL), F32), pltpu.SemaphoreType.DMA((2,))],
        compiler_params=_cparams("arbitrary"),
        name="ple",
    )(pos_pad, pos_pad, x2_sorted, p_p, p_s, wg_bf, wp_bf)


def _router_weights(w_group, w_router):
    wr = jnp.zeros((ROUTE_ROWS, D_MODEL), F32)
    wr = wr.at[ROUTE_GROUP_ROW:ROUTE_GROUP_ROW + N_GROUPS, :].set(jnp.transpose(w_group))
    wr = wr.at[ROUTE_EXPERT_ROW:ROUTE_EXPERT_ROW + N_EXPERTS, :].set(jnp.transpose(w_router))
    hi = wr.astype(BF16)
    lo = (wr - hi.astype(F32)).astype(BF16)
    return jnp.concatenate([hi, lo], axis=0)


def _layer_tail(i, acts_p, acts_s, w_list, xp, xs, p_p, p_s, ln_mix_g, ln_mix_b, ln_ffn_g, ln_ffn_b,
                w_group, w_router, w_exp_gate, w_exp_up, w_exp_down, w_ple_proj, w_ple_gate):
    rw = _router_weights(w_group[i], w_router[i])
    x1e, rt = _mix_out(acts_p, acts_s, w_list, xp, xs, ln_mix_g[i][None, :], ln_mix_b[i][None, :], rw)
    plan = _route_plan(rt[:, :x1e.shape[0]], MOE_TILE)
    xs_sorted, wg_bf, wu_bf, wd_bf = _dispatch(x1e, plan, i, w_exp_gate, w_exp_up, w_exp_down, MOE_TILE)
    x2_sorted = _moe(xs_sorted, plan, wg_bf, wu_bf, wd_bf, ln_ffn_g[i][None, :], ln_ffn_b[i][None, :], MOE_TILE)
    return _ple(x2_sorted, plan["pos"], i, p_p, p_s, w_ple_gate[i].astype(BF16), w_ple_proj[i].astype(BF16))


def kernel(x_prompt, x_sample, cache_k, cache_v, page_table, state_conv, state_mlstm_C, state_mlstm_n,
           state_mlstm_m, p_prompt, p_sample, w_in_even, conv_w, lambda_q1, lambda_k1, lambda_q2, lambda_k2,
           subln_w, w_out_even, w_in_odd, b_gates_odd, mh_norm_w, w_out_odd, ln_mix_g, ln_mix_b, ln_ffn_g,
           ln_ffn_b, w_group, w_router, w_exp_gate, w_exp_up, w_exp_down, w_ple_proj, w_ple_gate):
    bp, tp, _ = x_prompt.shape
    bs, ts, _ = x_sample.shape
    assert ts == 1 and tp % ROW_TILE == 0 and tp % ATTN_TILE == 0 and tp % MLSTM_CHUNK == 0
    assert (bp * tp) % GATHER_TILE == 0
    n_p = bp * tp
    past_len = page_table.shape[1] * cache_k.shape[2]
    xp = x_prompt.reshape(n_p, D_MODEL)
    xs = x_sample.reshape(bs, D_MODEL)
    tail_w = (ln_mix_g, ln_mix_b, ln_ffn_g, ln_ffn_b, w_group, w_router, w_exp_gate, w_exp_up, w_exp_down,
              w_ple_proj, w_ple_gate)
    outs_p, outs_s = {}, {}
    for i in range(DEPTH):
        j = i // 2
        p_p = p_prompt.reshape(DEPTH, n_p, D_PLE)
        p_s = p_sample.reshape(DEPTH, bs, D_PLE)
        if i % 2 == 0:
            lam_init = 0.8 - 0.6 * math.exp(-0.3 * i)
            lam_vecs = jnp.stack([lambda_q1[j], lambda_k1[j], lambda_q2[j], lambda_k2[j]])
            sub = subln_w[j][None, :]
            w_bf = w_in_even[j].astype(BF16)
            tabs_p = _rope_tables(jnp.arange(tp))
            q0 = 3 * D_CONV
            wqt_bf = jnp.transpose(w_in_even[j][:, q0:q0 + QK_B]).astype(BF16)
            yc, qt, kf, vf, kb, vt, cst = _even_in_prompt(
                x_prompt if i == 0 else xp.reshape(bp, tp, D_MODEL), w_bf, wqt_bf, conv_w[j],
                jnp.zeros((bp, CONV_W - 1, D_CONV), F32), tabs_p)
            o_p = _attn_prompt(qt, kb, vt, lam_vecs, sub, lam_init)
            outs_p.setdefault("k", []).append(kf.reshape(bp, tp, 2 * H_B, DH_B))
            outs_p.setdefault("v", []).append(vf.reshape(bp, tp, H_B, 2 * DH_B))
            outs_p.setdefault("c", []).append(cst)
            tabs_s = _rope_tables(jnp.full((1,), past_len, I32))
            prev_t = jnp.swapaxes(state_conv[j], 0, 1)
            yc_s, q_s, kf_s, vf_s, u_s = _even_in_decode(xs, w_bf, conv_w[j], prev_t, tabs_s)
            sub_head = jnp.arange(2 * H_B)
            sub_head = jnp.where(sub_head < H_B, 2 * sub_head, 2 * (sub_head - H_B) + 1)
            lane_head = jnp.arange(QK_B) // DH_B
            qbd = jnp.where(lane_head[None, None, :] == sub_head[None, :, None], q_s[:, None, :],
                            jnp.zeros((), BF16))
            n_pool = cache_k.shape[1]
            pages = cache_k.shape[0] * n_pool
            k_view = jnp.transpose(cache_k, (0, 1, 3, 4, 2)).reshape(pages, QK_B, PAGE_SIZE)
            v_view = cache_v.reshape(pages, PAGE_SIZE * H_B, 2 * DH_B)
            o8 = _attn_decode(qbd, kf_s[:, None, :], vf_s[:, None, :], k_view, v_view,
                              page_table + j * n_pool, lam_vecs, sub, lam_init)
            o_s = o8[:, :H_B, :].reshape(bs, V_B).astype(BF16)
            outs_s.setdefault("k", []).append(kf_s.reshape(bs, ts, 2 * H_B, DH_B))
            outs_s.setdefault("v", []).append(vf_s.reshape(bs, ts, H_B, 2 * DH_B))
            outs_s.setdefault("c", []).append(jnp.stack([state_conv[j][:, 1, :], u_s], axis=1))
            w_out = w_out_even[j].astype(BF16)
            w_list = [w_out[:D_CONV], w_out[D_CONV:]]
            acts_p = [yc.reshape(n_p, D_CONV), o_p.reshape(n_p, V_B)]
            acts_s = [yc_s, o_s]
        else:
            w_in = w_in_odd[j]
            qw, vw = H_C * DK_C, H_C * DV_C
            w_bf = w_in[:, :2 * qw + 2 * vw].astype(BF16)
            wg = jnp.zeros((D_MODEL, LANES), F32).at[:, :2 * H_C].set(w_in[:, 2 * qw + 2 * vw:]).astype(BF16)
            wgt = jnp.transpose(wg[:, :2 * H_C])
            bg = jnp.zeros((1, LANES), F32).at[0, :2 * H_C].set(b_gates_odd[j])
            bgt = jnp.broadcast_to(b_gates_odd[j][:, None], (2 * H_C, LANES))
            nw = mh_norm_w[j][None, :]
            q, k, v, o, gc, gr, kt = _odd_in(xp.reshape(bp, tp, D_MODEL), w_bf, wg, wgt, bg, bgt, decode=False)
            h_p, c_p, n_pp, m_p = _mlstm(q, k, kt, v, o, gc, gr, nw,
                                         jnp.zeros((bp, H_C, DK_C, DV_C), F32), jnp.zeros((bp, H_C, DK_C), F32),
                                         jnp.zeros((bp, H_C, LANES), F32), MLSTM_CHUNK)
            outs_p.setdefault("C", []).append(c_p)
            outs_p.setdefault("n", []).append(n_pp)
            outs_p.setdefault("m", []).append(m_p[:, :, 0])
            q_s, k_s, v_s, o_s2, gc_s, gr_s, kt_s = _odd_in(xs, w_bf, wg, wgt, bg, bgt, decode=True)
            ch = MLSTM_DECODE_CHUNK
            pad_rows = lambda a: jnp.zeros((bs, ch, a.shape[1]), a.dtype).at[:, 0, :].set(a)
            lane = jnp.arange(LANES)
            inert_c = jnp.where(lane < H_C, -jnp.inf, 0.0).astype(F32)
            gc_pad = jnp.broadcast_to(inert_c[None, None, :], (bs, ch, LANES)).at[:, 0, :].set(gc_s)
            inert_r = jnp.where(jnp.arange(2 * H_C) < H_C, -jnp.inf, 0.0).astype(F32)
            gr_pad = jnp.broadcast_to(inert_r[None, :, None], (bs, 2 * H_C, ch)).at[:, :, 0].set(gr_s.T)
            m0 = jnp.broadcast_to(state_mlstm_m[j][:, :, None], (bs, H_C, LANES))
            kt_pad = jnp.zeros((bs, qw, ch), BF16).at[:, :, 0].set(kt_s.T)
            h_s, c_s, n_s, m_s = _mlstm(pad_rows(q_s), pad_rows(k_s), kt_pad, pad_rows(v_s), pad_rows(o_s2),
                                        gc_pad, gr_pad, nw, state_mlstm_C[j], state_mlstm_n[j], m0, ch)
            outs_s.setdefault("C", []).append(c_s)
            outs_s.setdefault("n", []).append(n_s)
            outs_s.setdefault("m", []).append(m_s[:, :, 0])
            w_list = [w_out_odd[j].astype(BF16)]
            acts_p = [h_p.reshape(n_p, vw)]
            acts_s = [h_s[:, 0, :]]
        xp, xs = _layer_tail(i, acts_p, acts_s, w_list, xp, xs, p_p, p_s, *tail_w)
    st = lambda lst: jnp.stack(lst)
    return (xp.reshape(bp, tp, D_MODEL), xs.reshape(bs, ts, D_MODEL),
            st(outs_p["k"]), st(outs_p["v"]), st(outs_p["c"]), st(outs_p["C"]), st(outs_p["n"]), st(outs_p["m"]),
            st(outs_s["k"]), st(outs_s["v"]), st(outs_s["c"]), st(outs_s["C"]), st(outs_s["n"]), st(outs_s["m"]))
```

```python
import functools
import math

import numpy as np
import jax
import jax.numpy as jnp
from jax import lax
from jax.experimental import pallas as pl
from jax.experimental.pallas import tpu as pltpu

F32 = jnp.float32
BF16 = jnp.bfloat16
I32 = jnp.int32

D_MODEL = 1024
DEPTH = 2
PAGE_SIZE = 128
D_CONV = D_MODEL // 2
CONV_W = 3
H_B = 4
DH_B = 64
ROT_DIM = DH_B // 4
ROPE_THETA = 500000.0
H_C = 4
DK_C = (D_MODEL // 2) // H_C
DV_C = D_MODEL // H_C
N_GROUPS = 4
EXP_PER_GROUP = 4
N_EXPERTS = N_GROUPS * EXP_PER_GROUP
D_EXPERT = 512
D_PLE = 256
LN_EPS = 1e-5
LOG2_E = 1.4426950408889634
DEEPNORM_ALPHA = (2 * DEPTH) ** 0.25
QK_B = 2 * H_B * DH_B
V_B = H_B * 2 * DH_B
N_PAIRS = EXP_PER_GROUP * (EXP_PER_GROUP - 1) // 2
N_CLASSES = N_GROUPS * N_PAIRS
PAIR_SEQ = ((0, 1), (0, 2), (1, 2), (1, 3), (0, 3), (2, 3))
assert EXP_PER_GROUP == 4 and len(PAIR_SEQ) == N_PAIRS

LANES = 128
VMEM_LIMIT = 56 * 1024 * 1024
ROW_TILE = 1024
GATHER_TILE = 512
ATTN_TILE = 512
ATTN_HEADS_PER_STEP = 4
ONES_ROWS = 16
MLSTM_CHUNK = 128
MLSTM_DECODE_CHUNK = 16
MLSTM_BATCH_PER_STEP = 1
DISPATCH_TILE_CAP = 1024
MOE_TILE = 256
PAGES_PER_STEP = 32
ROUTE_ROWS = 32
ROUTE_GROUP_ROW = 0
ROUTE_EXPERT_ROW = 8


def _cparams(*sem):
    return pltpu.CompilerParams(dimension_semantics=sem, vmem_limit_bytes=VMEM_LIMIT)


def _dot(a, b):
    return jnp.dot(a, b, preferred_element_type=F32)


def _dot_nt(a, b):
    return lax.dot_general(a, b, (((1,), (1,)), ((), ())), preferred_element_type=F32)


def _layer_norm_rows(z, g, b):
    mu = jnp.mean(z, axis=-1, keepdims=True)
    zc = z - mu
    var = jnp.mean(zc * zc, axis=-1, keepdims=True)
    return zc * lax.rsqrt(var + LN_EPS) * g + b


def _rounding_specs(weights, layer, steps, flat_step):
    assert steps % N_EXPERTS == 0
    parts = steps // N_EXPERTS

    def spec_in(a):
        return pl.BlockSpec((1, 1, a.shape[2] // parts, a.shape[3]),
                            lambda *g: (layer, flat_step(*g) // parts, flat_step(*g) % parts, 0))

    def spec_out(a):
        return pl.BlockSpec((1, a.shape[2] // parts, a.shape[3]),
                            lambda *g: (flat_step(*g) // parts, flat_step(*g) % parts, 0))

    return ([spec_in(a) for a in weights], [spec_out(a) for a in weights],
            [jax.ShapeDtypeStruct(a.shape[1:], BF16) for a in weights])


def _round_blocks(in_refs, out_refs):
    for src, dst in zip(in_refs, out_refs):
        dst[0] = src[0, 0].astype(BF16)


def _log_sigmoid(x):
    return jnp.minimum(x, 0.0) - jnp.log1p(jnp.exp(-jnp.abs(x)))


def _gated_conv(gate_b, u, um1, um2, cw_ref):
    cw = cw_ref[...]
    conv = um2 * cw[0:1, :] + um1 * cw[1:2, :] + u * cw[2:3, :]
    return (gate_b * conv).astype(BF16)


def _rope(z, cos, sin_lo, sin_hi, axis):
    half = ROT_DIM // 2
    return z * cos + pltpu.roll(z, QK_B - half, axis) * sin_lo + pltpu.roll(z, half, axis) * sin_hi


def _even_in_prompt_kernel(x_ref, w_ref, wqt_ref, cw_ref, prev_ref, rc_ref, rs1_ref, rs2_ref,
                           rct_ref, rs1t_ref, rs2t_ref,
                           yc_ref, qt_ref, kf_ref, vf_ref, kb_ref, vt_ref, u_ref, carry_ref):
    xb = x_ref[0].astype(BF16)
    tm = xb.shape[0]

    def proj(c0, n):
        return _dot(xb, w_ref[:, c0:c0 + n])

    gate_b = proj(0, D_CONV)
    u = proj(D_CONV, D_CONV) * proj(2 * D_CONV, D_CONV)
    j = pl.program_id(1)

    @pl.when(j == 0)
    def _():
        carry_ref[...] = prev_ref[0]

    row = lax.broadcasted_iota(I32, u.shape, 0)
    c2 = carry_ref[0:1, :]
    c1 = carry_ref[1:2, :]
    um1 = jnp.where(row == 0, c1, pltpu.roll(u, 1, 0))
    um2 = jnp.where(row == 0, c2, jnp.where(row == 1, c1, pltpu.roll(u, 2, 0)))
    carry_ref[...] = u[tm - 2:tm, :]
    u_ref[0] = u[tm - 2:tm, :]
    yc_ref[0] = _gated_conv(gate_b, u, um1, um2, cw_ref)

    reps = QK_B // LANES
    tile = lambda r, ax: jnp.concatenate([r[...]] * reps, axis=ax)
    k = _rope(proj(3 * D_CONV + QK_B, QK_B), tile(rc_ref, 1), tile(rs1_ref, 1), tile(rs2_ref, 1), 1)
    kf_ref[0] = k
    kb_ref[0] = k.astype(BF16)
    v = proj(3 * D_CONV + 2 * QK_B, V_B)
    for h in range(H_B):
        vf_ref[0, pl.ds(h, tm, stride=H_B), :] = v[:, h * 2 * DH_B:(h + 1) * 2 * DH_B]
    qt = _rope(_dot_nt(wqt_ref[...], xb), tile(rct_ref, 0), tile(rs1t_ref, 0), tile(rs2t_ref, 0), 0)
    qt_ref[0] = (qt * (DH_B ** -0.5 * LOG2_E)).astype(BF16)
    vt_ref[0] = jnp.transpose(v).astype(BF16)


def _even_in_decode_kernel(x_ref, w_ref, cw_ref, prev_ref, rc_ref, rs1_ref, rs2_ref,
                           yc_ref, q_ref, kf_ref, vf_ref, u_ref):
    xb = x_ref[...].astype(BF16)

    def proj(c0, n):
        return _dot(xb, w_ref[:, c0:c0 + n])

    gate_b = proj(0, D_CONV)
    u = proj(D_CONV, D_CONV) * proj(2 * D_CONV, D_CONV)
    u_ref[...] = u
    yc_ref[...] = _gated_conv(gate_b, u, prev_ref[1], prev_ref[0], cw_ref)
    reps = QK_B // LANES
    tile = lambda r: jnp.concatenate([r[...]] * reps, axis=1)
    cos, sin_lo, sin_hi = tile(rc_ref), tile(rs1_ref), tile(rs2_ref)
    q_ref[...] = (_rope(proj(3 * D_CONV, QK_B), cos, sin_lo, sin_hi, 1) * (DH_B ** -0.5)).astype(BF16)
    kf_ref[...] = _rope(proj(3 * D_CONV + QK_B, QK_B), cos, sin_lo, sin_hi, 1)
    vf_ref[...] = proj(3 * D_CONV + 2 * QK_B, V_B)


def _rope_tables(pos):
    half = ROT_DIM // 2
    inv = ROPE_THETA ** (-jnp.arange(half, dtype=F32) / half)
    ang = pos.astype(F32)[:, None] * inv[None, :]
    cos, sin = jnp.cos(ang), jnp.sin(ang)
    t = pos.shape[0]
    ones = jnp.ones((t, DH_B - ROT_DIM), F32)
    zeros = jnp.zeros((t, DH_B - ROT_DIM), F32)
    zh = jnp.zeros((t, half), F32)
    c = jnp.concatenate([cos, cos, ones], axis=1)
    s_lo = jnp.concatenate([-sin, zh, zeros], axis=1)
    s_hi = jnp.concatenate([zh, sin, zeros], axis=1)
    tile2 = lambda a: jnp.concatenate([a, a], axis=1)
    return tile2(c), tile2(s_lo), tile2(s_hi)


def _even_in_prompt(x, w_bf, wqt_bf, conv_w, conv_prev, tables):
    b, t, _ = x.shape
    tm = ROW_TILE
    row3 = lambda n: pl.BlockSpec((1, tm, n), lambda i, j: (i, j, 0))
    col3 = lambda n: pl.BlockSpec((1, n, tm), lambda i, j: (i, 0, j))
    full2 = lambda a: pl.BlockSpec(a.shape, lambda i, j: (0, 0))
    tab = pl.BlockSpec((tm, LANES), lambda i, j: (j, 0))
    tab_t = pl.BlockSpec((LANES, tm), lambda i, j: (0, j))
    st = pl.BlockSpec((1, CONV_W - 1, D_CONV), lambda i, j: (i, 0, 0))
    tables_t = [jnp.transpose(a) for a in tables]
    outs = [jax.ShapeDtypeStruct((b, t, D_CONV), BF16), jax.ShapeDtypeStruct((b, QK_B, t), BF16),
            jax.ShapeDtypeStruct((b, t, QK_B), F32), jax.ShapeDtypeStruct((b, t * H_B, 2 * DH_B), F32),
            jax.ShapeDtypeStruct((b, t, QK_B), BF16), jax.ShapeDtypeStruct((b, V_B, t), BF16),
            jax.ShapeDtypeStruct((b, CONV_W - 1, D_CONV), F32)]
    return pl.pallas_call(
        _even_in_prompt_kernel,
        grid=(b, t // tm),
        in_specs=[row3(D_MODEL), full2(w_bf), full2(wqt_bf), full2(conv_w), st,
                  tab, tab, tab, tab_t, tab_t, tab_t],
        out_specs=[row3(D_CONV), col3(QK_B), row3(QK_B),
                   pl.BlockSpec((1, tm * H_B, 2 * DH_B), lambda i, j: (i, j, 0)), row3(QK_B), col3(V_B), st],
        out_shape=outs,
        scratch_shapes=[pltpu.VMEM((CONV_W - 1, D_CONV), F32)],
        compiler_params=_cparams("arbitrary", "arbitrary"),
        name="even_in_prompt",
    )(x, w_bf, wqt_bf, conv_w, conv_prev, *tables, *tables_t)


def _even_in_decode(x, w_bf, conv_w, conv_prev_t, tables):
    n = x.shape[0]
    full = lambda a: pl.BlockSpec(a.shape, lambda i: (0,) * a.ndim)
    o2 = lambda c, dt: jax.ShapeDtypeStruct((n, c), dt)
    outs = [o2(D_CONV, BF16), o2(QK_B, BF16), o2(QK_B, F32), o2(V_B, F32), o2(D_CONV, F32)]
    ins = [x, w_bf, conv_w, conv_prev_t, *tables]
    return pl.pallas_call(
        _even_in_decode_kernel,
        grid=(1,),
        in_specs=[full(a) for a in ins],
        out_specs=[pl.BlockSpec(o.shape, lambda i: (0, 0)) for o in outs],
        out_shape=outs,
        compiler_params=_cparams("arbitrary"),
        name="even_in_decode",
    )(*ins)


def _lambda_value(lam_ref, lam_init):
    lv = lam_ref[...]
    a = jnp.sum(lv[0:1, :] * lv[1:2, :], axis=1, keepdims=True)
    b = jnp.sum(lv[2:3, :] * lv[3:4, :], axis=1, keepdims=True)
    return jnp.exp(a) - jnp.exp(b) + lam_init


def _sub_norm(o, sub_ref, lam_init):
    ms = jnp.mean(o * o, axis=-1, keepdims=True)
    return o * lax.rsqrt(ms + LN_EPS) * sub_ref[...] * (1.0 - lam_init)


def _attn_prompt_kernel(qt_ref, k_ref, vt_ref, lam_ref, sub_ref, w0_ref, w1_ref, w2_ref,
                        o_ref, r0_ref, r1_ref, r2_ref, *scratch, lam_init):
    i = pl.program_id(2)
    tq = qt_ref.shape[2]
    tk = tq
    n_heads = qt_ref.shape[1] // LANES
    _round_blocks((w0_ref, w1_ref, w2_ref), (r0_ref, r1_ref, r2_ref))
    m_scrs, acc_scrs = scratch[:n_heads], scratch[n_heads:]
    for m_scr, acc_scr in zip(m_scrs, acc_scrs):
        m_scr[...] = jnp.full(m_scr.shape, -jnp.inf, F32)
        acc_scr[...] = jnp.zeros(acc_scr.shape, F32)
    ones = jnp.ones((ONES_ROWS, tk), BF16)
    qqs = []
    for h in range(n_heads):
        qt = qt_ref[0, h * LANES:(h + 1) * LANES, :].astype(F32)
        feat = lax.broadcasted_iota(I32, qt.shape, 0)
        qqs.append(jnp.concatenate([jnp.where(feat < DH_B, qt, 0.0), jnp.where(feat >= DH_B, qt, 0.0)],
                                   axis=1).astype(BF16))

    def step(j, masked):
        start = pl.multiple_of(j * tk, tk)
        scores = [_dot(k_ref[0, pl.ds(start, tk), h * LANES:(h + 1) * LANES], qqs[h])
                  for h in range(n_heads)]
        probs, alphas = [], []
        for h in range(n_heads):
            s = scores[h]
            if masked:
                key = lax.broadcasted_iota(I32, s.shape, 0)
                qry = lax.broadcasted_iota(I32, s.shape, 1)
                qry = jnp.where(qry >= tq, qry - tq, qry)
                s = jnp.where(key <= qry, s, -jnp.inf)
            m_prev = m_scrs[h][...]
            m_new = jnp.maximum(m_prev, jnp.max(s, axis=0, keepdims=True))
            alphas.append(jnp.exp2(m_prev - m_new))
            probs.append(jnp.exp2(s - m_new).astype(BF16))
            m_scrs[h][...] = m_new
        for h in range(n_heads):
            vtj = jnp.concatenate([vt_ref[0, h * LANES:(h + 1) * LANES, pl.ds(start, tk)], ones], axis=0)
            acc_scrs[h][...] = alphas[h] * acc_scrs[h][...] + _dot(vtj, probs[h])

    def body(j, carry):
        step(j, False)
        return carry

    lax.fori_loop(0, i, body, 0)
    step(i, True)
    lam = _lambda_value(lam_ref, lam_init)
    for h in range(n_heads):
        on = acc_scrs[h][0:LANES, :] / acc_scrs[h][LANES:LANES + 1, :]
        o = jnp.transpose(on[:, 0:tq] - lam * on[:, tq:2 * tq])
        o_ref[0, :, h * LANES:(h + 1) * LANES] = _sub_norm(o, sub_ref, lam_init).astype(BF16)


def _attn_prompt(qt, k, vt, lam_vecs, subln, lam_init, layer, expert_weights):
    b, t, _ = k.shape
    tq = ATTN_TILE
    nh = ATTN_HEADS_PER_STEP
    hw = nh * LANES
    n_h, n_q = H_B // nh, t // tq
    full = lambda a: pl.BlockSpec(a.shape, lambda bi, h, i: (0, 0))
    w_in, w_out, w_shapes = _rounding_specs(expert_weights, layer, b * n_h * n_q,
                                            lambda bi, h, i: (bi * n_h + h) * n_q + i)
    return pl.pallas_call(
        functools.partial(_attn_prompt_kernel, lam_init=lam_init),
        grid=(b, n_h, n_q),
        in_specs=[pl.BlockSpec((1, hw, tq), lambda bi, h, i: (bi, h, i)),
                  pl.BlockSpec((1, t, hw), lambda bi, h, i: (bi, 0, h)),
                  pl.BlockSpec((1, hw, t), lambda bi, h, i: (bi, h, 0)),
                  full(lam_vecs), full(subln)] + w_in,
        out_specs=[pl.BlockSpec((1, tq, hw), lambda bi, h, i: (bi, i, h))] + w_out,
        out_shape=[jax.ShapeDtypeStruct((b, t, V_B), BF16)] + w_shapes,
        scratch_shapes=[pltpu.VMEM((1, 2 * tq), F32)] * nh + [pltpu.VMEM((LANES + ONES_ROWS, 2 * tq), F32)] * nh,
        compiler_params=_cparams("arbitrary", "arbitrary", "arbitrary"),
        name="attn_prompt",
    )(qt, k, vt, lam_vecs, subln, *expert_weights)


def _attn_decode_kernel(pt_ref, qbd_ref, kn_ref, vn_ref, lam_ref, sub_ref, *rest, lam_init, n_pages):
    del pt_ref
    k_refs = rest[:n_pages]
    v_refs = rest[n_pages:2 * n_pages]
    o_ref, m_scr, l_scr, acc_scr = rest[2 * n_pages:]
    j = pl.program_id(1)
    qbd = qbd_ref[0]

    @pl.when(j == 0)
    def _():
        s_new = jnp.sum(qbd.astype(F32) * kn_ref[0], axis=1, keepdims=True)
        m_scr[...] = s_new
        l_scr[...] = jnp.ones(l_scr.shape, F32)
        acc_scr[...] = jnp.broadcast_to(vn_ref[0], acc_scr.shape)

    s = jnp.concatenate([_dot(qbd, k_refs[r][0].astype(BF16)) for r in range(n_pages)], axis=1)
    m_prev = m_scr[...]
    m_new = jnp.maximum(m_prev, jnp.max(s, axis=1, keepdims=True))
    alpha = jnp.exp(m_prev - m_new)
    p = jnp.exp(s - m_new)
    l_scr[...] = alpha * l_scr[...] + jnp.sum(p, axis=1, keepdims=True)

    def head_pv(h):
        acc = None
        for r in range(n_pages):
            vh = v_refs[r][0, pl.ds(h, PAGE_SIZE, stride=H_B), :].astype(BF16)
            term = _dot(p[:, r * PAGE_SIZE:(r + 1) * PAGE_SIZE].astype(BF16), vh)
            acc = term if acc is None else acc + term
        return acc

    pv = jnp.concatenate([head_pv(h) for h in range(H_B)], axis=1)
    acc_scr[...] = alpha * acc_scr[...] + pv
    m_scr[...] = m_new

    @pl.when(j == pl.num_programs(1) - 1)
    def _():
        on = acc_scr[...] / l_scr[...]
        row = lax.broadcasted_iota(I32, (2 * H_B, 2 * DH_B), 0)
        head = jnp.where(row >= H_B, row - H_B, row)
        o8 = jnp.zeros((2 * H_B, 2 * DH_B), F32)
        for c in range(H_B):
            o8 = o8 + jnp.where(head == c, on[:, c * 2 * DH_B:(c + 1) * 2 * DH_B], 0.0)
        lam = _lambda_value(lam_ref, lam_init)
        o = o8 - lam * pltpu.roll(o8, H_B, 0)
        o_ref[0] = _sub_norm(o, sub_ref, lam_init)


def _attn_decode(qbd, k_new, v_new, cache_k, cache_v, page_table, lam_vecs, subln, lam_init):
    n = qbd.shape[0]
    n_pages = page_table.shape[1]
    pp = PAGES_PER_STEP
    width = V_B
    c2 = lambda a: pl.BlockSpec(a.shape, lambda b, j, pt: (0, 0))
    per_b = lambda a: pl.BlockSpec((1,) + a.shape[1:], lambda b, j, pt: (b, 0, 0))

    def page(r, arr):
        return pl.BlockSpec((1,) + arr.shape[1:], lambda b, j, pt: (pt[b, j * pp + r], 0, 0))

    grid_spec = pltpu.PrefetchScalarGridSpec(
        num_scalar_prefetch=1,
        grid=(n, n_pages // pp),
        in_specs=[per_b(qbd), per_b(k_new), per_b(v_new), c2(lam_vecs), c2(subln)]
        + [page(r, cache_k) for r in range(pp)] + [page(r, cache_v) for r in range(pp)],
        out_specs=pl.BlockSpec((1, 2 * H_B, 2 * DH_B), lambda b, j, pt: (b, 0, 0)),
        scratch_shapes=[pltpu.VMEM((2 * H_B, 1), F32), pltpu.VMEM((2 * H_B, 1), F32),
                        pltpu.VMEM((2 * H_B, width), F32)],
    )
    return pl.pallas_call(
        functools.partial(_attn_decode_kernel, lam_init=lam_init, n_pages=pp),
        grid_spec=grid_spec,
        out_shape=jax.ShapeDtypeStruct((n, 2 * H_B, 2 * DH_B), F32),
        compiler_params=_cparams("arbitrary", "arbitrary"),
        name="attn_decode",
    )(page_table, qbd, k_new, v_new, lam_vecs, subln, *([cache_k] * pp), *([cache_v] * pp))


def _odd_in_kernel(x_ref, w_ref, wg_ref, wgt_ref, bg_ref, bgt_ref,
                   q_ref, k_ref, v_ref, o_ref, gc_ref, gr_ref, kt_ref, *, decode):
    xb = x_ref[0].astype(BF16) if not decode else x_ref[...].astype(BF16)
    qw = H_C * DK_C
    vw = H_C * DV_C
    q = _dot(xb, w_ref[:, 0:qw]) * (DK_C ** -0.5)
    k = _dot(xb, w_ref[:, qw:2 * qw])
    v = _dot(xb, w_ref[:, 2 * qw:2 * qw + vw])
    o = _dot(xb, w_ref[:, 2 * qw + vw:2 * qw + 2 * vw])
    g_col = _dot(xb, wg_ref[...]) + bg_ref[...]
    lane = lax.broadcasted_iota(I32, g_col.shape, 1)
    g_col = jnp.where(lane < H_C, g_col, _log_sigmoid(g_col))
    k_t = jnp.transpose(k).astype(BF16)
    g_row = _dot_nt(wgt_ref[...], xb) + bgt_ref[:, 0:1]
    sub = lax.broadcasted_iota(I32, g_row.shape, 0)
    g_row = jnp.where(sub < H_C, g_row, _log_sigmoid(g_row))
    if decode:
        kt_ref[...] = k_t
        q_ref[...] = q.astype(BF16)
        k_ref[...] = k.astype(BF16)
        v_ref[...] = v.astype(BF16)
        o_ref[...] = o
        gc_ref[...] = g_col
        gr_ref[...] = g_row
    else:
        q_ref[0] = q.astype(BF16)
        k_ref[0] = k.astype(BF16)
        v_ref[0] = v.astype(BF16)
        o_ref[0] = o
        gc_ref[0] = g_col
        gr_ref[0] = g_row
        kt_ref[0] = k_t


def _odd_in(x, w_bf, wg, wgt, bg, bgt, decode):
    qw, vw = H_C * DK_C, H_C * DV_C
    if decode:
        n = x.shape[0]
        ins = [x, w_bf, wg, wgt, bg, bgt]
        outs = [jax.ShapeDtypeStruct((n, qw), BF16), jax.ShapeDtypeStruct((n, qw), BF16),
                jax.ShapeDtypeStruct((n, vw), BF16), jax.ShapeDtypeStruct((n, vw), F32),
                jax.ShapeDtypeStruct((n, LANES), F32), jax.ShapeDtypeStruct((2 * H_C, n), F32),
                jax.ShapeDtypeStruct((qw, n), BF16)]
        return pl.pallas_call(
            functools.partial(_odd_in_kernel, decode=True),
            grid=(1,),
            in_specs=[pl.BlockSpec(a.shape, lambda i: (0, 0)) for a in ins],
            out_specs=[pl.BlockSpec(o.shape, lambda i: (0, 0)) for o in outs],
            out_shape=outs,
            compiler_params=_cparams("arbitrary"),
            name="odd_in_decode",
        )(*ins)
    b, t, _ = x.shape
    tm = ROW_TILE
    row3 = lambda n: pl.BlockSpec((1, tm, n), lambda i, j: (i, j, 0))
    full2 = lambda a: pl.BlockSpec(a.shape, lambda i, j: (0, 0))
    outs = [jax.ShapeDtypeStruct((b, t, qw), BF16), jax.ShapeDtypeStruct((b, t, qw), BF16),
            jax.ShapeDtypeStruct((b, t, vw), BF16), jax.ShapeDtypeStruct((b, t, vw), F32),
            jax.ShapeDtypeStruct((b, t, LANES), F32), jax.ShapeDtypeStruct((b, 2 * H_C, t), F32),
            jax.ShapeDtypeStruct((b, qw, t), BF16)]
    return pl.pallas_call(
        functools.partial(_odd_in_kernel, decode=False),
        grid=(b, t // tm),
        in_specs=[row3(D_MODEL), full2(w_bf), full2(wg), full2(wgt), full2(bg), full2(bgt)],
        out_specs=[row3(qw), row3(qw), row3(vw), row3(vw), row3(LANES),
                   pl.BlockSpec((1, 2 * H_C, tm), lambda i, j: (i, 0, j)),
                   pl.BlockSpec((1, qw, tm), lambda i, j: (i, 0, j))],
        out_shape=outs,
        compiler_params=_cparams("arbitrary", "arbitrary"),
        name="odd_in_prompt",
    )(x, w_bf, wg, wgt, bg, bgt)


def _mlstm_kernel(q_ref, k_ref, kt_ref, v_ref, o_ref, gc_ref, gr_ref, nw_ref, c0_ref, n0_ref, m0_ref,
                  *rest, n_round):
    w_refs, rest = rest[:n_round], rest[n_round:]
    h_ref, c_out, n_out, m_out = rest[:4]
    r_refs = rest[4:4 + n_round]
    c_scr, n_scr, m_scr = rest[4 + n_round:]
    ci = pl.program_id(1)
    chunk = q_ref.shape[1]
    _round_blocks(w_refs, r_refs)

    @pl.when(ci == 0)
    def _():
        c_scr[...] = c0_ref[...]
        n_scr[...] = n0_ref[...]
        m_scr[...] = m0_ref[...]

    t_idx = lax.broadcasted_iota(I32, (chunk, chunk), 0)
    s_idx = lax.broadcasted_iota(I32, (chunk, chunk), 1)
    causal = s_idx <= t_idx
    for bb, h in [(bb, h) for bb in range(q_ref.shape[0]) for h in range(H_C)]:
        q = q_ref[bb, :, h * DK_C:(h + 1) * DK_C]
        k = k_ref[bb, :, h * DK_C:(h + 1) * DK_C]
        v = v_ref[bb, :, h * DV_C:(h + 1) * DV_C]
        ig_r = gr_ref[bb, h:h + 1, :]
        lf_r = gr_ref[bb, H_C + h:H_C + h + 1, :]
        ig_c = gc_ref[bb, :, h:h + 1]
        lf_c = gc_ref[bb, :, H_C + h:H_C + h + 1]
        bcum_c = jnp.sum(jnp.where(causal, lf_r, 0.0), axis=1, keepdims=True)
        bcum_r = jnp.sum(jnp.where(t_idx <= s_idx, lf_c, 0.0), axis=0, keepdims=True)
        m0 = m_scr[bb, h:h + 1, 0:1]
        dmat = jnp.where(causal, bcum_c - bcum_r + ig_r, -jnp.inf)
        inter = bcum_c + m0
        m = jnp.maximum(inter, jnp.max(dmat, axis=1, keepdims=True))
        w = jnp.exp(dmat - m)
        g = jnp.exp(inter - m)
        s = _dot_nt(q, k) * w
        c0 = c_scr[bb, h]
        n0 = n_scr[bb, h:h + 1, :]
        num = g * _dot(q, c0.astype(BF16)) + _dot(s.astype(BF16), v)
        den = g * jnp.sum(q.astype(F32) * n0, axis=1, keepdims=True) + jnp.sum(s, axis=1, keepdims=True)
        hid = num / jnp.maximum(jnp.abs(den), jnp.exp(-m))
        m_last = m[chunk - 1:chunk, :]
        b_last = bcum_c[chunk - 1:chunk, :]
        w_last = jnp.exp(b_last - bcum_c + ig_c - m_last)
        g_last = jnp.exp(b_last + m0 - m_last)
        kw = k.astype(F32) * w_last
        w_last_r = jnp.exp(b_last - bcum_r + ig_r - m_last)
        kw_t = (kt_ref[bb, h * DK_C:(h + 1) * DK_C, :].astype(F32) * w_last_r).astype(BF16)
        c_scr[bb, h] = g_last * c0 + _dot(kw_t, v)
        n_scr[bb, h:h + 1, :] = g_last * n0 + jnp.sum(kw, axis=0, keepdims=True)
        m_scr[bb, h:h + 1, :] = jnp.broadcast_to(m_last, (1, LANES))
        mu = jnp.mean(hid, axis=1, keepdims=True)
        hc = hid - mu
        var = jnp.mean(hc * hc, axis=1, keepdims=True)
        hn = hc * lax.rsqrt(var + LN_EPS) * nw_ref[:, h * DV_C:(h + 1) * DV_C]
        gate = jax.nn.sigmoid(o_ref[bb, :, h * DV_C:(h + 1) * DV_C])
        h_ref[bb, :, h * DV_C:(h + 1) * DV_C] = (gate * hn).astype(BF16)

    c_out[...] = c_scr[...]
    n_out[...] = n_scr[...]
    m_out[...] = m_scr[...]


def _mlstm(q, k, k_t, v, o, g_col, g_row, norm_w, c0, n0, m0, ch, layer=None, expert_weights=()):
    b, t, _ = q.shape
    nb = MLSTM_BATCH_PER_STEP
    qw, vw = H_C * DK_C, H_C * DV_C
    n_c = t // ch
    w_in, w_out, w_shapes = ([], [], [])
    if expert_weights:
        w_in, w_out, w_shapes = _rounding_specs(expert_weights, layer, (b // nb) * n_c, lambda i, j: i * n_c + j)
    row = lambda n: pl.BlockSpec((nb, ch, n), lambda i, j: (i, j, 0))
    st4 = pl.BlockSpec((nb, H_C, DK_C, DV_C), lambda i, j: (i, 0, 0, 0))
    st3 = pl.BlockSpec((nb, H_C, LANES), lambda i, j: (i, 0, 0))
    outs = [jax.ShapeDtypeStruct((b, t, vw), BF16), jax.ShapeDtypeStruct((b, H_C, DK_C, DV_C), F32),
            jax.ShapeDtypeStruct((b, H_C, DK_C), F32), jax.ShapeDtypeStruct((b, H_C, LANES), F32)]
    return pl.pallas_call(
        functools.partial(_mlstm_kernel, n_round=len(w_in)),
        grid=(b // nb, n_c),
        in_specs=[row(qw), row(qw), pl.BlockSpec((nb, qw, ch), lambda i, j: (i, 0, j)), row(vw), row(vw),
                  row(LANES), pl.BlockSpec((nb, 2 * H_C, ch), lambda i, j: (i, 0, j)),
                  pl.BlockSpec(norm_w.shape, lambda i, j: (0, 0)), st4, st3, st3] + w_in,
        out_specs=[row(vw), st4, st3, st3] + w_out,
        out_shape=outs + w_shapes,
        scratch_shapes=[pltpu.VMEM((nb, H_C, DK_C, DV_C), F32), pltpu.VMEM((nb, H_C, DK_C), F32),
                        pltpu.VMEM((nb, H_C, LANES), F32)],
        compiler_params=_cparams("arbitrary", "arbitrary"),
        name="mlstm",
    )(q, k, k_t, v, o, g_col, g_row, norm_w, c0, n0, m0, *expert_weights)


def _mix_project(acts, w_refs):
    y = _dot(acts[0], w_refs[0][...])
    for a, w in zip(acts[1:], w_refs[1:]):
        y = y + _dot(a, w[...])
    return y


def _mixed_rows(acts, w_refs, x, g_ref, b_ref):
    return _layer_norm_rows(DEEPNORM_ALPHA * x + _mix_project(acts, w_refs), g_ref[...], b_ref[...])


def _route_logits(x1, rw_ref):
    xh = x1.astype(BF16)
    xl = (x1 - xh.astype(F32)).astype(BF16)
    both = _dot_nt(rw_ref[...], xh)
    return both[0:ROUTE_ROWS, :] + both[ROUTE_ROWS:2 * ROUTE_ROWS, :] + _dot_nt(rw_ref[0:ROUTE_ROWS, :], xl)


def _route(x1, rw_ref):
    return _route_decide(_route_logits(x1, rw_ref))


def _route_decide(lg):
    sub = lax.broadcasted_iota(I32, lg.shape, 0)
    big = jnp.int32(4 * ROUTE_ROWS)
    neg = -jnp.inf
    gl = jnp.where(sub < ROUTE_GROUP_ROW + N_GROUPS, lg, neg)
    g_max = jnp.max(gl, axis=0, keepdims=True)
    g_w = 1.0 / jnp.sum(jnp.exp(gl - g_max), axis=0, keepdims=True)
    g_idx = jnp.min(jnp.where(gl == g_max, sub, big), axis=0, keepdims=True)
    row_group = (sub - ROUTE_EXPERT_ROW) >> 2
    el = jnp.where(row_group == g_idx, lg, neg)
    e1 = jnp.max(el, axis=0, keepdims=True)
    i1 = jnp.min(jnp.where(el == e1, sub, big), axis=0, keepdims=True)
    z = jnp.sum(jnp.exp(el - e1), axis=0, keepdims=True)
    el2 = jnp.where(sub == i1, neg, el)
    e2 = jnp.max(el2, axis=0, keepdims=True)
    i2 = jnp.min(jnp.where(el2 == e2, sub, big), axis=0, keepdims=True)
    p1 = 1.0 / z
    p2 = jnp.exp(e2 - e1) / z
    w1 = p1 / (p1 + p2) * g_w
    w2 = p2 / (p1 + p2) * g_w
    id1 = (i1 - ROUTE_EXPERT_ROW).astype(F32)
    id2 = (i2 - ROUTE_EXPERT_ROW).astype(F32)
    return jnp.where(sub == 0, w1, jnp.where(sub == 1, w2, jnp.where(sub == 2, id1, jnp.where(sub == 3, id2, 0.0))))


def _slab_columns(rt):
    pad = jnp.zeros((LANES - rt.shape[0], rt.shape[1]), F32)
    return jnp.transpose(jnp.concatenate([rt, pad], axis=0))


def _mix_out_kernel(*refs, n_in, n_s):
    ap_refs = refs[:n_in]
    as_refs = refs[n_in:2 * n_in]
    w_refs = refs[2 * n_in:3 * n_in]
    xp_ref, xs_ref, g_ref, b_ref, rw_ref, out_ref, rt_ref = refs[3 * n_in:]
    i = pl.program_id(0)
    last = pl.num_programs(0) - 1

    @pl.when(i < last)
    def _():
        x1 = _mixed_rows([a[...] for a in ap_refs], w_refs, xp_ref[...], g_ref, b_ref)
        rt = _route(x1, rw_ref)
        out_ref[:, 0:D_MODEL] = x1
        out_ref[:, D_MODEL:D_MODEL + LANES] = _slab_columns(rt)
        rt_ref[...] = rt[0:8, :]

    @pl.when(i == last)
    def _():
        x1 = _mixed_rows([a[...] for a in as_refs], w_refs, xs_ref[...], g_ref, b_ref)
        rt = _route(jnp.concatenate([x1, jnp.zeros((LANES - n_s, D_MODEL), F32)], axis=0), rw_ref)
        out_ref[0:n_s, 0:D_MODEL] = x1
        out_ref[0:n_s, D_MODEL:D_MODEL + LANES] = _slab_columns(rt)[0:n_s, :]
        rt_ref[...] = jnp.zeros(rt_ref.shape, F32)
        rt_ref[:, 0:LANES] = rt[0:8, :]


def _mix_out(acts_p, acts_s, weights, xp, xs, ln_g, ln_b, rw):
    n_p, n_s = xp.shape[0], xs.shape[0]
    assert n_s <= LANES
    tm = ROW_TILE
    nb = n_p // tm
    n_in = len(acts_p)
    prow = lambda n: pl.BlockSpec((tm, n), lambda i: (jnp.minimum(i, nb - 1), 0))
    full = lambda a: pl.BlockSpec(a.shape, lambda i: (0, 0))
    width = D_MODEL + LANES
    return pl.pallas_call(
        functools.partial(_mix_out_kernel, n_in=n_in, n_s=n_s),
        grid=(nb + 1,),
        in_specs=[prow(a.shape[1]) for a in acts_p] + [full(a) for a in acts_s] + [full(w) for w in weights]
        + [prow(D_MODEL), full(xs), full(ln_g), full(ln_b), full(rw)],
        out_specs=[pl.BlockSpec((tm, width), lambda i: (i, 0)), pl.BlockSpec((8, tm), lambda i: (0, i))],
        out_shape=[jax.ShapeDtypeStruct((n_p + n_s, width), F32),
                   jax.ShapeDtypeStruct((8, (nb + 1) * tm), F32)],
        compiler_params=_cparams("arbitrary"),
        name="mix_out",
    )(*acts_p, *acts_s, *weights, xp, xs, ln_g, ln_b, rw)


def _route_plan(rt, tm):
    n = rt.shape[1]
    e1, e2 = rt[2, :].astype(I32), rt[3, :].astype(I32)
    ea, eb = jnp.minimum(e1, e2), jnp.maximum(e1, e2)
    la, lb = ea % EXP_PER_GROUP, eb % EXP_PER_GROUP
    pair_rank = sum(k * ((la == a) & (lb == b)).astype(I32) for k, (a, b) in enumerate(PAIR_SEQ))
    cls = (ea // EXP_PER_GROUP) * N_PAIRS + pair_rank
    onehot = (cls[:, None] == jnp.arange(N_CLASSES, dtype=I32)[None, :]).astype(I32)
    csum = jnp.cumsum(onehot, axis=0)
    rank = jnp.sum(onehot * csum, axis=1) - 1
    cnt = csum[-1]
    ntile = (cnt + tm - 1) // tm
    tile_end = jnp.cumsum(ntile)
    tile_start = tile_end - ntile
    n_used = tile_end[-1]
    n_tiles = -(-(n + N_CLASSES * (tm - 1)) // tm)
    pos = (jnp.sum(onehot * tile_start[None, :], axis=1) * tm + rank).astype(I32)
    pair_lo = np.array([a for a, _ in PAIR_SEQ], np.int32)
    pair_hi = np.array([b for _, b in PAIR_SEQ], np.int32)
    cls_ids = np.arange(N_CLASSES)
    cls_a = jnp.asarray((cls_ids // N_PAIRS) * EXP_PER_GROUP + pair_lo[cls_ids % N_PAIRS], I32)
    cls_b = jnp.asarray((cls_ids // N_PAIRS) * EXP_PER_GROUP + pair_hi[cls_ids % N_PAIRS], I32)
    tile_ids = jnp.arange(n_tiles, dtype=I32)
    tile_cls = jnp.sum((tile_end[None, :] <= jnp.minimum(tile_ids, n_used - 1)[:, None]).astype(I32), axis=1)
    tile_cls = jnp.minimum(tile_cls, N_CLASSES - 1)
    onehot_t = (tile_cls[:, None] == jnp.arange(N_CLASSES, dtype=I32)[None, :]).astype(I32)
    last_tile = jnp.where(ntile > 0, tile_end - 1, -1).astype(I32)
    ta = jnp.sum(onehot_t * cls_a[None, :], axis=1)
    tb = jnp.sum(onehot_t * cls_b[None, :], axis=1)
    return dict(pos=pos, ta=ta, tb=tb, nu=n_used.reshape(1).astype(I32), last_tile=last_tile, n_tiles=n_tiles)


def _dispatch_kernel(lt_ref, nu_ref, pos_ref, src_ref, dst_hbm, zbuf, zsem, rsem, *, moe_tile, n_tiles):
    i = pl.program_id(0)
    td = pos_ref.shape[2]

    def zero_copy(tile):
        start = pl.multiple_of(tile * moe_tile, moe_tile)
        return pltpu.make_async_copy(zbuf, dst_hbm.at[pl.ds(start, moe_tile)], zsem)

    @pl.when(i == 0)
    def _():
        zbuf[...] = jnp.zeros(zbuf.shape, F32)
        for c in range(N_CLASSES):
            @pl.when(lt_ref[c] >= 0)
            def _(c=c):
                zero_copy(lt_ref[c]).start()
        for c in range(N_CLASSES):
            @pl.when(lt_ref[c] >= 0)
            def _(c=c):
                zero_copy(lt_ref[c]).wait()

        def spare_start(t, carry):
            zero_copy(t).start()
            return carry

        def spare_wait(t, carry):
            zero_copy(t).wait()
            return carry

        lax.fori_loop(nu_ref[0], n_tiles, spare_start, 0)
        lax.fori_loop(nu_ref[0], n_tiles, spare_wait, 0)

    for r in range(td):
        pltpu.make_async_copy(src_ref.at[pl.ds(r, 1)], dst_hbm.at[pl.ds(pos_ref[0, 0, r], 1)],
                              rsem).start(priority=r % 2)
    pltpu.make_async_copy(src_ref, dst_hbm.at[pl.ds(0, td)], rsem).wait()


def _largest_divisor_tile(n, cap):
    for t in range(cap - cap % 8, 7, -8):
        if n % t == 0:
            return t
    raise ValueError(f"no row tile for {n} rows")


def _dispatch(x1e, plan, tm):
    n, width = x1e.shape
    n_tiles = plan["n_tiles"]
    td = _largest_divisor_tile(n, DISPATCH_TILE_CAP)
    steps = n // td
    grid_spec = pltpu.PrefetchScalarGridSpec(
        num_scalar_prefetch=2,
        grid=(steps,),
        in_specs=[pl.BlockSpec((1, 1, td), lambda i, lt, nu: (i, 0, 0), memory_space=pltpu.SMEM),
                  pl.BlockSpec((td, width), lambda i, lt, nu: (i, 0))],
        out_specs=pl.BlockSpec(memory_space=pl.ANY),
        scratch_shapes=[pltpu.VMEM((tm, width), F32), pltpu.SemaphoreType.DMA(()),
                        pltpu.SemaphoreType.DMA(())],
    )
    return pl.pallas_call(
        functools.partial(_dispatch_kernel, moe_tile=tm, n_tiles=n_tiles),
        grid_spec=grid_spec,
        out_shape=jax.ShapeDtypeStruct((n_tiles * tm, width), F32),
        compiler_params=_cparams("arbitrary"),
        name="dispatch",
    )(plan["last_tile"], plan["nu"], plan["pos"].reshape(steps, 1, td), x1e)


def _moe_kernel(ta_ref, tb_ref, nu_ref, x_ref, ga_ref, ua_ref, da_ref, gb_ref, ub_ref, db_ref, lg_ref, lb_ref,
                o_ref):
    del ta_ref, tb_ref
    g = pl.program_id(0)

    @pl.when(g < nu_ref[0])
    def _():
        x = x_ref[:, 0:D_MODEL]
        slab = x_ref[:, D_MODEL:D_MODEL + LANES]
        w1, w2, e1, e2 = slab[:, 0:1], slab[:, 1:2], slab[:, 2:3], slab[:, 3:4]
        first = e1 < e2
        wa = jnp.where(first, w1, w2)
        wb = jnp.where(first, w2, w1)
        xb = x.astype(BF16)
        gate_a, up_a = _dot(xb, ga_ref[0]), _dot(xb, ua_ref[0])
        gate_b, up_b = _dot(xb, gb_ref[0]), _dot(xb, ub_ref[0])
        y = wa * _dot((jax.nn.silu(gate_a) * up_a).astype(BF16), da_ref[0])
        y = y + wb * _dot((jax.nn.silu(gate_b) * up_b).astype(BF16), db_ref[0])
        o_ref[...] = _layer_norm_rows(DEEPNORM_ALPHA * x + y, lg_ref[...], lb_ref[...])

    @pl.when(g >= nu_ref[0])
    def _():
        o_ref[...] = jnp.zeros(o_ref.shape, F32)


def _moe(xs_sorted, plan, wg_bf, wu_bf, wd_bf, ln_g, ln_b, tm):
    n_tiles = plan["n_tiles"]
    width = xs_sorted.shape[1]
    up_a = pl.BlockSpec((1, D_MODEL, D_EXPERT), lambda g, ta, tb, nu: (ta[g], 0, 0))
    dn_a = pl.BlockSpec((1, D_EXPERT, D_MODEL), lambda g, ta, tb, nu: (ta[g], 0, 0))
    up_b = pl.BlockSpec((1, D_MODEL, D_EXPERT), lambda g, ta, tb, nu: (tb[g], 0, 0))
    dn_b = pl.BlockSpec((1, D_EXPERT, D_MODEL), lambda g, ta, tb, nu: (tb[g], 0, 0))
    vec = pl.BlockSpec((1, D_MODEL), lambda g, ta, tb, nu: (0, 0))
    grid_spec = pltpu.PrefetchScalarGridSpec(
        num_scalar_prefetch=3,
        grid=(n_tiles,),
        in_specs=[pl.BlockSpec((tm, width), lambda g, ta, tb, nu: (g, 0)),
                  up_a, up_a, dn_a, up_b, up_b, dn_b, vec, vec],
        out_specs=pl.BlockSpec((tm, D_MODEL), lambda g, ta, tb, nu: (g, 0)),
    )
    return pl.pallas_call(
        _moe_kernel,
        grid_spec=grid_spec,
        out_shape=jax.ShapeDtypeStruct((n_tiles * tm, D_MODEL), F32),
        compiler_params=_cparams("arbitrary"),
        name="moe",
    )(plan["ta"], plan["tb"], plan["nu"], xs_sorted, wg_bf, wu_bf, wd_bf, wg_bf, wu_bf, wd_bf, ln_g, ln_b)


def _ple_kernel(pos_ref, posn_ref, x2_hbm, pp_ref, ps_ref, wg_ref, wp_ref, op_ref, os_ref, buf, sems, *, n_s):
    i = pl.program_id(0)
    last = pl.num_programs(0) - 1
    tm = buf.shape[1]

    def issue(idx_ref, s):
        for r in range(tm):
            pltpu.make_async_copy(x2_hbm.at[pl.ds(idx_ref[0, 0, r], 1)], buf.at[s, pl.ds(r, 1)],
                                  sems.at[s]).start(priority=r % 2)

    def rows(x, p):
        gate = jax.nn.sigmoid(_dot(x.astype(BF16), wg_ref[...]))
        return x + gate * _dot(p.astype(BF16), wp_ref[...])

    def step(slot):
        if slot == 0:
            @pl.when(i == 0)
            def _():
                issue(pos_ref, 0)

        @pl.when(i < last)
        def _():
            issue(posn_ref, 1 - slot)

        pltpu.make_async_copy(x2_hbm.at[pl.ds(0, tm)], buf.at[slot], sems.at[slot]).wait()

        @pl.when(i < last)
        def _():
            op_ref[...] = rows(buf[slot], pp_ref[0])

        @pl.when(i == last)
        def _():
            os_ref[...] = rows(buf[slot, 0:n_s, :], ps_ref[0])

    for slot in range(2):
        @pl.when(lax.rem(i, 2) == slot)
        def _(slot=slot):
            step(slot)


def _ple(x2_sorted, pos, layer, p_p, p_s, wg_bf, wp_bf):
    n_p, n_s = p_p.shape[1], p_s.shape[1]
    tm = GATHER_TILE
    nb = n_p // tm
    steps = nb + 1
    pos_pad = jnp.zeros((steps * tm,), I32).at[:n_p + n_s].set(pos).reshape(steps, 1, tm)
    full = lambda a: pl.BlockSpec(a.shape, lambda i: (0, 0))
    prow = lambda n: pl.BlockSpec((tm, n), lambda i: (jnp.minimum(i, nb - 1), 0))
    return pl.pallas_call(
        functools.partial(_ple_kernel, n_s=n_s),
        grid=(steps,),
        in_specs=[pl.BlockSpec((1, 1, tm), lambda i: (i, 0, 0), memory_space=pltpu.SMEM),
                  pl.BlockSpec((1, 1, tm), lambda i: (jnp.minimum(i + 1, nb), 0, 0), memory_space=pltpu.SMEM),
                  pl.BlockSpec(memory_space=pl.ANY),
                  pl.BlockSpec((1, tm, D_PLE), lambda i: (layer, jnp.minimum(i, nb - 1), 0)),
                  pl.BlockSpec((1, n_s, D_PLE), lambda i: (layer, 0, 0)), full(wg_bf), full(wp_bf)],
        out_specs=[prow(D_MODEL), pl.BlockSpec((n_s, D_MODEL), lambda i: (0, 0))],
        out_shape=[jax.ShapeDtypeStruct((n_p, D_MODEL), F32), jax.ShapeDtypeStruct((n_s, D_MODEL), F32)],
        scratch_shapes=[pltpu.VMEM((2, tm, D_MODEL), F32), pltpu.SemaphoreType.DMA((2,))],
        compiler_params=_cparams("arbitrary"),
        name="ple",
    )(pos_pad, pos_pad, x2_sorted, p_p, p_s, wg_bf, wp_bf)


def _router_weights(w_group, w_router):
    wr = jnp.zeros((ROUTE_ROWS, D_MODEL), F32)
    wr = wr.at[ROUTE_GROUP_ROW:ROUTE_GROUP_ROW + N_GROUPS, :].set(jnp.transpose(w_group))
    wr = wr.at[ROUTE_EXPERT_ROW:ROUTE_EXPERT_ROW + N_EXPERTS, :].set(jnp.transpose(w_router))
    hi = wr.astype(BF16)
    lo = (wr - hi.astype(F32)).astype(BF16)
    return jnp.concatenate([hi, lo], axis=0)


def _layer_tail(i, acts_p, acts_s, w_list, xp, xs, p_p, p_s, experts_bf, ln_mix_g, ln_mix_b, ln_ffn_g, ln_ffn_b,
                w_group, w_router, w_ple_proj, w_ple_gate):
    rw = _router_weights(w_group[i], w_router[i])
    x1e, rt = _mix_out(acts_p, acts_s, w_list, xp, xs, ln_mix_g[i][None, :], ln_mix_b[i][None, :], rw)
    plan = _route_plan(rt[:, :x1e.shape[0]], MOE_TILE)
    xs_sorted = _dispatch(x1e, plan, MOE_TILE)
    x2_sorted = _moe(xs_sorted, plan, *experts_bf, ln_ffn_g[i][None, :], ln_ffn_b[i][None, :], MOE_TILE)
    return _ple(x2_sorted, plan["pos"], i, p_p, p_s, w_ple_gate[i].astype(BF16), w_ple_proj[i].astype(BF16))


def kernel(x_prompt, x_sample, cache_k, cache_v, page_table, state_conv, state_mlstm_C, state_mlstm_n,
           state_mlstm_m, p_prompt, p_sample, w_in_even, conv_w, lambda_q1, lambda_k1, lambda_q2, lambda_k2,
           subln_w, w_out_even, w_in_odd, b_gates_odd, mh_norm_w, w_out_odd, ln_mix_g, ln_mix_b, ln_ffn_g,
           ln_ffn_b, w_group, w_router, w_exp_gate, w_exp_up, w_exp_down, w_ple_proj, w_ple_gate):
    bp, tp, _ = x_prompt.shape
    bs, ts, _ = x_sample.shape
    assert ts == 1 and tp % ROW_TILE == 0 and tp % ATTN_TILE == 0 and tp % MLSTM_CHUNK == 0
    assert (bp * tp) % GATHER_TILE == 0
    n_p = bp * tp
    past_len = page_table.shape[1] * cache_k.shape[2]
    xp = x_prompt.reshape(n_p, D_MODEL)
    xs = x_sample.reshape(bs, D_MODEL)
    tail_w = (ln_mix_g, ln_mix_b, ln_ffn_g, ln_ffn_b, w_group, w_router, w_ple_proj, w_ple_gate)
    expert_w = (w_exp_gate, w_exp_up, w_exp_down)
    outs_p, outs_s = {}, {}
    for i in range(DEPTH):
        j = i // 2
        p_p = p_prompt.reshape(DEPTH, n_p, D_PLE)
        p_s = p_sample.reshape(DEPTH, bs, D_PLE)
        if i % 2 == 0:
            lam_init = 0.8 - 0.6 * math.exp(-0.3 * i)
            lam_vecs = jnp.stack([lambda_q1[j], lambda_k1[j], lambda_q2[j], lambda_k2[j]])
            sub = subln_w[j][None, :]
            w_bf = w_in_even[j].astype(BF16)
            tabs_p = _rope_tables(jnp.arange(tp))
            q0 = 3 * D_CONV
            wqt_bf = jnp.transpose(w_in_even[j][:, q0:q0 + QK_B]).astype(BF16)
            yc, qt, kf, vf, kb, vt, cst = _even_in_prompt(
                x_prompt if i == 0 else xp.reshape(bp, tp, D_MODEL), w_bf, wqt_bf, conv_w[j],
                jnp.zeros((bp, CONV_W - 1, D_CONV), F32), tabs_p)
            o_p, *experts_bf = _attn_prompt(qt, kb, vt, lam_vecs, sub, lam_init, i, expert_w)
            outs_p.setdefault("k", []).append(kf.reshape(bp, tp, 2 * H_B, DH_B))
            outs_p.setdefault("v", []).append(vf.reshape(bp, tp, H_B, 2 * DH_B))
            outs_p.setdefault("c", []).append(cst)
            tabs_s = _rope_tables(jnp.full((1,), past_len, I32))
            prev_t = jnp.swapaxes(state_conv[j], 0, 1)
            yc_s, q_s, kf_s, vf_s, u_s = _even_in_decode(xs, w_bf, conv_w[j], prev_t, tabs_s)
            sub_head = jnp.arange(2 * H_B)
            sub_head = jnp.where(sub_head < H_B, 2 * sub_head, 2 * (sub_head - H_B) + 1)
            lane_head = jnp.arange(QK_B) // DH_B
            qbd = jnp.where(lane_head[None, None, :] == sub_head[None, :, None], q_s[:, None, :],
                            jnp.zeros((), BF16))
            n_pool = cache_k.shape[1]
            pages = cache_k.shape[0] * n_pool
            k_view = jnp.transpose(cache_k, (0, 1, 3, 4, 2)).reshape(pages, QK_B, PAGE_SIZE)
            v_view = cache_v.reshape(pages, PAGE_SIZE * H_B, 2 * DH_B)
            o8 = _attn_decode(qbd, kf_s[:, None, :], vf_s[:, None, :], k_view, v_view,
                              page_table + j * n_pool, lam_vecs, sub, lam_init)
            o_s = o8[:, :H_B, :].reshape(bs, V_B).astype(BF16)
            outs_s.setdefault("k", []).append(kf_s.reshape(bs, ts, 2 * H_B, DH_B))
            outs_s.setdefault("v", []).append(vf_s.reshape(bs, ts, H_B, 2 * DH_B))
            outs_s.setdefault("c", []).append(jnp.stack([state_conv[j][:, 1, :], u_s], axis=1))
            w_out = w_out_even[j].astype(BF16)
            w_list = [w_out[:D_CONV], w_out[D_CONV:]]
            acts_p = [yc.reshape(n_p, D_CONV), o_p.reshape(n_p, V_B)]
            acts_s = [yc_s, o_s]
        else:
            w_in = w_in_odd[j]
            qw, vw = H_C * DK_C, H_C * DV_C
            w_bf = w_in[:, :2 * qw + 2 * vw].astype(BF16)
            wg = jnp.zeros((D_MODEL, LANES), F32).at[:, :2 * H_C].set(w_in[:, 2 * qw + 2 * vw:]).astype(BF16)
            wgt = jnp.transpose(wg[:, :2 * H_C])
            bg = jnp.zeros((1, LANES), F32).at[0, :2 * H_C].set(b_gates_odd[j])
            bgt = jnp.broadcast_to(b_gates_odd[j][:, None], (2 * H_C, LANES))
            nw = mh_norm_w[j][None, :]
            q, k, v, o, gc, gr, kt = _odd_in(xp.reshape(bp, tp, D_MODEL), w_bf, wg, wgt, bg, bgt, decode=False)
            h_p, c_p, n_pp, m_p, *experts_bf = _mlstm(
                q, k, kt, v, o, gc, gr, nw, jnp.zeros((bp, H_C, DK_C, DV_C), F32),
                jnp.zeros((bp, H_C, DK_C), F32), jnp.zeros((bp, H_C, LANES), F32), MLSTM_CHUNK, i, expert_w)
            outs_p.setdefault("C", []).append(c_p)
            outs_p.setdefault("n", []).append(n_pp)
            outs_p.setdefault("m", []).append(m_p[:, :, 0])
            q_s, k_s, v_s, o_s2, gc_s, gr_s, kt_s = _odd_in(xs, w_bf, wg, wgt, bg, bgt, decode=True)
            ch = MLSTM_DECODE_CHUNK
            pad_rows = lambda a: jnp.zeros((bs, ch, a.shape[1]), a.dtype).at[:, 0, :].set(a)
            lane = jnp.arange(LANES)
            inert_c = jnp.where(lane < H_C, -jnp.inf, 0.0).astype(F32)
            gc_pad = jnp.broadcast_to(inert_c[None, None, :], (bs, ch, LANES)).at[:, 0, :].set(gc_s)
            inert_r = jnp.where(jnp.arange(2 * H_C) < H_C, -jnp.inf, 0.0).astype(F32)
            gr_pad = jnp.broadcast_to(inert_r[None, :, None], (bs, 2 * H_C, ch)).at[:, :, 0].set(gr_s.T)
            m0 = jnp.broadcast_to(state_mlstm_m[j][:, :, None], (bs, H_C, LANES))
            kt_pad = jnp.zeros((bs, qw, ch), BF16).at[:, :, 0].set(kt_s.T)
            h_s, c_s, n_s, m_s = _mlstm(pad_rows(q_s), pad_rows(k_s), kt_pad, pad_rows(v_s), pad_rows(o_s2),
                                        gc_pad, gr_pad, nw, state_mlstm_C[j], state_mlstm_n[j], m0, ch)
            outs_s.setdefault("C", []).append(c_s)
            outs_s.setdefault("n", []).append(n_s)
            outs_s.setdefault("m", []).append(m_s[:, :, 0])
            w_list = [w_out_odd[j].astype(BF16)]
            acts_p = [h_p.reshape(n_p, vw)]
            acts_s = [h_s[:, 0, :]]
        xp, xs = _layer_tail(i, acts_p, acts_s, w_list, xp, xs, p_p, p_s, experts_bf, *tail_w)
    st = lambda lst: jnp.stack(lst)
    return (xp.reshape(bp, tp, D_MODEL), xs.reshape(bs, ts, D_MODEL),
            st(outs_p["k"]), st(outs_p["v"]), st(outs_p["c"]), st(outs_p["C"]), st(outs_p["n"]), st(outs_p["m"]),
            st(outs_s["k"]), st(outs_s["v"]), st(outs_s["c"]), st(outs_s["C"]), st(outs_s["n"]), st(outs_s["m"]))
```
